```python
import jax, jax.numpy as jnp
from jax import lax
import numpy as np

D_MODEL = 2048
BATCH = 4
SEQ = 2048
DEPTH = 2
DEC_BATCH = 32
DEC_SEQ = 4
PAST_LEN = 8192
PAGE_SIZE = 128

HEAD_DIM = 128
ROPE_DIMS = HEAD_DIM // 4
ROPE_THETA = 500000.0
NORM_EPS = 1e-6
Q_BLOCK = 128
H_RET = 8
RET_CHUNK = 128
RET_THETA = 10000.0
H_DIL = 8
DIL_PATTERNS = ((128, 1), (512, 4), (2048, 16))
DIL_WMAX = 2048
H_NSA = 8
G_NSA = 2
NSA_BLOCK = 64
NSA_TOPK = 16
NSA_WINDOW = 512
D_RNN = 1024
RNN_BLOCKS = 8
CONV_W = 4
LRU_C = 8.0
N_MEM = 256
H_MEM = 4
D_FF = 4 * D_MODEL
N_EVEN = (DEPTH + 1) // 2
N_ODD = DEPTH // 2

kernel_name = "hybrid_retention_dilated_nsa_rglru_step"


def _offsets(sizes):
    return [int(v) for v in np.cumsum(sizes)[:-1]]


def rmsnorm(x, g):
    xf = x.astype(jnp.float32)
    y = xf * lax.rsqrt(jnp.mean(xf * xf, axis=-1, keepdims=True) + NORM_EPS)
    return (y * g.astype(jnp.float32)).astype(x.dtype)


def rope(x, pos, n_rot, theta):
    half = n_rot // 2
    inv = 1.0 / (theta ** (jnp.arange(half, dtype=jnp.float32) * (2.0 / n_rot)))
    ang = pos.astype(jnp.float32)[:, None] * inv[None, :]
    cos = jnp.cos(ang)[:, None, :]
    sin = jnp.sin(ang)[:, None, :]
    xr = x[..., :n_rot].astype(jnp.float32)
    x1, x2 = xr[..., :half], xr[..., half:]
    rot = jnp.concatenate([x1 * cos - x2 * sin, x2 * cos + x1 * sin], axis=-1)
    return jnp.concatenate([rot.astype(x.dtype), x[..., n_rot:]], axis=-1)


def masked_softmax(s, mask):
    s = jnp.where(mask, s, -jnp.inf)
    m = jnp.max(s, axis=-1, keepdims=True)
    m = jnp.where(jnp.isfinite(m), m, 0.0)
    p = jnp.exp(s - m)
    l = jnp.sum(p, axis=-1, keepdims=True)
    return p / jnp.maximum(l, 1e-30), m, l


def ret_log_decay():
    return jnp.log1p(-jnp.exp2(-5.0 - jnp.arange(H_RET, dtype=jnp.float32)))


def retention_chunk(s_prev, q, k, v, log_g):
    c = q.shape[2]
    i = jnp.arange(c, dtype=jnp.float32)
    diff = i[:, None] - i[None, :]
    decay = jnp.where(diff >= 0, jnp.exp(jnp.maximum(diff, 0.0)[None] * log_g[:, None, None]), 0.0)
    inner = jnp.einsum('nhid,nhjd->nhij', q, k) * decay
    q_decay = jnp.exp((i[None, :] + 1.0) * log_g[:, None])[None, :, :, None]
    k_decay = jnp.exp((c - 1.0 - i[None, :]) * log_g[:, None])[None, :, :, None]
    o = jnp.einsum('nhij,nhjv->nhiv', inner, v) + jnp.einsum('nhid,nhdv->nhiv', q, s_prev) * q_decay
    s_new = jnp.exp(c * log_g)[None, :, None, None] * s_prev + jnp.einsum('nhjd,nhjv->nhdv', k * k_decay, v)
    return s_new, o


def dilated_attention(q, kx, vx, q_idx):
    outs, lses = [], []
    for window, dil in DIL_PATTERNS:
        offs = jnp.arange(window // dil + 1, dtype=jnp.int32) * dil
        idx = q_idx[:, None] - offs[None, :]
        valid = idx >= 0
        idx = jnp.maximum(idx, 0)
        kg = kx[:, idx]
        vg = vx[:, idx]
        s = jnp.einsum('nthd,ntkhd->nthk', q, kg).astype(jnp.float32)
        p, m, l = masked_softmax(s, valid[None, :, None, :])
        outs.append(jnp.einsum('nthk,ntkhd->nthd', p, vg).astype(jnp.float32))
        lses.append((m + jnp.log(l))[..., 0])
    w = jax.nn.softmax(jnp.stack(lses), axis=0)
    return jnp.einsum('pnth,pnthd->nthd', w, jnp.stack(outs))


def even_project(xn, pos, w_in, dq_g, dk_g):
    n, t, _ = xn.shape
    sizes = [H_RET * HEAD_DIM] * 4 + [H_DIL * HEAD_DIM] * 3
    rq, rk, rv, rg, dq, dk, dv = jnp.split(xn @ w_in, _offsets(sizes), axis=-1)
    heads = lambda a, h: a.reshape(n, t, h, HEAD_DIM)
    scale = HEAD_DIM ** -0.5
    rq = rope(heads(rq, H_RET), pos, HEAD_DIM, RET_THETA) * scale
    rk = rope(heads(rk, H_RET), pos, HEAD_DIM, RET_THETA)
    dq = rope(rmsnorm(heads(dq, H_DIL), dq_g), pos, ROPE_DIMS, ROPE_THETA) * scale
    dk = rope(rmsnorm(heads(dk, H_DIL), dk_g), pos, ROPE_DIMS, ROPE_THETA)
    return rq, rk, heads(rv, H_RET), rg, dq, dk, heads(dv, H_DIL)


def even_output(ret_o, rg, dil_o, gn_g, w_out):
    n, t = rg.shape[:2]
    mu = jnp.mean(ret_o, axis=-1, keepdims=True)
    var = jnp.mean(jnp.square(ret_o - mu), axis=-1, keepdims=True)
    y = (ret_o - mu) * lax.rsqrt(var + NORM_EPS) * gn_g.astype(jnp.float32)
    y = y.reshape(n, t, -1) * jax.nn.silu(rg.astype(jnp.float32))
    cat = jnp.concatenate([y, dil_o.reshape(n, t, -1).astype(jnp.float32)], axis=-1)
    return cat.astype(w_out.dtype) @ w_out


def even_mixer_prompt(xn, pos, w_in, gn_g, dq_g, dk_g, w_out):
    n, t, _ = xn.shape
    rq, rk, rv, rg, dq, dk, dv = even_project(xn, pos, w_in, dq_g, dk_g)
    nc = t // RET_CHUNK
    def to_chunks(a):
        return a.astype(jnp.float32).reshape(n, nc, RET_CHUNK, H_RET, HEAD_DIM).transpose(1, 0, 3, 2, 4)
    log_g = ret_log_decay()
    s0 = jnp.zeros((n, H_RET, HEAD_DIM, HEAD_DIM), jnp.float32)
    s_fin, oc = lax.scan(lambda s, c: retention_chunk(s, c[0], c[1], c[2], log_g), s0,
                         (to_chunks(rq), to_chunks(rk), to_chunks(rv)))
    ret_o = oc.transpose(1, 0, 3, 2, 4).reshape(n, t, H_RET, HEAD_DIM)
    nqb = t // Q_BLOCK
    qb = dq.reshape(n, nqb, Q_BLOCK, H_DIL, HEAD_DIM).swapaxes(0, 1)
    ib = jnp.arange(t, dtype=jnp.int32).reshape(nqb, Q_BLOCK)
    ob = lax.map(lambda a: dilated_attention(a[0], dk, dv, a[1]), (qb, ib))
    dil_o = ob.swapaxes(0, 1).reshape(n, t, H_DIL, HEAD_DIM)
    y = even_output(ret_o, rg, dil_o, gn_g, w_out)
    keep = min(DIL_WMAX, t)
    return y, s_fin, dk[:, t - keep:], dv[:, t - keep:]


def even_mixer_sample(xn, pos, s_past, buf_k, buf_v, w_in, gn_g, dq_g, dk_g, w_out):
    n, t, _ = xn.shape
    rq, rk, rv, rg, dq, dk, dv = even_project(xn, pos, w_in, dq_g, dk_g)
    hf = lambda a: a.astype(jnp.float32).transpose(0, 2, 1, 3)
    s_new, o = retention_chunk(s_past.astype(jnp.float32), hf(rq), hf(rk), hf(rv), ret_log_decay())
    ret_o = o.transpose(0, 2, 1, 3)
    kx = jnp.concatenate([buf_k.astype(dk.dtype), dk], axis=1)
    vx = jnp.concatenate([buf_v.astype(dv.dtype), dv], axis=1)
    q_idx = buf_k.shape[1] + jnp.arange(t, dtype=jnp.int32)
    dil_o = dilated_attention(dq, kx, vx, q_idx)
    y = even_output(ret_o, rg, dil_o, gn_g, w_out)
    return y, s_new, dk, dv


def compress(rows, pw):
    n, l = rows.shape[:2]
    nb = l // NSA_BLOCK
    blocks = rows[:, :nb * NSA_BLOCK].reshape(n, nb, NSA_BLOCK, G_NSA, HEAD_DIM)
    return jnp.einsum('nbrgd,rd->nbgd', blocks, pw)


def nsa_attention(q, qpos, kc, vc, fetch_sel, kw, vw, kwpos, gates):
    n, t, h, d = q.shape
    nb = kc.shape[1]
    qg = q.reshape(n, t, G_NSA, h // G_NSA, d)
    blk = jnp.arange(nb, dtype=jnp.int32)
    s_c = jnp.einsum('ntghd,nbgd->ntghb', qg, kc).astype(jnp.float32)
    cmask = ((blk[None, :] + 1) * NSA_BLOCK - 1) <= qpos[:, None]
    p_c, _, _ = masked_softmax(s_c, cmask[None, :, None, None, :])
    o_c = jnp.einsum('ntghb,nbgd->ntghd', p_c, vc)
    cur = qpos // NSA_BLOCK
    cand = blk[None, :] < cur[:, None]
    imp = jnp.where(cand[None, :, None, :], jnp.sum(p_c, axis=3), -jnp.inf)
    top_v, top_i = lax.top_k(imp, min(NSA_TOPK, nb))
    blocks = jnp.concatenate([top_i, jnp.broadcast_to(cur[None, :, None, None], (n, t, G_NSA, 1))], axis=-1)
    bvalid = jnp.concatenate([jnp.isfinite(top_v), jnp.ones((n, t, G_NSA, 1), bool)], axis=-1)
    ks, vs = fetch_sel(blocks)
    nsel = blocks.shape[-1]
    kpos = blocks[..., None] * NSA_BLOCK + jnp.arange(NSA_BLOCK, dtype=jnp.int32)
    smask = bvalid[..., None] & (kpos <= qpos[None, :, None, None, None])
    ks = ks.reshape(n, t, G_NSA, nsel * NSA_BLOCK, d)
    vs = vs.reshape(n, t, G_NSA, nsel * NSA_BLOCK, d)
    smask = smask.reshape(n, t, G_NSA, 1, nsel * NSA_BLOCK)
    s_s = jnp.einsum('ntghd,ntgkd->ntghk', qg, ks).astype(jnp.float32)
    p_s, _, _ = masked_softmax(s_s, smask)
    o_s = jnp.einsum('ntghk,ntgkd->ntghd', p_s, vs)
    s_w = jnp.einsum('ntghd,nsgd->ntghs', qg, kw).astype(jnp.float32)
    dist = qpos[:, None] - kwpos[None, :]
    wmask = (dist >= 0) & (dist < NSA_WINDOW) & (kwpos[None, :] >= 0)
    p_w, _, _ = masked_softmax(s_w, wmask[None, :, None, None, :])
    o_w = jnp.einsum('ntghs,nsgd->ntghd', p_w, vw)
    g = gates.reshape(n, t, G_NSA, h // G_NSA, 3)
    o = g[..., 0:1] * o_c + g[..., 1:2] * o_s + g[..., 2:3] * o_w
    return o.reshape(n, t, h, d)


def odd_project(xn, pos, w_in, q_g, k_g):
    n, t, _ = xn.shape
    sizes = [H_NSA * HEAD_DIM] + [G_NSA * HEAD_DIM] * 6 + [3 * H_NSA, D_RNN, D_RNN]
    cq, kcm, vcm, ksl, vsl, kwn, vwn, gt, lx, ly = jnp.split(xn @ w_in, _offsets(sizes), axis=-1)
    q = rope(rmsnorm(cq.reshape(n, t, H_NSA, HEAD_DIM), q_g), pos, ROPE_DIMS, ROPE_THETA) * HEAD_DIM ** -0.5
    kv = lambda a: a.reshape(n, t, G_NSA, HEAD_DIM)
    key = lambda a, g: rope(rmsnorm(kv(a), g), pos, ROPE_DIMS, ROPE_THETA)
    gates = jax.nn.sigmoid(gt.reshape(n, t, H_NSA, 3).astype(jnp.float32))
    return (q, key(kcm, k_g[0]), kv(vcm), key(ksl, k_g[1]), kv(vsl), key(kwn, k_g[2]), kv(vwn),
            gates, lx, ly)


def rglru(xb, h0, conv_buf, conv_w, conv_b, w_a, b_a, w_i, b_i, lam):
    n, t, d = xb.shape
    xin = jnp.concatenate([conv_buf.astype(jnp.float32), xb.astype(jnp.float32)], axis=1)
    xc = conv_b.astype(jnp.float32) + sum(xin[:, j:j + t] * conv_w[j].astype(jnp.float32) for j in range(CONV_W))
    xr = xc.reshape(n, t, RNN_BLOCKS, d // RNN_BLOCKS)
    r = jax.nn.sigmoid(jnp.einsum('ntbi,bij->ntbj', xr, w_a.astype(jnp.float32)).reshape(n, t, d) + b_a)
    i = jax.nn.sigmoid(jnp.einsum('ntbi,bij->ntbj', xr, w_i.astype(jnp.float32)).reshape(n, t, d) + b_i)
    log_a = -LRU_C * r * jax.nn.softplus(-lam.astype(jnp.float32))
    a = jnp.exp(log_a)
    b = jnp.sqrt(-jnp.expm1(2.0 * log_a)) * (i * xc)
    b = b.at[:, 0].add(a[:, 0] * h0.astype(jnp.float32))
    def combine(lhs, rhs):
        return rhs[0] * lhs[0], rhs[0] * lhs[1] + rhs[1]
    _, h = lax.associative_scan(combine, (a, b), axis=1)
    return h, h[:, -1], xin[:, -(CONV_W - 1):]


def odd_output(o_nsa, h, ly, w_out):
    n, t = ly.shape[:2]
    y_rnn = jax.nn.gelu(ly.astype(jnp.float32)) * h
    cat = jnp.concatenate([o_nsa.reshape(n, t, -1).astype(jnp.float32), y_rnn], axis=-1)
    return cat.astype(w_out.dtype) @ w_out


def odd_mixer_prompt(xn, pos, w_in, q_g, k_g, pw_k, pw_v, conv_w, conv_b, w_a, b_a, w_i, b_i, lam, w_out):
    n, t, _ = xn.shape
    q, kcm, vcm, ksl, vsl, kwn, vwn, gates, lx, ly = odd_project(xn, pos, w_in, q_g, k_g)
    kc, vc = compress(kcm, pw_k), compress(vcm, pw_v)
    nb = t // NSA_BLOCK
    kb = ksl.reshape(n, nb, NSA_BLOCK, G_NSA, HEAD_DIM)
    vb = vsl.reshape(n, nb, NSA_BLOCK, G_NSA, HEAD_DIM)
    n_idx = jnp.arange(n)[:, None, None, None]
    g_idx = jnp.arange(G_NSA)[None, None, :, None]
    def fetch(blocks):
        return kb[n_idx, blocks, :, g_idx], vb[n_idx, blocks, :, g_idx]
    pad = ((0, 0), (NSA_WINDOW, 0), (0, 0), (0, 0))
    kw_pad, vw_pad = jnp.pad(kwn, pad), jnp.pad(vwn, pad)
    nqb = t // Q_BLOCK
    def blocks_of(a):
        return a.reshape((n, nqb, Q_BLOCK) + a.shape[2:]).swapaxes(0, 1)
    def one_block(args):
        qb, gb, s0 = args
        pb = s0 + jnp.arange(Q_BLOCK, dtype=jnp.int32)
        kw = lax.dynamic_slice_in_dim(kw_pad, s0, NSA_WINDOW + Q_BLOCK, axis=1)
        vw = lax.dynamic_slice_in_dim(vw_pad, s0, NSA_WINDOW + Q_BLOCK, axis=1)
        kwpos = s0 - NSA_WINDOW + jnp.arange(NSA_WINDOW + Q_BLOCK, dtype=jnp.int32)
        return nsa_attention(qb, pb, kc, vc, fetch, kw, vw, kwpos, gb)
    starts = jnp.arange(nqb, dtype=jnp.int32) * Q_BLOCK
    ob = lax.map(one_block, (blocks_of(q), blocks_of(gates), starts))
    o = ob.swapaxes(0, 1).reshape(n, t, H_NSA, HEAD_DIM)
    h, h_last, conv_new = rglru(lx, jnp.zeros((n, D_RNN), jnp.float32), jnp.zeros((n, CONV_W - 1, D_RNN), jnp.float32),
                               conv_w, conv_b, w_a, b_a, w_i, b_i, lam)
    y = odd_output(o, h, ly, w_out)
    keep = min(NSA_WINDOW, t)
    return y, kcm, vcm, ksl, vsl, kwn[:, t - keep:], vwn[:, t - keep:], h_last, conv_new


def odd_mixer_sample(xn, pos, pool_ck, pool_cv, pool_sk, pool_sv, win_k, win_v, h0, conv_buf, page_table,
                     w_in, q_g, k_g, pw_k, pw_v, conv_w, conv_b, w_a, b_a, w_i, b_i, lam, w_out):
    n, t, _ = xn.shape
    n_pages = page_table.shape[1]
    past = n_pages * PAGE_SIZE
    q, kcm, vcm, ksl, vsl, kwn, vwn, gates, lx, ly = odd_project(xn, pos, w_in, q_g, k_g)
    def past_rows(pool):
        return pool[page_table].reshape(n, past, G_NSA, HEAD_DIM)
    kc = compress(jnp.concatenate([past_rows(pool_ck).astype(kcm.dtype), kcm], axis=1), pw_k)
    vc = compress(jnp.concatenate([past_rows(pool_cv).astype(vcm.dtype), vcm], axis=1), pw_v)
    bpp = PAGE_SIZE // NSA_BLOCK
    npb = past // NSA_BLOCK
    nbn = -(-t // NSA_BLOCK)
    def new_blocks(a):
        a = jnp.pad(a, ((0, 0), (0, nbn * NSA_BLOCK - t), (0, 0), (0, 0)))
        return a.reshape(n, nbn, NSA_BLOCK, G_NSA, HEAD_DIM)
    kb_new, vb_new = new_blocks(ksl), new_blocks(vsl)
    kb_pool = pool_sk.reshape(-1, NSA_BLOCK, G_NSA, HEAD_DIM)
    vb_pool = pool_sv.reshape(-1, NSA_BLOCK, G_NSA, HEAD_DIM)
    n_idx = jnp.arange(n)[:, None, None, None]
    g_idx = jnp.arange(G_NSA)[None, None, :, None]
    def fetch(blocks):
        is_past = (blocks < npb)[..., None, None]
        phys = page_table[n_idx, jnp.minimum(blocks // bpp, n_pages - 1)] * bpp + blocks % bpp
        nb_i = jnp.clip(blocks - npb, 0, nbn - 1)
        ks = jnp.where(is_past, kb_pool[phys, :, g_idx].astype(ksl.dtype), kb_new[n_idx, nb_i, :, g_idx])
        vs = jnp.where(is_past, vb_pool[phys, :, g_idx].astype(vsl.dtype), vb_new[n_idx, nb_i, :, g_idx])
        return ks, vs
    kw = jnp.concatenate([win_k.astype(kwn.dtype), kwn], axis=1)
    vw = jnp.concatenate([win_v.astype(vwn.dtype), vwn], axis=1)
    kwpos = past - win_k.shape[1] + jnp.arange(win_k.shape[1] + t, dtype=jnp.int32)
    o = nsa_attention(q, pos, kc, vc, fetch, kw, vw, kwpos, gates)
    h, h_last, conv_new = rglru(lx, h0, conv_buf, conv_w, conv_b, w_a, b_a, w_i, b_i, lam)
    y = odd_output(o, h, ly, w_out)
    return y, kcm, vcm, ksl, vsl, kwn, vwn, h_last, conv_new


def mem_kv(mem, norm_g, w_k, w_v, k_g):
    mn = rmsnorm(mem, norm_g)
    n, m, _ = mn.shape
    k = rmsnorm((mn @ w_k).reshape(n, m, H_MEM, HEAD_DIM), k_g)
    v = (mn @ w_v).reshape(n, m, H_MEM, HEAD_DIM)
    return k, v


def mem_attend(xn, k, v, w_q, q_g, w_o):
    n, t, _ = xn.shape
    q = rmsnorm((xn @ w_q).reshape(n, t, H_MEM, HEAD_DIM), q_g) * HEAD_DIM ** -0.5
    s = jnp.einsum('nthd,nmhd->nhtm', q, k.astype(q.dtype)).astype(jnp.float32)
    p = jax.nn.softmax(s, axis=-1)
    o = jnp.einsum('nhtm,nmhd->nthd', p, v).reshape(n, t, -1)
    return o.astype(w_o.dtype) @ w_o


def sq_relu_mlp(xn, w1, w2):
    hid = jax.nn.relu(xn @ w1)
    return (hid * hid) @ w2


def setup_inputs(seed: int = 0) -> dict:
    key = jax.random.key(seed)
    ks = jax.random.split(key, 96)
    counter = [0]
    def nk():
        k = ks[counter[0]]
        counter[0] += 1
        return k
    def nrm(shape, scale=1.0):
        return scale * jax.random.normal(nk(), shape, jnp.float32)
    def gain(shape):
        return 1.0 + 0.02 * nrm(shape)
    hd = HEAD_DIM
    n_pages = PAST_LEN // PAGE_SIZE
    n_used = DEC_BATCH * n_pages
    n_phys = n_used + max(1, n_used // 4)
    dil_keep = min(DIL_WMAX, PAST_LEN)
    win_keep = min(NSA_WINDOW, PAST_LEN)
    ev_in = 4 * H_RET * hd + 3 * H_DIL * hd
    od_in = H_NSA * hd + 6 * G_NSA * hd + 3 * H_NSA + 2 * D_RNN
    ev_mix = (H_RET + H_DIL) * hd
    od_mix = H_NSA * hd + D_RNN
    bw = D_RNN // RNN_BLOCKS
    d = D_MODEL
    x_prompt = nrm((BATCH, SEQ, d))
    x_sample = nrm((DEC_BATCH, DEC_SEQ, d))
    mem_prompt = nrm((BATCH, N_MEM, d))
    state_ret = nrm((N_EVEN, DEC_BATCH, H_RET, hd, hd), 0.5)
    cache_dil_k = nrm((N_EVEN, DEC_BATCH, dil_keep, H_DIL, hd))
    cache_dil_v = nrm((N_EVEN, DEC_BATCH, dil_keep, H_DIL, hd))
    cache_nsa_cmp_k = nrm((N_ODD, n_phys, PAGE_SIZE, G_NSA, hd))
    cache_nsa_cmp_v = nrm((N_ODD, n_phys, PAGE_SIZE, G_NSA, hd))
    cache_nsa_slc_k = nrm((N_ODD, n_phys, PAGE_SIZE, G_NSA, hd))
    cache_nsa_slc_v = nrm((N_ODD, n_phys, PAGE_SIZE, G_NSA, hd))
    cache_nsa_win_k = nrm((N_ODD, DEC_BATCH, win_keep, G_NSA, hd))
    cache_nsa_win_v = nrm((N_ODD, DEC_BATCH, win_keep, G_NSA, hd))
    state_lru_h = nrm((N_ODD, DEC_BATCH, D_RNN), 0.5)
    state_lru_conv = nrm((N_ODD, DEC_BATCH, CONV_W - 1, D_RNN))
    cache_mem_k = nrm((DEPTH, DEC_BATCH, N_MEM, H_MEM, hd))
    cache_mem_v = nrm((DEPTH, DEC_BATCH, N_MEM, H_MEM, hd))
    page_table = jax.random.permutation(nk(), n_phys)[:n_used].reshape(DEC_BATCH, n_pages).astype(jnp.int32)
    u = jax.random.uniform(nk(), (N_ODD, D_RNN), jnp.float32, 0.9, 0.999)
    s = u ** (1.0 / LRU_C)
    lru_lambda = jnp.log(s) - jnp.log1p(-s)
    pw_scale = NSA_BLOCK ** -0.5
    return {
        "x_prompt": x_prompt, "x_sample": x_sample, "mem_prompt": mem_prompt,
        "state_ret": state_ret, "cache_dil_k": cache_dil_k, "cache_dil_v": cache_dil_v,
        "cache_nsa_cmp_k": cache_nsa_cmp_k, "cache_nsa_cmp_v": cache_nsa_cmp_v,
        "cache_nsa_slc_k": cache_nsa_slc_k, "cache_nsa_slc_v": cache_nsa_slc_v,
        "cache_nsa_win_k": cache_nsa_win_k, "cache_nsa_win_v": cache_nsa_win_v,
        "state_lru_h": state_lru_h, "state_lru_conv": state_lru_conv,
        "cache_mem_k": cache_mem_k, "cache_mem_v": cache_mem_v, "page_table": page_table,
        "ev_w_in": nrm((N_EVEN, d, ev_in), d ** -0.5),
        "ret_gn_g": gain((N_EVEN, H_RET, hd)),
        "dil_q_norm_g": gain((N_EVEN, hd)),
        "dil_k_norm_g": gain((N_EVEN, hd)),
        "ev_w_out": nrm((N_EVEN, ev_mix, d), ev_mix ** -0.5),
        "od_w_in": nrm((N_ODD, d, od_in), d ** -0.5),
        "nsa_q_norm_g": gain((N_ODD, hd)),
        "nsa_k_norm_g": gain((N_ODD, 3, hd)),
        "nsa_pw_k": pw_scale * (1.0 + nrm((N_ODD, NSA_BLOCK, hd), 0.5)),
        "nsa_pw_v": pw_scale * (1.0 + nrm((N_ODD, NSA_BLOCK, hd), 0.5)),
        "lru_conv_w": nrm((N_ODD, CONV_W, D_RNN), CONV_W ** -0.5),
        "lru_conv_b": nrm((N_ODD, D_RNN), 0.01),
        "lru_w_a": nrm((N_ODD, RNN_BLOCKS, bw, bw), bw ** -0.5),
        "lru_b_a": nrm((N_ODD, D_RNN), 0.01),
        "lru_w_i": nrm((N_ODD, RNN_BLOCKS, bw, bw), bw ** -0.5),
        "lru_b_i": nrm((N_ODD, D_RNN), 0.01),
        "lru_lambda": lru_lambda,
        "od_w_out": nrm((N_ODD, od_mix, d), od_mix ** -0.5),
        "norm_mix_g": gain((DEPTH, d)),
        "norm_mem_g": gain((DEPTH, d)),
        "norm_mlp_g": gain((DEPTH, d)),
        "mem_norm_g": gain((DEPTH, d)),
        "mem_w_q": nrm((DEPTH, d, H_MEM * hd), d ** -0.5),
        "mem_w_k": nrm((DEPTH, d, H_MEM * hd), d ** -0.5),
        "mem_w_v": nrm((DEPTH, d, H_MEM * hd), d ** -0.5),
        "mem_q_norm_g": gain((DEPTH, hd)),
        "mem_k_norm_g": gain((DEPTH, hd)),
        "mem_w_o": nrm((DEPTH, H_MEM * hd, d), (H_MEM * hd) ** -0.5),
        "mlp_w1": nrm((DEPTH, d, D_FF), d ** -0.5),
        "mlp_w2": nrm((DEPTH, D_FF, d), 0.5 * D_FF ** -0.5),
    }


def reference(x_prompt, x_sample, mem_prompt, state_ret, cache_dil_k, cache_dil_v,
              cache_nsa_cmp_k, cache_nsa_cmp_v, cache_nsa_slc_k, cache_nsa_slc_v,
              cache_nsa_win_k, cache_nsa_win_v, state_lru_h, state_lru_conv,
              cache_mem_k, cache_mem_v, page_table,
              ev_w_in, ret_gn_g, dil_q_norm_g, dil_k_norm_g, ev_w_out,
              od_w_in, nsa_q_norm_g, nsa_k_norm_g, nsa_pw_k, nsa_pw_v,
              lru_conv_w, lru_conv_b, lru_w_a, lru_b_a, lru_w_i, lru_b_i, lru_lambda, od_w_out,
              norm_mix_g, norm_mem_g, norm_mlp_g, mem_norm_g,
              mem_w_q, mem_w_k, mem_w_v, mem_q_norm_g, mem_k_norm_g, mem_w_o,
              mlp_w1, mlp_w2):
    past = page_table.shape[1] * PAGE_SIZE
    pos_p = jnp.arange(x_prompt.shape[1], dtype=jnp.int32)
    pos_s = past + jnp.arange(x_sample.shape[1], dtype=jnp.int32)
    xp, xs = x_prompt, x_sample
    ret_p, ret_s, dk_p, dv_p, dk_s, dv_s = [], [], [], [], [], []
    ck_p, cv_p, ck_s, cv_s, sk_p, sv_p, sk_s, sv_s = [], [], [], [], [], [], [], []
    wk_p, wv_p, wk_s, wv_s, lh_p, lh_s, lc_p, lc_s = [], [], [], [], [], [], [], []
    mk_p, mv_p = [], []
    for layer in range(DEPTH):
        hp = rmsnorm(xp, norm_mix_g[layer])
        hs = rmsnorm(xs, norm_mix_g[layer])
        if layer % 2 == 0:
            e = layer // 2
            yp, s_p, k_p, v_p = even_mixer_prompt(hp, pos_p, ev_w_in[e], ret_gn_g[e], dil_q_norm_g[e],
                                                  dil_k_norm_g[e], ev_w_out[e])
            ys, s_s, k_s, v_s = even_mixer_sample(hs, pos_s, state_ret[e], cache_dil_k[e], cache_dil_v[e],
                                                  ev_w_in[e], ret_gn_g[e], dil_q_norm_g[e], dil_k_norm_g[e], ev_w_out[e])
            ret_p.append(s_p); ret_s.append(s_s)
            dk_p.append(k_p); dv_p.append(v_p); dk_s.append(k_s); dv_s.append(v_s)
        else:
            o = layer // 2
            odd_w = (od_w_in[o], nsa_q_norm_g[o], nsa_k_norm_g[o], nsa_pw_k[o], nsa_pw_v[o],
                     lru_conv_w[o], lru_conv_b[o], lru_w_a[o], lru_b_a[o], lru_w_i[o], lru_b_i[o],
                     lru_lambda[o], od_w_out[o])
            yp, a1, a2, a3, a4, a5, a6, a7, a8 = odd_mixer_prompt(hp, pos_p, *odd_w)
            ys, b1, b2, b3, b4, b5, b6, b7, b8 = odd_mixer_sample(
                hs, pos_s, cache_nsa_cmp_k[o], cache_nsa_cmp_v[o], cache_nsa_slc_k[o], cache_nsa_slc_v[o],
                cache_nsa_win_k[o], cache_nsa_win_v[o], state_lru_h[o], state_lru_conv[o], page_table, *odd_w)
            ck_p.append(a1); cv_p.append(a2); sk_p.append(a3); sv_p.append(a4)
            wk_p.append(a5); wv_p.append(a6); lh_p.append(a7); lc_p.append(a8)
            ck_s.append(b1); cv_s.append(b2); sk_s.append(b3); sv_s.append(b4)
            wk_s.append(b5); wv_s.append(b6); lh_s.append(b7); lc_s.append(b8)
        xp = xp + yp.astype(xp.dtype)
        xs = xs + ys.astype(xs.dtype)
        mk, mv = mem_kv(mem_prompt, mem_norm_g[layer], mem_w_k[layer], mem_w_v[layer], mem_k_norm_g[layer])
        mk_p.append(mk); mv_p.append(mv)
        xp = xp + mem_attend(rmsnorm(xp, norm_mem_g[layer]), mk, mv, mem_w_q[layer], mem_q_norm_g[layer],
                             mem_w_o[layer]).astype(xp.dtype)
        xs = xs + mem_attend(rmsnorm(xs, norm_mem_g[layer]), cache_mem_k[layer], cache_mem_v[layer],
                             mem_w_q[layer], mem_q_norm_g[layer], mem_w_o[layer]).astype(xs.dtype)
        xp = xp + sq_relu_mlp(rmsnorm(xp, norm_mlp_g[layer]), mlp_w1[layer], mlp_w2[layer]).astype(xp.dtype)
        xs = xs + sq_relu_mlp(rmsnorm(xs, norm_mlp_g[layer]), mlp_w1[layer], mlp_w2[layer]).astype(xs.dtype)
    new_ret_p, new_ret_s = jnp.stack(ret_p), jnp.stack(ret_s)
    new_dil_k_p, new_dil_v_p = jnp.stack(dk_p), jnp.stack(dv_p)
    new_dil_k_s, new_dil_v_s = jnp.stack(dk_s), jnp.stack(dv_s)
    new_cmp_k_p, new_cmp_v_p = jnp.stack(ck_p), jnp.stack(cv_p)
    new_cmp_k_s, new_cmp_v_s = jnp.stack(ck_s), jnp.stack(cv_s)
    new_slc_k_p, new_slc_v_p = jnp.stack(sk_p), jnp.stack(sv_p)
    new_slc_k_s, new_slc_v_s = jnp.stack(sk_s), jnp.stack(sv_s)
    new_win_k_p, new_win_v_p = jnp.stack(wk_p), jnp.stack(wv_p)
    new_win_k_s, new_win_v_s = jnp.stack(wk_s), jnp.stack(wv_s)
    new_lru_h_p, new_lru_h_s = jnp.stack(lh_p), jnp.stack(lh_s)
    new_lru_conv_p, new_lru_conv_s = jnp.stack(lc_p), jnp.stack(lc_s)
    new_mem_k_p, new_mem_v_p = jnp.stack(mk_p), jnp.stack(mv_p)
    return (xp, xs, new_ret_p, new_ret_s,
            new_dil_k_p, new_dil_v_p, new_dil_k_s, new_dil_v_s,
            new_cmp_k_p, new_cmp_v_p, new_cmp_k_s, new_cmp_v_s,
            new_slc_k_p, new_slc_v_p, new_slc_k_s, new_slc_v_s,
            new_win_k_p, new_win_v_p, new_win_k_s, new_win_v_s,
            new_lru_h_p, new_lru_h_s, new_lru_conv_p, new_lru_conv_s,
            new_mem_k_p, new_mem_v_p)
```

```python
import functools
import math

import jax
import jax.numpy as jnp
from jax import lax
from jax.experimental import pallas as pl
from jax.experimental.pallas import tpu as pltpu

F32 = jnp.float32
BF16 = jnp.bfloat16

HEAD_DIM = 128
ROPE_DIMS = HEAD_DIM // 4
ROPE_THETA = 500000.0
NORM_EPS = 1e-6
H_RET = 8
RET_CHUNK = 128
RET_THETA = 10000.0
H_DIL = 8
DIL_PATTERNS = ((128, 1), (512, 4), (2048, 16))
H_NSA = 8
G_NSA = 2
HPG = H_NSA // G_NSA
NSA_BLOCK = 64
NSA_TOPK = 16
NSA_WINDOW = 512
D_RNN = 1024
RNN_BLOCKS = 8
CONV_W = 4
LRU_C = 8.0
H_MEM = 4
PAGE_SIZE = 128

LANES = 128
SUBLANES = 8
KEY_BLOCK = 128
SAMPLE_ROWS = 8
VMEM_LIMIT = 48 * 1024 * 1024
NEG = -1e30
SCALE = HEAD_DIM ** -0.5


def _cp(*sem):
    return pltpu.CompilerParams(dimension_semantics=sem, vmem_limit_bytes=VMEM_LIMIT)


def _tile(n, pref):
    t = min(n, pref)
    while n % t:
        t -= SUBLANES
    return t


def _rms(x, g):
    return x * lax.rsqrt(jnp.mean(x * x, axis=-1, keepdims=True) + NORM_EPS) * g


def _rope(x, c, sa, sb, half):
    return x * c + pltpu.roll(x, LANES - half, 1) * sa + pltpu.roll(x, half, 1) * sb


def _dot(a, b):
    return jnp.dot(a, b, preferred_element_type=F32)


def _dot_nt(a, b):
    return lax.dot_general(a, b, (((1,), (1,)), ((), ())), preferred_element_type=F32)


def _pad_rows(x, rows):
    if x.shape[0] == rows:
        return x
    return jnp.concatenate([x, jnp.zeros((rows - x.shape[0], x.shape[1]), x.dtype)], axis=0)


def _softmax_step(s, w, vb, m, l, acc):
    s = jnp.where(w > 0, s, NEG)
    m_new = jnp.maximum(m, jnp.max(s, axis=-1, keepdims=True))
    alpha = jnp.exp(m - m_new)
    p = jnp.exp(s - m_new) * w
    l = alpha * l + jnp.sum(p, axis=-1, keepdims=True)
    acc = alpha * acc + _dot(p.astype(BF16), vb)
    return m_new, l, acc


def _softmax_init(rows):
    return (jnp.full((rows, 1), NEG, F32), jnp.zeros((rows, 1), F32), jnp.zeros((rows, HEAD_DIM), F32))


def _softmax_done(l, acc):
    return acc / jnp.maximum(l, 1e-30)


def _dil_weight(d):
    w = jnp.zeros(d.shape, F32)
    for window, dil in DIL_PATTERNS:
        hit = jnp.where(d <= window, 1.0, 0.0)
        if dil > 1:
            rem = (d & (dil - 1)) if dil & (dil - 1) == 0 else lax.rem(d, dil)
            hit = jnp.where(rem == 0, hit, 0.0)
        w = w + hit
    return jnp.where(d >= 0, w, 0.0)


def _iota(shape, dim):
    return lax.broadcasted_iota(jnp.int32, shape, dim)


def _rms_matmul_kernel(x_ref, g_ref, w_ref, o_ref, xn_ref):
    @pl.when(pl.program_id(1) == 0)
    def _():
        xn_ref[...] = _rms(x_ref[...], g_ref[...]).astype(BF16)

    o_ref[...] = _dot(xn_ref[...], w_ref[...])


def rms_matmul(x, g, w, tn):
    m, d = x.shape
    n = w.shape[1]
    tm = _tile(m, 512)
    return pl.pallas_call(
        _rms_matmul_kernel,
        grid=(m // tm, n // tn),
        in_specs=[pl.BlockSpec((tm, d), lambda i, j: (i, 0)),
                  pl.BlockSpec((1, d), lambda i, j: (0, 0)),
                  pl.BlockSpec((d, tn), lambda i, j: (0, j))],
        out_specs=pl.BlockSpec((tm, tn), lambda i, j: (i, j)),
        out_shape=jax.ShapeDtypeStruct((m, n), F32),
        scratch_shapes=[pltpu.VMEM((tm, d), BF16)],
        compiler_params=_cp("parallel", "arbitrary"),
        name="rms_matmul",
    )(x, g.reshape(1, d), w)


def _matmul_res_kernel(a1_ref, a2_ref, w1_ref, w2_ref, r_ref, o_ref):
    o_ref[...] = r_ref[...] + _dot(a1_ref[...], w1_ref[...]) + _dot(a2_ref[...], w2_ref[...])


def matmul_res(a1, a2, w1, w2, res):
    m, k1 = a1.shape
    k2 = a2.shape[1]
    n = w1.shape[1]
    tm = _tile(m, 512)
    tn = _tile(n, 1024)
    return pl.pallas_call(
        _matmul_res_kernel,
        grid=(m // tm, n // tn),
        in_specs=[pl.BlockSpec((tm, k1), lambda i, j: (i, 0)),
                  pl.BlockSpec((tm, k2), lambda i, j: (i, 0)),
                  pl.BlockSpec((k1, tn), lambda i, j: (0, j)),
                  pl.BlockSpec((k2, tn), lambda i, j: (0, j)),
                  pl.BlockSpec((tm, tn), lambda i, j: (i, j))],
        out_specs=pl.BlockSpec((tm, tn), lambda i, j: (i, j)),
        out_shape=jax.ShapeDtypeStruct((m, n), F32),
        compiler_params=_cp("parallel", "parallel"),
        name="matmul_res",
    )(a1, a2, w1, w2, res)


def _mlp_kernel(x_ref, g_ref, w1_ref, w2_ref, o_ref, xn_ref, acc_ref):
    f = pl.program_id(1)

    @pl.when(f == 0)
    def _():
        xn_ref[...] = _rms(x_ref[...], g_ref[...]).astype(BF16)
        acc_ref[...] = jnp.zeros_like(acc_ref)

    hid = jnp.maximum(_dot(xn_ref[...], w1_ref[...]), 0.0)
    acc_ref[...] += _dot((hid * hid).astype(BF16), w2_ref[...])

    @pl.when(f == pl.num_programs(1) - 1)
    def _():
        o_ref[...] = x_ref[...] + acc_ref[...]


def mlp_block(x, g, w1, w2):
    m, d = x.shape
    ff = w1.shape[1]
    tm = _tile(m, 512)
    tf = _tile(ff, 512)
    return pl.pallas_call(
        _mlp_kernel,
        grid=(m // tm, ff // tf),
        in_specs=[pl.BlockSpec((tm, d), lambda i, f: (i, 0)),
                  pl.BlockSpec((1, d), lambda i, f: (0, 0)),
                  pl.BlockSpec((d, tf), lambda i, f: (0, f)),
                  pl.BlockSpec((tf, d), lambda i, f: (f, 0))],
        out_specs=pl.BlockSpec((tm, d), lambda i, f: (i, 0)),
        out_shape=jax.ShapeDtypeStruct((m, d), F32),
        scratch_shapes=[pltpu.VMEM((tm, d), BF16), pltpu.VMEM((tm, d), F32)],
        compiler_params=_cp("parallel", "arbitrary"),
        name="mlp_block",
    )(x, g.reshape(1, d), w1, w2)


def _mem_attn_kernel(x_ref, g_ref, wq_ref, qg_ref, k_ref, v_ref, wo_ref, o_ref, oh_ref, *, nb, seg):
    x = x_ref[...]
    q = _dot(_rms(x, g_ref[...]).astype(BF16), wq_ref[...])
    qg = qg_ref[...]
    for b in range(nb):
        for h in range(H_MEM):
            cols = slice(h * HEAD_DIM, (h + 1) * HEAD_DIM)
            rows = slice(b * seg, (b + 1) * seg)
            qh = (_rms(q[rows, cols], qg) * SCALE).astype(BF16)
            s = _dot_nt(qh, k_ref[b][:, cols].astype(BF16))
            p = jnp.exp(s - jnp.max(s, axis=-1, keepdims=True))
            o = _dot(p.astype(BF16), v_ref[b][:, cols].astype(BF16)) / jnp.sum(p, axis=-1, keepdims=True)
            oh_ref[rows, cols] = o.astype(BF16)
    o_ref[...] = x + _dot(oh_ref[...], wo_ref[...])


def mem_attn_block(x, g, wq, qg, k, v, wo, rows_per_seq):
    m, d = x.shape
    n_mem, dm = k.shape[1], k.shape[2]
    if rows_per_seq >= 128:
        tm = _tile(rows_per_seq, 512)
        nb, seg = 1, tm
        kv_map = lambda i: (i // (rows_per_seq // tm), 0, 0)
    else:
        nb = _tile(m // rows_per_seq, 8)
        tm, seg = nb * rows_per_seq, rows_per_seq
        kv_map = lambda i: (i, 0, 0)
    return pl.pallas_call(
        functools.partial(_mem_attn_kernel, nb=nb, seg=seg),
        grid=(m // tm,),
        in_specs=[pl.BlockSpec((tm, d), lambda i: (i, 0)),
                  pl.BlockSpec((1, d), lambda i: (0, 0)),
                  pl.BlockSpec((d, dm), lambda i: (0, 0)),
                  pl.BlockSpec((1, HEAD_DIM), lambda i: (0, 0)),
                  pl.BlockSpec((nb, n_mem, dm), kv_map),
                  pl.BlockSpec((nb, n_mem, dm), kv_map),
                  pl.BlockSpec((dm, d), lambda i: (0, 0))],
        out_specs=pl.BlockSpec((tm, d), lambda i: (i, 0)),
        out_shape=jax.ShapeDtypeStruct((m, d), F32),
        scratch_shapes=[pltpu.VMEM((tm, dm), BF16)],
        compiler_params=_cp("parallel"),
        name="mem_attn_block",
    )(x, g.reshape(1, d), wq, qg.reshape(1, HEAD_DIM), k, v, wo)


def _kv_transform_kernel(*refs, modes, heads):
    ns = len(modes)
    x_refs = refs[:ns]
    g_ref, c_ref, sa_ref, sb_ref = refs[ns:ns + 4]
    o_refs = refs[ns + 4:]
    for s in range(ns):
        for h in range(heads):
            cols = slice(h * HEAD_DIM, (h + 1) * HEAD_DIM)
            x = x_refs[s][:, cols]
            if modes[s] != "copy":
                x = _rms(x, g_ref[s:s + 1, :])
            if modes[s] == "norm_rope":
                x = _rope(x, c_ref[...], sa_ref[...], sb_ref[...], ROPE_DIMS // 2)
            o_refs[s][:, cols] = x


def kv_transform(proj, sections, gains, tables, heads, pos_period):
    m = proj.shape[0]
    width = heads * HEAD_DIM
    tr = _tile(pos_period, 512)
    nper = pos_period // tr
    modes = tuple(mode for _, mode in sections)
    in_specs = [pl.BlockSpec((tr, width), functools.partial(lambda i, cb: (i, cb), cb=cb)) for cb, _ in sections]
    in_specs.append(pl.BlockSpec(gains.shape, lambda i: (0, 0)))
    in_specs += [pl.BlockSpec((tr, HEAD_DIM), lambda i: (i % nper, 0))] * 3
    outs = pl.pallas_call(
        functools.partial(_kv_transform_kernel, modes=modes, heads=heads),
        grid=(m // tr,),
        in_specs=in_specs,
        out_specs=[pl.BlockSpec((tr, width), lambda i: (i, 0))] * len(sections),
        out_shape=[jax.ShapeDtypeStruct((m, width), F32)] * len(sections),
        compiler_params=_cp("parallel"),
        name="kv_transform",
    )(*([proj] * len(sections)), gains, *tables)
    return outs


def _retention_kernel(q_ref, k_ref, v_ref, rg_ref, s0_ref, lg_ref, gn_ref, c_ref, sa_ref, sb_ref,
                      y_ref, s_ref, st_ref, *, rows, c_true):
    ci = pl.program_id(2)
    cc = RET_CHUNK

    @pl.when(ci == 0)
    def _():
        st_ref[...] = s0_ref[0, 0]

    lg = lg_ref[0]
    c, sa, sb = c_ref[...], sa_ref[...], sb_ref[...]
    q = _pad_rows(_rope(q_ref[0], c, sa, sb, HEAD_DIM // 2) * SCALE, cc)
    k = _pad_rows(_rope(k_ref[0], c, sa, sb, HEAD_DIM // 2), cc)
    vb = _pad_rows(v_ref[0], cc).astype(BF16)
    qb = q.astype(BF16)
    ii = _iota((cc, 1), 0).astype(F32)
    diff = ii - _iota((1, cc), 1).astype(F32)
    decay = jnp.where(diff >= 0, jnp.exp(jnp.maximum(diff, 0.0) * lg), 0.0)
    inner = _dot_nt(qb, k.astype(BF16)) * decay
    st = st_ref[...]
    o = _dot(inner.astype(BF16), vb) + _dot(qb, st.astype(BF16)) * jnp.exp((ii + 1.0) * lg)
    kd = jnp.where(ii < c_true, k * jnp.exp((c_true - 1.0 - ii) * lg), 0.0)
    st_new = jnp.exp(c_true * lg) * st + _dot(kd.T.astype(BF16), vb)
    st_ref[...] = st_new

    @pl.when(ci == pl.num_programs(2) - 1)
    def _():
        s_ref[0, 0] = st_new

    o = o[:rows]
    mu = jnp.mean(o, axis=-1, keepdims=True)
    var = jnp.mean(jnp.square(o - mu), axis=-1, keepdims=True)
    y = (o - mu) * lax.rsqrt(var + NORM_EPS) * gn_ref[0]
    rg = rg_ref[0]
    y_ref[0] = (y * (rg * jax.nn.sigmoid(rg))).astype(BF16)


def retention(proj3, s0, log_g, gn_g, tables, c_true):
    n, t, _ = proj3.shape
    rows = min(t, RET_CHUNK)
    nc = t // rows
    h = H_RET
    blk = lambda off: pl.BlockSpec((1, rows, HEAD_DIM), functools.partial(lambda b, hh, c, off: (b, c, off + hh), off=off))
    tab = pl.BlockSpec((rows, HEAD_DIM), lambda b, hh, c: (c, 0))
    y, s_new = pl.pallas_call(
        functools.partial(_retention_kernel, rows=rows, c_true=float(c_true)),
        grid=(n, h, nc),
        in_specs=[blk(0), blk(h), blk(2 * h), blk(3 * h),
                  pl.BlockSpec((1, 1, HEAD_DIM, HEAD_DIM), lambda b, hh, c: (b, hh, 0, 0)),
                  pl.BlockSpec((1, 1, HEAD_DIM), lambda b, hh, c: (hh, 0, 0)),
                  pl.BlockSpec((1, 1, HEAD_DIM), lambda b, hh, c: (hh, 0, 0)),
                  tab, tab, tab],
        out_specs=[pl.BlockSpec((1, rows, HEAD_DIM), lambda b, hh, c: (b, c, hh)),
                   pl.BlockSpec((1, 1, HEAD_DIM, HEAD_DIM), lambda b, hh, c: (b, hh, 0, 0))],
        out_shape=[jax.ShapeDtypeStruct((n, t, h * HEAD_DIM), BF16),
                   jax.ShapeDtypeStruct((n, h, HEAD_DIM, HEAD_DIM), F32)],
        scratch_shapes=[pltpu.VMEM((HEAD_DIM, HEAD_DIM), F32)],
        compiler_params=_cp("parallel", "parallel", "arbitrary"),
        name="retention",
    )(proj3, proj3, proj3, proj3, s0,
      jnp.broadcast_to(log_g[:, None, None], (h, 1, HEAD_DIM)), gn_g.reshape(h, 1, HEAD_DIM), *tables)
    return y, s_new


def _dil_q(q_ref, qg_ref, c_ref, sa_ref, sb_ref):
    q = _rope(_rms(q_ref[0], qg_ref[...]), c_ref[...], sa_ref[...], sb_ref[...], ROPE_DIMS // 2)
    return (q * SCALE).astype(BF16)


def _dil_prompt_kernel(q_ref, k_ref, v_ref, qg_ref, c_ref, sa_ref, sb_ref, o_ref, *, tq):
    qi = pl.program_id(2)
    qb = _dil_q(q_ref, qg_ref, c_ref, sa_ref, sb_ref)
    dbase = _iota((tq, KEY_BLOCK), 0) - _iota((tq, KEY_BLOCK), 1)

    def body(kj, carry):
        off = pl.multiple_of(kj * KEY_BLOCK, KEY_BLOCK)
        kb = k_ref[0, pl.ds(off, KEY_BLOCK), :].astype(BF16)
        vb = v_ref[0, pl.ds(off, KEY_BLOCK), :].astype(BF16)
        w = _dil_weight(dbase + (qi * tq - kj * KEY_BLOCK))
        return _softmax_step(_dot_nt(qb, kb), w, vb, *carry)

    _, l, acc = lax.fori_loop(0, qi + 1, body, _softmax_init(tq))
    o_ref[0] = _softmax_done(l, acc).astype(BF16)


def dilated_prompt(proj3, q_cb, k3, v3, qg, tables):
    n, t, _ = proj3.shape
    tq = KEY_BLOCK
    tab = pl.BlockSpec((tq, HEAD_DIM), lambda b, h, i: (i, 0))
    kv = pl.BlockSpec((1, t, HEAD_DIM), lambda b, h, i: (b, 0, h))
    return pl.pallas_call(
        functools.partial(_dil_prompt_kernel, tq=tq),
        grid=(n, H_DIL, t // tq),
        in_specs=[pl.BlockSpec((1, tq, HEAD_DIM), lambda b, h, i: (b, i, q_cb + h)), kv, kv,
                  pl.BlockSpec((1, HEAD_DIM), lambda b, h, i: (0, 0)), tab, tab, tab],
        out_specs=pl.BlockSpec((1, tq, HEAD_DIM), lambda b, h, i: (b, i, h)),
        out_shape=jax.ShapeDtypeStruct((n, t, H_DIL * HEAD_DIM), BF16),
        compiler_params=_cp("parallel", "parallel", "arbitrary"),
        name="dilated_prompt",
    )(proj3, k3, v3, qg.reshape(1, HEAD_DIM), *tables)


def _dil_sample_kernel(q_ref, kc_ref, vc_ref, kn_ref, vn_ref, qg_ref, c_ref, sa_ref, sb_ref, o_ref, *, lc):
    rows = q_ref.shape[1]
    qb = _dil_q(q_ref, qg_ref, c_ref, sa_ref, sb_ref)
    tq = _iota((rows, KEY_BLOCK), 0)
    lane = _iota((rows, KEY_BLOCK), 1)

    def body(cb, carry):
        off = pl.multiple_of(cb * KEY_BLOCK, KEY_BLOCK)
        kb = kc_ref[0, pl.ds(off, KEY_BLOCK), :].astype(BF16)
        vb = vc_ref[0, pl.ds(off, KEY_BLOCK), :].astype(BF16)
        w = _dil_weight(lc + tq - (off + lane))
        return _softmax_step(_dot_nt(qb, kb), w, vb, *carry)

    carry = lax.fori_loop(0, lc // KEY_BLOCK, body, _softmax_init(rows))
    kb = _pad_rows(kn_ref[0], KEY_BLOCK).astype(BF16)
    vb = _pad_rows(vn_ref[0], KEY_BLOCK).astype(BF16)
    _, l, acc = _softmax_step(_dot_nt(qb, kb), _dil_weight(tq - lane), vb, *carry)
    o_ref[0] = _softmax_done(l, acc).astype(BF16)


def dilated_sample(proj3, q_cb, kc3, vc3, kn3, vn3, qg, tables):
    n, rows, _ = proj3.shape
    lc = kc3.shape[1]
    tab = pl.BlockSpec((rows, HEAD_DIM), lambda b, h: (0, 0))
    cache = pl.BlockSpec((1, lc, HEAD_DIM), lambda b, h: (b, 0, h))
    new = pl.BlockSpec((1, rows, HEAD_DIM), lambda b, h: (b, 0, h))
    return pl.pallas_call(
        functools.partial(_dil_sample_kernel, lc=lc),
        grid=(n, H_DIL),
        in_specs=[pl.BlockSpec((1, rows, HEAD_DIM), lambda b, h: (b, 0, q_cb + h)), cache, cache, new, new,
                  pl.BlockSpec((1, HEAD_DIM), lambda b, h: (0, 0)), tab, tab, tab],
        out_specs=pl.BlockSpec((1, rows, HEAD_DIM), lambda b, h: (b, 0, h)),
        out_shape=jax.ShapeDtypeStruct((n, rows, H_DIL * HEAD_DIM), BF16),
        compiler_params=_cp("parallel", "parallel"),
        name="dilated_sample",
    )(proj3, kc3, vc3, kn3, vn3, qg.reshape(1, HEAD_DIM), *tables)


def _nsa_q(q_ref, g, qg_ref, c_ref, sa_ref, sb_ref):
    qs = []
    for hl in range(HPG):
        col = (g * HPG + hl) * HEAD_DIM
        x = _rms(q_ref[0][:, col:col + HEAD_DIM], qg_ref[...])
        qs.append(_rope(x, c_ref[...], sa_ref[...], sb_ref[...], ROPE_DIMS // 2) * SCALE)
    return jnp.concatenate(qs, axis=0).astype(BF16)


def _nsa_compressed(q4, kc, vc, qpos, nb):
    rows = qpos.shape[0]
    nbp = kc.shape[0]
    blk = _iota((rows, nbp), 1)
    cmask = jnp.where((blk + 1) * NSA_BLOCK - 1 <= qpos, 1.0, 0.0)
    cmask4 = jnp.concatenate([cmask] * HPG, axis=0)
    s = jnp.where(cmask4 > 0, _dot_nt(q4, kc.astype(BF16)), NEG)
    p = jnp.exp(s - jnp.max(s, axis=-1, keepdims=True)) * cmask4
    p = p / jnp.maximum(jnp.sum(p, axis=-1, keepdims=True), 1e-30)
    o_c = _dot(p.astype(BF16), vc.astype(BF16))
    imp = p[0:rows]
    for hl in range(1, HPG):
        imp = imp + p[hl * rows:(hl + 1) * rows]
    cur = jnp.right_shift(qpos, NSA_BLOCK.bit_length() - 1)
    cand = blk < cur
    imp = jnp.where(cand, imp, -jnp.inf)
    rank = jnp.zeros((rows, nbp), F32)
    for b in range(nb):
        col = imp[:, b:b + 1]
        tie = jnp.where(col == imp, jnp.where(blk > b, 1.0, 0.0), 0.0)
        rank = rank + jnp.where(col > imp, 1.0, tie)
    sel = jnp.where(cand, jnp.where(rank < min(NSA_TOPK, nb), 1.0, 0.0), 0.0)
    sel = jnp.where(blk == cur, 1.0, sel)
    return o_c, sel


def _nsa_combine(gates, g, o_c, o_s, o_w, rows):
    outs = []
    for hl in range(HPG):
        r = slice(hl * rows, (hl + 1) * rows)
        col = 3 * hl
        outs.append(gates[:, col:col + 1] * o_c[r] + gates[:, col + 1:col + 2] * o_s[r]
                    + gates[:, col + 2:col + 3] * o_w[r])
    return outs


def _nsa_prompt_kernel(q_ref, gt_ref, kc_ref, vc_ref, ks_ref, vs_ref, kw_ref, vw_ref, e_ref,
                       qg_ref, c_ref, sa_ref, sb_ref, o_ref, selx_ref, *, tq, nb):
    qi = pl.program_id(2)
    q4 = _nsa_q(q_ref, 0, qg_ref, c_ref, sa_ref, sb_ref)
    qpos = qi * tq + _iota((tq, 1), 0)
    o_c, sel = _nsa_compressed(q4, kc_ref[0], vc_ref[0], qpos, nb)
    selx_ref[...] = _dot(sel.astype(BF16), e_ref[...])
    lane = _iota((tq, KEY_BLOCK), 1)

    def sel_body(kj, carry):
        off = pl.multiple_of(kj * KEY_BLOCK, KEY_BLOCK)
        kb = ks_ref[0, pl.ds(off, KEY_BLOCK), :].astype(BF16)
        vb = vs_ref[0, pl.ds(off, KEY_BLOCK), :].astype(BF16)
        w = jnp.where(off + lane <= qpos, selx_ref[:, pl.ds(off, KEY_BLOCK)], 0.0)
        return _softmax_step(_dot_nt(q4, kb), jnp.concatenate([w] * HPG, axis=0), vb, *carry)

    _, l, acc = lax.fori_loop(0, qi + 1, sel_body, _softmax_init(HPG * tq))
    o_s = _softmax_done(l, acc)

    def win_body(kj, carry):
        off = pl.multiple_of(kj * KEY_BLOCK, KEY_BLOCK)
        kb = kw_ref[0, pl.ds(off, KEY_BLOCK), :].astype(BF16)
        vb = vw_ref[0, pl.ds(off, KEY_BLOCK), :].astype(BF16)
        d = qpos - (off + lane)
        w = jnp.where(d >= 0, jnp.where(d < NSA_WINDOW, 1.0, 0.0), 0.0)
        return _softmax_step(_dot_nt(q4, kb), jnp.concatenate([w] * HPG, axis=0), vb, *carry)

    lo = jnp.maximum(qi - NSA_WINDOW // KEY_BLOCK, 0)
    _, l, acc = lax.fori_loop(lo, qi + 1, win_body, _softmax_init(HPG * tq))
    o_w = _softmax_done(l, acc)
    gates = jax.nn.sigmoid(gt_ref[0])
    for hl, o in enumerate(_nsa_combine(gates, 0, o_c, o_s, o_w, tq)):
        o_ref[0, :, hl * HEAD_DIM:(hl + 1) * HEAD_DIM] = o.astype(BF16)


def nsa_prompt(proj3, gate_cb, kc3, vc3, ks3, vs3, kw3, vw3, expand, qg, tables, nb):
    n, t, _ = proj3.shape
    tq = KEY_BLOCK
    nbp = kc3.shape[1]
    gw = HPG * HEAD_DIM
    tab = pl.BlockSpec((tq, HEAD_DIM), lambda b, g, i: (i, 0))
    cmp_spec = pl.BlockSpec((1, nbp, HEAD_DIM), lambda b, g, i: (b, 0, g))
    kv = pl.BlockSpec((1, t, HEAD_DIM), lambda b, g, i: (b, 0, g))
    return pl.pallas_call(
        functools.partial(_nsa_prompt_kernel, tq=tq, nb=nb),
        grid=(n, G_NSA, t // tq),
        in_specs=[pl.BlockSpec((1, tq, gw), lambda b, g, i: (b, i, g)),
                  pl.BlockSpec((1, tq, HEAD_DIM), lambda b, g, i: (b, i, gate_cb + g)),
                  cmp_spec, cmp_spec, kv, kv, kv, kv,
                  pl.BlockSpec((nbp, t), lambda b, g, i: (0, 0)),
                  pl.BlockSpec((1, HEAD_DIM), lambda b, g, i: (0, 0)), tab, tab, tab],
        out_specs=pl.BlockSpec((1, tq, gw), lambda b, g, i: (b, i, g)),
        out_shape=jax.ShapeDtypeStruct((n, t, H_NSA * HEAD_DIM), BF16),
        scratch_shapes=[pltpu.VMEM((tq, t), F32)],
        compiler_params=_cp("parallel", "parallel", "arbitrary"),
        name="nsa_prompt",
    )(proj3, proj3, kc3, vc3, ks3, vs3, kw3, vw3, expand, qg.reshape(1, HEAD_DIM), *tables)


def _compress_rows_kernel(k_ref, v_ref, pk_ref, pv_ref, ko_ref, vo_ref):
    for x_ref, p_ref, o_ref in ((k_ref, pk_ref, ko_ref), (v_ref, pv_ref, vo_ref)):
        x = x_ref[0]
        nblk = x.shape[0] // NSA_BLOCK
        pw = jnp.concatenate([p_ref[...]] * nblk, axis=0)
        o_ref[0] = jnp.sum((x * pw).reshape(nblk, NSA_BLOCK, x.shape[1]), axis=1)


def compress_rows(k3, v3, pwk2, pwv2):
    n, t, w = k3.shape
    tr = _tile(t, 512)
    row = pl.BlockSpec((1, tr, w), lambda b, i: (b, i, 0))
    pw = pl.BlockSpec((NSA_BLOCK, w), lambda b, i: (0, 0))
    out = pl.BlockSpec((1, tr // NSA_BLOCK, w), lambda b, i: (b, i, 0))
    return pl.pallas_call(
        _compress_rows_kernel,
        grid=(n, t // tr),
        in_specs=[row, row, pw, pw],
        out_specs=[out, out],
        out_shape=[jax.ShapeDtypeStruct((n, t // NSA_BLOCK, w), F32)] * 2,
        compiler_params=_cp("parallel", "parallel"),
        name="compress_rows",
    )(k3, v3, pwk2, pwv2)


def _compress_pages_kernel(pt_ref, *refs, pps):
    k_refs, v_refs = refs[:pps], refs[pps:2 * pps]
    pk_ref, pv_ref, ko_ref, vo_ref = refs[2 * pps:]
    bpp = PAGE_SIZE // NSA_BLOCK
    for x_refs, p_ref, o_ref in ((k_refs, pk_ref, ko_ref), (v_refs, pv_ref, vo_ref)):
        pw = jnp.concatenate([p_ref[...]] * bpp, axis=0)
        outs = [jnp.sum((x_ref[0] * pw).reshape(bpp, NSA_BLOCK, pw.shape[1]), axis=1) for x_ref in x_refs]
        o_ref[0] = jnp.concatenate(outs, axis=0)


def compress_pages(pool_k, pool_v, page_table, pwk2, pwv2, pps):
    n, n_pages = page_table.shape
    w = pool_k.shape[2]
    bpp = PAGE_SIZE // NSA_BLOCK
    page = lambda i: pl.BlockSpec((1, PAGE_SIZE, w), functools.partial(lambda b, c, pt, i: (pt[b, c * pps + i], 0, 0), i=i))
    pw = pl.BlockSpec((NSA_BLOCK, w), lambda b, c, pt: (0, 0))
    out = pl.BlockSpec((1, pps * bpp, w), lambda b, c, pt: (b, c, 0))
    return pl.pallas_call(
        functools.partial(_compress_pages_kernel, pps=pps),
        grid_spec=pltpu.PrefetchScalarGridSpec(
            num_scalar_prefetch=1, grid=(n, n_pages // pps),
            in_specs=[page(i) for i in range(pps)] * 2 + [pw, pw],
            out_specs=[out, out]),
        out_shape=[jax.ShapeDtypeStruct((n, n_pages * bpp, w), F32)] * 2,
        compiler_params=_cp("parallel", "parallel"),
        name="compress_pages",
    )(page_table, *([pool_k] * pps), *([pool_v] * pps), pwk2, pwv2)


def _nsa_sample_kernel(pt_ref, *refs, pps, past, rows, nb, lw):
    (q_ref, gt_ref, kc_ref, vc_ref), refs = refs[:4], refs[4:]
    kp_refs, vp_refs, refs = refs[:pps], refs[pps:2 * pps], refs[2 * pps:]
    (kn_ref, vn_ref, kwc_ref, vwc_ref, kwn_ref, vwn_ref, e_ref, qg_ref, c_ref, sa_ref, sb_ref,
     o_ref, q4_ref, sel_ref, oc_ref, m_ref, l_ref, acc_ref) = refs
    pc = pl.program_id(1)
    r4 = HPG * rows
    qpos = past + _iota((rows, 1), 0)
    lane = _iota((rows, KEY_BLOCK), 1)

    @pl.when(pc == 0)
    def _():
        for g in range(G_NSA):
            cols = slice(g * HEAD_DIM, (g + 1) * HEAD_DIM)
            q4 = _nsa_q(q_ref, g, qg_ref, c_ref, sa_ref, sb_ref)
            o_c, sel = _nsa_compressed(q4, kc_ref[0][:, cols], vc_ref[0][:, cols], qpos, nb)
            q4_ref[g] = q4
            sel_ref[g] = sel
            oc_ref[g] = o_c
            m0, l0, a0 = _softmax_init(r4)
            m_ref[g], l_ref[g], acc_ref[g] = m0, l0, a0

    for g in range(G_NSA):
        cols = slice(g * HEAD_DIM, (g + 1) * HEAD_DIM)
        q4 = q4_ref[g]
        selx = _dot(sel_ref[g].astype(BF16), e_ref[...])
        carry = (m_ref[g], l_ref[g], acc_ref[g])
        for i in range(pps):
            kb = kp_refs[i][0][:, cols].astype(BF16)
            vb = vp_refs[i][0][:, cols].astype(BF16)
            kpos = (pc * pps + i) * PAGE_SIZE + lane
            w = jnp.where(kpos <= qpos, selx[:, i * PAGE_SIZE:(i + 1) * PAGE_SIZE], 0.0)
            carry = _softmax_step(_dot_nt(q4, kb), jnp.concatenate([w] * HPG, axis=0), vb, *carry)
        m_ref[g], l_ref[g], acc_ref[g] = carry

    @pl.when(pc == pl.num_programs(1) - 1)
    def _():
        gates_all = jax.nn.sigmoid(gt_ref[0])
        tq = _iota((rows, KEY_BLOCK), 0)
        for g in range(G_NSA):
            cols = slice(g * HEAD_DIM, (g + 1) * HEAD_DIM)
            q4 = q4_ref[g]
            kb = _pad_rows(kn_ref[0][:, cols], KEY_BLOCK).astype(BF16)
            vb = _pad_rows(vn_ref[0][:, cols], KEY_BLOCK).astype(BF16)
            w = jnp.where(lane <= tq, 1.0, 0.0)
            _, l, acc = _softmax_step(_dot_nt(q4, kb), jnp.concatenate([w] * HPG, axis=0), vb,
                                      m_ref[g], l_ref[g], acc_ref[g])
            o_s = _softmax_done(l, acc)
            carry = _softmax_init(r4)
            for cb in range(lw // KEY_BLOCK):
                rs = slice(cb * KEY_BLOCK, (cb + 1) * KEY_BLOCK)
                kb = kwc_ref[0][rs, cols].astype(BF16)
                vb = vwc_ref[0][rs, cols].astype(BF16)
                d = tq + lw - (cb * KEY_BLOCK + lane)
                w = jnp.where(d < NSA_WINDOW, jnp.where(past - d + tq >= 0, 1.0, 0.0), 0.0)
                carry = _softmax_step(_dot_nt(q4, kb), jnp.concatenate([w] * HPG, axis=0), vb, *carry)
            kb = _pad_rows(kwn_ref[0][:, cols], KEY_BLOCK).astype(BF16)
            vb = _pad_rows(vwn_ref[0][:, cols], KEY_BLOCK).astype(BF16)
            w = jnp.where(lane <= tq, 1.0, 0.0)
            _, l, acc = _softmax_step(_dot_nt(q4, kb), jnp.concatenate([w] * HPG, axis=0), vb, *carry)
            o_w = _softmax_done(l, acc)
            gates = gates_all[:, g * HEAD_DIM:(g + 1) * HEAD_DIM]
            for hl, o in enumerate(_nsa_combine(gates, g, oc_ref[g], o_s, o_w, rows)):
                col = (g * HPG + hl) * HEAD_DIM
                o_ref[0, :, col:col + HEAD_DIM] = o.astype(BF16)


def nsa_sample(proj3, gate_cb, kc3, vc3, pool_k, pool_v, page_table, kn3, vn3, kwc3, vwc3, kwn3, vwn3,
               expand, qg, tables, pps):
    n, rows, _ = proj3.shape
    n_pages = page_table.shape[1]
    past = n_pages * PAGE_SIZE
    nbp = kc3.shape[1]
    w = G_NSA * HEAD_DIM
    lw = kwc3.shape[1]
    qw = H_NSA * HEAD_DIM
    r4 = HPG * rows
    fixed = lambda shape: pl.BlockSpec(shape, lambda b, c, pt: (0,) * len(shape))
    per_seq = lambda r, width, cb=0: pl.BlockSpec((1, r, width), functools.partial(lambda b, c, pt, cb: (b, 0, cb), cb=cb))
    page = lambda i: pl.BlockSpec((1, PAGE_SIZE, w), functools.partial(lambda b, c, pt, i: (pt[b, c * pps + i], 0, 0), i=i))
    in_specs = ([per_seq(rows, qw), per_seq(rows, w, gate_cb // G_NSA), per_seq(nbp, w), per_seq(nbp, w)]
                + [page(i) for i in range(pps)] * 2
                + [per_seq(rows, w), per_seq(rows, w), per_seq(lw, w), per_seq(lw, w), per_seq(rows, w), per_seq(rows, w),
                   pl.BlockSpec((nbp, pps * PAGE_SIZE), lambda b, c, pt: (0, c)),
                   fixed((1, HEAD_DIM)), fixed((rows, HEAD_DIM)), fixed((rows, HEAD_DIM)), fixed((rows, HEAD_DIM))])
    return pl.pallas_call(
        functools.partial(_nsa_sample_kernel, pps=pps, past=past, rows=rows, nb=past // NSA_BLOCK, lw=lw),
        grid_spec=pltpu.PrefetchScalarGridSpec(
            num_scalar_prefetch=1, grid=(n, n_pages // pps),
            in_specs=in_specs,
            out_specs=pl.BlockSpec((1, rows, qw), lambda b, c, pt: (b, 0, 0)),
            scratch_shapes=[pltpu.VMEM((G_NSA, r4, HEAD_DIM), BF16), pltpu.VMEM((G_NSA, rows, nbp), F32),
                            pltpu.VMEM((G_NSA, r4, HEAD_DIM), F32), pltpu.VMEM((G_NSA, r4, 1), F32),
                            pltpu.VMEM((G_NSA, r4, 1), F32), pltpu.VMEM((G_NSA, r4, HEAD_DIM), F32)]),
        out_shape=jax.ShapeDtypeStruct((n, rows, qw), BF16),
        compiler_params=_cp("parallel", "arbitrary"),
        name="nsa_sample",
    )(page_table, proj3, proj3, kc3, vc3, *([pool_k] * pps), *([pool_v] * pps), kn3, vn3, kwc3, vwc3, kwn3, vwn3,
      expand, qg.reshape(1, HEAD_DIM), *tables)


def _lru_kernel(x_ref, y_ref, hist_ref, h0_ref, cw_ref, cb_ref, wa_ref, ba_ref, wi_ref, bi_ref, sp_ref,
                o_ref, tail_ref, a_ref, b_ref):
    t = x_ref.shape[1]
    x = x_ref[0]
    hist = hist_ref[0]
    row8 = _iota((SUBLANES, LANES), 0)
    xc = cb_ref[0] + x * cw_ref[0, CONV_W - 1:CONV_W, :]
    for s in range(1, CONV_W):
        sh = pltpu.roll(x, s, 0)
        head = jnp.where(row8 >= s, sh[:SUBLANES], pltpu.roll(hist, (SUBLANES - (CONV_W - 1) + s) % SUBLANES, 0))
        sh = head if t == SUBLANES else jnp.concatenate([head, sh[SUBLANES:]], axis=0)
        xc = xc + sh * cw_ref[0, CONV_W - 1 - s:CONV_W - s, :]
    xb = xc.astype(BF16)
    r = jax.nn.sigmoid(_dot(xb, wa_ref[0]) + ba_ref[0])
    gi = jax.nn.sigmoid(_dot(xb, wi_ref[0]) + bi_ref[0])
    log_a = -LRU_C * r * sp_ref[0]
    th = jnp.tanh(log_a)
    a_ref[...] = jnp.exp(log_a)
    b_ref[...] = jnp.sqrt(-2.0 * th / (1.0 - th)) * (gi * xc)

    def body(gidx, hc):
        off = pl.multiple_of(gidx * SUBLANES, SUBLANES)
        a = a_ref[pl.ds(off, SUBLANES), :]
        b = b_ref[pl.ds(off, SUBLANES), :]
        for s in (1, 2, 4):
            ok = row8 >= s
            b = jnp.where(ok, a * pltpu.roll(b, s, 0) + b, b)
            a = jnp.where(ok, a * pltpu.roll(a, s, 0), a)
        hrows = a * hc + b
        a_ref[pl.ds(off, SUBLANES), :] = hrows
        return jnp.broadcast_to(hrows[SUBLANES - 1:SUBLANES, :], (SUBLANES, LANES))

    lax.fori_loop(0, t // SUBLANES, body, jnp.broadcast_to(h0_ref[0], (SUBLANES, LANES)))
    h = a_ref[...]
    tail_ref[0] = h[t - SUBLANES:]
    o_ref[0] = (jax.nn.gelu(y_ref[0], approximate=True) * h).astype(BF16)


def rglru(proj3, x_cb, y_cb, hist8, h0, conv_w, conv_b, w_a, b_a, w_i, b_i, softplus_neg_lam):
    n, t, _ = proj3.shape
    bw = D_RNN // RNN_BLOCKS
    vec = lambda a: a.reshape(RNN_BLOCKS, 1, bw)
    vspec = pl.BlockSpec((1, 1, bw), lambda b, k: (k, 0, 0))
    mspec = pl.BlockSpec((1, bw, bw), lambda b, k: (k, 0, 0))
    return pl.pallas_call(
        _lru_kernel,
        grid=(n, RNN_BLOCKS),
        in_specs=[pl.BlockSpec((1, t, bw), lambda b, k: (b, 0, x_cb + k)),
                  pl.BlockSpec((1, t, bw), lambda b, k: (b, 0, y_cb + k)),
                  pl.BlockSpec((1, SUBLANES, bw), lambda b, k: (b, 0, k)),
                  pl.BlockSpec((1, 1, bw), lambda b, k: (b, 0, k)),
                  pl.BlockSpec((1, CONV_W, bw), lambda b, k: (k, 0, 0)),
                  vspec, mspec, vspec, mspec, vspec, vspec],
        out_specs=[pl.BlockSpec((1, t, bw), lambda b, k: (b, 0, k)),
                   pl.BlockSpec((1, SUBLANES, bw), lambda b, k: (b, 0, k))],
        out_shape=[jax.ShapeDtypeStruct((n, t, D_RNN), BF16), jax.ShapeDtypeStruct((n, SUBLANES, D_RNN), F32)],
        scratch_shapes=[pltpu.VMEM((t, bw), F32), pltpu.VMEM((t, bw), F32)],
        compiler_params=_cp("parallel", "parallel"),
        name="rglru",
    )(proj3, proj3, hist8, h0.reshape(n, 1, D_RNN),
      conv_w.reshape(CONV_W, RNN_BLOCKS, bw).transpose(1, 0, 2), vec(conv_b),
      w_a.astype(BF16), vec(b_a), w_i.astype(BF16), vec(b_i), vec(softplus_neg_lam))


def _rope_tables(pos, n_rot, theta):
    half = n_rot // 2
    inv = 1.0 / (theta ** (jnp.arange(half, dtype=F32) * (2.0 / n_rot)))
    ang = pos.astype(F32)[:, None] * inv[None, :]
    cos, sin = jnp.cos(ang), jnp.sin(ang)
    p = pos.shape[0]
    rest = HEAD_DIM - n_rot
    zh = jnp.zeros((p, half), F32)
    c = jnp.concatenate([cos, cos, jnp.ones((p, rest), F32)], axis=1)
    sa = jnp.concatenate([-sin, zh, jnp.zeros((p, rest), F32)], axis=1)
    sb = jnp.concatenate([zh, sin, jnp.zeros((p, rest), F32)], axis=1)
    return c, sa, sb


def _block_expand(nbp, length):
    return (jnp.arange(length, dtype=jnp.int32)[None, :] // NSA_BLOCK
            == jnp.arange(nbp, dtype=jnp.int32)[:, None]).astype(BF16)


def _pad_axis(a, axis, size):
    pad = [(0, 0)] * a.ndim
    pad[axis] = (0, size - a.shape[axis])
    return jnp.pad(a, pad)


def _odd_w_in(w):
    qkv = H_NSA * HEAD_DIM + 6 * G_NSA * HEAD_DIM
    ng = 3 * H_NSA
    gate = w[:, qkv:qkv + ng]
    per = 3 * HPG
    gcols = [_pad_axis(gate[:, g * per:(g + 1) * per], 1, LANES) for g in range(G_NSA)]
    out = jnp.concatenate([w[:, :qkv], w[:, qkv + ng:]] + gcols, axis=1)
    return _pad_axis(out, 1, -(-out.shape[1] // 1024) * 1024)


def kernel(x_prompt, x_sample, mem_prompt, state_ret, cache_dil_k, cache_dil_v, cache_nsa_cmp_k, cache_nsa_cmp_v, cache_nsa_slc_k, cache_nsa_slc_v, cache_nsa_win_k, cache_nsa_win_v, state_lru_h, state_lru_conv, cache_mem_k, cache_mem_v, page_table, ev_w_in, ret_gn_g, dil_q_norm_g, dil_k_norm_g, ev_w_out, od_w_in, nsa_q_norm_g, nsa_k_norm_g, nsa_pw_k, nsa_pw_v, lru_conv_w, lru_conv_b, lru_w_a, lru_b_a, lru_w_i, lru_b_i, lru_lambda, od_w_out, norm_mix_g, norm_mem_g, norm_mlp_g, mem_norm_g, mem_w_q, mem_w_k, mem_w_v, mem_q_norm_g, mem_k_norm_g, mem_w_o, mlp_w1, mlp_w2):
    nb_p, t_p, d = x_prompt.shape
    nb_s, t_s, _ = x_sample.shape
    depth = norm_mix_g.shape[0]
    n_pages = page_table.shape[1]
    past = n_pages * PAGE_SIZE
    rs = SAMPLE_ROWS
    assert t_p % RET_CHUNK == 0 and t_s <= rs and past % PAGE_SIZE == 0 and t_s < NSA_BLOCK
    assert cache_nsa_win_k.shape[2] % KEY_BLOCK == 0 and cache_dil_k.shape[2] % KEY_BLOCK == 0
    hd = HEAD_DIM

    xp = x_prompt.reshape(nb_p * t_p, d)
    xs = _pad_axis(x_sample, 1, rs).reshape(nb_s * rs, d)
    mem2 = mem_prompt.reshape(-1, d)
    n_mem = mem_prompt.shape[1]

    pos_p = jnp.arange(t_p, dtype=jnp.int32)
    pos_s = past + jnp.arange(rs, dtype=jnp.int32)
    tile_s = lambda tabs: tuple(jnp.tile(tb, (nb_s, 1)) for tb in tabs)
    rope_p, rope_s = _rope_tables(pos_p, ROPE_DIMS, ROPE_THETA), _rope_tables(pos_s, ROPE_DIMS, ROPE_THETA)
    ret_p, ret_s = _rope_tables(pos_p, hd, RET_THETA), _rope_tables(pos_s, hd, RET_THETA)
    log_g = jnp.log1p(-jnp.exp2(-5.0 - jnp.arange(H_RET, dtype=F32)))

    outs = {k: [] for k in ("ret_p", "ret_s", "dk_p", "dv_p", "dk_s", "dv_s", "ck_p", "cv_p", "ck_s", "cv_s",
                            "sk_p", "sv_p", "sk_s", "sv_s", "wk_p", "wv_p", "wk_s", "wv_s", "lh_p", "lh_s",
                            "lc_p", "lc_s", "mk_p", "mv_p")}
    new_rows = lambda a, w: a.reshape(nb_s, rs, w)[:, :t_s]

    for layer in range(depth):
        if layer % 2 == 0:
            e = layer // 2
            w_in = ev_w_in[e].astype(BF16)
            w_out = ev_w_out[e].astype(BF16)
            hw = H_RET * hd
            ones = jnp.ones((1, hd), F32)
            gains = jnp.concatenate([dil_k_norm_g[e][None], ones], axis=0)
            sections = [(5, "norm_rope"), (6, "copy")]
            proj = rms_matmul(xp, norm_mix_g[layer], w_in, 1024)
            proj3 = proj.reshape(nb_p, t_p, -1)
            y_ret, s_new = retention(proj3, jnp.zeros((nb_p, H_RET, hd, hd), F32), log_g, ret_gn_g[e], ret_p, RET_CHUNK)
            dk, dv = kv_transform(proj, sections, gains, rope_p, H_DIL, t_p)
            y_dil = dilated_prompt(proj3, 4 * H_RET, dk.reshape(nb_p, t_p, hw), dv.reshape(nb_p, t_p, hw),
                                   dil_q_norm_g[e], rope_p)
            xp = matmul_res(y_ret.reshape(-1, hw), y_dil.reshape(-1, hw), w_out[:hw], w_out[hw:], xp)
            keep = min(DIL_PATTERNS[-1][0], t_p)
            outs["ret_p"].append(s_new)
            outs["dk_p"].append(dk.reshape(nb_p, t_p, H_DIL, hd)[:, t_p - keep:])
            outs["dv_p"].append(dv.reshape(nb_p, t_p, H_DIL, hd)[:, t_p - keep:])
            proj = rms_matmul(xs, norm_mix_g[layer], w_in, 1024)
            proj3 = proj.reshape(nb_s, rs, -1)
            y_ret, s_new = retention(proj3, state_ret[e], log_g, ret_gn_g[e], ret_s, t_s)
            dk, dv = kv_transform(proj, sections, gains, tile_s(rope_s), H_DIL, nb_s * rs)
            lc = cache_dil_k.shape[2]
            y_dil = dilated_sample(proj3, 4 * H_RET, cache_dil_k[e].reshape(nb_s, lc, hw),
                                   cache_dil_v[e].reshape(nb_s, lc, hw), dk.reshape(nb_s, rs, hw),
                                   dv.reshape(nb_s, rs, hw), dil_q_norm_g[e], rope_s)
            xs = matmul_res(y_ret.reshape(-1, hw), y_dil.reshape(-1, hw), w_out[:hw], w_out[hw:], xs)
            outs["ret_s"].append(s_new)
            outs["dk_s"].append(new_rows(dk, hw).reshape(nb_s, t_s, H_DIL, hd))
            outs["dv_s"].append(new_rows(dv, hw).reshape(nb_s, t_s, H_DIL, hd))
        else:
            o = layer // 2
            w_in = _odd_w_in(od_w_in[o]).astype(BF16)
            w_out = od_w_out[o].astype(BF16)
            qw = H_NSA * hd
            gw = G_NSA * hd
            kv_cb = qw // gw
            x_cb = (qw + 6 * gw) // hd
            y_cb = x_cb + D_RNN // hd
            gate_cb = y_cb + D_RNN // hd
            ones = jnp.ones((1, hd), F32)
            kg = nsa_k_norm_g[o]
            gains = jnp.concatenate([kg[0:1], ones, kg[1:2], ones, kg[2:3], ones], axis=0)
            sections = [(kv_cb + i, "norm_rope" if i % 2 == 0 else "copy") for i in range(6)]
            pwk2 = jnp.concatenate([nsa_pw_k[o]] * G_NSA, axis=1)
            pwv2 = jnp.concatenate([nsa_pw_v[o]] * G_NSA, axis=1)
            sp = jax.nn.softplus(-lru_lambda[o].astype(F32))
            lru_w = (lru_conv_w[o], lru_conv_b[o], lru_w_a[o], lru_b_a[o], lru_w_i[o], lru_b_i[o], sp)
            proj = rms_matmul(xp, norm_mix_g[layer], w_in, 1024)
            proj3 = proj.reshape(nb_p, t_p, -1)
            kcm, vcm, ksl, vsl, kwn, vwn = kv_transform(proj, sections, gains, rope_p, G_NSA, t_p)
            r3 = lambda a: a.reshape(nb_p, t_p, gw)
            nb = t_p // NSA_BLOCK
            nbp = -(-nb // LANES) * LANES
            kc, vc = compress_rows(r3(kcm), r3(vcm), pwk2, pwv2)
            o_nsa = nsa_prompt(proj3, gate_cb, _pad_axis(kc, 1, nbp), _pad_axis(vc, 1, nbp), r3(ksl), r3(vsl),
                               r3(kwn), r3(vwn), _block_expand(nbp, t_p), nsa_q_norm_g[o], rope_p, nb)
            y_rnn, tail = rglru(proj3, x_cb, y_cb, jnp.zeros((nb_p, SUBLANES, D_RNN), F32),
                                jnp.zeros((nb_p, D_RNN), F32), *lru_w)
            xp = matmul_res(o_nsa.reshape(-1, qw), y_rnn.reshape(-1, D_RNN), w_out[:qw], w_out[qw:], xp)
            r4 = lambda a: a.reshape(nb_p, t_p, G_NSA, hd)
            keep = min(NSA_WINDOW, t_p)
            for key, val in (("ck_p", r4(kcm)), ("cv_p", r4(vcm)), ("sk_p", r4(ksl)), ("sv_p", r4(vsl)),
                             ("wk_p", r4(kwn)[:, t_p - keep:]), ("wv_p", r4(vwn)[:, t_p - keep:])):
                outs[key].append(val)
            outs["lh_p"].append(tail[:, SUBLANES - 1])
            lx = proj3[:, :, x_cb * hd:x_cb * hd + D_RNN]
            outs["lc_p"].append(jnp.concatenate([jnp.zeros((nb_p, CONV_W - 1, D_RNN), F32), lx], axis=1)[:, -(CONV_W - 1):])
            proj = rms_matmul(xs, norm_mix_g[layer], w_in, 1024)
            proj3 = proj.reshape(nb_s, rs, -1)
            kcm, vcm, ksl, vsl, kwn, vwn = kv_transform(proj, sections, gains, tile_s(rope_s), G_NSA, nb_s * rs)
            r3 = lambda a: a.reshape(nb_s, rs, gw)
            nph = cache_nsa_cmp_k.shape[1]
            pool = lambda a: a[o].reshape(nph, PAGE_SIZE, gw)
            pps = _tile(n_pages, 8) if n_pages % 8 == 0 else 1
            nb = (past + t_s) // NSA_BLOCK
            nbp = -(-nb // LANES) * LANES
            kc, vc = compress_pages(pool(cache_nsa_cmp_k), pool(cache_nsa_cmp_v), page_table, pwk2, pwv2, pps)
            lw = cache_nsa_win_k.shape[2]
            o_nsa = nsa_sample(proj3, gate_cb, _pad_axis(kc, 1, nbp), _pad_axis(vc, 1, nbp),
                               pool(cache_nsa_slc_k), pool(cache_nsa_slc_v), page_table, r3(ksl), r3(vsl),
                               cache_nsa_win_k[o].reshape(nb_s, lw, gw), cache_nsa_win_v[o].reshape(nb_s, lw, gw),
                               r3(kwn), r3(vwn), _block_expand(nbp, past), nsa_q_norm_g[o], rope_s, pps)
            hist8 = _pad_axis(state_lru_conv[o].astype(F32), 1, SUBLANES)
            y_rnn, tail = rglru(proj3, x_cb, y_cb, hist8, state_lru_h[o].astype(F32), *lru_w)
            xs = matmul_res(o_nsa.reshape(-1, qw), y_rnn.reshape(-1, D_RNN), w_out[:qw], w_out[qw:], xs)
            for key, val in (("ck_s", kcm), ("cv_s", vcm), ("sk_s", ksl), ("sv_s", vsl), ("wk_s", kwn), ("wv_s", vwn)):
                outs[key].append(new_rows(val, gw).reshape(nb_s, t_s, G_NSA, hd))
            outs["lh_s"].append(tail[:, t_s - 1])
            lx = proj3[:, :t_s, x_cb * hd:x_cb * hd + D_RNN]
            outs["lc_s"].append(jnp.concatenate([state_lru_conv[o].astype(F32), lx], axis=1)[:, -(CONV_W - 1):])

        dm = H_MEM * hd
        w_kv = jnp.concatenate([mem_w_k[layer], mem_w_v[layer]], axis=1).astype(BF16)
        mkv = rms_matmul(mem2, mem_norm_g[layer], w_kv, dm)
        gains = jnp.concatenate([mem_k_norm_g[layer][None], jnp.ones((1, hd), F32)], axis=0)
        mk, mv = kv_transform(mkv, [(0, "norm"), (1, "copy")], gains, rope_p, H_MEM, n_mem)
        mk3, mv3 = mk.reshape(nb_p, n_mem, dm), mv.reshape(nb_p, n_mem, dm)
        outs["mk_p"].append(mk3.reshape(nb_p, n_mem, H_MEM, hd))
        outs["mv_p"].append(mv3.reshape(nb_p, n_mem, H_MEM, hd))
        wq, wo = mem_w_q[layer].astype(BF16), mem_w_o[layer].astype(BF16)
        xp = mem_attn_block(xp, norm_mem_g[layer], wq, mem_q_norm_g[layer], mk3, mv3, wo, t_p)
        xs = mem_attn_block(xs, norm_mem_g[layer], wq, mem_q_norm_g[layer], cache_mem_k[layer].reshape(nb_s, n_mem, dm),
                            cache_mem_v[layer].reshape(nb_s, n_mem, dm), wo, rs)
        w1, w2 = mlp_w1[layer].astype(BF16), mlp_w2[layer].astype(BF16)
        xp = mlp_block(xp, norm_mlp_g[layer], w1, w2)
        xs = mlp_block(xs, norm_mlp_g[layer], w1, w2)

    st = lambda k: jnp.stack(outs[k])
    return (xp.reshape(nb_p, t_p, d), xs.reshape(nb_s, rs, d)[:, :t_s], st("ret_p"), st("ret_s"),
            st("dk_p"), st("dv_p"), st("dk_s"), st("dv_s"),
            st("ck_p"), st("cv_p"), st("ck_s"), st("cv_s"),
            st("sk_p"), st("sv_p"), st("sk_s"), st("sv_s"),
            st("wk_p"), st("wv_p"), st("wk_s"), st("wv_s"),
            st("lh_p"), st("lh_s"), st("lc_p"), st("lc_s"),
            st("mk_p"), st("mv_p"))
```

```python
import functools
import math

import jax
import jax.numpy as jnp
from jax import lax
from jax.experimental import pallas as pl
from jax.experimental.pallas import tpu as pltpu

F32 = jnp.float32
BF16 = jnp.bfloat16

HEAD_DIM = 128
ROPE_DIMS = HEAD_DIM // 4
ROPE_THETA = 500000.0
NORM_EPS = 1e-6
H_RET = 8
RET_CHUNK = 128
RET_THETA = 10000.0
H_DIL = 8
DIL_PATTERNS = ((128, 1), (512, 4), (2048, 16))
H_NSA = 8
G_NSA = 2
HPG = H_NSA // G_NSA
NSA_BLOCK = 64
NSA_TOPK = 16
NSA_WINDOW = 512
D_RNN = 1024
RNN_BLOCKS = 8
CONV_W = 4
LRU_C = 8.0
H_MEM = 4
PAGE_SIZE = 128

LANES = 128
SUBLANES = 8
KEY_BLOCK = 128
SAMPLE_ROWS = 8
VMEM_LIMIT = 48 * 1024 * 1024
NEG = -1e30
SCALE = HEAD_DIM ** -0.5


def _cp(*sem):
    return pltpu.CompilerParams(dimension_semantics=sem, vmem_limit_bytes=VMEM_LIMIT)


def _tile(n, pref):
    t = min(n, pref)
    while n % t:
        t -= SUBLANES
    return t


def _rms(x, g):
    return x * lax.rsqrt(jnp.mean(x * x, axis=-1, keepdims=True) + NORM_EPS) * g


def _rope(x, c, sa, sb, half):
    return x * c + pltpu.roll(x, LANES - half, 1) * sa + pltpu.roll(x, half, 1) * sb


def _dot(a, b):
    return jnp.dot(a, b, preferred_element_type=F32)


def _dot_nt(a, b):
    return lax.dot_general(a, b, (((1,), (1,)), ((), ())), preferred_element_type=F32)


def _pad_rows(x, rows):
    if x.shape[0] == rows:
        return x
    return jnp.concatenate([x, jnp.zeros((rows - x.shape[0], x.shape[1]), x.dtype)], axis=0)


def _softmax_step(s, w, vb, m, l, acc):
    s = jnp.where(w > 0, s, NEG)
    m_new = jnp.maximum(m, jnp.max(s, axis=-1, keepdims=True))
    alpha = jnp.exp(m - m_new)
    p = jnp.exp(s - m_new) * w
    l = alpha * l + jnp.sum(p, axis=-1, keepdims=True)
    acc = alpha * acc + _dot(p.astype(BF16), vb)
    return m_new, l, acc


def _softmax_init(rows):
    return (jnp.full((rows, 1), NEG, F32), jnp.zeros((rows, 1), F32), jnp.zeros((rows, HEAD_DIM), F32))


def _softmax_done(l, acc):
    return acc / jnp.maximum(l, 1e-30)


def _dil_weight(d):
    w = jnp.zeros(d.shape, F32)
    for window, dil in DIL_PATTERNS:
        hit = jnp.where(d <= window, 1.0, 0.0)
        if dil > 1:
            rem = (d & (dil - 1)) if dil & (dil - 1) == 0 else lax.rem(d, dil)
            hit = jnp.where(rem == 0, hit, 0.0)
        w = w + hit
    return jnp.where(d >= 0, w, 0.0)


def _iota(shape, dim):
    return lax.broadcasted_iota(jnp.int32, shape, dim)


def _rms_matmul_kernel(x_ref, g_ref, w_ref, o_ref, xn_ref):
    @pl.when(pl.program_id(1) == 0)
    def _():
        xn_ref[...] = _rms(x_ref[...], g_ref[...]).astype(BF16)

    o_ref[...] = _dot(xn_ref[...], w_ref[...])


def rms_matmul(x, g, w, tn):
    m, d = x.shape
    n = w.shape[1]
    tm = _tile(m, 512)
    return pl.pallas_call(
        _rms_matmul_kernel,
        grid=(m // tm, n // tn),
        in_specs=[pl.BlockSpec((tm, d), lambda i, j: (i, 0)),
                  pl.BlockSpec((1, d), lambda i, j: (0, 0)),
                  pl.BlockSpec((d, tn), lambda i, j: (0, j))],
        out_specs=pl.BlockSpec((tm, tn), lambda i, j: (i, j)),
        out_shape=jax.ShapeDtypeStruct((m, n), F32),
        scratch_shapes=[pltpu.VMEM((tm, d), BF16)],
        compiler_params=_cp("parallel", "arbitrary"),
        name="rms_matmul",
    )(x, g.reshape(1, d), w)


def _matmul_res_kernel(a1_ref, a2_ref, w1_ref, w2_ref, r_ref, o_ref):
    o_ref[...] = r_ref[...] + _dot(a1_ref[...], w1_ref[...]) + _dot(a2_ref[...], w2_ref[...])


def matmul_res(a1, a2, w1, w2, res):
    m, k1 = a1.shape
    k2 = a2.shape[1]
    n = w1.shape[1]
    tm = _tile(m, 512)
    tn = _tile(n, 1024)
    return pl.pallas_call(
        _matmul_res_kernel,
        grid=(m // tm, n // tn),
        in_specs=[pl.BlockSpec((tm, k1), lambda i, j: (i, 0)),
                  pl.BlockSpec((tm, k2), lambda i, j: (i, 0)),
                  pl.BlockSpec((k1, tn), lambda i, j: (0, j)),
                  pl.BlockSpec((k2, tn), lambda i, j: (0, j)),
                  pl.BlockSpec((tm, tn), lambda i, j: (i, j))],
        out_specs=pl.BlockSpec((tm, tn), lambda i, j: (i, j)),
        out_shape=jax.ShapeDtypeStruct((m, n), F32),
        compiler_params=_cp("parallel", "parallel"),
        name="matmul_res",
    )(a1, a2, w1, w2, res)


def _mlp_kernel(x_ref, g_ref, w1_ref, w2_ref, o_ref, xn_ref, acc_ref):
    f = pl.program_id(1)

    @pl.when(f == 0)
    def _():
        xn_ref[...] = _rms(x_ref[...], g_ref[...]).astype(BF16)
        acc_ref[...] = jnp.zeros_like(acc_ref)

    hid = jnp.maximum(_dot(xn_ref[...], w1_ref[...]), 0.0)
    acc_ref[...] += _dot((hid * hid).astype(BF16), w2_ref[...])

    @pl.when(f == pl.num_programs(1) - 1)
    def _():
        o_ref[...] = x_ref[...] + acc_ref[...]


def mlp_block(x, g, w1, w2):
    m, d = x.shape
    ff = w1.shape[1]
    tm = _tile(m, 512)
    tf = _tile(ff, 512)
    return pl.pallas_call(
        _mlp_kernel,
        grid=(m // tm, ff // tf),
        in_specs=[pl.BlockSpec((tm, d), lambda i, f: (i, 0)),
                  pl.BlockSpec((1, d), lambda i, f: (0, 0)),
                  pl.BlockSpec((d, tf), lambda i, f: (0, f)),
                  pl.BlockSpec((tf, d), lambda i, f: (f, 0))],
        out_specs=pl.BlockSpec((tm, d), lambda i, f: (i, 0)),
        out_shape=jax.ShapeDtypeStruct((m, d), F32),
        scratch_shapes=[pltpu.VMEM((tm, d), BF16), pltpu.VMEM((tm, d), F32)],
        compiler_params=_cp("parallel", "arbitrary"),
        name="mlp_block",
    )(x, g.reshape(1, d), w1, w2)


def _mem_attn_kernel(x_ref, g_ref, wq_ref, qg_ref, k_ref, v_ref, wo_ref, o_ref, oh_ref, *, nb, seg):
    x = x_ref[...]
    q = _dot(_rms(x, g_ref[...]).astype(BF16), wq_ref[...])
    qg = qg_ref[...]
    for b in range(nb):
        for h in range(H_MEM):
            cols = slice(h * HEAD_DIM, (h + 1) * HEAD_DIM)
            rows = slice(b * seg, (b + 1) * seg)
            qh = (_rms(q[rows, cols], qg) * SCALE).astype(BF16)
            s = _dot_nt(qh, k_ref[b][:, cols].astype(BF16))
            p = jnp.exp(s - jnp.max(s, axis=-1, keepdims=True))
            o = _dot(p.astype(BF16), v_ref[b][:, cols].astype(BF16)) / jnp.sum(p, axis=-1, keepdims=True)
            oh_ref[rows, cols] = o.astype(BF16)
    o_ref[...] = x + _dot(oh_ref[...], wo_ref[...])


def mem_attn_block(x, g, wq, qg, k, v, wo, rows_per_seq, seq_off=0):
    m, d = x.shape
    n_mem, dm = k.shape[1], k.shape[2]
    if rows_per_seq >= 128:
        tm = _tile(rows_per_seq, 512)
        nb, seg = 1, tm
        kv_map = lambda i: (seq_off + i // (rows_per_seq // tm), 0, 0)
    else:
        nb = _tile(m // rows_per_seq, 8)
        tm, seg = nb * rows_per_seq, rows_per_seq
        assert seq_off % nb == 0
        kv_map = lambda i: (seq_off // nb + i, 0, 0)
    return pl.pallas_call(
        functools.partial(_mem_attn_kernel, nb=nb, seg=seg),
        grid=(m // tm,),
        in_specs=[pl.BlockSpec((tm, d), lambda i: (i, 0)),
                  pl.BlockSpec((1, d), lambda i: (0, 0)),
                  pl.BlockSpec((d, dm), lambda i: (0, 0)),
                  pl.BlockSpec((1, HEAD_DIM), lambda i: (0, 0)),
                  pl.BlockSpec((nb, n_mem, dm), kv_map),
                  pl.BlockSpec((nb, n_mem, dm), kv_map),
                  pl.BlockSpec((dm, d), lambda i: (0, 0))],
        out_specs=pl.BlockSpec((tm, d), lambda i: (i, 0)),
        out_shape=jax.ShapeDtypeStruct((m, d), F32),
        scratch_shapes=[pltpu.VMEM((tm, dm), BF16)],
        compiler_params=_cp("parallel"),
        name="mem_attn_block",
    )(x, g.reshape(1, d), wq, qg.reshape(1, HEAD_DIM), k, v, wo)


def _kv_transform_kernel(*refs, modes, heads):
    ns = len(modes)
    x_refs = refs[:ns]
    g_ref, c_ref, sa_ref, sb_ref = refs[ns:ns + 4]
    o_refs = refs[ns + 4:]
    for s in range(ns):
        for h in range(heads):
            cols = slice(h * HEAD_DIM, (h + 1) * HEAD_DIM)
            x = x_refs[s][:, cols]
            if modes[s] != "copy":
                x = _rms(x, g_ref[s:s + 1, :])
            if modes[s] == "norm_rope":
                x = _rope(x, c_ref[...], sa_ref[...], sb_ref[...], ROPE_DIMS // 2)
            o_refs[s][:, cols] = x


def kv_transform(proj, sections, gains, tables, heads, pos_period):
    m = proj.shape[0]
    width = heads * HEAD_DIM
    tr = _tile(pos_period, 512)
    nper = pos_period // tr
    modes = tuple(mode for _, mode in sections)
    in_specs = [pl.BlockSpec((tr, width), functools.partial(lambda i, cb: (i, cb), cb=cb)) for cb, _ in sections]
    in_specs.append(pl.BlockSpec(gains.shape, lambda i: (0, 0)))
    in_specs += [pl.BlockSpec((tr, HEAD_DIM), lambda i: (i % nper, 0))] * 3
    outs = pl.pallas_call(
        functools.partial(_kv_transform_kernel, modes=modes, heads=heads),
        grid=(m // tr,),
        in_specs=in_specs,
        out_specs=[pl.BlockSpec((tr, width), lambda i: (i, 0))] * len(sections),
        out_shape=[jax.ShapeDtypeStruct((m, width), F32)] * len(sections),
        compiler_params=_cp("parallel"),
        name="kv_transform",
    )(*([proj] * len(sections)), gains, *tables)
    return outs


def _retention_kernel(q_ref, k_ref, v_ref, rg_ref, s0_ref, lg_ref, gn_ref, c_ref, sa_ref, sb_ref,
                      y_ref, s_ref, st_ref, *, rows, c_true):
    ci = pl.program_id(2)
    cc = RET_CHUNK

    @pl.when(ci == 0)
    def _():
        st_ref[...] = s0_ref[0, 0]

    lg = lg_ref[0]
    c, sa, sb = c_ref[...], sa_ref[...], sb_ref[...]
    q = _pad_rows(_rope(q_ref[0], c, sa, sb, HEAD_DIM // 2) * SCALE, cc)
    k = _pad_rows(_rope(k_ref[0], c, sa, sb, HEAD_DIM // 2), cc)
    vb = _pad_rows(v_ref[0], cc).astype(BF16)
    qb = q.astype(BF16)
    ii = _iota((cc, 1), 0).astype(F32)
    diff = ii - _iota((1, cc), 1).astype(F32)
    decay = jnp.where(diff >= 0, jnp.exp(jnp.maximum(diff, 0.0) * lg), 0.0)
    inner = _dot_nt(qb, k.astype(BF16)) * decay
    st = st_ref[...]
    o = _dot(inner.astype(BF16), vb) + _dot(qb, st.astype(BF16)) * jnp.exp((ii + 1.0) * lg)
    kd = jnp.where(ii < c_true, k * jnp.exp((c_true - 1.0 - ii) * lg), 0.0)
    st_new = jnp.exp(c_true * lg) * st + _dot(kd.T.astype(BF16), vb)
    st_ref[...] = st_new

    @pl.when(ci == pl.num_programs(2) - 1)
    def _():
        s_ref[0, 0] = st_new

    o = o[:rows]
    mu = jnp.mean(o, axis=-1, keepdims=True)
    var = jnp.mean(jnp.square(o - mu), axis=-1, keepdims=True)
    y = (o - mu) * lax.rsqrt(var + NORM_EPS) * gn_ref[0]
    rg = rg_ref[0]
    y_ref[0] = (y * (rg * jax.nn.sigmoid(rg))).astype(BF16)


def retention(proj3, s0, log_g, gn_g, tables, c_true, seq_off=0):
    n, t, _ = proj3.shape
    rows = min(t, RET_CHUNK)
    nc = t // rows
    h = H_RET
    blk = lambda off: pl.BlockSpec((1, rows, HEAD_DIM), functools.partial(lambda b, hh, c, off: (b, c, off + hh), off=off))
    tab = pl.BlockSpec((rows, HEAD_DIM), lambda b, hh, c: (c, 0))
    y, s_new = pl.pallas_call(
        functools.partial(_retention_kernel, rows=rows, c_true=float(c_true)),
        grid=(n, h, nc),
        in_specs=[blk(0), blk(h), blk(2 * h), blk(3 * h),
                  pl.BlockSpec((1, 1, HEAD_DIM, HEAD_DIM), lambda b, hh, c: (seq_off + b, hh, 0, 0)),
                  pl.BlockSpec((1, 1, HEAD_DIM), lambda b, hh, c: (hh, 0, 0)),
                  pl.BlockSpec((1, 1, HEAD_DIM), lambda b, hh, c: (hh, 0, 0)),
                  tab, tab, tab],
        out_specs=[pl.BlockSpec((1, rows, HEAD_DIM), lambda b, hh, c: (b, c, hh)),
                   pl.BlockSpec((1, 1, HEAD_DIM, HEAD_DIM), lambda b, hh, c: (b, hh, 0, 0))],
        out_shape=[jax.ShapeDtypeStruct((n, t, h * HEAD_DIM), BF16),
                   jax.ShapeDtypeStruct((n, h, HEAD_DIM, HEAD_DIM), F32)],
        scratch_shapes=[pltpu.VMEM((HEAD_DIM, HEAD_DIM), F32)],
        compiler_params=_cp("parallel", "parallel", "arbitrary"),
        name="retention",
    )(proj3, proj3, proj3, proj3, s0,
      jnp.broadcast_to(log_g[:, None, None], (h, 1, HEAD_DIM)), gn_g.reshape(h, 1, HEAD_DIM), *tables)
    return y, s_new


def _rows(start, size, stride):
    return pl.ds(start, size) if stride == 1 else pl.ds(start, size, stride=stride)


def _dil_prompt_kernel(q_ref, k_ref, v_ref, qg_ref, c_ref, sa_ref, sb_ref, o_ref, qs_ref, m_ref, l_ref, acc_ref, *, t):
    chunk = _tile(t, 512)
    for c in range(t // chunk):
        r = slice(c * chunk, (c + 1) * chunk)
        q = _rope(_rms(q_ref[0, r, :], qg_ref[...]), c_ref[r, :], sa_ref[r, :], sb_ref[r, :], ROPE_DIMS // 2)
        qs_ref[r, :] = q * SCALE

    def block(dil, start, bq, has_prev):
        nk = bq + (KEY_BLOCK if has_prev else 0)
        kstart = start - dil * KEY_BLOCK if has_prev else start
        qb = qs_ref[_rows(start, bq, dil), :].astype(BF16)
        kb = k_ref[0, _rows(kstart, nk, dil), :].astype(BF16)
        vb = v_ref[0, _rows(kstart, nk, dil), :].astype(BF16)
        rel = (nk - bq) + _iota((bq, nk), 0) - _iota((bq, nk), 1)
        ok = jnp.where(rel >= 0, jnp.where(rel <= KEY_BLOCK, 1.0, 0.0), 0.0)
        s = jnp.where(ok > 0, _dot_nt(qb, kb), NEG)
        m = jnp.max(s, axis=-1, keepdims=True)
        p = jnp.exp(s - m) * ok
        return m, jnp.sum(p, axis=-1, keepdims=True), _dot(p.astype(BF16), vb)

    def store(dil, start, bq, m, l, acc, first):
        rows = _rows(start, bq, dil)
        m = jnp.broadcast_to(m, (bq, HEAD_DIM))
        l = jnp.broadcast_to(l, (bq, HEAD_DIM))
        if not first:
            m0 = m_ref[rows, :]
            mx = jnp.maximum(m0, m)
            e0, e1 = jnp.exp(m0 - mx), jnp.exp(m - mx)
            m, l, acc = mx, e0 * l_ref[rows, :] + e1 * l, e0 * acc_ref[rows, :] + e1 * acc
        m_ref[rows, :] = m
        l_ref[rows, :] = l
        acc_ref[rows, :] = acc

    for pi, (window, dil) in enumerate(DIL_PATTERNS):
        length = t // dil
        bq = min(length, KEY_BLOCK)
        first = pi == 0

        def class_body(r, _, dil=dil, bq=bq, length=length, first=first):
            store(dil, r, bq, *block(dil, r, bq, False), first)

            def blk_body(i, _):
                start = r + dil * KEY_BLOCK * i
                if dil == 1:
                    start = pl.multiple_of(start, KEY_BLOCK)
                store(dil, start, bq, *block(dil, start, bq, True), first)
                return 0

            if length > bq:
                lax.fori_loop(1, length // bq, blk_body, 0)
            return 0

        if dil == 1:
            class_body(0, 0)
        else:
            lax.fori_loop(0, dil, class_body, 0)

    for c in range(t // chunk):
        r = slice(c * chunk, (c + 1) * chunk)
        o_ref[0, r, :] = (acc_ref[r, :] / l_ref[r, :]).astype(BF16)


def dilated_prompt(proj3, q_cb, k3, v3, qg, tables):
    n, t, _ = proj3.shape
    for window, dil in DIL_PATTERNS:
        length = t // dil
        assert window // dil == KEY_BLOCK and t % dil == 0
        assert length % KEY_BLOCK == 0 or (length < KEY_BLOCK and length % SUBLANES == 0)
    tab = pl.BlockSpec((t, HEAD_DIM), lambda b, h: (0, 0))
    seq = lambda cb: pl.BlockSpec((1, t, HEAD_DIM), functools.partial(lambda b, h, cb: (b, 0, cb + h), cb=cb))
    return pl.pallas_call(
        functools.partial(_dil_prompt_kernel, t=t),
        grid=(n, H_DIL),
        in_specs=[seq(q_cb), seq(0), seq(0), pl.BlockSpec((1, HEAD_DIM), lambda b, h: (0, 0)), tab, tab, tab],
        out_specs=seq(0),
        out_shape=jax.ShapeDtypeStruct((n, t, H_DIL * HEAD_DIM), BF16),
        scratch_shapes=[pltpu.VMEM((t, HEAD_DIM), F32)] * 4,
        compiler_params=_cp("parallel", "parallel"),
        name="dilated_prompt",
    )(proj3, k3, v3, qg.reshape(1, HEAD_DIM), *tables)


def _dil_sample_kernel(q_ref, kc_ref, vc_ref, kn_ref, vn_ref, qg_ref, c_ref, sa_ref, sb_ref, o_ref, *, lc, hps):
    rows = q_ref.shape[1]
    w_c = _dil_weight(lc + _iota((rows, lc), 0) - _iota((rows, lc), 1))
    w_n = _dil_weight(_iota((rows, KEY_BLOCK), 0) - _iota((rows, KEY_BLOCK), 1))
    for h in range(hps):
        cols = slice(h * HEAD_DIM, (h + 1) * HEAD_DIM)
        q = _rope(_rms(q_ref[0, :, cols], qg_ref[...]), c_ref[...], sa_ref[...], sb_ref[...], ROPE_DIMS // 2)
        qb = (q * SCALE).astype(BF16)
        s_c = jnp.where(w_c > 0, _dot_nt(qb, kc_ref[0, :, cols].astype(BF16)), NEG)
        s_n = jnp.where(w_n > 0, _dot_nt(qb, _pad_rows(kn_ref[0, :, cols], KEY_BLOCK).astype(BF16)), NEG)
        m = jnp.maximum(jnp.max(s_c, axis=-1, keepdims=True), jnp.max(s_n, axis=-1, keepdims=True))
        p_c = jnp.exp(s_c - m) * w_c
        p_n = jnp.exp(s_n - m) * w_n
        l = jnp.sum(p_c, axis=-1, keepdims=True) + jnp.sum(p_n, axis=-1, keepdims=True)
        acc = (_dot(p_c.astype(BF16), vc_ref[0, :, cols].astype(BF16))
               + _dot(p_n.astype(BF16), _pad_rows(vn_ref[0, :, cols], KEY_BLOCK).astype(BF16)))
        o_ref[0, :, cols] = (acc / l).astype(BF16)


def dilated_sample(proj3, q_cb, kc3, vc3, seq_off, kn3, vn3, qg, tables):
    n, rows, _ = proj3.shape
    lc = kc3.shape[1]
    hps = 4
    wid = hps * HEAD_DIM
    assert (q_cb * HEAD_DIM) % wid == 0
    tab = pl.BlockSpec((rows, HEAD_DIM), lambda b, h: (0, 0))
    cache = pl.BlockSpec((1, lc, wid), lambda b, h: (seq_off + b, 0, h))
    new = pl.BlockSpec((1, rows, wid), lambda b, h: (b, 0, h))
    return pl.pallas_call(
        functools.partial(_dil_sample_kernel, lc=lc, hps=hps),
        grid=(n, H_DIL // hps),
        in_specs=[pl.BlockSpec((1, rows, wid), lambda b, h: (b, 0, q_cb * HEAD_DIM // wid + h)), cache, cache, new, new,
                  pl.BlockSpec((1, HEAD_DIM), lambda b, h: (0, 0)), tab, tab, tab],
        out_specs=pl.BlockSpec((1, rows, wid), lambda b, h: (b, 0, h)),
        out_shape=jax.ShapeDtypeStruct((n, rows, H_DIL * HEAD_DIM), BF16),
        compiler_params=_cp("parallel", "parallel"),
        name="dilated_sample",
    )(proj3, kc3, vc3, kn3, vn3, qg.reshape(1, HEAD_DIM), *tables)


def _nsa_q(q_ref, g, qg_ref, c_ref, sa_ref, sb_ref):
    qs = []
    for hl in range(HPG):
        col = (g * HPG + hl) * HEAD_DIM
        x = _rms(q_ref[0][:, col:col + HEAD_DIM], qg_ref[...])
        qs.append(_rope(x, c_ref[...], sa_ref[...], sb_ref[...], ROPE_DIMS // 2) * SCALE)
    return jnp.concatenate(qs, axis=0).astype(BF16)


def _nsa_compressed(q4, kc, vc, qpos, nb):
    rows = qpos.shape[0]
    nbp = kc.shape[0]
    blk = _iota((rows, nbp), 1)
    cmask = jnp.where((blk + 1) * NSA_BLOCK - 1 <= qpos, 1.0, 0.0)
    cmask4 = jnp.concatenate([cmask] * HPG, axis=0)
    s = jnp.where(cmask4 > 0, _dot_nt(q4, kc.astype(BF16)), NEG)
    p = jnp.exp(s - jnp.max(s, axis=-1, keepdims=True)) * cmask4
    p = p / jnp.maximum(jnp.sum(p, axis=-1, keepdims=True), 1e-30)
    o_c = _dot(p.astype(BF16), vc.astype(BF16))
    imp = p[0:rows]
    for hl in range(1, HPG):
        imp = imp + p[hl * rows:(hl + 1) * rows]
    cur = jnp.right_shift(qpos, NSA_BLOCK.bit_length() - 1)
    cand = blk < cur
    imp = jnp.where(cand, imp, -jnp.inf)
    rank = jnp.zeros((rows, nbp), F32)
    for b in range(nb):
        col = imp[:, b:b + 1]
        tie = jnp.where(col == imp, jnp.where(blk > b, 1.0, 0.0), 0.0)
        rank = rank + jnp.where(col > imp, 1.0, tie)
    sel = jnp.where(cand, jnp.where(rank < min(NSA_TOPK, nb), 1.0, 0.0), 0.0)
    sel = jnp.where(blk == cur, 1.0, sel)
    return o_c, sel


def _nsa_combine(gates, g, o_c, o_s, o_w, rows):
    outs = []
    for hl in range(HPG):
        r = slice(hl * rows, (hl + 1) * rows)
        col = 3 * hl
        outs.append(gates[:, col:col + 1] * o_c[r] + gates[:, col + 1:col + 2] * o_s[r]
                    + gates[:, col + 2:col + 3] * o_w[r])
    return outs


def _nsa_prompt_kernel(q_ref, gt_ref, kc_ref, vc_ref, ks_ref, vs_ref, kw_ref, vw_ref, e_ref,
                       qg_ref, c_ref, sa_ref, sb_ref, o_ref, selx_ref, *, tq, nb):
    qi = pl.program_id(2)
    q4 = _nsa_q(q_ref, 0, qg_ref, c_ref, sa_ref, sb_ref)
    qpos = qi * tq + _iota((tq, 1), 0)
    o_c, sel = _nsa_compressed(q4, kc_ref[0], vc_ref[0], qpos, nb)
    selx_ref[...] = _dot(sel.astype(BF16), e_ref[...])
    lane = _iota((tq, KEY_BLOCK), 1)

    def sel_body(kj, carry):
        off = pl.multiple_of(kj * KEY_BLOCK, KEY_BLOCK)
        kb = ks_ref[0, pl.ds(off, KEY_BLOCK), :].astype(BF16)
        vb = vs_ref[0, pl.ds(off, KEY_BLOCK), :].astype(BF16)
        w = jnp.where(off + lane <= qpos, selx_ref[:, pl.ds(off, KEY_BLOCK)], 0.0)
        return _softmax_step(_dot_nt(q4, kb), jnp.concatenate([w] * HPG, axis=0), vb, *carry)

    _, l, acc = lax.fori_loop(0, qi + 1, sel_body, _softmax_init(HPG * tq))
    o_s = _softmax_done(l, acc)

    def win_body(kj, carry):
        off = pl.multiple_of(kj * KEY_BLOCK, KEY_BLOCK)
        kb = kw_ref[0, pl.ds(off, KEY_BLOCK), :].astype(BF16)
        vb = vw_ref[0, pl.ds(off, KEY_BLOCK), :].astype(BF16)
        d = qpos - (off + lane)
        w = jnp.where(d >= 0, jnp.where(d < NSA_WINDOW, 1.0, 0.0), 0.0)
        return _softmax_step(_dot_nt(q4, kb), jnp.concatenate([w] * HPG, axis=0), vb, *carry)

    lo = jnp.maximum(qi - NSA_WINDOW // KEY_BLOCK, 0)
    _, l, acc = lax.fori_loop(lo, qi + 1, win_body, _softmax_init(HPG * tq))
    o_w = _softmax_done(l, acc)
    gates = jax.nn.sigmoid(gt_ref[0])
    for hl, o in enumerate(_nsa_combine(gates, 0, o_c, o_s, o_w, tq)):
        o_ref[0, :, hl * HEAD_DIM:(hl + 1) * HEAD_DIM] = o.astype(BF16)


def nsa_prompt(proj3, gate_cb, kc3, vc3, ks3, vs3, kw3, vw3, expand, qg, tables, nb):
    n, t, _ = proj3.shape
    tq = KEY_BLOCK
    nbp = kc3.shape[1]
    gw = HPG * HEAD_DIM
    tab = pl.BlockSpec((tq, HEAD_DIM), lambda b, g, i: (i, 0))
    cmp_spec = pl.BlockSpec((1, nbp, HEAD_DIM), lambda b, g, i: (b, 0, g))
    kv = pl.BlockSpec((1, t, HEAD_DIM), lambda b, g, i: (b, 0, g))
    return pl.pallas_call(
        functools.partial(_nsa_prompt_kernel, tq=tq, nb=nb),
        grid=(n, G_NSA, t // tq),
        in_specs=[pl.BlockSpec((1, tq, gw), lambda b, g, i: (b, i, g)),
                  pl.BlockSpec((1, tq, HEAD_DIM), lambda b, g, i: (b, i, gate_cb + g)),
                  cmp_spec, cmp_spec, kv, kv, kv, kv,
                  pl.BlockSpec((nbp, t), lambda b, g, i: (0, 0)),
                  pl.BlockSpec((1, HEAD_DIM), lambda b, g, i: (0, 0)), tab, tab, tab],
        out_specs=pl.BlockSpec((1, tq, gw), lambda b, g, i: (b, i, g)),
        out_shape=jax.ShapeDtypeStruct((n, t, H_NSA * HEAD_DIM), BF16),
        scratch_shapes=[pltpu.VMEM((tq, t), F32)],
        compiler_params=_cp("parallel", "parallel", "arbitrary"),
        name="nsa_prompt",
    )(proj3, proj3, kc3, vc3, ks3, vs3, kw3, vw3, expand, qg.reshape(1, HEAD_DIM), *tables)


def _compress_rows_kernel(k_ref, v_ref, pk_ref, pv_ref, ko_ref, vo_ref):
    for x_ref, p_ref, o_ref in ((k_ref, pk_ref, ko_ref), (v_ref, pv_ref, vo_ref)):
        x = x_ref[0]
        nblk = x.shape[0] // NSA_BLOCK
        pw = jnp.concatenate([p_ref[...]] * nblk, axis=0)
        o_ref[0] = jnp.sum((x * pw).reshape(nblk, NSA_BLOCK, x.shape[1]), axis=1)


def compress_rows(k3, v3, pwk2, pwv2):
    n, t, w = k3.shape
    tr = _tile(t, 512)
    row = pl.BlockSpec((1, tr, w), lambda b, i: (b, i, 0))
    pw = pl.BlockSpec((NSA_BLOCK, w), lambda b, i: (0, 0))
    out = pl.BlockSpec((1, tr // NSA_BLOCK, w), lambda b, i: (b, i, 0))
    return pl.pallas_call(
        _compress_rows_kernel,
        grid=(n, t // tr),
        in_specs=[row, row, pw, pw],
        out_specs=[out, out],
        out_shape=[jax.ShapeDtypeStruct((n, t // NSA_BLOCK, w), F32)] * 2,
        compiler_params=_cp("parallel", "parallel"),
        name="compress_rows",
    )(k3, v3, pwk2, pwv2)


def _compress_pages_kernel(pt_ref, *refs, pps):
    k_refs, v_refs = refs[:pps], refs[pps:2 * pps]
    pk_ref, pv_ref, ko_ref, vo_ref = refs[2 * pps:]
    bpp = PAGE_SIZE // NSA_BLOCK
    for x_refs, p_ref, o_ref in ((k_refs, pk_ref, ko_ref), (v_refs, pv_ref, vo_ref)):
        pw = jnp.concatenate([p_ref[...]] * bpp, axis=0)
        outs = [jnp.sum((x_ref[0] * pw).reshape(bpp, NSA_BLOCK, pw.shape[1]), axis=1) for x_ref in x_refs]
        o_ref[0] = jnp.concatenate(outs, axis=0)


def compress_pages(pool_k, pool_v, page_off, page_table, pwk2, pwv2, pps):
    n, n_pages = page_table.shape
    w = pool_k.shape[2]
    bpp = PAGE_SIZE // NSA_BLOCK
    page = lambda i: pl.BlockSpec((1, PAGE_SIZE, w),
                                  functools.partial(lambda b, c, pt, i: (page_off + pt[b, c * pps + i], 0, 0), i=i))
    pw = pl.BlockSpec((NSA_BLOCK, w), lambda b, c, pt: (0, 0))
    out = pl.BlockSpec((1, pps * bpp, w), lambda b, c, pt: (b, c, 0))
    return pl.pallas_call(
        functools.partial(_compress_pages_kernel, pps=pps),
        grid_spec=pltpu.PrefetchScalarGridSpec(
            num_scalar_prefetch=1, grid=(n, n_pages // pps),
            in_specs=[page(i) for i in range(pps)] * 2 + [pw, pw],
            out_specs=[out, out]),
        out_shape=[jax.ShapeDtypeStruct((n, n_pages * bpp, w), F32)] * 2,
        compiler_params=_cp("parallel", "parallel"),
        name="compress_pages",
    )(page_table, *([pool_k] * pps), *([pool_v] * pps), pwk2, pwv2)


def _nsa_sample_kernel(pt_ref, *refs, pps, past, rows, nb, lw):
    (q_ref, gt_ref, kc_ref, vc_ref), refs = refs[:4], refs[4:]
    kp_refs, vp_refs, refs = refs[:pps], refs[pps:2 * pps], refs[2 * pps:]
    (kn_ref, vn_ref, kwc_ref, vwc_ref, kwn_ref, vwn_ref, e_ref, qg_ref, c_ref, sa_ref, sb_ref,
     o_ref, q4_ref, sel_ref, oc_ref, m_ref, l_ref, acc_ref) = refs
    pc = pl.program_id(1)
    r4 = HPG * rows
    qpos = past + _iota((rows, 1), 0)
    lane = _iota((rows, KEY_BLOCK), 1)

    @pl.when(pc == 0)
    def _():
        for g in range(G_NSA):
            cols = slice(g * HEAD_DIM, (g + 1) * HEAD_DIM)
            q4 = _nsa_q(q_ref, g, qg_ref, c_ref, sa_ref, sb_ref)
            o_c, sel = _nsa_compressed(q4, kc_ref[0][:, cols], vc_ref[0][:, cols], qpos, nb)
            q4_ref[g] = q4
            sel_ref[g] = sel
            oc_ref[g] = o_c
            m0, l0, a0 = _softmax_init(r4)
            m_ref[g], l_ref[g], acc_ref[g] = m0, l0, a0

    nkeys = pps * PAGE_SIZE
    kpos = pc * nkeys + _iota((rows, nkeys), 1)
    for g in range(G_NSA):
        cols = slice(g * HEAD_DIM, (g + 1) * HEAD_DIM)
        kb = jnp.concatenate([r[0, :, cols].astype(BF16) for r in kp_refs], axis=0)
        vb = jnp.concatenate([r[0, :, cols].astype(BF16) for r in vp_refs], axis=0)
        selx = _dot(sel_ref[g].astype(BF16), e_ref[...])
        w = jnp.where(kpos <= qpos, selx, 0.0)
        m_ref[g], l_ref[g], acc_ref[g] = _softmax_step(
            _dot_nt(q4_ref[g], kb), jnp.concatenate([w] * HPG, axis=0), vb, m_ref[g], l_ref[g], acc_ref[g])

    @pl.when(pc == pl.num_programs(1) - 1)
    def _():
        gates_all = jax.nn.sigmoid(gt_ref[0])
        tq = _iota((rows, KEY_BLOCK), 0)
        for g in range(G_NSA):
            cols = slice(g * HEAD_DIM, (g + 1) * HEAD_DIM)
            q4 = q4_ref[g]
            kb = _pad_rows(kn_ref[0][:, cols], KEY_BLOCK).astype(BF16)
            vb = _pad_rows(vn_ref[0][:, cols], KEY_BLOCK).astype(BF16)
            w = jnp.where(lane <= tq, 1.0, 0.0)
            _, l, acc = _softmax_step(_dot_nt(q4, kb), jnp.concatenate([w] * HPG, axis=0), vb,
                                      m_ref[g], l_ref[g], acc_ref[g])
            o_s = _softmax_done(l, acc)
            tqw = _iota((rows, lw), 0)
            d = tqw + lw - _iota((rows, lw), 1)
            w = jnp.where(d < NSA_WINDOW, jnp.where(past - d + tqw >= 0, 1.0, 0.0), 0.0)
            carry = _softmax_step(_dot_nt(q4, kwc_ref[0, :, cols].astype(BF16)), jnp.concatenate([w] * HPG, axis=0),
                                  vwc_ref[0, :, cols].astype(BF16), *_softmax_init(r4))
            kb = _pad_rows(kwn_ref[0][:, cols], KEY_BLOCK).astype(BF16)
            vb = _pad_rows(vwn_ref[0][:, cols], KEY_BLOCK).astype(BF16)
            w = jnp.where(lane <= tq, 1.0, 0.0)
            _, l, acc = _softmax_step(_dot_nt(q4, kb), jnp.concatenate([w] * HPG, axis=0), vb, *carry)
            o_w = _softmax_done(l, acc)
            gates = gates_all[:, g * HEAD_DIM:(g + 1) * HEAD_DIM]
            for hl, o in enumerate(_nsa_combine(gates, g, oc_ref[g], o_s, o_w, rows)):
                col = (g * HPG + hl) * HEAD_DIM
                o_ref[0, :, col:col + HEAD_DIM] = o.astype(BF16)


def nsa_sample(proj3, gate_cb, kc3, vc3, pool_k, pool_v, page_off, page_table, kn3, vn3, kwc3, vwc3, seq_off,
               kwn3, vwn3, expand, qg, tables, pps):
    n, rows, _ = proj3.shape
    n_pages = page_table.shape[1]
    past = n_pages * PAGE_SIZE
    nbp = kc3.shape[1]
    w = G_NSA * HEAD_DIM
    lw = kwc3.shape[1]
    qw = H_NSA * HEAD_DIM
    r4 = HPG * rows
    fixed = lambda shape: pl.BlockSpec(shape, lambda b, c, pt: (0,) * len(shape))
    per_seq = lambda r, width, cb=0, off=0: pl.BlockSpec(
        (1, r, width), functools.partial(lambda b, c, pt, cb, off: (off + b, 0, cb), cb=cb, off=off))
    page = lambda i: pl.BlockSpec((1, PAGE_SIZE, w),
                                  functools.partial(lambda b, c, pt, i: (page_off + pt[b, c * pps + i], 0, 0), i=i))
    win = per_seq(lw, w, 0, seq_off)
    in_specs = ([per_seq(rows, qw), per_seq(rows, w, gate_cb // G_NSA), per_seq(nbp, w), per_seq(nbp, w)]
                + [page(i) for i in range(pps)] * 2
                + [per_seq(rows, w), per_seq(rows, w), win, win, per_seq(rows, w), per_seq(rows, w),
                   pl.BlockSpec((nbp, pps * PAGE_SIZE), lambda b, c, pt: (0, c)),
                   fixed((1, HEAD_DIM)), fixed((rows, HEAD_DIM)), fixed((rows, HEAD_DIM)), fixed((rows, HEAD_DIM))])
    return pl.pallas_call(
        functools.partial(_nsa_sample_kernel, pps=pps, past=past, rows=rows, nb=past // NSA_BLOCK, lw=lw),
        grid_spec=pltpu.PrefetchScalarGridSpec(
            num_scalar_prefetch=1, grid=(n, n_pages // pps),
            in_specs=in_specs,
            out_specs=pl.BlockSpec((1, rows, qw), lambda b, c, pt: (b, 0, 0)),
            scratch_shapes=[pltpu.VMEM((G_NSA, r4, HEAD_DIM), BF16), pltpu.VMEM((G_NSA, rows, nbp), F32),
                            pltpu.VMEM((G_NSA, r4, HEAD_DIM), F32), pltpu.VMEM((G_NSA, r4, 1), F32),
                            pltpu.VMEM((G_NSA, r4, 1), F32), pltpu.VMEM((G_NSA, r4, HEAD_DIM), F32)]),
        out_shape=jax.ShapeDtypeStruct((n, rows, qw), BF16),
        compiler_params=_cp("parallel", "arbitrary"),
        name="nsa_sample",
    )(page_table, proj3, proj3, kc3, vc3, *([pool_k] * pps), *([pool_v] * pps), kn3, vn3, kwc3, vwc3, kwn3, vwn3,
      expand, qg.reshape(1, HEAD_DIM), *tables)


def _lru_kernel(x_ref, y_ref, hist_ref, h0_ref, cw_ref, cb_ref, wa_ref, ba_ref, wi_ref, bi_ref, sp_ref,
                o_ref, tail_ref, a_ref, b_ref):
    t = x_ref.shape[1]
    x = x_ref[0]
    hist = hist_ref[0]
    row8 = _iota((SUBLANES, LANES), 0)
    xc = cb_ref[0] + x * cw_ref[0, CONV_W - 1:CONV_W, :]
    for s in range(1, CONV_W):
        sh = pltpu.roll(x, s, 0)
        head = jnp.where(row8 >= s, sh[:SUBLANES], pltpu.roll(hist, (SUBLANES - (CONV_W - 1) + s) % SUBLANES, 0))
        sh = head if t == SUBLANES else jnp.concatenate([head, sh[SUBLANES:]], axis=0)
        xc = xc + sh * cw_ref[0, CONV_W - 1 - s:CONV_W - s, :]
    xb = xc.astype(BF16)
    r = jax.nn.sigmoid(_dot(xb, wa_ref[0]) + ba_ref[0])
    gi = jax.nn.sigmoid(_dot(xb, wi_ref[0]) + bi_ref[0])
    log_a = -LRU_C * r * sp_ref[0]
    th = jnp.tanh(log_a)
    a_ref[...] = jnp.exp(log_a)
    b_ref[...] = jnp.sqrt(-2.0 * th / (1.0 - th)) * (gi * xc)

    def body(gidx, hc):
        off = pl.multiple_of(gidx * SUBLANES, SUBLANES)
        a = a_ref[pl.ds(off, SUBLANES), :]
        b = b_ref[pl.ds(off, SUBLANES), :]
        for s in (1, 2, 4):
            ok = row8 >= s
            b = jnp.where(ok, a * pltpu.roll(b, s, 0) + b, b)
            a = jnp.where(ok, a * pltpu.roll(a, s, 0), a)
        hrows = a * hc + b
        a_ref[pl.ds(off, SUBLANES), :] = hrows
        return jnp.broadcast_to(hrows[SUBLANES - 1:SUBLANES, :], (SUBLANES, LANES))

    lax.fori_loop(0, t // SUBLANES, body, jnp.broadcast_to(h0_ref[0], (SUBLANES, LANES)))
    h = a_ref[...]
    tail_ref[0] = h[t - SUBLANES:]
    o_ref[0] = (jax.nn.gelu(y_ref[0], approximate=True) * h).astype(BF16)


def rglru(proj3, x_cb, y_cb, hist8, h0, conv_w, conv_b, w_a, b_a, w_i, b_i, softplus_neg_lam):
    n, t, _ = proj3.shape
    bw = D_RNN // RNN_BLOCKS
    vec = lambda a: a.reshape(RNN_BLOCKS, 1, bw)
    vspec = pl.BlockSpec((1, 1, bw), lambda b, k: (k, 0, 0))
    mspec = pl.BlockSpec((1, bw, bw), lambda b, k: (k, 0, 0))
    return pl.pallas_call(
        _lru_kernel,
        grid=(n, RNN_BLOCKS),
        in_specs=[pl.BlockSpec((1, t, bw), lambda b, k: (b, 0, x_cb + k)),
                  pl.BlockSpec((1, t, bw), lambda b, k: (b, 0, y_cb + k)),
                  pl.BlockSpec((1, SUBLANES, bw), lambda b, k: (b, 0, k)),
                  pl.BlockSpec((1, 1, bw), lambda b, k: (b, 0, k)),
                  pl.BlockSpec((1, CONV_W, bw), lambda b, k: (k, 0, 0)),
                  vspec, mspec, vspec, mspec, vspec, vspec],
        out_specs=[pl.BlockSpec((1, t, bw), lambda b, k: (b, 0, k)),
                   pl.BlockSpec((1, SUBLANES, bw), lambda b, k: (b, 0, k))],
        out_shape=[jax.ShapeDtypeStruct((n, t, D_RNN), BF16), jax.ShapeDtypeStruct((n, SUBLANES, D_RNN), F32)],
        scratch_shapes=[pltpu.VMEM((t, bw), F32), pltpu.VMEM((t, bw), F32)],
        compiler_params=_cp("parallel", "parallel"),
        name="rglru",
    )(proj3, proj3, hist8, h0.reshape(n, 1, D_RNN),
      conv_w.reshape(CONV_W, RNN_BLOCKS, bw).transpose(1, 0, 2), vec(conv_b),
      w_a.astype(BF16), vec(b_a), w_i.astype(BF16), vec(b_i), vec(softplus_neg_lam))


def _rope_tables(pos, n_rot, theta):
    half = n_rot // 2
    inv = 1.0 / (theta ** (jnp.arange(half, dtype=F32) * (2.0 / n_rot)))
    ang = pos.astype(F32)[:, None] * inv[None, :]
    cos, sin = jnp.cos(ang), jnp.sin(ang)
    p = pos.shape[0]
    rest = HEAD_DIM - n_rot
    zh = jnp.zeros((p, half), F32)
    c = jnp.concatenate([cos, cos, jnp.ones((p, rest), F32)], axis=1)
    sa = jnp.concatenate([-sin, zh, jnp.zeros((p, rest), F32)], axis=1)
    sb = jnp.concatenate([zh, sin, jnp.zeros((p, rest), F32)], axis=1)
    return c, sa, sb


def _block_expand(nbp, length):
    return (jnp.arange(length, dtype=jnp.int32)[None, :] // NSA_BLOCK
            == jnp.arange(nbp, dtype=jnp.int32)[:, None]).astype(BF16)


def _pad_axis(a, axis, size):
    pad = [(0, 0)] * a.ndim
    pad[axis] = (0, size - a.shape[axis])
    return jnp.pad(a, pad)


def _odd_w_in(w):
    qkv = H_NSA * HEAD_DIM + 6 * G_NSA * HEAD_DIM
    ng = 3 * H_NSA
    gate = w[:, qkv:qkv + ng]
    per = 3 * HPG
    gcols = [_pad_axis(gate[:, g * per:(g + 1) * per], 1, LANES) for g in range(G_NSA)]
    out = jnp.concatenate([w[:, :qkv], w[:, qkv + ng:]] + gcols, axis=1)
    return _pad_axis(out, 1, -(-out.shape[1] // 1024) * 1024)


def kernel(x_prompt, x_sample, mem_prompt, state_ret, cache_dil_k, cache_dil_v, cache_nsa_cmp_k, cache_nsa_cmp_v, cache_nsa_slc_k, cache_nsa_slc_v, cache_nsa_win_k, cache_nsa_win_v, state_lru_h, state_lru_conv, cache_mem_k, cache_mem_v, page_table, ev_w_in, ret_gn_g, dil_q_norm_g, dil_k_norm_g, ev_w_out, od_w_in, nsa_q_norm_g, nsa_k_norm_g, nsa_pw_k, nsa_pw_v, lru_conv_w, lru_conv_b, lru_w_a, lru_b_a, lru_w_i, lru_b_i, lru_lambda, od_w_out, norm_mix_g, norm_mem_g, norm_mlp_g, mem_norm_g, mem_w_q, mem_w_k, mem_w_v, mem_q_norm_g, mem_k_norm_g, mem_w_o, mlp_w1, mlp_w2):
    nb_p, t_p, d = x_prompt.shape
    nb_s, t_s, _ = x_sample.shape
    depth = norm_mix_g.shape[0]
    n_pages = page_table.shape[1]
    past = n_pages * PAGE_SIZE
    rs = SAMPLE_ROWS
    assert t_p % RET_CHUNK == 0 and t_s <= rs and past % PAGE_SIZE == 0 and t_s < NSA_BLOCK
    assert cache_nsa_win_k.shape[2] % KEY_BLOCK == 0 and cache_dil_k.shape[2] % KEY_BLOCK == 0
    hd = HEAD_DIM

    xp = x_prompt.reshape(nb_p * t_p, d)
    xs = _pad_axis(x_sample, 1, rs).reshape(nb_s * rs, d)
    mem2 = mem_prompt.reshape(-1, d)
    n_mem = mem_prompt.shape[1]

    pos_p = jnp.arange(t_p, dtype=jnp.int32)
    pos_s = past + jnp.arange(rs, dtype=jnp.int32)
    tile_s = lambda tabs: tuple(jnp.tile(tb, (nb_s, 1)) for tb in tabs)
    rope_p, rope_s = _rope_tables(pos_p, ROPE_DIMS, ROPE_THETA), _rope_tables(pos_s, ROPE_DIMS, ROPE_THETA)
    ret_p, ret_s = _rope_tables(pos_p, hd, RET_THETA), _rope_tables(pos_s, hd, RET_THETA)
    log_g = jnp.log1p(-jnp.exp2(-5.0 - jnp.arange(H_RET, dtype=F32)))

    outs = {k: [] for k in ("ret_p", "ret_s", "dk_p", "dv_p", "dk_s", "dv_s", "ck_p", "cv_p", "ck_s", "cv_s",
                            "sk_p", "sv_p", "sk_s", "sv_s", "wk_p", "wv_p", "wk_s", "wv_s", "lh_p", "lh_s",
                            "lc_p", "lc_s", "mk_p", "mv_p")}
    new_rows = lambda a, w: a.reshape(nb_s, rs, w)[:, :t_s]

    for layer in range(depth):
        if layer % 2 == 0:
            e = layer // 2
            w_in = ev_w_in[e].astype(BF16)
            w_out = ev_w_out[e].astype(BF16)
            hw = H_RET * hd
            ones = jnp.ones((1, hd), F32)
            gains = jnp.concatenate([dil_k_norm_g[e][None], ones], axis=0)
            sections = [(5, "norm_rope"), (6, "copy")]
            proj = rms_matmul(xp, norm_mix_g[layer], w_in, 1024)
            proj3 = proj.reshape(nb_p, t_p, -1)
            y_ret, s_new = retention(proj3, jnp.zeros((nb_p, H_RET, hd, hd), F32), log_g, ret_gn_g[e], ret_p, RET_CHUNK)
            dk, dv = kv_transform(proj, sections, gains, rope_p, H_DIL, t_p)
            y_dil = dilated_prompt(proj3, 4 * H_RET, dk.reshape(nb_p, t_p, hw), dv.reshape(nb_p, t_p, hw),
                                   dil_q_norm_g[e], rope_p)
            xp = matmul_res(y_ret.reshape(-1, hw), y_dil.reshape(-1, hw), w_out[:hw], w_out[hw:], xp)
            keep = min(DIL_PATTERNS[-1][0], t_p)
            outs["ret_p"].append(s_new)
            outs["dk_p"].append(dk.reshape(nb_p, t_p, H_DIL, hd)[:, t_p - keep:])
            outs["dv_p"].append(dv.reshape(nb_p, t_p, H_DIL, hd)[:, t_p - keep:])
            proj = rms_matmul(xs, norm_mix_g[layer], w_in, 1024)
            proj3 = proj.reshape(nb_s, rs, -1)
            y_ret, s_new = retention(proj3, state_ret.reshape(-1, H_RET, hd, hd), log_g, ret_gn_g[e], ret_s, t_s,
                                     seq_off=e * nb_s)
            dk, dv = kv_transform(proj, sections, gains, tile_s(rope_s), H_DIL, nb_s * rs)
            lc = cache_dil_k.shape[2]
            y_dil = dilated_sample(proj3, 4 * H_RET, cache_dil_k.reshape(-1, lc, hw), cache_dil_v.reshape(-1, lc, hw),
                                   e * nb_s, dk.reshape(nb_s, rs, hw), dv.reshape(nb_s, rs, hw), dil_q_norm_g[e], rope_s)
            xs = matmul_res(y_ret.reshape(-1, hw), y_dil.reshape(-1, hw), w_out[:hw], w_out[hw:], xs)
            outs["ret_s"].append(s_new)
            outs["dk_s"].append(new_rows(dk, hw).reshape(nb_s, t_s, H_DIL, hd))
            outs["dv_s"].append(new_rows(dv, hw).reshape(nb_s, t_s, H_DIL, hd))
        else:
            o = layer // 2
            w_in = _odd_w_in(od_w_in[o]).astype(BF16)
            w_out = od_w_out[o].astype(BF16)
            qw = H_NSA * hd
            gw = G_NSA * hd
            kv_cb = qw // gw
            x_cb = (qw + 6 * gw) // hd
            y_cb = x_cb + D_RNN // hd
            gate_cb = y_cb + D_RNN // hd
            ones = jnp.ones((1, hd), F32)
            kg = nsa_k_norm_g[o]
            gains = jnp.concatenate([kg[0:1], ones, kg[1:2], ones, kg[2:3], ones], axis=0)
            sections = [(kv_cb + i, "norm_rope" if i % 2 == 0 else "copy") for i in range(6)]
            pwk2 = jnp.concatenate([nsa_pw_k[o]] * G_NSA, axis=1)
            pwv2 = jnp.concatenate([nsa_pw_v[o]] * G_NSA, axis=1)
            sp = jax.nn.softplus(-lru_lambda[o].astype(F32))
            lru_w = (lru_conv_w[o], lru_conv_b[o], lru_w_a[o], lru_b_a[o], lru_w_i[o], lru_b_i[o], sp)
            proj = rms_matmul(xp, norm_mix_g[layer], w_in, 1024)
            proj3 = proj.reshape(nb_p, t_p, -1)
            kcm, vcm, ksl, vsl, kwn, vwn = kv_transform(proj, sections, gains, rope_p, G_NSA, t_p)
            r3 = lambda a: a.reshape(nb_p, t_p, gw)
            nb = t_p // NSA_BLOCK
            nbp = -(-nb // LANES) * LANES
            kc, vc = compress_rows(r3(kcm), r3(vcm), pwk2, pwv2)
            o_nsa = nsa_prompt(proj3, gate_cb, _pad_axis(kc, 1, nbp), _pad_axis(vc, 1, nbp), r3(ksl), r3(vsl),
                               r3(kwn), r3(vwn), _block_expand(nbp, t_p), nsa_q_norm_g[o], rope_p, nb)
            y_rnn, tail = rglru(proj3, x_cb, y_cb, jnp.zeros((nb_p, SUBLANES, D_RNN), F32),
                                jnp.zeros((nb_p, D_RNN), F32), *lru_w)
            xp = matmul_res(o_nsa.reshape(-1, qw), y_rnn.reshape(-1, D_RNN), w_out[:qw], w_out[qw:], xp)
            r4 = lambda a: a.reshape(nb_p, t_p, G_NSA, hd)
            keep = min(NSA_WINDOW, t_p)
            for key, val in (("ck_p", r4(kcm)), ("cv_p", r4(vcm)), ("sk_p", r4(ksl)), ("sv_p", r4(vsl)),
                             ("wk_p", r4(kwn)[:, t_p - keep:]), ("wv_p", r4(vwn)[:, t_p - keep:])):
                outs[key].append(val)
            outs["lh_p"].append(tail[:, SUBLANES - 1])
            lx = proj3[:, :, x_cb * hd:x_cb * hd + D_RNN]
            outs["lc_p"].append(jnp.concatenate([jnp.zeros((nb_p, CONV_W - 1, D_RNN), F32), lx], axis=1)[:, -(CONV_W - 1):])
            proj = rms_matmul(xs, norm_mix_g[layer], w_in, 1024)
            proj3 = proj.reshape(nb_s, rs, -1)
            kcm, vcm, ksl, vsl, kwn, vwn = kv_transform(proj, sections, gains, tile_s(rope_s), G_NSA, nb_s * rs)
            r3 = lambda a: a.reshape(nb_s, rs, gw)
            nph = cache_nsa_cmp_k.shape[1]
            pool = lambda a: a.reshape(-1, PAGE_SIZE, gw)
            pps = math.gcd(n_pages, 16)
            nb = (past + t_s) // NSA_BLOCK
            nbp = -(-nb // LANES) * LANES
            kc, vc = compress_pages(pool(cache_nsa_cmp_k), pool(cache_nsa_cmp_v), o * nph, page_table, pwk2, pwv2, pps)
            lw = cache_nsa_win_k.shape[2]
            o_nsa = nsa_sample(proj3, gate_cb, _pad_axis(kc, 1, nbp), _pad_axis(vc, 1, nbp),
                               pool(cache_nsa_slc_k), pool(cache_nsa_slc_v), o * nph, page_table, r3(ksl), r3(vsl),
                               cache_nsa_win_k.reshape(-1, lw, gw), cache_nsa_win_v.reshape(-1, lw, gw), o * nb_s,
                               r3(kwn), r3(vwn), _block_expand(nbp, past), nsa_q_norm_g[o], rope_s, pps)
            hist8 = _pad_axis(state_lru_conv[o].astype(F32), 1, SUBLANES)
            y_rnn, tail = rglru(proj3, x_cb, y_cb, hist8, state_lru_h[o].astype(F32), *lru_w)
            xs = matmul_res(o_nsa.reshape(-1, qw), y_rnn.reshape(-1, D_RNN), w_out[:qw], w_out[qw:], xs)
            for key, val in (("ck_s", kcm), ("cv_s", vcm), ("sk_s", ksl), ("sv_s", vsl), ("wk_s", kwn), ("wv_s", vwn)):
                outs[key].append(new_rows(val, gw).reshape(nb_s, t_s, G_NSA, hd))
            outs["lh_s"].append(tail[:, t_s - 1])
            lx = proj3[:, :t_s, x_cb * hd:x_cb * hd + D_RNN]
            outs["lc_s"].append(jnp.concatenate([state_lru_conv[o].astype(F32), lx], axis=1)[:, -(CONV_W - 1):])

        dm = H_MEM * hd
        w_kv = jnp.concatenate([mem_w_k[layer], mem_w_v[layer]], axis=1).astype(BF16)
        mkv = rms_matmul(mem2, mem_norm_g[layer], w_kv, dm)
        gains = jnp.concatenate([mem_k_norm_g[layer][None], jnp.ones((1, hd), F32)], axis=0)
        mk, mv = kv_transform(mkv, [(0, "norm"), (1, "copy")], gains, rope_p, H_MEM, n_mem)
        mk3, mv3 = mk.reshape(nb_p, n_mem, dm), mv.reshape(nb_p, n_mem, dm)
        outs["mk_p"].append(mk3.reshape(nb_p, n_mem, H_MEM, hd))
        outs["mv_p"].append(mv3.reshape(nb_p, n_mem, H_MEM, hd))
        wq, wo = mem_w_q[layer].astype(BF16), mem_w_o[layer].astype(BF16)
        xp = mem_attn_block(xp, norm_mem_g[layer], wq, mem_q_norm_g[layer], mk3, mv3, wo, t_p)
        xs = mem_attn_block(xs, norm_mem_g[layer], wq, mem_q_norm_g[layer], cache_mem_k.reshape(-1, n_mem, dm),
                            cache_mem_v.reshape(-1, n_mem, dm), wo, rs, seq_off=layer * nb_s)
        w1, w2 = mlp_w1[layer].astype(BF16), mlp_w2[layer].astype(BF16)
        xp = mlp_block(xp, norm_mlp_g[layer], w1, w2)
        xs = mlp_block(xs, norm_mlp_g[layer], w1, w2)

    st = lambda k: jnp.stack(outs[k])
    return (xp.reshape(nb_p, t_p, d), xs.reshape(nb_s, rs, d)[:, :t_s], st("ret_p"), st("ret_s"),
            st("dk_p"), st("dv_p"), st("dk_s"), st("dv_s"),
            st("ck_p"), st("cv_p"), st("ck_s"), st("cv_s"),
            st("sk_p"), st("sv_p"), st("sk_s"), st("sv_s"),
            st("wk_p"), st("wv_p"), st("wk_s"), st("wv_s"),
            st("lh_p"), st("lh_s"), st("lc_p"), st("lc_s"),
            st("mk_p"), st("mv_p"))
```

```python
import functools
import math

import jax
import jax.numpy as jnp
from jax import lax
from jax.experimental import pallas as pl
from jax.experimental.pallas import tpu as pltpu

F32 = jnp.float32
BF16 = jnp.bfloat16

HEAD_DIM = 128
ROPE_DIMS = HEAD_DIM // 4
ROPE_THETA = 500000.0
NORM_EPS = 1e-6
H_RET = 8
RET_CHUNK = 128
RET_THETA = 10000.0
H_DIL = 8
DIL_PATTERNS = ((128, 1), (512, 4), (2048, 16))
H_NSA = 8
G_NSA = 2
HPG = H_NSA // G_NSA
NSA_BLOCK = 64
NSA_TOPK = 16
NSA_WINDOW = 512
D_RNN = 1024
RNN_BLOCKS = 8
CONV_W = 4
LRU_C = 8.0
H_MEM = 4
PAGE_SIZE = 128

LANES = 128
SUBLANES = 8
KEY_BLOCK = 128
SAMPLE_ROWS = 8
VMEM_LIMIT = 48 * 1024 * 1024
NEG = -1e30
SCALE = HEAD_DIM ** -0.5


def _cp(*sem):
    return pltpu.CompilerParams(dimension_semantics=sem, vmem_limit_bytes=VMEM_LIMIT)


def _tile(n, pref):
    t = min(n, pref)
    while n % t:
        t -= SUBLANES
    return t


def _rms(x, g):
    return x * lax.rsqrt(jnp.mean(x * x, axis=-1, keepdims=True) + NORM_EPS) * g


def _rope(x, c, sa, sb, half):
    return x * c + pltpu.roll(x, LANES - half, 1) * sa + pltpu.roll(x, half, 1) * sb


def _dot(a, b):
    return jnp.dot(a, b, preferred_element_type=F32)


def _dot_nt(a, b):
    return lax.dot_general(a, b, (((1,), (1,)), ((), ())), preferred_element_type=F32)


def _pad_rows(x, rows):
    if x.shape[0] == rows:
        return x
    return jnp.concatenate([x, jnp.zeros((rows - x.shape[0], x.shape[1]), x.dtype)], axis=0)


def _softmax_step(s, w, vb, m, l, acc):
    s = jnp.where(w > 0, s, NEG)
    m_new = jnp.maximum(m, jnp.max(s, axis=-1, keepdims=True))
    alpha = jnp.exp(m - m_new)
    p = jnp.exp(s - m_new) * w
    l = alpha * l + jnp.sum(p, axis=-1, keepdims=True)
    acc = alpha * acc + _dot(p.astype(BF16), vb)
    return m_new, l, acc


def _softmax_init(rows):
    return (jnp.full((rows, 1), NEG, F32), jnp.zeros((rows, 1), F32), jnp.zeros((rows, HEAD_DIM), F32))


def _softmax_done(l, acc):
    return acc / jnp.maximum(l, 1e-30)


def _dil_weight(d):
    w = jnp.zeros(d.shape, F32)
    for window, dil in DIL_PATTERNS:
        hit = jnp.where(d <= window, 1.0, 0.0)
        if dil > 1:
            rem = (d & (dil - 1)) if dil & (dil - 1) == 0 else lax.rem(d, dil)
            hit = jnp.where(rem == 0, hit, 0.0)
        w = w + hit
    return jnp.where(d >= 0, w, 0.0)


def _iota(shape, dim):
    return lax.broadcasted_iota(jnp.int32, shape, dim)


def _rms_matmul_kernel(x_ref, g_ref, w_ref, o_ref, xn_ref):
    @pl.when(pl.program_id(1) == 0)
    def _():
        xn_ref[...] = _rms(x_ref[...], g_ref[...]).astype(BF16)

    o_ref[...] = _dot(xn_ref[...], w_ref[...])


def rms_matmul(x, g, w, tn):
    m, d = x.shape
    n = w.shape[1]
    tm = _tile(m, 512)
    return pl.pallas_call(
        _rms_matmul_kernel,
        grid=(m // tm, n // tn),
        in_specs=[pl.BlockSpec((tm, d), lambda i, j: (i, 0)),
                  pl.BlockSpec((1, d), lambda i, j: (0, 0)),
                  pl.BlockSpec((d, tn), lambda i, j: (0, j))],
        out_specs=pl.BlockSpec((tm, tn), lambda i, j: (i, j)),
        out_shape=jax.ShapeDtypeStruct((m, n), F32),
        scratch_shapes=[pltpu.VMEM((tm, d), BF16)],
        compiler_params=_cp("parallel", "arbitrary"),
        name="rms_matmul",
    )(x, g.reshape(1, d), w)


def _matmul_res_kernel(a1_ref, a2_ref, w1_ref, w2_ref, r_ref, o_ref):
    o_ref[...] = r_ref[...] + _dot(a1_ref[...], w1_ref[...]) + _dot(a2_ref[...], w2_ref[...])


def matmul_res(a1, a2, w1, w2, res):
    m, k1 = a1.shape
    k2 = a2.shape[1]
    n = w1.shape[1]
    tm = _tile(m, 512)
    tn = _tile(n, 1024)
    return pl.pallas_call(
        _matmul_res_kernel,
        grid=(m // tm, n // tn),
        in_specs=[pl.BlockSpec((tm, k1), lambda i, j: (i, 0)),
                  pl.BlockSpec((tm, k2), lambda i, j: (i, 0)),
                  pl.BlockSpec((k1, tn), lambda i, j: (0, j)),
                  pl.BlockSpec((k2, tn), lambda i, j: (0, j)),
                  pl.BlockSpec((tm, tn), lambda i, j: (i, j))],
        out_specs=pl.BlockSpec((tm, tn), lambda i, j: (i, j)),
        out_shape=jax.ShapeDtypeStruct((m, n), F32),
        compiler_params=_cp("parallel", "parallel"),
        name="matmul_res",
    )(a1, a2, w1, w2, res)


def _mlp_kernel(x_ref, g_ref, w1_ref, w2_ref, o_ref, xn_ref, acc_ref):
    f = pl.program_id(1)

    @pl.when(f == 0)
    def _():
        xn_ref[...] = _rms(x_ref[...], g_ref[...]).astype(BF16)
        acc_ref[...] = jnp.zeros_like(acc_ref)

    hid = jnp.maximum(_dot(xn_ref[...], w1_ref[...]), 0.0)
    acc_ref[...] += _dot((hid * hid).astype(BF16), w2_ref[...])

    @pl.when(f == pl.num_programs(1) - 1)
    def _():
        o_ref[...] = x_ref[...] + acc_ref[...]


def mlp_block(x, g, w1, w2):
    m, d = x.shape
    ff = w1.shape[1]
    tm = _tile(m, 512)
    tf = _tile(ff, 512)
    return pl.pallas_call(
        _mlp_kernel,
        grid=(m // tm, ff // tf),
        in_specs=[pl.BlockSpec((tm, d), lambda i, f: (i, 0)),
                  pl.BlockSpec((1, d), lambda i, f: (0, 0)),
                  pl.BlockSpec((d, tf), lambda i, f: (0, f)),
                  pl.BlockSpec((tf, d), lambda i, f: (f, 0))],
        out_specs=pl.BlockSpec((tm, d), lambda i, f: (i, 0)),
        out_shape=jax.ShapeDtypeStruct((m, d), F32),
        scratch_shapes=[pltpu.VMEM((tm, d), BF16), pltpu.VMEM((tm, d), F32)],
        compiler_params=_cp("parallel", "arbitrary"),
        name="mlp_block",
    )(x, g.reshape(1, d), w1, w2)


def _mem_attn_kernel(x_ref, g_ref, wq_ref, qg_ref, k_ref, v_ref, wo_ref, o_ref, oh_ref, *, nb, seg):
    x = x_ref[...]
    q = _dot(_rms(x, g_ref[...]).astype(BF16), wq_ref[...])
    qg = qg_ref[...]
    for b in range(nb):
        for h in range(H_MEM):
            cols = slice(h * HEAD_DIM, (h + 1) * HEAD_DIM)
            rows = slice(b * seg, (b + 1) * seg)
            qh = (_rms(q[rows, cols], qg) * SCALE).astype(BF16)
            s = _dot_nt(qh, k_ref[b][:, cols].astype(BF16))
            p = jnp.exp(s - jnp.max(s, axis=-1, keepdims=True))
            o = _dot(p.astype(BF16), v_ref[b][:, cols].astype(BF16)) / jnp.sum(p, axis=-1, keepdims=True)
            oh_ref[rows, cols] = o.astype(BF16)
    o_ref[...] = x + _dot(oh_ref[...], wo_ref[...])


def mem_attn_block(x, g, wq, qg, k, v, wo, rows_per_seq, seq_off=0):
    m, d = x.shape
    n_mem, dm = k.shape[1], k.shape[2]
    if rows_per_seq >= 128:
        tm = _tile(rows_per_seq, 512)
        nb, seg = 1, tm
        kv_map = lambda i: (seq_off + i // (rows_per_seq // tm), 0, 0)
    else:
        nb = _tile(m // rows_per_seq, 8)
        tm, seg = nb * rows_per_seq, rows_per_seq
        assert seq_off % nb == 0
        kv_map = lambda i: (seq_off // nb + i, 0, 0)
    return pl.pallas_call(
        functools.partial(_mem_attn_kernel, nb=nb, seg=seg),
        grid=(m // tm,),
        in_specs=[pl.BlockSpec((tm, d), lambda i: (i, 0)),
                  pl.BlockSpec((1, d), lambda i: (0, 0)),
                  pl.BlockSpec((d, dm), lambda i: (0, 0)),
                  pl.BlockSpec((1, HEAD_DIM), lambda i: (0, 0)),
                  pl.BlockSpec((nb, n_mem, dm), kv_map),
                  pl.BlockSpec((nb, n_mem, dm), kv_map),
                  pl.BlockSpec((dm, d), lambda i: (0, 0))],
        out_specs=pl.BlockSpec((tm, d), lambda i: (i, 0)),
        out_shape=jax.ShapeDtypeStruct((m, d), F32),
        scratch_shapes=[pltpu.VMEM((tm, dm), BF16)],
        compiler_params=_cp("parallel"),
        name="mem_attn_block",
    )(x, g.reshape(1, d), wq, qg.reshape(1, HEAD_DIM), k, v, wo)


def _kv_transform_kernel(*refs, modes, heads):
    ns = len(modes)
    x_refs = refs[:ns]
    g_ref, c_ref, sa_ref, sb_ref = refs[ns:ns + 4]
    o_refs = refs[ns + 4:]
    for s in range(ns):
        for h in range(heads):
            cols = slice(h * HEAD_DIM, (h + 1) * HEAD_DIM)
            x = x_refs[s][:, cols]
            if modes[s] != "copy":
                x = _rms(x, g_ref[s:s + 1, :])
            if modes[s] == "norm_rope":
                x = _rope(x, c_ref[...], sa_ref[...], sb_ref[...], ROPE_DIMS // 2)
            o_refs[s][:, cols] = x


def kv_transform(proj, sections, gains, tables, heads, pos_period):
    m = proj.shape[0]
    width = heads * HEAD_DIM
    tr = _tile(pos_period, 512)
    nper = pos_period // tr
    modes = tuple(mode for _, mode in sections)
    in_specs = [pl.BlockSpec((tr, width), functools.partial(lambda i, cb: (i, cb), cb=cb)) for cb, _ in sections]
    in_specs.append(pl.BlockSpec(gains.shape, lambda i: (0, 0)))
    in_specs += [pl.BlockSpec((tr, HEAD_DIM), lambda i: (i % nper, 0))] * 3
    outs = pl.pallas_call(
        functools.partial(_kv_transform_kernel, modes=modes, heads=heads),
        grid=(m // tr,),
        in_specs=in_specs,
        out_specs=[pl.BlockSpec((tr, width), lambda i: (i, 0))] * len(sections),
        out_shape=[jax.ShapeDtypeStruct((m, width), F32)] * len(sections),
        compiler_params=_cp("parallel"),
        name="kv_transform",
    )(*([proj] * len(sections)), gains, *tables)
    return outs


def _retention_kernel(q_ref, k_ref, v_ref, rg_ref, s0_ref, lg_ref, gn_ref, c_ref, sa_ref, sb_ref,
                      y_ref, s_ref, st_ref, *, rows, c_true, hps):
    ci = pl.program_id(2)
    cc = RET_CHUNK

    @pl.when(ci == 0)
    def _():
        st_ref[...] = s0_ref[0]

    c, sa, sb = c_ref[...], sa_ref[...], sb_ref[...]
    ii = _iota((cc, 1), 0).astype(F32)
    diff = ii - _iota((1, cc), 1).astype(F32)
    for h in range(hps):
        cols = slice(h * HEAD_DIM, (h + 1) * HEAD_DIM)
        lg = lg_ref[h]
        q = _pad_rows(_rope(q_ref[0, :, cols], c, sa, sb, HEAD_DIM // 2) * SCALE, cc)
        k = _pad_rows(_rope(k_ref[0, :, cols], c, sa, sb, HEAD_DIM // 2), cc)
        vb = _pad_rows(v_ref[0, :, cols], cc).astype(BF16)
        qb = q.astype(BF16)
        decay = jnp.where(diff >= 0, jnp.exp(jnp.maximum(diff, 0.0) * lg), 0.0)
        inner = _dot_nt(qb, k.astype(BF16)) * decay
        st = st_ref[h]
        o = _dot(inner.astype(BF16), vb) + _dot(qb, st.astype(BF16)) * jnp.exp((ii + 1.0) * lg)
        kd = jnp.where(ii < c_true, k * jnp.exp((c_true - 1.0 - ii) * lg), 0.0)
        st_ref[h] = jnp.exp(c_true * lg) * st + _dot(kd.T.astype(BF16), vb)
        o = o[:rows]
        mu = jnp.mean(o, axis=-1, keepdims=True)
        var = jnp.mean(jnp.square(o - mu), axis=-1, keepdims=True)
        y = (o - mu) * lax.rsqrt(var + NORM_EPS) * gn_ref[h]
        rg = rg_ref[0, :, cols]
        y_ref[0, :, cols] = (y * (rg * jax.nn.sigmoid(rg))).astype(BF16)

    @pl.when(ci == pl.num_programs(2) - 1)
    def _():
        s_ref[0] = st_ref[...]


def retention(proj3, s0, log_g, gn_g, tables, c_true, seq_off=0):
    n, t, _ = proj3.shape
    rows = min(t, RET_CHUNK)
    nc = t // rows
    h = H_RET
    hps = 4
    ng = h // hps
    wid = hps * HEAD_DIM
    blk = lambda sec: pl.BlockSpec((1, rows, wid), functools.partial(lambda b, hg, c, sec: (b, c, sec * ng + hg), sec=sec))
    tab = pl.BlockSpec((rows, HEAD_DIM), lambda b, hg, c: (c, 0))
    y, s_new = pl.pallas_call(
        functools.partial(_retention_kernel, rows=rows, c_true=float(c_true), hps=hps),
        grid=(n, ng, nc),
        in_specs=[blk(0), blk(1), blk(2), blk(3),
                  pl.BlockSpec((1, hps, HEAD_DIM, HEAD_DIM), lambda b, hg, c: (seq_off + b, hg, 0, 0)),
                  pl.BlockSpec((hps, 1, HEAD_DIM), lambda b, hg, c: (hg, 0, 0)),
                  pl.BlockSpec((hps, 1, HEAD_DIM), lambda b, hg, c: (hg, 0, 0)),
                  tab, tab, tab],
        out_specs=[pl.BlockSpec((1, rows, wid), lambda b, hg, c: (b, c, hg)),
                   pl.BlockSpec((1, hps, HEAD_DIM, HEAD_DIM), lambda b, hg, c: (b, hg, 0, 0))],
        out_shape=[jax.ShapeDtypeStruct((n, t, h * HEAD_DIM), BF16),
                   jax.ShapeDtypeStruct((n, h, HEAD_DIM, HEAD_DIM), F32)],
        scratch_shapes=[pltpu.VMEM((hps, HEAD_DIM, HEAD_DIM), F32)],
        compiler_params=_cp("parallel", "parallel", "arbitrary"),
        name="retention",
    )(proj3, proj3, proj3, proj3, s0,
      jnp.broadcast_to(log_g[:, None, None], (h, 1, HEAD_DIM)), gn_g.reshape(h, 1, HEAD_DIM), *tables)
    return y, s_new


def _rows(start, size, stride):
    return pl.ds(start, size) if stride == 1 else pl.ds(start, size, stride=stride)


def _dil_prompt_kernel(q_ref, k_ref, v_ref, qg_ref, c_ref, sa_ref, sb_ref, o_ref, qs_ref, m_ref, l_ref, acc_ref, *, t):
    chunk = _tile(t, 512)
    for c in range(t // chunk):
        r = slice(c * chunk, (c + 1) * chunk)
        q = _rope(_rms(q_ref[0, r, :], qg_ref[...]), c_ref[r, :], sa_ref[r, :], sb_ref[r, :], ROPE_DIMS // 2)
        qs_ref[r, :] = q * SCALE

    def block(dil, start, bq, has_prev):
        nk = bq + (KEY_BLOCK if has_prev else 0)
        kstart = start - dil * KEY_BLOCK if has_prev else start
        qb = qs_ref[_rows(start, bq, dil), :].astype(BF16)
        kb = k_ref[0, _rows(kstart, nk, dil), :].astype(BF16)
        vb = v_ref[0, _rows(kstart, nk, dil), :].astype(BF16)
        rel = (nk - bq) + _iota((bq, nk), 0) - _iota((bq, nk), 1)
        ok = jnp.where(rel >= 0, jnp.where(rel <= KEY_BLOCK, 1.0, 0.0), 0.0)
        s = jnp.where(ok > 0, _dot_nt(qb, kb), NEG)
        m = jnp.max(s, axis=-1, keepdims=True)
        p = jnp.exp(s - m) * ok
        return m, jnp.sum(p, axis=-1, keepdims=True), _dot(p.astype(BF16), vb)

    def store(dil, start, bq, m, l, acc, first):
        rows = _rows(start, bq, dil)
        m = jnp.broadcast_to(m, (bq, HEAD_DIM))
        l = jnp.broadcast_to(l, (bq, HEAD_DIM))
        if not first:
            m0 = m_ref[rows, :]
            mx = jnp.maximum(m0, m)
            e0, e1 = jnp.exp(m0 - mx), jnp.exp(m - mx)
            m, l, acc = mx, e0 * l_ref[rows, :] + e1 * l, e0 * acc_ref[rows, :] + e1 * acc
        m_ref[rows, :] = m
        l_ref[rows, :] = l
        acc_ref[rows, :] = acc

    for pi, (window, dil) in enumerate(DIL_PATTERNS):
        length = t // dil
        bq = min(length, KEY_BLOCK)
        first = pi == 0

        def class_body(r, _, dil=dil, bq=bq, length=length, first=first):
            store(dil, r, bq, *block(dil, r, bq, False), first)

            def blk_body(i, _):
                start = r + dil * KEY_BLOCK * i
                if dil == 1:
                    start = pl.multiple_of(start, KEY_BLOCK)
                store(dil, start, bq, *block(dil, start, bq, True), first)
                return 0

            trips = length // bq - 1
            if trips > 0:
                lax.fori_loop(1, trips + 1, blk_body, 0, unroll=trips if trips <= 4 else (3 if trips % 3 == 0 else 1))
            return 0

        if dil == 1:
            class_body(0, 0)
        else:
            lax.fori_loop(0, dil, class_body, 0, unroll=4 if (length == bq and dil % 4 == 0) else 1)

    for c in range(t // chunk):
        r = slice(c * chunk, (c + 1) * chunk)
        o_ref[0, r, :] = (acc_ref[r, :] / l_ref[r, :]).astype(BF16)


def dilated_prompt(proj3, q_cb, k3, v3, qg, tables):
    n, t, _ = proj3.shape
    for window, dil in DIL_PATTERNS:
        length = t // dil
        assert window // dil == KEY_BLOCK and t % dil == 0
        assert length % KEY_BLOCK == 0 or (length < KEY_BLOCK and length % SUBLANES == 0)
    tab = pl.BlockSpec((t, HEAD_DIM), lambda b, h: (0, 0))
    seq = lambda cb: pl.BlockSpec((1, t, HEAD_DIM), functools.partial(lambda b, h, cb: (b, 0, cb + h), cb=cb))
    return pl.pallas_call(
        functools.partial(_dil_prompt_kernel, t=t),
        grid=(n, H_DIL),
        in_specs=[seq(q_cb), seq(0), seq(0), pl.BlockSpec((1, HEAD_DIM), lambda b, h: (0, 0)), tab, tab, tab],
        out_specs=seq(0),
        out_shape=jax.ShapeDtypeStruct((n, t, H_DIL * HEAD_DIM), BF16),
        scratch_shapes=[pltpu.VMEM((t, HEAD_DIM), F32)] * 4,
        compiler_params=_cp("parallel", "parallel"),
        name="dilated_prompt",
    )(proj3, k3, v3, qg.reshape(1, HEAD_DIM), *tables)


def _dil_sample_kernel(q_ref, kc_ref, vc_ref, kn_ref, vn_ref, qg_ref, c_ref, sa_ref, sb_ref, o_ref,
                       qb_ref, m_ref, l_ref, acc_ref, *, lc, ck):
    ci = pl.program_id(1)
    rows = q_ref.shape[1]

    @pl.when(ci == 0)
    def _():
        for h in range(H_DIL):
            cols = slice(h * HEAD_DIM, (h + 1) * HEAD_DIM)
            q = _rope(_rms(q_ref[0, :, cols], qg_ref[...]), c_ref[...], sa_ref[...], sb_ref[...], ROPE_DIMS // 2)
            qb_ref[h] = q * SCALE
            m_ref[h], l_ref[h], acc_ref[h] = _softmax_init(rows)

    w_c = _dil_weight(lc + _iota((rows, ck), 0) - (ci * ck + _iota((rows, ck), 1)))
    for h in range(H_DIL):
        keys = pl.ds(h, ck, stride=H_DIL)
        m_ref[h], l_ref[h], acc_ref[h] = _softmax_step(
            _dot_nt(qb_ref[h].astype(BF16), kc_ref[0, keys, :].astype(BF16)), w_c, vc_ref[0, keys, :].astype(BF16),
            m_ref[h], l_ref[h], acc_ref[h])

    @pl.when(ci == pl.num_programs(1) - 1)
    def _():
        w_n = _dil_weight(_iota((rows, KEY_BLOCK), 0) - _iota((rows, KEY_BLOCK), 1))
        for h in range(H_DIL):
            cols = slice(h * HEAD_DIM, (h + 1) * HEAD_DIM)
            kb = _pad_rows(kn_ref[0, :, cols], KEY_BLOCK).astype(BF16)
            vb = _pad_rows(vn_ref[0, :, cols], KEY_BLOCK).astype(BF16)
            _, l, acc = _softmax_step(_dot_nt(qb_ref[h].astype(BF16), kb), w_n, vb, m_ref[h], l_ref[h], acc_ref[h])
            o_ref[0, :, cols] = _softmax_done(l, acc).astype(BF16)


def dilated_sample(proj3, q_cb, kc3, vc3, seq_off, kn3, vn3, qg, tables):
    n, rows, _ = proj3.shape
    lc = kc3.shape[1] // H_DIL
    wid = H_DIL * HEAD_DIM
    ck = lc // 2 if lc % (2 * KEY_BLOCK) == 0 else lc
    assert (q_cb * HEAD_DIM) % wid == 0
    tab = pl.BlockSpec((rows, HEAD_DIM), lambda b, c: (0, 0))
    cache = pl.BlockSpec((1, ck * H_DIL, HEAD_DIM), lambda b, c: (seq_off + b, c, 0))
    new = pl.BlockSpec((1, rows, wid), lambda b, c: (b, 0, 0))
    return pl.pallas_call(
        functools.partial(_dil_sample_kernel, lc=lc, ck=ck),
        grid=(n, lc // ck),
        in_specs=[pl.BlockSpec((1, rows, wid), lambda b, c: (b, 0, q_cb * HEAD_DIM // wid)), cache, cache, new, new,
                  pl.BlockSpec((1, HEAD_DIM), lambda b, c: (0, 0)), tab, tab, tab],
        out_specs=new,
        out_shape=jax.ShapeDtypeStruct((n, rows, wid), BF16),
        scratch_shapes=[pltpu.VMEM((H_DIL, rows, HEAD_DIM), F32), pltpu.VMEM((H_DIL, rows, 1), F32),
                        pltpu.VMEM((H_DIL, rows, 1), F32), pltpu.VMEM((H_DIL, rows, HEAD_DIM), F32)],
        compiler_params=_cp("parallel", "arbitrary"),
        name="dilated_sample",
    )(proj3, kc3, vc3, kn3, vn3, qg.reshape(1, HEAD_DIM), *tables)


def _nsa_q(q_ref, g, qg_ref, c_ref, sa_ref, sb_ref):
    qs = []
    for hl in range(HPG):
        col = (g * HPG + hl) * HEAD_DIM
        x = _rms(q_ref[0][:, col:col + HEAD_DIM], qg_ref[...])
        qs.append(_rope(x, c_ref[...], sa_ref[...], sb_ref[...], ROPE_DIMS // 2) * SCALE)
    return jnp.concatenate(qs, axis=0).astype(BF16)


def _nsa_compressed(q4, kc, vc, qpos, nb):
    rows = qpos.shape[0]
    nbp = kc.shape[0]
    blk = _iota((rows, nbp), 1)
    cmask = jnp.where((blk + 1) * NSA_BLOCK - 1 <= qpos, 1.0, 0.0)
    cmask4 = jnp.concatenate([cmask] * HPG, axis=0)
    s = jnp.where(cmask4 > 0, _dot_nt(q4, kc.astype(BF16)), NEG)
    p = jnp.exp(s - jnp.max(s, axis=-1, keepdims=True)) * cmask4
    p = p / jnp.maximum(jnp.sum(p, axis=-1, keepdims=True), 1e-30)
    o_c = _dot(p.astype(BF16), vc.astype(BF16))
    imp = p[0:rows]
    for hl in range(1, HPG):
        imp = imp + p[hl * rows:(hl + 1) * rows]
    cur = jnp.right_shift(qpos, NSA_BLOCK.bit_length() - 1)
    cand = blk < cur
    imp = jnp.where(cand, imp, -jnp.inf)
    rank = jnp.zeros((rows, nbp), F32)
    for b in range(nb):
        col = imp[:, b:b + 1]
        tie = jnp.where(col == imp, jnp.where(blk > b, 1.0, 0.0), 0.0)
        rank = rank + jnp.where(col > imp, 1.0, tie)
    sel = jnp.where(cand, jnp.where(rank < min(NSA_TOPK, nb), 1.0, 0.0), 0.0)
    sel = jnp.where(blk == cur, 1.0, sel)
    return o_c, sel


def _nsa_combine(gates, g, o_c, o_s, o_w, rows):
    outs = []
    for hl in range(HPG):
        r = slice(hl * rows, (hl + 1) * rows)
        col = 3 * hl
        outs.append(gates[:, col:col + 1] * o_c[r] + gates[:, col + 1:col + 2] * o_s[r]
                    + gates[:, col + 2:col + 3] * o_w[r])
    return outs


def _nsa_prompt_kernel(q_ref, gt_ref, kc_ref, vc_ref, ks_ref, vs_ref, kw_ref, vw_ref, e_ref,
                       qg_ref, c_ref, sa_ref, sb_ref, o_ref, selx_ref, s_ref, mx_ref, ls_ref, acc_ref, *, tq, nb, t):
    qi = pl.program_id(2)
    q4 = _nsa_q(q_ref, 0, qg_ref, c_ref, sa_ref, sb_ref)
    qpos = qi * tq + _iota((tq, 1), 0)
    o_c, sel = _nsa_compressed(q4, kc_ref[0], vc_ref[0], qpos, nb)
    selx_ref[...] = _dot(sel.astype(BF16), e_ref[...])

    ck = _tile(t, 4 * KEY_BLOCK)
    nchunks = lax.div(qi * tq + (tq + ck - 1), ck)
    lane = _iota((tq, ck), 1)
    mx_ref[...] = jnp.full(mx_ref.shape, NEG, F32)

    def scores(c, _):
        off = pl.multiple_of(c * ck, ck)
        w = jnp.where(off + lane <= qpos, selx_ref[:, pl.ds(off, ck)], 0.0)
        s = _dot_nt(q4, ks_ref[0, pl.ds(off, ck), :].astype(BF16))
        s = jnp.where(jnp.concatenate([w] * HPG, axis=0) > 0, s, NEG)
        s_ref[:, pl.ds(off, ck)] = s
        part = s[:, :KEY_BLOCK]
        for j in range(1, ck // KEY_BLOCK):
            part = jnp.maximum(part, s[:, j * KEY_BLOCK:(j + 1) * KEY_BLOCK])
        mx_ref[...] = jnp.maximum(mx_ref[...], part)
        return 0

    lax.fori_loop(0, nchunks, scores, 0)
    m = jnp.max(mx_ref[...], axis=-1, keepdims=True)
    ls_ref[...] = jnp.zeros(ls_ref.shape, F32)
    acc_ref[...] = jnp.zeros(acc_ref.shape, F32)

    def values(c, _):
        off = pl.multiple_of(c * ck, ck)
        s = s_ref[:, pl.ds(off, ck)]
        p = jnp.where(s > 0.5 * NEG, jnp.exp(s - m), 0.0)
        part = p[:, :KEY_BLOCK]
        for j in range(1, ck // KEY_BLOCK):
            part = part + p[:, j * KEY_BLOCK:(j + 1) * KEY_BLOCK]
        ls_ref[...] += part
        acc_ref[...] += _dot(p.astype(BF16), vs_ref[0, pl.ds(off, ck), :].astype(BF16))
        return 0

    lax.fori_loop(0, nchunks, values, 0)
    o_s = acc_ref[...] / jnp.maximum(jnp.sum(ls_ref[...], axis=-1, keepdims=True), 1e-30)

    span = min(NSA_WINDOW + KEY_BLOCK, t)
    first = jnp.clip(qi - NSA_WINDOW // KEY_BLOCK, 0, (t - span) // KEY_BLOCK)
    off = pl.multiple_of(first * KEY_BLOCK, KEY_BLOCK)
    d = qpos - (off + _iota((tq, span), 1))
    w = jnp.where(d >= 0, jnp.where(d < NSA_WINDOW, 1.0, 0.0), 0.0)
    _, l, acc = _softmax_step(_dot_nt(q4, kw_ref[0, pl.ds(off, span), :].astype(BF16)),
                              jnp.concatenate([w] * HPG, axis=0), vw_ref[0, pl.ds(off, span), :].astype(BF16),
                              *_softmax_init(HPG * tq))
    o_w = _softmax_done(l, acc)
    gates = jax.nn.sigmoid(gt_ref[0])
    for hl, o in enumerate(_nsa_combine(gates, 0, o_c, o_s, o_w, tq)):
        o_ref[0, :, hl * HEAD_DIM:(hl + 1) * HEAD_DIM] = o.astype(BF16)


def nsa_prompt(proj3, gate_cb, kc3, vc3, ks3, vs3, kw3, vw3, expand, qg, tables, nb):
    n, t, _ = proj3.shape
    tq = KEY_BLOCK
    nbp = kc3.shape[1]
    gw = HPG * HEAD_DIM
    tab = pl.BlockSpec((tq, HEAD_DIM), lambda b, g, i: (i, 0))
    cmp_spec = pl.BlockSpec((1, nbp, HEAD_DIM), lambda b, g, i: (b, 0, g))
    kv = pl.BlockSpec((1, t, HEAD_DIM), lambda b, g, i: (b, 0, g))
    return pl.pallas_call(
        functools.partial(_nsa_prompt_kernel, tq=tq, nb=nb, t=t),
        grid=(n, G_NSA, t // tq),
        in_specs=[pl.BlockSpec((1, tq, gw), lambda b, g, i: (b, i, g)),
                  pl.BlockSpec((1, tq, HEAD_DIM), lambda b, g, i: (b, i, gate_cb + g)),
                  cmp_spec, cmp_spec, kv, kv, kv, kv,
                  pl.BlockSpec((nbp, t), lambda b, g, i: (0, 0)),
                  pl.BlockSpec((1, HEAD_DIM), lambda b, g, i: (0, 0)), tab, tab, tab],
        out_specs=pl.BlockSpec((1, tq, gw), lambda b, g, i: (b, i, g)),
        out_shape=jax.ShapeDtypeStruct((n, t, H_NSA * HEAD_DIM), BF16),
        scratch_shapes=[pltpu.VMEM((tq, t), F32), pltpu.VMEM((HPG * tq, t), F32)]
        + [pltpu.VMEM((HPG * tq, HEAD_DIM), F32)] * 3,
        compiler_params=_cp("parallel", "parallel", "arbitrary"),
        name="nsa_prompt",
    )(proj3, proj3, kc3, vc3, ks3, vs3, kw3, vw3, expand, qg.reshape(1, HEAD_DIM), *tables)


def _compress_rows_kernel(k_ref, v_ref, pk_ref, pv_ref, ko_ref, vo_ref):
    for x_ref, p_ref, o_ref in ((k_ref, pk_ref, ko_ref), (v_ref, pv_ref, vo_ref)):
        x = x_ref[0]
        nblk = x.shape[0] // NSA_BLOCK
        pw = jnp.concatenate([p_ref[...]] * nblk, axis=0)
        o_ref[0] = jnp.sum((x * pw).reshape(nblk, NSA_BLOCK, x.shape[1]), axis=1)


def compress_rows(k3, v3, pwk2, pwv2):
    n, t, w = k3.shape
    tr = _tile(t, 512)
    row = pl.BlockSpec((1, tr, w), lambda b, i: (b, i, 0))
    pw = pl.BlockSpec((NSA_BLOCK, w), lambda b, i: (0, 0))
    out = pl.BlockSpec((1, tr // NSA_BLOCK, w), lambda b, i: (b, i, 0))
    return pl.pallas_call(
        _compress_rows_kernel,
        grid=(n, t // tr),
        in_specs=[row, row, pw, pw],
        out_specs=[out, out],
        out_shape=[jax.ShapeDtypeStruct((n, t // NSA_BLOCK, w), F32)] * 2,
        compiler_params=_cp("parallel", "parallel"),
        name="compress_rows",
    )(k3, v3, pwk2, pwv2)


def _compress_pages_kernel(pt_ref, *refs, pps):
    k_refs, v_refs = refs[:pps], refs[pps:2 * pps]
    pk_ref, pv_ref, ko_ref, vo_ref = refs[2 * pps:]
    bpp = PAGE_SIZE // NSA_BLOCK
    prows = PAGE_SIZE * G_NSA
    for x_refs, p_ref, o_ref in ((k_refs, pk_ref, ko_ref), (v_refs, pv_ref, vo_ref)):
        pw = p_ref[...]
        outs = []
        for x_ref in x_refs:
            z = jnp.sum((x_ref[0] * pw).reshape(bpp, prows // bpp // SUBLANES, SUBLANES, HEAD_DIM), axis=1)
            per_g = [sum(z[:, s] for s in range(g, SUBLANES, G_NSA)) for g in range(G_NSA)]
            outs.append(jnp.concatenate(per_g, axis=1))
        o_ref[0] = jnp.concatenate(outs, axis=0)


def compress_pages(pool_k, pool_v, page_off, page_table, pwk, pwv, pps):
    n, n_pages = page_table.shape
    w = G_NSA * HEAD_DIM
    prows = PAGE_SIZE * G_NSA
    bpp = PAGE_SIZE // NSA_BLOCK
    assert SUBLANES % G_NSA == 0
    page_w = lambda p: jnp.repeat(jnp.tile(p, (bpp, 1)), G_NSA, axis=0)
    pwk2, pwv2 = page_w(pwk), page_w(pwv)
    page = lambda i: pl.BlockSpec((1, prows, HEAD_DIM),
                                  functools.partial(lambda b, c, pt, i: (page_off + pt[b, c * pps + i], 0, 0), i=i))
    pw = pl.BlockSpec((prows, HEAD_DIM), lambda b, c, pt: (0, 0))
    out = pl.BlockSpec((1, pps * bpp, w), lambda b, c, pt: (b, c, 0))
    return pl.pallas_call(
        functools.partial(_compress_pages_kernel, pps=pps),
        grid_spec=pltpu.PrefetchScalarGridSpec(
            num_scalar_prefetch=1, grid=(n, n_pages // pps),
            in_specs=[page(i) for i in range(pps)] * 2 + [pw, pw],
            out_specs=[out, out]),
        out_shape=[jax.ShapeDtypeStruct((n, n_pages * bpp, w), F32)] * 2,
        compiler_params=_cp("parallel", "parallel"),
        name="compress_pages",
    )(page_table, *([pool_k] * pps), *([pool_v] * pps), pwk2, pwv2)


def _nsa_sample_kernel(pt_ref, *refs, pps, past, rows, nb, lw):
    (q_ref, gt_ref, kc_ref, vc_ref), refs = refs[:4], refs[4:]
    kp_refs, vp_refs, refs = refs[:pps], refs[pps:2 * pps], refs[2 * pps:]
    (kn_ref, vn_ref, kwc_ref, vwc_ref, kwn_ref, vwn_ref, e_ref, qg_ref, c_ref, sa_ref, sb_ref,
     o_ref, q4_ref, sel_ref, oc_ref, m_ref, l_ref, acc_ref) = refs
    pc = pl.program_id(1)
    r4 = HPG * rows
    qpos = past + _iota((rows, 1), 0)
    lane = _iota((rows, KEY_BLOCK), 1)

    @pl.when(pc == 0)
    def _():
        for g in range(G_NSA):
            cols = slice(g * HEAD_DIM, (g + 1) * HEAD_DIM)
            q4 = _nsa_q(q_ref, g, qg_ref, c_ref, sa_ref, sb_ref)
            o_c, sel = _nsa_compressed(q4, kc_ref[0][:, cols], vc_ref[0][:, cols], qpos, nb)
            q4_ref[g] = q4
            sel_ref[g] = sel
            oc_ref[g] = o_c
            m0, l0, a0 = _softmax_init(r4)
            m_ref[g], l_ref[g], acc_ref[g] = m0, l0, a0

    nkeys = pps * PAGE_SIZE
    kpos = pc * nkeys + _iota((rows, nkeys), 1)
    for g in range(G_NSA):
        cols = slice(g * HEAD_DIM, (g + 1) * HEAD_DIM)
        grp = pl.ds(g, PAGE_SIZE, stride=G_NSA)
        kb = jnp.concatenate([r[0, grp, :].astype(BF16) for r in kp_refs], axis=0)
        vb = jnp.concatenate([r[0, grp, :].astype(BF16) for r in vp_refs], axis=0)
        selx = _dot(sel_ref[g].astype(BF16), e_ref[...])
        w = jnp.where(kpos <= qpos, selx, 0.0)
        m_ref[g], l_ref[g], acc_ref[g] = _softmax_step(
            _dot_nt(q4_ref[g], kb), jnp.concatenate([w] * HPG, axis=0), vb, m_ref[g], l_ref[g], acc_ref[g])

    @pl.when(pc == pl.num_programs(1) - 1)
    def _():
        gates_all = jax.nn.sigmoid(gt_ref[0])
        tq = _iota((rows, KEY_BLOCK), 0)
        for g in range(G_NSA):
            cols = slice(g * HEAD_DIM, (g + 1) * HEAD_DIM)
            q4 = q4_ref[g]
            kb = _pad_rows(kn_ref[0][:, cols], KEY_BLOCK).astype(BF16)
            vb = _pad_rows(vn_ref[0][:, cols], KEY_BLOCK).astype(BF16)
            w = jnp.where(lane <= tq, 1.0, 0.0)
            _, l, acc = _softmax_step(_dot_nt(q4, kb), jnp.concatenate([w] * HPG, axis=0), vb,
                                      m_ref[g], l_ref[g], acc_ref[g])
            o_s = _softmax_done(l, acc)
            tqw = _iota((rows, lw), 0)
            d = tqw + lw - _iota((rows, lw), 1)
            w = jnp.where(d < NSA_WINDOW, jnp.where(past - d + tqw >= 0, 1.0, 0.0), 0.0)
            grp = pl.ds(g, lw, stride=G_NSA)
            carry = _softmax_step(_dot_nt(q4, kwc_ref[0, grp, :].astype(BF16)), jnp.concatenate([w] * HPG, axis=0),
                                  vwc_ref[0, grp, :].astype(BF16), *_softmax_init(r4))
            kb = _pad_rows(kwn_ref[0][:, cols], KEY_BLOCK).astype(BF16)
            vb = _pad_rows(vwn_ref[0][:, cols], KEY_BLOCK).astype(BF16)
            w = jnp.where(lane <= tq, 1.0, 0.0)
            _, l, acc = _softmax_step(_dot_nt(q4, kb), jnp.concatenate([w] * HPG, axis=0), vb, *carry)
            o_w = _softmax_done(l, acc)
            gates = gates_all[:, g * HEAD_DIM:(g + 1) * HEAD_DIM]
            for hl, o in enumerate(_nsa_combine(gates, g, oc_ref[g], o_s, o_w, rows)):
                col = (g * HPG + hl) * HEAD_DIM
                o_ref[0, :, col:col + HEAD_DIM] = o.astype(BF16)


def nsa_sample(proj3, gate_cb, kc3, vc3, pool_k, pool_v, page_off, page_table, kn3, vn3, kwc3, vwc3, seq_off,
               kwn3, vwn3, expand, qg, tables, pps):
    n, rows, _ = proj3.shape
    n_pages = page_table.shape[1]
    past = n_pages * PAGE_SIZE
    nbp = kc3.shape[1]
    w = G_NSA * HEAD_DIM
    lw = kwc3.shape[1] // G_NSA
    qw = H_NSA * HEAD_DIM
    r4 = HPG * rows
    fixed = lambda shape: pl.BlockSpec(shape, lambda b, c, pt: (0,) * len(shape))
    per_seq = lambda r, width, cb=0, off=0: pl.BlockSpec(
        (1, r, width), functools.partial(lambda b, c, pt, cb, off: (off + b, 0, cb), cb=cb, off=off))
    page = lambda i: pl.BlockSpec((1, PAGE_SIZE * G_NSA, HEAD_DIM),
                                  functools.partial(lambda b, c, pt, i: (page_off + pt[b, c * pps + i], 0, 0), i=i))
    win = per_seq(lw * G_NSA, HEAD_DIM, 0, seq_off)
    in_specs = ([per_seq(rows, qw), per_seq(rows, w, gate_cb // G_NSA), per_seq(nbp, w), per_seq(nbp, w)]
                + [page(i) for i in range(pps)] * 2
                + [per_seq(rows, w), per_seq(rows, w), win, win, per_seq(rows, w), per_seq(rows, w),
                   pl.BlockSpec((nbp, pps * PAGE_SIZE), lambda b, c, pt: (0, c)),
                   fixed((1, HEAD_DIM)), fixed((rows, HEAD_DIM)), fixed((rows, HEAD_DIM)), fixed((rows, HEAD_DIM))])
    return pl.pallas_call(
        functools.partial(_nsa_sample_kernel, pps=pps, past=past, rows=rows, nb=past // NSA_BLOCK, lw=lw),
        grid_spec=pltpu.PrefetchScalarGridSpec(
            num_scalar_prefetch=1, grid=(n, n_pages // pps),
            in_specs=in_specs,
            out_specs=pl.BlockSpec((1, rows, qw), lambda b, c, pt: (b, 0, 0)),
            scratch_shapes=[pltpu.VMEM((G_NSA, r4, HEAD_DIM), BF16), pltpu.VMEM((G_NSA, rows, nbp), F32),
                            pltpu.VMEM((G_NSA, r4, HEAD_DIM), F32), pltpu.VMEM((G_NSA, r4, 1), F32),
                            pltpu.VMEM((G_NSA, r4, 1), F32), pltpu.VMEM((G_NSA, r4, HEAD_DIM), F32)]),
        out_shape=jax.ShapeDtypeStruct((n, rows, qw), BF16),
        compiler_params=_cp("parallel", "arbitrary"),
        name="nsa_sample",
    )(page_table, proj3, proj3, kc3, vc3, *([pool_k] * pps), *([pool_v] * pps), kn3, vn3, kwc3, vwc3, kwn3, vwn3,
      expand, qg.reshape(1, HEAD_DIM), *tables)


def _lru_kernel(x_ref, y_ref, hist_ref, h0_ref, cw_ref, cb_ref, wa_ref, ba_ref, wi_ref, bi_ref, sp_ref,
                o_ref, tail_ref, a_ref, b_ref):
    t = x_ref.shape[1]
    x = x_ref[0]
    hist = hist_ref[0]
    row8 = _iota((SUBLANES, LANES), 0)
    xc = cb_ref[0] + x * cw_ref[0, CONV_W - 1:CONV_W, :]
    for s in range(1, CONV_W):
        sh = pltpu.roll(x, s, 0)
        head = jnp.where(row8 >= s, sh[:SUBLANES], pltpu.roll(hist, (SUBLANES - (CONV_W - 1) + s) % SUBLANES, 0))
        sh = head if t == SUBLANES else jnp.concatenate([head, sh[SUBLANES:]], axis=0)
        xc = xc + sh * cw_ref[0, CONV_W - 1 - s:CONV_W - s, :]
    xb = xc.astype(BF16)
    r = jax.nn.sigmoid(_dot(xb, wa_ref[0]) + ba_ref[0])
    gi = jax.nn.sigmoid(_dot(xb, wi_ref[0]) + bi_ref[0])
    log_a = -LRU_C * r * sp_ref[0]
    th = jnp.tanh(log_a)
    a_ref[...] = jnp.exp(log_a)
    b_ref[...] = jnp.sqrt(-2.0 * th / (1.0 - th)) * (gi * xc)

    def body(gidx, hc):
        off = pl.multiple_of(gidx * SUBLANES, SUBLANES)
        a = a_ref[pl.ds(off, SUBLANES), :]
        b = b_ref[pl.ds(off, SUBLANES), :]
        for s in (1, 2, 4):
            ok = row8 >= s
            b = jnp.where(ok, a * pltpu.roll(b, s, 0) + b, b)
            a = jnp.where(ok, a * pltpu.roll(a, s, 0), a)
        hrows = a * hc + b
        a_ref[pl.ds(off, SUBLANES), :] = hrows
        return jnp.broadcast_to(hrows[SUBLANES - 1:SUBLANES, :], (SUBLANES, LANES))

    lax.fori_loop(0, t // SUBLANES, body, jnp.broadcast_to(h0_ref[0], (SUBLANES, LANES)))
    h = a_ref[...]
    tail_ref[0] = h[t - SUBLANES:]
    o_ref[0] = (jax.nn.gelu(y_ref[0], approximate=True) * h).astype(BF16)


def rglru(proj3, x_cb, y_cb, hist8, h0, conv_w, conv_b, w_a, b_a, w_i, b_i, softplus_neg_lam):
    n, t, _ = proj3.shape
    bw = D_RNN // RNN_BLOCKS
    vec = lambda a: a.reshape(RNN_BLOCKS, 1, bw)
    vspec = pl.BlockSpec((1, 1, bw), lambda b, k: (k, 0, 0))
    mspec = pl.BlockSpec((1, bw, bw), lambda b, k: (k, 0, 0))
    return pl.pallas_call(
        _lru_kernel,
        grid=(n, RNN_BLOCKS),
        in_specs=[pl.BlockSpec((1, t, bw), lambda b, k: (b, 0, x_cb + k)),
                  pl.BlockSpec((1, t, bw), lambda b, k: (b, 0, y_cb + k)),
                  pl.BlockSpec((1, SUBLANES, bw), lambda b, k: (b, 0, k)),
                  pl.BlockSpec((1, 1, bw), lambda b, k: (b, 0, k)),
                  pl.BlockSpec((1, CONV_W, bw), lambda b, k: (k, 0, 0)),
                  vspec, mspec, vspec, mspec, vspec, vspec],
        out_specs=[pl.BlockSpec((1, t, bw), lambda b, k: (b, 0, k)),
                   pl.BlockSpec((1, SUBLANES, bw), lambda b, k: (b, 0, k))],
        out_shape=[jax.ShapeDtypeStruct((n, t, D_RNN), BF16), jax.ShapeDtypeStruct((n, SUBLANES, D_RNN), F32)],
        scratch_shapes=[pltpu.VMEM((t, bw), F32), pltpu.VMEM((t, bw), F32)],
        compiler_params=_cp("parallel", "parallel"),
        name="rglru",
    )(proj3, proj3, hist8, h0.reshape(n, 1, D_RNN),
      conv_w.reshape(CONV_W, RNN_BLOCKS, bw).transpose(1, 0, 2), vec(conv_b),
      w_a.astype(BF16), vec(b_a), w_i.astype(BF16), vec(b_i), vec(softplus_neg_lam))


def _rope_tables(pos, n_rot, theta):
    half = n_rot // 2
    inv = 1.0 / (theta ** (jnp.arange(half, dtype=F32) * (2.0 / n_rot)))
    ang = pos.astype(F32)[:, None] * inv[None, :]
    cos, sin = jnp.cos(ang), jnp.sin(ang)
    p = pos.shape[0]
    rest = HEAD_DIM - n_rot
    zh = jnp.zeros((p, half), F32)
    c = jnp.concatenate([cos, cos, jnp.ones((p, rest), F32)], axis=1)
    sa = jnp.concatenate([-sin, zh, jnp.zeros((p, rest), F32)], axis=1)
    sb = jnp.concatenate([zh, sin, jnp.zeros((p, rest), F32)], axis=1)
    return c, sa, sb


def _block_expand(nbp, length):
    return (jnp.arange(length, dtype=jnp.int32)[None, :] // NSA_BLOCK
            == jnp.arange(nbp, dtype=jnp.int32)[:, None]).astype(BF16)


def _pad_axis(a, axis, size):
    pad = [(0, 0)] * a.ndim
    pad[axis] = (0, size - a.shape[axis])
    return jnp.pad(a, pad)


def _odd_w_in(w):
    qkv = H_NSA * HEAD_DIM + 6 * G_NSA * HEAD_DIM
    ng = 3 * H_NSA
    gate = w[:, qkv:qkv + ng]
    per = 3 * HPG
    gcols = [_pad_axis(gate[:, g * per:(g + 1) * per], 1, LANES) for g in range(G_NSA)]
    out = jnp.concatenate([w[:, :qkv], w[:, qkv + ng:]] + gcols, axis=1)
    return _pad_axis(out, 1, -(-out.shape[1] // 1024) * 1024)


def kernel(x_prompt, x_sample, mem_prompt, state_ret, cache_dil_k, cache_dil_v, cache_nsa_cmp_k, cache_nsa_cmp_v, cache_nsa_slc_k, cache_nsa_slc_v, cache_nsa_win_k, cache_nsa_win_v, state_lru_h, state_lru_conv, cache_mem_k, cache_mem_v, page_table, ev_w_in, ret_gn_g, dil_q_norm_g, dil_k_norm_g, ev_w_out, od_w_in, nsa_q_norm_g, nsa_k_norm_g, nsa_pw_k, nsa_pw_v, lru_conv_w, lru_conv_b, lru_w_a, lru_b_a, lru_w_i, lru_b_i, lru_lambda, od_w_out, norm_mix_g, norm_mem_g, norm_mlp_g, mem_norm_g, mem_w_q, mem_w_k, mem_w_v, mem_q_norm_g, mem_k_norm_g, mem_w_o, mlp_w1, mlp_w2):
    nb_p, t_p, d = x_prompt.shape
    nb_s, t_s, _ = x_sample.shape
    depth = norm_mix_g.shape[0]
    n_pages = page_table.shape[1]
    past = n_pages * PAGE_SIZE
    rs = SAMPLE_ROWS
    assert t_p % RET_CHUNK == 0 and t_s <= rs and past % PAGE_SIZE == 0 and t_s < NSA_BLOCK
    assert cache_nsa_win_k.shape[2] % KEY_BLOCK == 0 and cache_dil_k.shape[2] % KEY_BLOCK == 0
    hd = HEAD_DIM

    xp = x_prompt.reshape(nb_p * t_p, d)
    xs = _pad_axis(x_sample, 1, rs).reshape(nb_s * rs, d)
    mem2 = mem_prompt.reshape(-1, d)
    n_mem = mem_prompt.shape[1]

    pos_p = jnp.arange(t_p, dtype=jnp.int32)
    pos_s = past + jnp.arange(rs, dtype=jnp.int32)
    tile_s = lambda tabs: tuple(jnp.tile(tb, (nb_s, 1)) for tb in tabs)
    rope_p, rope_s = _rope_tables(pos_p, ROPE_DIMS, ROPE_THETA), _rope_tables(pos_s, ROPE_DIMS, ROPE_THETA)
    ret_p, ret_s = _rope_tables(pos_p, hd, RET_THETA), _rope_tables(pos_s, hd, RET_THETA)
    log_g = jnp.log1p(-jnp.exp2(-5.0 - jnp.arange(H_RET, dtype=F32)))

    outs = {k: [] for k in ("ret_p", "ret_s", "dk_p", "dv_p", "dk_s", "dv_s", "ck_p", "cv_p", "ck_s", "cv_s",
                            "sk_p", "sv_p", "sk_s", "sv_s", "wk_p", "wv_p", "wk_s", "wv_s", "lh_p", "lh_s",
                            "lc_p", "lc_s", "mk_p", "mv_p")}
    new_rows = lambda a, w: a.reshape(nb_s, rs, w)[:, :t_s]

    for layer in range(depth):
        if layer % 2 == 0:
            e = layer // 2
            w_in = ev_w_in[e].astype(BF16)
            w_out = ev_w_out[e].astype(BF16)
            hw = H_RET * hd
            ones = jnp.ones((1, hd), F32)
            gains = jnp.concatenate([dil_k_norm_g[e][None], ones], axis=0)
            sections = [(5, "norm_rope"), (6, "copy")]
            proj = rms_matmul(xp, norm_mix_g[layer], w_in, 1024)
            proj3 = proj.reshape(nb_p, t_p, -1)
            y_ret, s_new = retention(proj3, jnp.zeros((nb_p, H_RET, hd, hd), F32), log_g, ret_gn_g[e], ret_p, RET_CHUNK)
            dk, dv = kv_transform(proj, sections, gains, rope_p, H_DIL, t_p)
            y_dil = dilated_prompt(proj3, 4 * H_RET, dk.reshape(nb_p, t_p, hw), dv.reshape(nb_p, t_p, hw),
                                   dil_q_norm_g[e], rope_p)
            xp = matmul_res(y_ret.reshape(-1, hw), y_dil.reshape(-1, hw), w_out[:hw], w_out[hw:], xp)
            keep = min(DIL_PATTERNS[-1][0], t_p)
            outs["ret_p"].append(s_new)
            outs["dk_p"].append(dk.reshape(nb_p, t_p, H_DIL, hd)[:, t_p - keep:])
            outs["dv_p"].append(dv.reshape(nb_p, t_p, H_DIL, hd)[:, t_p - keep:])
            proj = rms_matmul(xs, norm_mix_g[layer], w_in, 1024)
            proj3 = proj.reshape(nb_s, rs, -1)
            y_ret, s_new = retention(proj3, state_ret.reshape(-1, H_RET, hd, hd), log_g, ret_gn_g[e], ret_s, t_s,
                                     seq_off=e * nb_s)
            dk, dv = kv_transform(proj, sections, gains, tile_s(rope_s), H_DIL, nb_s * rs)
            lc = cache_dil_k.shape[2]
            y_dil = dilated_sample(proj3, 4 * H_RET, cache_dil_k.reshape(-1, lc * H_DIL, hd),
                                   cache_dil_v.reshape(-1, lc * H_DIL, hd), e * nb_s, dk.reshape(nb_s, rs, hw),
                                   dv.reshape(nb_s, rs, hw), dil_q_norm_g[e], rope_s)
            xs = matmul_res(y_ret.reshape(-1, hw), y_dil.reshape(-1, hw), w_out[:hw], w_out[hw:], xs)
            outs["ret_s"].append(s_new)
            outs["dk_s"].append(new_rows(dk, hw).reshape(nb_s, t_s, H_DIL, hd))
            outs["dv_s"].append(new_rows(dv, hw).reshape(nb_s, t_s, H_DIL, hd))
        else:
            o = layer // 2
            w_in = _odd_w_in(od_w_in[o]).astype(BF16)
            w_out = od_w_out[o].astype(BF16)
            qw = H_NSA * hd
            gw = G_NSA * hd
            kv_cb = qw // gw
            x_cb = (qw + 6 * gw) // hd
            y_cb = x_cb + D_RNN // hd
            gate_cb = y_cb + D_RNN // hd
            ones = jnp.ones((1, hd), F32)
            kg = nsa_k_norm_g[o]
            gains = jnp.concatenate([kg[0:1], ones, kg[1:2], ones, kg[2:3], ones], axis=0)
            sections = [(kv_cb + i, "norm_rope" if i % 2 == 0 else "copy") for i in range(6)]
            pwk2 = jnp.concatenate([nsa_pw_k[o]] * G_NSA, axis=1)
            pwv2 = jnp.concatenate([nsa_pw_v[o]] * G_NSA, axis=1)
            sp = jax.nn.softplus(-lru_lambda[o].astype(F32))
            lru_w = (lru_conv_w[o], lru_conv_b[o], lru_w_a[o], lru_b_a[o], lru_w_i[o], lru_b_i[o], sp)
            proj = rms_matmul(xp, norm_mix_g[layer], w_in, 1024)
            proj3 = proj.reshape(nb_p, t_p, -1)
            kcm, vcm, ksl, vsl, kwn, vwn = kv_transform(proj, sections, gains, rope_p, G_NSA, t_p)
            r3 = lambda a: a.reshape(nb_p, t_p, gw)
            nb = t_p // NSA_BLOCK
            nbp = -(-nb // LANES) * LANES
            kc, vc = compress_rows(r3(kcm), r3(vcm), pwk2, pwv2)
            o_nsa = nsa_prompt(proj3, gate_cb, _pad_axis(kc, 1, nbp), _pad_axis(vc, 1, nbp), r3(ksl), r3(vsl),
                               r3(kwn), r3(vwn), _block_expand(nbp, t_p), nsa_q_norm_g[o], rope_p, nb)
            y_rnn, tail = rglru(proj3, x_cb, y_cb, jnp.zeros((nb_p, SUBLANES, D_RNN), F32),
                                jnp.zeros((nb_p, D_RNN), F32), *lru_w)
            xp = matmul_res(o_nsa.reshape(-1, qw), y_rnn.reshape(-1, D_RNN), w_out[:qw], w_out[qw:], xp)
            r4 = lambda a: a.reshape(nb_p, t_p, G_NSA, hd)
            keep = min(NSA_WINDOW, t_p)
            for key, val in (("ck_p", r4(kcm)), ("cv_p", r4(vcm)), ("sk_p", r4(ksl)), ("sv_p", r4(vsl)),
                             ("wk_p", r4(kwn)[:, t_p - keep:]), ("wv_p", r4(vwn)[:, t_p - keep:])):
                outs[key].append(val)
            outs["lh_p"].append(tail[:, SUBLANES - 1])
            lx = proj3[:, :, x_cb * hd:x_cb * hd + D_RNN]
            outs["lc_p"].append(jnp.concatenate([jnp.zeros((nb_p, CONV_W - 1, D_RNN), F32), lx], axis=1)[:, -(CONV_W - 1):])
            proj = rms_matmul(xs, norm_mix_g[layer], w_in, 1024)
            proj3 = proj.reshape(nb_s, rs, -1)
            kcm, vcm, ksl, vsl, kwn, vwn = kv_transform(proj, sections, gains, tile_s(rope_s), G_NSA, nb_s * rs)
            r3 = lambda a: a.reshape(nb_s, rs, gw)
            nph = cache_nsa_cmp_k.shape[1]
            pool = lambda a: a.reshape(-1, PAGE_SIZE * G_NSA, hd)
            pps = math.gcd(n_pages, 16)
            nb = (past + t_s) // NSA_BLOCK
            nbp = -(-nb // LANES) * LANES
            kc, vc = compress_pages(pool(cache_nsa_cmp_k), pool(cache_nsa_cmp_v), o * nph, page_table,
                                    nsa_pw_k[o], nsa_pw_v[o], pps)
            lw = cache_nsa_win_k.shape[2]
            o_nsa = nsa_sample(proj3, gate_cb, _pad_axis(kc, 1, nbp), _pad_axis(vc, 1, nbp),
                               pool(cache_nsa_slc_k), pool(cache_nsa_slc_v), o * nph, page_table, r3(ksl), r3(vsl),
                               cache_nsa_win_k.reshape(-1, lw * G_NSA, hd), cache_nsa_win_v.reshape(-1, lw * G_NSA, hd),
                               o * nb_s, r3(kwn), r3(vwn), _block_expand(nbp, past), nsa_q_norm_g[o], rope_s, pps)
            hist8 = _pad_axis(state_lru_conv[o].astype(F32), 1, SUBLANES)
            y_rnn, tail = rglru(proj3, x_cb, y_cb, hist8, state_lru_h[o].astype(F32), *lru_w)
            xs = matmul_res(o_nsa.reshape(-1, qw), y_rnn.reshape(-1, D_RNN), w_out[:qw], w_out[qw:], xs)
            for key, val in (("ck_s", kcm), ("cv_s", vcm), ("sk_s", ksl), ("sv_s", vsl), ("wk_s", kwn), ("wv_s", vwn)):
                outs[key].append(new_rows(val, gw).reshape(nb_s, t_s, G_NSA, hd))
            outs["lh_s"].append(tail[:, t_s - 1])
            lx = proj3[:, :t_s, x_cb * hd:x_cb * hd + D_RNN]
            outs["lc_s"].append(jnp.concatenate([state_lru_conv[o].astype(F32), lx], axis=1)[:, -(CONV_W - 1):])

        dm = H_MEM * hd
        w_kv = jnp.concatenate([mem_w_k[layer], mem_w_v[layer]], axis=1).astype(BF16)
        mkv = rms_matmul(mem2, mem_norm_g[layer], w_kv, dm)
        gains = jnp.concatenate([mem_k_norm_g[layer][None], jnp.ones((1, hd), F32)], axis=0)
        mk, mv = kv_transform(mkv, [(0, "norm"), (1, "copy")], gains, rope_p, H_MEM, n_mem)
        mk3, mv3 = mk.reshape(nb_p, n_mem, dm), mv.reshape(nb_p, n_mem, dm)
        outs["mk_p"].append(mk3.reshape(nb_p, n_mem, H_MEM, hd))
        outs["mv_p"].append(mv3.reshape(nb_p, n_mem, H_MEM, hd))
        wq, wo = mem_w_q[layer].astype(BF16), mem_w_o[layer].astype(BF16)
        xp = mem_attn_block(xp, norm_mem_g[layer], wq, mem_q_norm_g[layer], mk3, mv3, wo, t_p)
        xs = mem_attn_block(xs, norm_mem_g[layer], wq, mem_q_norm_g[layer], cache_mem_k.reshape(-1, n_mem, dm),
                            cache_mem_v.reshape(-1, n_mem, dm), wo, rs, seq_off=layer * nb_s)
        w1, w2 = mlp_w1[layer].astype(BF16), mlp_w2[layer].astype(BF16)
        xp = mlp_block(xp, norm_mlp_g[layer], w1, w2)
        xs = mlp_block(xs, norm_mlp_g[layer], w1, w2)

    st = lambda k: jnp.stack(outs[k])
    return (xp.reshape(nb_p, t_p, d), xs.reshape(nb_s, rs, d)[:, :t_s], st("ret_p"), st("ret_s"),
            st("dk_p"), st("dv_p"), st("dk_s"), st("dv_s"),
            st("ck_p"), st("cv_p"), st("ck_s"), st("cv_s"),
            st("sk_p"), st("sv_p"), st("sk_s"), st("sv_s"),
            st("wk_p"), st("wv_p"), st("wk_s"), st("wv_s"),
            st("lh_p"), st("lh_s"), st("lc_p"), st("lc_s"),
            st("mk_p"), st("mv_p"))
```

```python
import functools
import math

import jax
import jax.numpy as jnp
from jax import lax
from jax.experimental import pallas as pl
from jax.experimental.pallas import tpu as pltpu

F32 = jnp.float32
BF16 = jnp.bfloat16

HEAD_DIM = 128
ROPE_DIMS = HEAD_DIM // 4
ROPE_THETA = 500000.0
NORM_EPS = 1e-6
H_RET = 8
RET_CHUNK = 128
RET_THETA = 10000.0
H_DIL = 8
DIL_PATTERNS = ((128, 1), (512, 4), (2048, 16))
H_NSA = 8
G_NSA = 2
HPG = H_NSA // G_NSA
NSA_BLOCK = 64
NSA_TOPK = 16
NSA_WINDOW = 512
D_RNN = 1024
RNN_BLOCKS = 8
CONV_W = 4
LRU_C = 8.0
H_MEM = 4
PAGE_SIZE = 128

LANES = 128
SUBLANES = 8
KEY_BLOCK = 128
SAMPLE_ROWS = 8
VMEM_LIMIT = 48 * 1024 * 1024
NEG = -1e30
SCALE = HEAD_DIM ** -0.5


def _cp(*sem):
    return pltpu.CompilerParams(dimension_semantics=sem, vmem_limit_bytes=VMEM_LIMIT)


def _tile(n, pref):
    t = min(n, pref)
    while n % t:
        t -= SUBLANES
    return t


def _rms(x, g):
    return x * lax.rsqrt(jnp.mean(x * x, axis=-1, keepdims=True) + NORM_EPS) * g


def _rope(x, c, sa, sb, half):
    return x * c + pltpu.roll(x, LANES - half, 1) * sa + pltpu.roll(x, half, 1) * sb


def _dot(a, b):
    return jnp.dot(a, b, preferred_element_type=F32)


def _dot_nt(a, b):
    return lax.dot_general(a, b, (((1,), (1,)), ((), ())), preferred_element_type=F32)


def _pad_rows(x, rows):
    if x.shape[0] == rows:
        return x
    return jnp.concatenate([x, jnp.zeros((rows - x.shape[0], x.shape[1]), x.dtype)], axis=0)


def _softmax_step(s, w, vb, m, l, acc):
    s = jnp.where(w > 0, s, NEG)
    m_new = jnp.maximum(m, jnp.max(s, axis=-1, keepdims=True))
    alpha = jnp.exp(m - m_new)
    p = jnp.exp(s - m_new) * w
    l = alpha * l + jnp.sum(p, axis=-1, keepdims=True)
    acc = alpha * acc + _dot(p.astype(BF16), vb)
    return m_new, l, acc


def _softmax_init(rows):
    return (jnp.full((rows, 1), NEG, F32), jnp.zeros((rows, 1), F32), jnp.zeros((rows, HEAD_DIM), F32))


def _softmax_done(l, acc):
    return acc / jnp.maximum(l, 1e-30)


def _dil_weight(d):
    w = jnp.zeros(d.shape, F32)
    for window, dil in DIL_PATTERNS:
        hit = jnp.where(d <= window, 1.0, 0.0)
        if dil > 1:
            rem = (d & (dil - 1)) if dil & (dil - 1) == 0 else lax.rem(d, dil)
            hit = jnp.where(rem == 0, hit, 0.0)
        w = w + hit
    return jnp.where(d >= 0, w, 0.0)


def _iota(shape, dim):
    return lax.broadcasted_iota(jnp.int32, shape, dim)


def _rms_matmul_kernel(x_ref, g_ref, w_ref, o_ref, xn_ref):
    @pl.when(pl.program_id(1) == 0)
    def _():
        xn_ref[...] = _rms(x_ref[...], g_ref[...]).astype(BF16)

    o_ref[...] = _dot(xn_ref[...], w_ref[...])


def rms_matmul(x, g, w, layer, tn):
    m, d = x.shape
    n = w.shape[2]
    tm = _tile(m, 512)
    return pl.pallas_call(
        _rms_matmul_kernel,
        grid=(m // tm, n // tn),
        in_specs=[pl.BlockSpec((tm, d), lambda i, j: (i, 0)),
                  pl.BlockSpec((1, d), lambda i, j: (0, 0)),
                  pl.BlockSpec((None, d, tn), lambda i, j: (layer, 0, j))],
        out_specs=pl.BlockSpec((tm, tn), lambda i, j: (i, j)),
        out_shape=jax.ShapeDtypeStruct((m, n), F32),
        scratch_shapes=[pltpu.VMEM((tm, d), BF16)],
        compiler_params=_cp("parallel", "arbitrary"),
        name="rms_matmul",
    )(x, g.reshape(1, d), w)


def _matmul_res_kernel(a1_ref, a2_ref, w1_ref, w2_ref, r_ref, o_ref):
    o_ref[...] = r_ref[...] + _dot(a1_ref[...], w1_ref[...]) + _dot(a2_ref[...], w2_ref[...])


def matmul_res(a1, a2, w, layer, res):
    m, k1 = a1.shape
    k2 = a2.shape[1]
    n = w.shape[2]
    assert k1 == k2 and w.shape[1] == k1 + k2
    tm = _tile(m, 512)
    tn = _tile(n, 1024)
    return pl.pallas_call(
        _matmul_res_kernel,
        grid=(m // tm, n // tn),
        in_specs=[pl.BlockSpec((tm, k1), lambda i, j: (i, 0)),
                  pl.BlockSpec((tm, k2), lambda i, j: (i, 0)),
                  pl.BlockSpec((None, k1, tn), lambda i, j: (layer, 0, j)),
                  pl.BlockSpec((None, k2, tn), lambda i, j: (layer, 1, j)),
                  pl.BlockSpec((tm, tn), lambda i, j: (i, j))],
        out_specs=pl.BlockSpec((tm, tn), lambda i, j: (i, j)),
        out_shape=jax.ShapeDtypeStruct((m, n), F32),
        compiler_params=_cp("parallel", "parallel"),
        name="matmul_res",
    )(a1, a2, w, w, res)


def _mlp_kernel(x_ref, g_ref, w1_ref, w2_ref, o_ref, xn_ref, acc_ref):
    f = pl.program_id(1)

    @pl.when(f == 0)
    def _():
        xn_ref[...] = _rms(x_ref[...], g_ref[...]).astype(BF16)
        acc_ref[...] = jnp.zeros_like(acc_ref)

    hid = jnp.maximum(_dot(xn_ref[...], w1_ref[...]), 0.0)
    acc_ref[...] += _dot((hid * hid).astype(BF16), w2_ref[...])

    @pl.when(f == pl.num_programs(1) - 1)
    def _():
        o_ref[...] = x_ref[...] + acc_ref[...]


def mlp_block(x, g, w1, w2, layer):
    m, d = x.shape
    ff = w1.shape[2]
    tm = _tile(m, 512)
    tf = _tile(ff, 512)
    return pl.pallas_call(
        _mlp_kernel,
        grid=(m // tm, ff // tf),
        in_specs=[pl.BlockSpec((tm, d), lambda i, f: (i, 0)),
                  pl.BlockSpec((1, d), lambda i, f: (0, 0)),
                  pl.BlockSpec((None, d, tf), lambda i, f: (layer, 0, f)),
                  pl.BlockSpec((None, tf, d), lambda i, f: (layer, f, 0))],
        out_specs=pl.BlockSpec((tm, d), lambda i, f: (i, 0)),
        out_shape=jax.ShapeDtypeStruct((m, d), F32),
        scratch_shapes=[pltpu.VMEM((tm, d), BF16), pltpu.VMEM((tm, d), F32)],
        compiler_params=_cp("parallel", "arbitrary"),
        name="mlp_block",
    )(x, g.reshape(1, d), w1, w2)


def _mem_attn_kernel(x_ref, g_ref, wq_ref, qg_ref, k_ref, v_ref, wo_ref, o_ref, oh_ref, *, nb, seg, by_row):
    x = x_ref[...]
    q = _dot(_rms(x, g_ref[...]).astype(BF16), wq_ref[...])
    qg = qg_ref[...]
    n_mem = k_ref.shape[1] // H_MEM if by_row else k_ref.shape[1]
    for b in range(nb):
        for h in range(H_MEM):
            cols = slice(h * HEAD_DIM, (h + 1) * HEAD_DIM)
            rows = slice(b * seg, (b + 1) * seg)
            head = (b, pl.ds(h, n_mem, stride=H_MEM), slice(None)) if by_row else (b, slice(None), cols)
            qh = (_rms(q[rows, cols], qg) * SCALE).astype(BF16)
            s = _dot_nt(qh, k_ref[head].astype(BF16))
            p = jnp.exp(s - jnp.max(s, axis=-1, keepdims=True))
            o = _dot(p.astype(BF16), v_ref[head].astype(BF16)) / jnp.sum(p, axis=-1, keepdims=True)
            oh_ref[rows, cols] = o.astype(BF16)
    o_ref[...] = x + _dot(oh_ref[...], wo_ref[...])


def mem_attn_block(x, g, wq, qg, k, v, wo, layer, rows_per_seq, seq_off=0):
    m, d = x.shape
    dm = H_MEM * HEAD_DIM
    by_row = k.shape[2] == HEAD_DIM
    kv_rows, kv_w = k.shape[1], k.shape[2]
    if rows_per_seq >= 128:
        tm = _tile(rows_per_seq, 512)
        nb, seg = 1, tm
        kv_map = lambda i: (seq_off + i // (rows_per_seq // tm), 0, 0)
    else:
        nb = _tile(m // rows_per_seq, 8)
        tm, seg = nb * rows_per_seq, rows_per_seq
        assert seq_off % nb == 0
        kv_map = lambda i: (seq_off // nb + i, 0, 0)
    return pl.pallas_call(
        functools.partial(_mem_attn_kernel, nb=nb, seg=seg, by_row=by_row),
        grid=(m // tm,),
        in_specs=[pl.BlockSpec((tm, d), lambda i: (i, 0)),
                  pl.BlockSpec((1, d), lambda i: (0, 0)),
                  pl.BlockSpec((None, d, dm), lambda i: (layer, 0, 0)),
                  pl.BlockSpec((1, HEAD_DIM), lambda i: (0, 0)),
                  pl.BlockSpec((nb, kv_rows, kv_w), kv_map),
                  pl.BlockSpec((nb, kv_rows, kv_w), kv_map),
                  pl.BlockSpec((None, dm, d), lambda i: (layer, 0, 0))],
        out_specs=pl.BlockSpec((tm, d), lambda i: (i, 0)),
        out_shape=jax.ShapeDtypeStruct((m, d), F32),
        scratch_shapes=[pltpu.VMEM((tm, dm), BF16)],
        compiler_params=_cp("parallel"),
        name="mem_attn_block",
    )(x, g.reshape(1, d), wq, qg.reshape(1, HEAD_DIM), k, v, wo)


def _kv_transform_kernel(*refs, modes, heads):
    ns = len(modes)
    x_refs = refs[:ns]
    g_ref, c_ref, sa_ref, sb_ref = refs[ns:ns + 4]
    o_refs = refs[ns + 4:]
    for s in range(ns):
        for h in range(heads):
            cols = slice(h * HEAD_DIM, (h + 1) * HEAD_DIM)
            x = x_refs[s][:, cols]
            if modes[s] != "copy":
                x = _rms(x, g_ref[s:s + 1, :])
            if modes[s] == "norm_rope":
                x = _rope(x, c_ref[...], sa_ref[...], sb_ref[...], ROPE_DIMS // 2)
            o_refs[s][:, cols] = x


def kv_transform(proj, sections, gains, tables, heads, pos_period):
    m = proj.shape[0]
    width = heads * HEAD_DIM
    tr = _tile(pos_period, 512)
    nper = pos_period // tr
    modes = tuple(mode for _, mode in sections)
    in_specs = [pl.BlockSpec((tr, width), functools.partial(lambda i, cb: (i, cb), cb=cb)) for cb, _ in sections]
    in_specs.append(pl.BlockSpec(gains.shape, lambda i: (0, 0)))
    in_specs += [pl.BlockSpec((tr, HEAD_DIM), lambda i: (i % nper, 0))] * 3
    outs = pl.pallas_call(
        functools.partial(_kv_transform_kernel, modes=modes, heads=heads),
        grid=(m // tr,),
        in_specs=in_specs,
        out_specs=[pl.BlockSpec((tr, width), lambda i: (i, 0))] * len(sections),
        out_shape=[jax.ShapeDtypeStruct((m, width), F32)] * len(sections),
        compiler_params=_cp("parallel"),
        name="kv_transform",
    )(*([proj] * len(sections)), gains, *tables)
    return outs


def _retention_kernel(q_ref, k_ref, v_ref, rg_ref, s0_ref, lg_ref, gn_ref, c_ref, sa_ref, sb_ref,
                      y_ref, s_ref, st_ref, *, rows, c_true, hps):
    ci = pl.program_id(2)
    cc = RET_CHUNK

    @pl.when(ci == 0)
    def _():
        st_ref[...] = s0_ref[0]

    c, sa, sb = c_ref[...], sa_ref[...], sb_ref[...]
    ii = _iota((cc, 1), 0).astype(F32)
    diff = ii - _iota((1, cc), 1).astype(F32)
    for h in range(hps):
        cols = slice(h * HEAD_DIM, (h + 1) * HEAD_DIM)
        lg = lg_ref[h]
        q = _pad_rows(_rope(q_ref[0, :, cols], c, sa, sb, HEAD_DIM // 2) * SCALE, cc)
        k = _pad_rows(_rope(k_ref[0, :, cols], c, sa, sb, HEAD_DIM // 2), cc)
        vb = _pad_rows(v_ref[0, :, cols], cc).astype(BF16)
        qb = q.astype(BF16)
        decay = jnp.where(diff >= 0, jnp.exp(jnp.maximum(diff, 0.0) * lg), 0.0)
        inner = _dot_nt(qb, k.astype(BF16)) * decay
        st = st_ref[h]
        o = _dot(inner.astype(BF16), vb) + _dot(qb, st.astype(BF16)) * jnp.exp((ii + 1.0) * lg)
        kd = jnp.where(ii < c_true, k * jnp.exp((c_true - 1.0 - ii) * lg), 0.0)
        st_ref[h] = jnp.exp(c_true * lg) * st + _dot(kd.T.astype(BF16), vb)
        o = o[:rows]
        mu = jnp.mean(o, axis=-1, keepdims=True)
        var = jnp.mean(jnp.square(o - mu), axis=-1, keepdims=True)
        y = (o - mu) * lax.rsqrt(var + NORM_EPS) * gn_ref[h]
        rg = rg_ref[0, :, cols]
        y_ref[0, :, cols] = (y * (rg * jax.nn.sigmoid(rg))).astype(BF16)

    @pl.when(ci == pl.num_programs(2) - 1)
    def _():
        s_ref[0] = st_ref[...]


def retention(proj3, s0, log_g, gn_g, tables, c_true, seq_off=0):
    n, t, _ = proj3.shape
    rows = min(t, RET_CHUNK)
    nc = t // rows
    h = H_RET
    hps = 4
    ng = h // hps
    wid = hps * HEAD_DIM
    blk = lambda sec: pl.BlockSpec((1, rows, wid), functools.partial(lambda b, hg, c, sec: (b, c, sec * ng + hg), sec=sec))
    tab = pl.BlockSpec((rows, HEAD_DIM), lambda b, hg, c: (c, 0))
    y, s_new = pl.pallas_call(
        functools.partial(_retention_kernel, rows=rows, c_true=float(c_true), hps=hps),
        grid=(n, ng, nc),
        in_specs=[blk(0), blk(1), blk(2), blk(3),
                  pl.BlockSpec((1, hps, HEAD_DIM, HEAD_DIM), lambda b, hg, c: (seq_off + b, hg, 0, 0)),
                  pl.BlockSpec((hps, 1, HEAD_DIM), lambda b, hg, c: (hg, 0, 0)),
                  pl.BlockSpec((hps, 1, HEAD_DIM), lambda b, hg, c: (hg, 0, 0)),
                  tab, tab, tab],
        out_specs=[pl.BlockSpec((1, rows, wid), lambda b, hg, c: (b, c, hg)),
                   pl.BlockSpec((1, hps, HEAD_DIM, HEAD_DIM), lambda b, hg, c: (b, hg, 0, 0))],
        out_shape=[jax.ShapeDtypeStruct((n, t, h * HEAD_DIM), BF16),
                   jax.ShapeDtypeStruct((n, h, HEAD_DIM, HEAD_DIM), F32)],
        scratch_shapes=[pltpu.VMEM((hps, HEAD_DIM, HEAD_DIM), F32)],
        compiler_params=_cp("parallel", "parallel", "arbitrary"),
        name="retention",
    )(proj3, proj3, proj3, proj3, s0,
      jnp.broadcast_to(log_g[:, None, None], (h, 1, HEAD_DIM)), gn_g.reshape(h, 1, HEAD_DIM), *tables)
    return y, s_new


def _rows(start, size, stride):
    return pl.ds(start, size) if stride == 1 else pl.ds(start, size, stride=stride)


def _dil_prompt_kernel(q_ref, k_ref, v_ref, qg_ref, c_ref, sa_ref, sb_ref, o_ref,
                       qs_ref, m_ref, l_ref, acc_ref, pm_ref, pls_ref, pacc_ref, qc_ref, kc_ref, vc_ref, *, t):
    chunk = _tile(t, 512)
    for c in range(t // chunk):
        r = slice(c * chunk, (c + 1) * chunk)
        q = _rope(_rms(q_ref[0, r, :], qg_ref[...]), c_ref[r, :], sa_ref[r, :], sb_ref[r, :], ROPE_DIMS // 2)
        qs_ref[r, :] = q * SCALE

    def bias(bq, nk, first_key_dist):
        rel = first_key_dist + _iota((bq, nk), 0) - _iota((bq, nk), 1)
        return jnp.where(rel >= 0, jnp.where(rel <= KEY_BLOCK, 0.0, NEG), NEG)

    for pi, (window, dil) in enumerate(DIL_PATTERNS):
        length = t // dil
        bq = min(length, KEY_BLOCK)
        nk = bq if length == bq else 2 * bq
        for r in range(dil):
            rows, crow = _rows(r, length, dil), slice(r * length, (r + 1) * length)
            qc_ref[crow, :] = qs_ref[rows, :].astype(BF16)
            kc_ref[crow, :] = k_ref[0, rows, :].astype(BF16)
            vc_ref[crow, :] = v_ref[0, rows, :].astype(BF16)
        blocks = [(r * length + i * bq, r * length + max(i - 1, 0) * bq, i == 0 and nk > bq)
                  for r in range(dil) for i in range(length // bq)]
        band, lead = bias(bq, nk, nk - bq), bias(bq, nk, 0)
        s = jnp.concatenate([_dot_nt(qc_ref[q0:q0 + bq, :], kc_ref[k0:k0 + nk, :]) + (lead if first else band)
                             for q0, k0, first in blocks], axis=0)
        m = jnp.max(s, axis=-1, keepdims=True)
        pb = jnp.exp(s - m).astype(BF16)
        m = jnp.broadcast_to(m, (t, HEAD_DIM))
        ones = jnp.ones((nk, HEAD_DIM), BF16)
        l = jnp.concatenate([_dot(pb[j * bq:(j + 1) * bq], ones) for j in range(len(blocks))], axis=0)
        acc = jnp.concatenate([_dot(pb[j * bq:(j + 1) * bq], vc_ref[k0:k0 + nk, :])
                               for j, (q0, k0, first) in enumerate(blocks)], axis=0)
        if pi == 0:
            assert dil == 1
            m_ref[...], l_ref[...], acc_ref[...] = m, l, acc
        else:
            pm_ref[...], pls_ref[...], pacc_ref[...] = m, l, acc
            for r in range(dil):
                rows, crow = _rows(r, length, dil), slice(r * length, (r + 1) * length)
                m0, m1 = m_ref[rows, :], pm_ref[crow, :]
                mx = jnp.maximum(m0, m1)
                e0, e1 = jnp.exp(m0 - mx), jnp.exp(m1 - mx)
                m_ref[rows, :] = mx
                l_ref[rows, :] = e0 * l_ref[rows, :] + e1 * pls_ref[crow, :]
                acc_ref[rows, :] = e0 * acc_ref[rows, :] + e1 * pacc_ref[crow, :]

    for c in range(t // chunk):
        r = slice(c * chunk, (c + 1) * chunk)
        o_ref[0, r, :] = (acc_ref[r, :] / l_ref[r, :]).astype(BF16)


def dilated_prompt(proj3, q_cb, k3, v3, qg, tables):
    n, t, _ = proj3.shape
    for window, dil in DIL_PATTERNS:
        length = t // dil
        assert window // dil == KEY_BLOCK and t % dil == 0
        assert length % KEY_BLOCK == 0 or (length < KEY_BLOCK and length % SUBLANES == 0)
    tab = pl.BlockSpec((t, HEAD_DIM), lambda b, h: (0, 0))
    seq = lambda cb: pl.BlockSpec((1, t, HEAD_DIM), functools.partial(lambda b, h, cb: (b, 0, cb + h), cb=cb))
    return pl.pallas_call(
        functools.partial(_dil_prompt_kernel, t=t),
        grid=(n, H_DIL),
        in_specs=[seq(q_cb), seq(0), seq(0), pl.BlockSpec((1, HEAD_DIM), lambda b, h: (0, 0)), tab, tab, tab],
        out_specs=seq(0),
        out_shape=jax.ShapeDtypeStruct((n, t, H_DIL * HEAD_DIM), BF16),
        scratch_shapes=[pltpu.VMEM((t, HEAD_DIM), F32)] * 7 + [pltpu.VMEM((t, HEAD_DIM), BF16)] * 3,
        compiler_params=_cp("parallel", "parallel"),
        name="dilated_prompt",
    )(proj3, k3, v3, qg.reshape(1, HEAD_DIM), *tables)


def _dil_sample_kernel(q_ref, kc_ref, vc_ref, kn_ref, vn_ref, qg_ref, c_ref, sa_ref, sb_ref, o_ref,
                       qb_ref, m_ref, l_ref, acc_ref, *, lc, ck):
    ci = pl.program_id(1)
    rows = q_ref.shape[1]

    @pl.when(ci == 0)
    def _():
        for h in range(H_DIL):
            cols = slice(h * HEAD_DIM, (h + 1) * HEAD_DIM)
            q = _rope(_rms(q_ref[0, :, cols], qg_ref[...]), c_ref[...], sa_ref[...], sb_ref[...], ROPE_DIMS // 2)
            qb_ref[h] = q * SCALE
            m_ref[h], l_ref[h], acc_ref[h] = _softmax_init(rows)

    w_c = _dil_weight(lc + _iota((rows, ck), 0) - (ci * ck + _iota((rows, ck), 1)))
    for h in range(H_DIL):
        keys = pl.ds(h, ck, stride=H_DIL)
        m_ref[h], l_ref[h], acc_ref[h] = _softmax_step(
            _dot_nt(qb_ref[h].astype(BF16), kc_ref[0, keys, :].astype(BF16)), w_c, vc_ref[0, keys, :].astype(BF16),
            m_ref[h], l_ref[h], acc_ref[h])

    @pl.when(ci == pl.num_programs(1) - 1)
    def _():
        w_n = _dil_weight(_iota((rows, KEY_BLOCK), 0) - _iota((rows, KEY_BLOCK), 1))
        for h in range(H_DIL):
            cols = slice(h * HEAD_DIM, (h + 1) * HEAD_DIM)
            kb = _pad_rows(kn_ref[0, :, cols], KEY_BLOCK).astype(BF16)
            vb = _pad_rows(vn_ref[0, :, cols], KEY_BLOCK).astype(BF16)
            _, l, acc = _softmax_step(_dot_nt(qb_ref[h].astype(BF16), kb), w_n, vb, m_ref[h], l_ref[h], acc_ref[h])
            o_ref[0, :, cols] = _softmax_done(l, acc).astype(BF16)


def dilated_sample(proj3, q_cb, kc3, vc3, seq_off, kn3, vn3, qg, tables):
    n, rows, _ = proj3.shape
    lc = kc3.shape[1] // H_DIL
    wid = H_DIL * HEAD_DIM
    ck = lc // 2 if lc % (2 * KEY_BLOCK) == 0 else lc
    assert (q_cb * HEAD_DIM) % wid == 0
    tab = pl.BlockSpec((rows, HEAD_DIM), lambda b, c: (0, 0))
    cache = pl.BlockSpec((1, ck * H_DIL, HEAD_DIM), lambda b, c: (seq_off + b, c, 0))
    new = pl.BlockSpec((1, rows, wid), lambda b, c: (b, 0, 0))
    return pl.pallas_call(
        functools.partial(_dil_sample_kernel, lc=lc, ck=ck),
        grid=(n, lc // ck),
        in_specs=[pl.BlockSpec((1, rows, wid), lambda b, c: (b, 0, q_cb * HEAD_DIM // wid)), cache, cache, new, new,
                  pl.BlockSpec((1, HEAD_DIM), lambda b, c: (0, 0)), tab, tab, tab],
        out_specs=new,
        out_shape=jax.ShapeDtypeStruct((n, rows, wid), BF16),
        scratch_shapes=[pltpu.VMEM((H_DIL, rows, HEAD_DIM), F32), pltpu.VMEM((H_DIL, rows, 1), F32),
                        pltpu.VMEM((H_DIL, rows, 1), F32), pltpu.VMEM((H_DIL, rows, HEAD_DIM), F32)],
        compiler_params=_cp("parallel", "arbitrary"),
        name="dilated_sample",
    )(proj3, kc3, vc3, kn3, vn3, qg.reshape(1, HEAD_DIM), *tables)


def _nsa_q(q_ref, g, qg_ref, c_ref, sa_ref, sb_ref):
    qs = []
    for hl in range(HPG):
        col = (g * HPG + hl) * HEAD_DIM
        x = _rms(q_ref[0][:, col:col + HEAD_DIM], qg_ref[...])
        qs.append(_rope(x, c_ref[...], sa_ref[...], sb_ref[...], ROPE_DIMS // 2) * SCALE)
    return jnp.concatenate(qs, axis=0).astype(BF16)


def _nsa_compressed(q4, kc, vc, qpos, nb):
    rows = qpos.shape[0]
    nbp = kc.shape[0]
    blk = _iota((rows, nbp), 1)
    cmask = jnp.where((blk + 1) * NSA_BLOCK - 1 <= qpos, 1.0, 0.0)
    cmask4 = jnp.concatenate([cmask] * HPG, axis=0)
    s = jnp.where(cmask4 > 0, _dot_nt(q4, kc.astype(BF16)), NEG)
    p = jnp.exp(s - jnp.max(s, axis=-1, keepdims=True)) * cmask4
    p = p / jnp.maximum(jnp.sum(p, axis=-1, keepdims=True), 1e-30)
    o_c = _dot(p.astype(BF16), vc.astype(BF16))
    imp = p[0:rows]
    for hl in range(1, HPG):
        imp = imp + p[hl * rows:(hl + 1) * rows]
    shift = NSA_BLOCK.bit_length() - 1
    if rows == nbp:
        nbr = -(-nb // SUBLANES) * SUBLANES
        blk_t = _iota((nbr, rows), 0)
        cur_t = jnp.right_shift(qpos[0:1, :] + _iota((1, rows), 1), shift)
        cand_t = blk_t < cur_t
        imp_t = jnp.where(cand_t, imp.T[:nbr], -jnp.inf)
        rank = jnp.zeros((nbr, rows), F32)
        for b in range(nb):
            row = imp_t[b:b + 1, :]
            tie = jnp.where(row == imp_t, jnp.where(blk_t > b, 1.0, 0.0), 0.0)
            rank = rank + jnp.where(row > imp_t, 1.0, tie)
        sel_t = jnp.where(cand_t, jnp.where(rank < min(NSA_TOPK, nb), 1.0, 0.0), 0.0)
        sel = _pad_rows(sel_t, nbp).T
    else:
        cand = blk < jnp.right_shift(qpos, shift)
        imp = jnp.where(cand, imp, -jnp.inf)
        rank = jnp.zeros((rows, nbp), F32)
        for b in range(nb):
            col = imp[:, b:b + 1]
            tie = jnp.where(col == imp, jnp.where(blk > b, 1.0, 0.0), 0.0)
            rank = rank + jnp.where(col > imp, 1.0, tie)
        sel = jnp.where(cand, jnp.where(rank < min(NSA_TOPK, nb), 1.0, 0.0), 0.0)
    sel = jnp.where(blk == jnp.right_shift(qpos, shift), 1.0, sel)
    return o_c, sel


def _nsa_combine(gates, g, o_c, o_s, o_w, rows):
    outs = []
    for hl in range(HPG):
        r = slice(hl * rows, (hl + 1) * rows)
        col = 3 * hl
        outs.append(gates[:, col:col + 1] * o_c[r] + gates[:, col + 1:col + 2] * o_s[r]
                    + gates[:, col + 2:col + 3] * o_w[r])
    return outs


def _nsa_prompt_kernel(q_ref, gt_ref, kc_ref, vc_ref, ks_ref, vs_ref, kw_ref, vw_ref, e_ref,
                       qg_ref, c_ref, sa_ref, sb_ref, o_ref, selx_ref, s_ref, mx_ref, ls_ref, acc_ref, *, tq, nb, t):
    qi = pl.program_id(2)
    q4 = _nsa_q(q_ref, 0, qg_ref, c_ref, sa_ref, sb_ref)
    qpos = qi * tq + _iota((tq, 1), 0)
    o_c, sel = _nsa_compressed(q4, kc_ref[0], vc_ref[0], qpos, nb)
    selx_ref[...] = _dot(sel.astype(BF16), e_ref[...])

    ck = _tile(t, 4 * KEY_BLOCK)
    nchunks = lax.div(qi * tq + (tq + ck - 1), ck)
    lane = _iota((tq, ck), 1)
    mx_ref[...] = jnp.full(mx_ref.shape, NEG, F32)

    def scores(c, _):
        off = pl.multiple_of(c * ck, ck)
        bias = jnp.where(off + lane <= qpos, (selx_ref[:, pl.ds(off, ck)] - 1.0) * -NEG, NEG)
        s = _dot_nt(q4, ks_ref[0, pl.ds(off, ck), :].astype(BF16))
        s = (s.reshape(HPG, tq, ck) + bias[None]).reshape(HPG * tq, ck)
        s_ref[:, pl.ds(off, ck)] = s
        part = s[:, :KEY_BLOCK]
        for j in range(1, ck // KEY_BLOCK):
            part = jnp.maximum(part, s[:, j * KEY_BLOCK:(j + 1) * KEY_BLOCK])
        mx_ref[...] = jnp.maximum(mx_ref[...], part)
        return 0

    lax.fori_loop(0, nchunks, scores, 0)
    m = jnp.max(mx_ref[...], axis=-1, keepdims=True)
    ls_ref[...] = jnp.zeros(ls_ref.shape, F32)
    acc_ref[...] = jnp.zeros(acc_ref.shape, F32)

    def values(c, _):
        off = pl.multiple_of(c * ck, ck)
        s = s_ref[:, pl.ds(off, ck)]
        p = jnp.exp(s - m)
        part = p[:, :KEY_BLOCK]
        for j in range(1, ck // KEY_BLOCK):
            part = part + p[:, j * KEY_BLOCK:(j + 1) * KEY_BLOCK]
        ls_ref[...] += part
        acc_ref[...] += _dot(p.astype(BF16), vs_ref[0, pl.ds(off, ck), :].astype(BF16))
        return 0

    lax.fori_loop(0, nchunks, values, 0)
    o_s = acc_ref[...] / jnp.maximum(jnp.sum(ls_ref[...], axis=-1, keepdims=True), 1e-30)

    span = min(NSA_WINDOW + KEY_BLOCK, t)
    first = jnp.clip(qi - NSA_WINDOW // KEY_BLOCK, 0, (t - span) // KEY_BLOCK)
    off = pl.multiple_of(first * KEY_BLOCK, KEY_BLOCK)
    d = qpos - (off + _iota((tq, span), 1))
    bias = jnp.where(d >= 0, jnp.where(d < NSA_WINDOW, 0.0, NEG), NEG)
    s = _dot_nt(q4, kw_ref[0, pl.ds(off, span), :].astype(BF16))
    s = (s.reshape(HPG, tq, span) + bias[None]).reshape(HPG * tq, span)
    p = jnp.exp(s - jnp.max(s, axis=-1, keepdims=True))
    o_w = _dot(p.astype(BF16), vw_ref[0, pl.ds(off, span), :].astype(BF16)) / jnp.sum(p, axis=-1, keepdims=True)
    gates = jax.nn.sigmoid(gt_ref[0])
    for hl, o in enumerate(_nsa_combine(gates, 0, o_c, o_s, o_w, tq)):
        o_ref[0, :, hl * HEAD_DIM:(hl + 1) * HEAD_DIM] = o.astype(BF16)


def nsa_prompt(proj3, gate_cb, kc3, vc3, ks3, vs3, kw3, vw3, expand, qg, tables, nb):
    n, t, _ = proj3.shape
    tq = KEY_BLOCK
    nbp = kc3.shape[1]
    gw = HPG * HEAD_DIM
    tab = pl.BlockSpec((tq, HEAD_DIM), lambda b, g, i: (i, 0))
    cmp_spec = pl.BlockSpec((1, nbp, HEAD_DIM), lambda b, g, i: (b, 0, g))
    kv = pl.BlockSpec((1, t, HEAD_DIM), lambda b, g, i: (b, 0, g))
    return pl.pallas_call(
        functools.partial(_nsa_prompt_kernel, tq=tq, nb=nb, t=t),
        grid=(n, G_NSA, t // tq),
        in_specs=[pl.BlockSpec((1, tq, gw), lambda b, g, i: (b, i, g)),
                  pl.BlockSpec((1, tq, HEAD_DIM), lambda b, g, i: (b, i, gate_cb + g)),
                  cmp_spec, cmp_spec, kv, kv, kv, kv,
                  pl.BlockSpec((nbp, t), lambda b, g, i: (0, 0)),
                  pl.BlockSpec((1, HEAD_DIM), lambda b, g, i: (0, 0)), tab, tab, tab],
        out_specs=pl.BlockSpec((1, tq, gw), lambda b, g, i: (b, i, g)),
        out_shape=jax.ShapeDtypeStruct((n, t, H_NSA * HEAD_DIM), BF16),
        scratch_shapes=[pltpu.VMEM((tq, t), F32), pltpu.VMEM((HPG * tq, t), F32)]
        + [pltpu.VMEM((HPG * tq, HEAD_DIM), F32)] * 3,
        compiler_params=_cp("parallel", "parallel", "arbitrary"),
        name="nsa_prompt",
    )(proj3, proj3, kc3, vc3, ks3, vs3, kw3, vw3, expand, qg.reshape(1, HEAD_DIM), *tables)


def _compress_rows_kernel(k_ref, v_ref, pk_ref, pv_ref, ko_ref, vo_ref):
    for x_ref, p_ref, o_ref in ((k_ref, pk_ref, ko_ref), (v_ref, pv_ref, vo_ref)):
        x = x_ref[0]
        nblk = x.shape[0] // NSA_BLOCK
        pw = jnp.concatenate([p_ref[...]] * nblk, axis=0)
        o_ref[0] = jnp.sum((x * pw).reshape(nblk, NSA_BLOCK, x.shape[1]), axis=1)


def compress_rows(k3, v3, pwk2, pwv2):
    n, t, w = k3.shape
    tr = _tile(t, 512)
    row = pl.BlockSpec((1, tr, w), lambda b, i: (b, i, 0))
    pw = pl.BlockSpec((NSA_BLOCK, w), lambda b, i: (0, 0))
    out = pl.BlockSpec((1, tr // NSA_BLOCK, w), lambda b, i: (b, i, 0))
    return pl.pallas_call(
        _compress_rows_kernel,
        grid=(n, t // tr),
        in_specs=[row, row, pw, pw],
        out_specs=[out, out],
        out_shape=[jax.ShapeDtypeStruct((n, t // NSA_BLOCK, w), F32)] * 2,
        compiler_params=_cp("parallel", "parallel"),
        name="compress_rows",
    )(k3, v3, pwk2, pwv2)


def _compress_pages_kernel(pt_ref, *refs, pps):
    k_refs, v_refs = refs[:pps], refs[pps:2 * pps]
    pk_ref, pv_ref, ko_ref, vo_ref = refs[2 * pps:]
    bpp = PAGE_SIZE // NSA_BLOCK
    prows = PAGE_SIZE * G_NSA
    for x_refs, p_ref, o_ref in ((k_refs, pk_ref, ko_ref), (v_refs, pv_ref, vo_ref)):
        pw = p_ref[...]
        outs = []
        for x_ref in x_refs:
            z = jnp.sum((x_ref[0] * pw).reshape(bpp, prows // bpp // SUBLANES, SUBLANES, HEAD_DIM), axis=1)
            per_g = [sum(z[:, s] for s in range(g, SUBLANES, G_NSA)) for g in range(G_NSA)]
            outs.append(jnp.concatenate(per_g, axis=1))
        o_ref[0] = jnp.concatenate(outs, axis=0)


def compress_pages(pool_k, pool_v, page_off, page_table, pwk, pwv, pps):
    n, n_pages = page_table.shape
    w = G_NSA * HEAD_DIM
    prows = PAGE_SIZE * G_NSA
    bpp = PAGE_SIZE // NSA_BLOCK
    assert SUBLANES % G_NSA == 0
    page_w = lambda p: jnp.repeat(jnp.tile(p, (bpp, 1)), G_NSA, axis=0)
    pwk2, pwv2 = page_w(pwk), page_w(pwv)
    page = lambda i: pl.BlockSpec((1, prows, HEAD_DIM),
                                  functools.partial(lambda b, c, pt, i: (page_off + pt[b, c * pps + i], 0, 0), i=i))
    pw = pl.BlockSpec((prows, HEAD_DIM), lambda b, c, pt: (0, 0))
    out = pl.BlockSpec((1, pps * bpp, w), lambda b, c, pt: (b, c, 0))
    return pl.pallas_call(
        functools.partial(_compress_pages_kernel, pps=pps),
        grid_spec=pltpu.PrefetchScalarGridSpec(
            num_scalar_prefetch=1, grid=(n, n_pages // pps),
            in_specs=[page(i) for i in range(pps)] * 2 + [pw, pw],
            out_specs=[out, out]),
        out_shape=[jax.ShapeDtypeStruct((n, n_pages * bpp, w), F32)] * 2,
        compiler_params=_cp("parallel", "parallel"),
        name="compress_pages",
    )(page_table, *([pool_k] * pps), *([pool_v] * pps), pwk2, pwv2)


def _nsa_sample_kernel(pt_ref, *refs, pps, past, rows, nb, lw):
    (q_ref, gt_ref, kc_ref, vc_ref), refs = refs[:4], refs[4:]
    kp_refs, vp_refs, refs = refs[:pps], refs[pps:2 * pps], refs[2 * pps:]
    (kn_ref, vn_ref, kwc_ref, vwc_ref, kwn_ref, vwn_ref, e_ref, qg_ref, c_ref, sa_ref, sb_ref,
     o_ref, q4_ref, sel_ref, oc_ref, m_ref, l_ref, acc_ref) = refs
    pc = pl.program_id(1)
    r4 = HPG * rows
    qpos = past + _iota((rows, 1), 0)
    lane = _iota((rows, KEY_BLOCK), 1)

    @pl.when(pc == 0)
    def _():
        for g in range(G_NSA):
            cols = slice(g * HEAD_DIM, (g + 1) * HEAD_DIM)
            q4 = _nsa_q(q_ref, g, qg_ref, c_ref, sa_ref, sb_ref)
            o_c, sel = _nsa_compressed(q4, kc_ref[0][:, cols], vc_ref[0][:, cols], qpos, nb)
            q4_ref[g] = q4
            sel_ref[g] = sel
            oc_ref[g] = o_c
            m0, l0, a0 = _softmax_init(r4)
            m_ref[g], l_ref[g], acc_ref[g] = m0, l0, a0

    nkeys = pps * PAGE_SIZE
    kpos = pc * nkeys + _iota((rows, nkeys), 1)
    for g in range(G_NSA):
        cols = slice(g * HEAD_DIM, (g + 1) * HEAD_DIM)
        grp = pl.ds(g, PAGE_SIZE, stride=G_NSA)
        kb = jnp.concatenate([r[0, grp, :].astype(BF16) for r in kp_refs], axis=0)
        vb = jnp.concatenate([r[0, grp, :].astype(BF16) for r in vp_refs], axis=0)
        selx = _dot(sel_ref[g].astype(BF16), e_ref[...])
        w = jnp.where(kpos <= qpos, selx, 0.0)
        m_ref[g], l_ref[g], acc_ref[g] = _softmax_step(
            _dot_nt(q4_ref[g], kb), jnp.concatenate([w] * HPG, axis=0), vb, m_ref[g], l_ref[g], acc_ref[g])

    @pl.when(pc == pl.num_programs(1) - 1)
    def _():
        gates_all = jax.nn.sigmoid(gt_ref[0])
        tq = _iota((rows, KEY_BLOCK), 0)
        for g in range(G_NSA):
            cols = slice(g * HEAD_DIM, (g + 1) * HEAD_DIM)
            q4 = q4_ref[g]
            kb = _pad_rows(kn_ref[0][:, cols], KEY_BLOCK).astype(BF16)
            vb = _pad_rows(vn_ref[0][:, cols], KEY_BLOCK).astype(BF16)
            w = jnp.where(lane <= tq, 1.0, 0.0)
            _, l, acc = _softmax_step(_dot_nt(q4, kb), jnp.concatenate([w] * HPG, axis=0), vb,
                                      m_ref[g], l_ref[g], acc_ref[g])
            o_s = _softmax_done(l, acc)
            tqw = _iota((rows, lw), 0)
            d = tqw + lw - _iota((rows, lw), 1)
            w = jnp.where(d < NSA_WINDOW, jnp.where(past - d + tqw >= 0, 1.0, 0.0), 0.0)
            grp = pl.ds(g, lw, stride=G_NSA)
            carry = _softmax_step(_dot_nt(q4, kwc_ref[0, grp, :].astype(BF16)), jnp.concatenate([w] * HPG, axis=0),
                                  vwc_ref[0, grp, :].astype(BF16), *_softmax_init(r4))
            kb = _pad_rows(kwn_ref[0][:, cols], KEY_BLOCK).astype(BF16)
            vb = _pad_rows(vwn_ref[0][:, cols], KEY_BLOCK).astype(BF16)
            w = jnp.where(lane <= tq, 1.0, 0.0)
            _, l, acc = _softmax_step(_dot_nt(q4, kb), jnp.concatenate([w] * HPG, axis=0), vb, *carry)
            o_w = _softmax_done(l, acc)
            gates = gates_all[:, g * HEAD_DIM:(g + 1) * HEAD_DIM]
            for hl, o in enumerate(_nsa_combine(gates, g, oc_ref[g], o_s, o_w, rows)):
                col = (g * HPG + hl) * HEAD_DIM
                o_ref[0, :, col:col + HEAD_DIM] = o.astype(BF16)


def nsa_sample(proj3, gate_cb, kc3, vc3, pool_k, pool_v, page_off, page_table, kn3, vn3, kwc3, vwc3, seq_off,
               kwn3, vwn3, expand, qg, tables, pps):
    n, rows, _ = proj3.shape
    n_pages = page_table.shape[1]
    past = n_pages * PAGE_SIZE
    nbp = kc3.shape[1]
    w = G_NSA * HEAD_DIM
    lw = kwc3.shape[1] // G_NSA
    qw = H_NSA * HEAD_DIM
    r4 = HPG * rows
    fixed = lambda shape: pl.BlockSpec(shape, lambda b, c, pt: (0,) * len(shape))
    per_seq = lambda r, width, cb=0, off=0: pl.BlockSpec(
        (1, r, width), functools.partial(lambda b, c, pt, cb, off: (off + b, 0, cb), cb=cb, off=off))
    page = lambda i: pl.BlockSpec((1, PAGE_SIZE * G_NSA, HEAD_DIM),
                                  functools.partial(lambda b, c, pt, i: (page_off + pt[b, c * pps + i], 0, 0), i=i))
    win = per_seq(lw * G_NSA, HEAD_DIM, 0, seq_off)
    in_specs = ([per_seq(rows, qw), per_seq(rows, w, gate_cb // G_NSA), per_seq(nbp, w), per_seq(nbp, w)]
                + [page(i) for i in range(pps)] * 2
                + [per_seq(rows, w), per_seq(rows, w), win, win, per_seq(rows, w), per_seq(rows, w),
                   pl.BlockSpec((nbp, pps * PAGE_SIZE), lambda b, c, pt: (0, c)),
                   fixed((1, HEAD_DIM)), fixed((rows, HEAD_DIM)), fixed((rows, HEAD_DIM)), fixed((rows, HEAD_DIM))])
    return pl.pallas_call(
        functools.partial(_nsa_sample_kernel, pps=pps, past=past, rows=rows, nb=past // NSA_BLOCK, lw=lw),
        grid_spec=pltpu.PrefetchScalarGridSpec(
            num_scalar_prefetch=1, grid=(n, n_pages // pps),
            in_specs=in_specs,
            out_specs=pl.BlockSpec((1, rows, qw), lambda b, c, pt: (b, 0, 0)),
            scratch_shapes=[pltpu.VMEM((G_NSA, r4, HEAD_DIM), BF16), pltpu.VMEM((G_NSA, rows, nbp), F32),
                            pltpu.VMEM((G_NSA, r4, HEAD_DIM), F32), pltpu.VMEM((G_NSA, r4, 1), F32),
                            pltpu.VMEM((G_NSA, r4, 1), F32), pltpu.VMEM((G_NSA, r4, HEAD_DIM), F32)]),
        out_shape=jax.ShapeDtypeStruct((n, rows, qw), BF16),
        compiler_params=_cp("parallel", "arbitrary"),
        name="nsa_sample",
    )(page_table, proj3, proj3, kc3, vc3, *([pool_k] * pps), *([pool_v] * pps), kn3, vn3, kwc3, vwc3, kwn3, vwn3,
      expand, qg.reshape(1, HEAD_DIM), *tables)


def _lru_kernel(x_ref, y_ref, hist_ref, h0_ref, cw_ref, cb_ref, wa_ref, ba_ref, wi_ref, bi_ref, sp_ref,
                o_ref, tail_ref, a_ref, b_ref, *, cps):
    t = x_ref.shape[1]
    bw = D_RNN // RNN_BLOCKS
    row8 = _iota((SUBLANES, bw), 0)
    for k in range(cps):
        cols = slice(k * bw, (k + 1) * bw)
        x = x_ref[0, :, cols]
        hist = hist_ref[0, :, cols]
        xc = cb_ref[:, cols] + x * cw_ref[CONV_W - 1:CONV_W, cols]
        for s in range(1, CONV_W):
            sh = pltpu.roll(x, s, 0)
            head = jnp.where(row8 >= s, sh[:SUBLANES], pltpu.roll(hist, (SUBLANES - (CONV_W - 1) + s) % SUBLANES, 0))
            sh = head if t == SUBLANES else jnp.concatenate([head, sh[SUBLANES:]], axis=0)
            xc = xc + sh * cw_ref[CONV_W - 1 - s:CONV_W - s, cols]
        xb = xc.astype(BF16)
        r = jax.nn.sigmoid(_dot(xb, wa_ref[k]) + ba_ref[:, cols])
        gi = jax.nn.sigmoid(_dot(xb, wi_ref[k]) + bi_ref[:, cols])
        log_a = -LRU_C * r * sp_ref[:, cols]
        th = jnp.tanh(log_a)
        a_ref[:, cols] = jnp.exp(log_a)
        b_ref[:, cols] = jnp.sqrt(-2.0 * th / (1.0 - th)) * (gi * xc)

    wid = cps * bw
    rowsw = _iota((SUBLANES, wid), 0)

    def body(gidx, hc):
        off = pl.multiple_of(gidx * SUBLANES, SUBLANES)
        a = a_ref[pl.ds(off, SUBLANES), :]
        b = b_ref[pl.ds(off, SUBLANES), :]
        for s in (1, 2, 4):
            ok = rowsw >= s
            b = jnp.where(ok, a * pltpu.roll(b, s, 0) + b, b)
            a = jnp.where(ok, a * pltpu.roll(a, s, 0), a)
        hrows = a * hc + b
        a_ref[pl.ds(off, SUBLANES), :] = hrows
        return jnp.broadcast_to(hrows[SUBLANES - 1:SUBLANES, :], (SUBLANES, wid))

    lax.fori_loop(0, t // SUBLANES, body, jnp.broadcast_to(h0_ref[0], (SUBLANES, wid)))
    tail_ref[0] = a_ref[t - SUBLANES:, :]
    chunk = _tile(t, 512)
    for c in range(t // chunk):
        rws = slice(c * chunk, (c + 1) * chunk)
        o_ref[0, rws, :] = (jax.nn.gelu(y_ref[0, rws, :], approximate=True) * a_ref[rws, :]).astype(BF16)


def rglru(proj3, x_cb, y_cb, hist8, h0, conv_w, conv_b, w_a, b_a, w_i, b_i, softplus_neg_lam):
    n, t, _ = proj3.shape
    bw = D_RNN // RNN_BLOCKS
    cps = 4 if t > SUBLANES else RNN_BLOCKS
    wid = cps * bw
    assert (x_cb * HEAD_DIM) % wid == 0 and (y_cb * HEAD_DIM) % wid == 0 and bw == HEAD_DIM
    vec = lambda a: a.reshape(1, D_RNN)
    vspec = pl.BlockSpec((1, wid), lambda b, k: (0, k))
    mspec = pl.BlockSpec((cps, bw, bw), lambda b, k: (k, 0, 0))
    return pl.pallas_call(
        functools.partial(_lru_kernel, cps=cps),
        grid=(n, RNN_BLOCKS // cps),
        in_specs=[pl.BlockSpec((1, t, wid), lambda b, k: (b, 0, x_cb * HEAD_DIM // wid + k)),
                  pl.BlockSpec((1, t, wid), lambda b, k: (b, 0, y_cb * HEAD_DIM // wid + k)),
                  pl.BlockSpec((1, SUBLANES, wid), lambda b, k: (b, 0, k)),
                  pl.BlockSpec((1, 1, wid), lambda b, k: (b, 0, k)),
                  pl.BlockSpec((CONV_W, wid), lambda b, k: (0, k)),
                  vspec, mspec, vspec, mspec, vspec, vspec],
        out_specs=[pl.BlockSpec((1, t, wid), lambda b, k: (b, 0, k)),
                   pl.BlockSpec((1, SUBLANES, wid), lambda b, k: (b, 0, k))],
        out_shape=[jax.ShapeDtypeStruct((n, t, D_RNN), BF16), jax.ShapeDtypeStruct((n, SUBLANES, D_RNN), F32)],
        scratch_shapes=[pltpu.VMEM((t, wid), F32), pltpu.VMEM((t, wid), F32)],
        compiler_params=_cp("parallel", "parallel"),
        name="rglru",
    )(proj3, proj3, hist8, h0.reshape(n, 1, D_RNN), conv_w, vec(conv_b),
      w_a.astype(BF16), vec(b_a), w_i.astype(BF16), vec(b_i), vec(softplus_neg_lam))


def _rope_tables(pos, n_rot, theta):
    half = n_rot // 2
    inv = 1.0 / (theta ** (jnp.arange(half, dtype=F32) * (2.0 / n_rot)))
    ang = pos.astype(F32)[:, None] * inv[None, :]
    cos, sin = jnp.cos(ang), jnp.sin(ang)
    p = pos.shape[0]
    rest = HEAD_DIM - n_rot
    zh = jnp.zeros((p, half), F32)
    c = jnp.concatenate([cos, cos, jnp.ones((p, rest), F32)], axis=1)
    sa = jnp.concatenate([-sin, zh, jnp.zeros((p, rest), F32)], axis=1)
    sb = jnp.concatenate([zh, sin, jnp.zeros((p, rest), F32)], axis=1)
    return c, sa, sb


def _block_expand(nbp, length):
    return (jnp.arange(length, dtype=jnp.int32)[None, :] // NSA_BLOCK
            == jnp.arange(nbp, dtype=jnp.int32)[:, None]).astype(BF16)


def _pad_axis(a, axis, size):
    pad = [(0, 0)] * a.ndim
    pad[axis] = (0, size - a.shape[axis])
    return jnp.pad(a, pad)


def _odd_w_in(w):
    qw = H_NSA * HEAD_DIM
    qkv = qw + 6 * G_NSA * HEAD_DIM
    ng = 3 * H_NSA
    gate = w[:, qkv:qkv + ng]
    per = 3 * HPG
    gcols = [_pad_axis(gate[:, g * per:(g + 1) * per], 1, LANES) for g in range(G_NSA)]
    out = jnp.concatenate([w[:, :qw], w[:, qkv + ng:], w[:, qw:qkv]] + gcols, axis=1)
    return _pad_axis(out, 1, -(-out.shape[1] // 1024) * 1024)


def kernel(x_prompt, x_sample, mem_prompt, state_ret, cache_dil_k, cache_dil_v, cache_nsa_cmp_k, cache_nsa_cmp_v, cache_nsa_slc_k, cache_nsa_slc_v, cache_nsa_win_k, cache_nsa_win_v, state_lru_h, state_lru_conv, cache_mem_k, cache_mem_v, page_table, ev_w_in, ret_gn_g, dil_q_norm_g, dil_k_norm_g, ev_w_out, od_w_in, nsa_q_norm_g, nsa_k_norm_g, nsa_pw_k, nsa_pw_v, lru_conv_w, lru_conv_b, lru_w_a, lru_b_a, lru_w_i, lru_b_i, lru_lambda, od_w_out, norm_mix_g, norm_mem_g, norm_mlp_g, mem_norm_g, mem_w_q, mem_w_k, mem_w_v, mem_q_norm_g, mem_k_norm_g, mem_w_o, mlp_w1, mlp_w2):
    nb_p, t_p, d = x_prompt.shape
    nb_s, t_s, _ = x_sample.shape
    depth = norm_mix_g.shape[0]
    n_pages = page_table.shape[1]
    past = n_pages * PAGE_SIZE
    rs = SAMPLE_ROWS
    assert t_p % RET_CHUNK == 0 and t_s <= rs and past % PAGE_SIZE == 0 and t_s < NSA_BLOCK
    assert cache_nsa_win_k.shape[2] % KEY_BLOCK == 0 and cache_dil_k.shape[2] % KEY_BLOCK == 0
    hd = HEAD_DIM

    xp = x_prompt.reshape(nb_p * t_p, d)
    xs = _pad_axis(x_sample, 1, rs).reshape(nb_s * rs, d)
    mem2 = mem_prompt.reshape(-1, d)
    n_mem = mem_prompt.shape[1]

    pos_p = jnp.arange(t_p, dtype=jnp.int32)
    pos_s = past + jnp.arange(rs, dtype=jnp.int32)
    tile_s = lambda tabs: tuple(jnp.tile(tb, (nb_s, 1)) for tb in tabs)
    rope_p, rope_s = _rope_tables(pos_p, ROPE_DIMS, ROPE_THETA), _rope_tables(pos_s, ROPE_DIMS, ROPE_THETA)
    ret_p, ret_s = _rope_tables(pos_p, hd, RET_THETA), _rope_tables(pos_s, hd, RET_THETA)
    log_g = jnp.log1p(-jnp.exp2(-5.0 - jnp.arange(H_RET, dtype=F32)))

    ev_w_in_b, ev_w_out_b, od_w_out_b = ev_w_in.astype(BF16), ev_w_out.astype(BF16), od_w_out.astype(BF16)
    od_w_in_b = jnp.stack([_odd_w_in(od_w_in[o]) for o in range(od_w_in.shape[0])]).astype(BF16)
    mem_wq_b, mem_wo_b = mem_w_q.astype(BF16), mem_w_o.astype(BF16)
    mem_wkv_b = jnp.concatenate([mem_w_k, mem_w_v], axis=2).astype(BF16)
    mlp_w1_b, mlp_w2_b = mlp_w1.astype(BF16), mlp_w2.astype(BF16)

    outs = {k: [] for k in ("ret_p", "ret_s", "dk_p", "dv_p", "dk_s", "dv_s", "ck_p", "cv_p", "ck_s", "cv_s",
                            "sk_p", "sv_p", "sk_s", "sv_s", "wk_p", "wv_p", "wk_s", "wv_s", "lh_p", "lh_s",
                            "lc_p", "lc_s", "mk_p", "mv_p")}
    new_rows = lambda a, w: a.reshape(nb_s, rs, w)[:, :t_s]

    for layer in range(depth):
        if layer % 2 == 0:
            e = layer // 2
            w_in, w_out, wl = ev_w_in_b, ev_w_out_b, e
            hw = H_RET * hd
            ones = jnp.ones((1, hd), F32)
            gains = jnp.concatenate([dil_k_norm_g[e][None], ones], axis=0)
            sections = [(5, "norm_rope"), (6, "copy")]
            proj = rms_matmul(xp, norm_mix_g[layer], w_in, wl, 1024)
            proj3 = proj.reshape(nb_p, t_p, -1)
            y_ret, s_new = retention(proj3, jnp.zeros((nb_p, H_RET, hd, hd), F32), log_g, ret_gn_g[e], ret_p, RET_CHUNK)
            dk, dv = kv_transform(proj, sections, gains, rope_p, H_DIL, t_p)
            y_dil = dilated_prompt(proj3, 4 * H_RET, dk.reshape(nb_p, t_p, hw), dv.reshape(nb_p, t_p, hw),
                                   dil_q_norm_g[e], rope_p)
            xp = matmul_res(y_ret.reshape(-1, hw), y_dil.reshape(-1, hw), w_out, wl, xp)
            keep = min(DIL_PATTERNS[-1][0], t_p)
            outs["ret_p"].append(s_new)
            outs["dk_p"].append(dk.reshape(nb_p, t_p, H_DIL, hd)[:, t_p - keep:])
            outs["dv_p"].append(dv.reshape(nb_p, t_p, H_DIL, hd)[:, t_p - keep:])
            proj = rms_matmul(xs, norm_mix_g[layer], w_in, wl, 1024)
            proj3 = proj.reshape(nb_s, rs, -1)
            y_ret, s_new = retention(proj3, state_ret.reshape(-1, H_RET, hd, hd), log_g, ret_gn_g[e], ret_s, t_s,
                                     seq_off=e * nb_s)
            dk, dv = kv_transform(proj, sections, gains, tile_s(rope_s), H_DIL, nb_s * rs)
            lc = cache_dil_k.shape[2]
            y_dil = dilated_sample(proj3, 4 * H_RET, cache_dil_k.reshape(-1, lc * H_DIL, hd),
                                   cache_dil_v.reshape(-1, lc * H_DIL, hd), e * nb_s, dk.reshape(nb_s, rs, hw),
                                   dv.reshape(nb_s, rs, hw), dil_q_norm_g[e], rope_s)
            xs = matmul_res(y_ret.reshape(-1, hw), y_dil.reshape(-1, hw), w_out, wl, xs)
            outs["ret_s"].append(s_new)
            outs["dk_s"].append(new_rows(dk, hw).reshape(nb_s, t_s, H_DIL, hd))
            outs["dv_s"].append(new_rows(dv, hw).reshape(nb_s, t_s, H_DIL, hd))
        else:
            o = layer // 2
            w_in, w_out, wl = od_w_in_b, od_w_out_b, o
            qw = H_NSA * hd
            gw = G_NSA * hd
            x_cb = qw // hd
            y_cb = x_cb + D_RNN // hd
            kv_cb = (qw + 2 * D_RNN) // gw
            gate_cb = (qw + 2 * D_RNN + 6 * gw) // hd
            ones = jnp.ones((1, hd), F32)
            kg = nsa_k_norm_g[o]
            gains = jnp.concatenate([kg[0:1], ones, kg[1:2], ones, kg[2:3], ones], axis=0)
            sections = [(kv_cb + i, "norm_rope" if i % 2 == 0 else "copy") for i in range(6)]
            pwk2 = jnp.concatenate([nsa_pw_k[o]] * G_NSA, axis=1)
            pwv2 = jnp.concatenate([nsa_pw_v[o]] * G_NSA, axis=1)
            sp = jax.nn.softplus(-lru_lambda[o].astype(F32))
            lru_w = (lru_conv_w[o], lru_conv_b[o], lru_w_a[o], lru_b_a[o], lru_w_i[o], lru_b_i[o], sp)
            proj = rms_matmul(xp, norm_mix_g[layer], w_in, wl, 1024)
            proj3 = proj.reshape(nb_p, t_p, -1)
            kcm, vcm, ksl, vsl, kwn, vwn = kv_transform(proj, sections, gains, rope_p, G_NSA, t_p)
            r3 = lambda a: a.reshape(nb_p, t_p, gw)
            nb = t_p // NSA_BLOCK
            nbp = -(-nb // LANES) * LANES
            kc, vc = compress_rows(r3(kcm), r3(vcm), pwk2, pwv2)
            o_nsa = nsa_prompt(proj3, gate_cb, _pad_axis(kc, 1, nbp), _pad_axis(vc, 1, nbp), r3(ksl), r3(vsl),
                               r3(kwn), r3(vwn), _block_expand(nbp, t_p), nsa_q_norm_g[o], rope_p, nb)
            y_rnn, tail = rglru(proj3, x_cb, y_cb, jnp.zeros((nb_p, SUBLANES, D_RNN), F32),
                                jnp.zeros((nb_p, D_RNN), F32), *lru_w)
            xp = matmul_res(o_nsa.reshape(-1, qw), y_rnn.reshape(-1, D_RNN), w_out, wl, xp)
            r4 = lambda a: a.reshape(nb_p, t_p, G_NSA, hd)
            keep = min(NSA_WINDOW, t_p)
            for key, val in (("ck_p", r4(kcm)), ("cv_p", r4(vcm)), ("sk_p", r4(ksl)), ("sv_p", r4(vsl)),
                             ("wk_p", r4(kwn)[:, t_p - keep:]), ("wv_p", r4(vwn)[:, t_p - keep:])):
                outs[key].append(val)
            outs["lh_p"].append(tail[:, SUBLANES - 1])
            outs["lc_p"].append(proj3[:, t_p - (CONV_W - 1):, x_cb * hd:x_cb * hd + D_RNN])
            proj = rms_matmul(xs, norm_mix_g[layer], w_in, wl, 1024)
            proj3 = proj.reshape(nb_s, rs, -1)
            kcm, vcm, ksl, vsl, kwn, vwn = kv_transform(proj, sections, gains, tile_s(rope_s), G_NSA, nb_s * rs)
            r3 = lambda a: a.reshape(nb_s, rs, gw)
            nph = cache_nsa_cmp_k.shape[1]
            pool = lambda a: a.reshape(-1, PAGE_SIZE * G_NSA, hd)
            pps = math.gcd(n_pages, 16)
            nb = (past + t_s) // NSA_BLOCK
            nbp = -(-nb // LANES) * LANES
            kc, vc = compress_pages(pool(cache_nsa_cmp_k), pool(cache_nsa_cmp_v), o * nph, page_table,
                                    nsa_pw_k[o], nsa_pw_v[o], pps)
            lw = cache_nsa_win_k.shape[2]
            o_nsa = nsa_sample(proj3, gate_cb, _pad_axis(kc, 1, nbp), _pad_axis(vc, 1, nbp),
                               pool(cache_nsa_slc_k), pool(cache_nsa_slc_v), o * nph, page_table, r3(ksl), r3(vsl),
                               cache_nsa_win_k.reshape(-1, lw * G_NSA, hd), cache_nsa_win_v.reshape(-1, lw * G_NSA, hd),
                               o * nb_s, r3(kwn), r3(vwn), _block_expand(nbp, past), nsa_q_norm_g[o], rope_s, pps)
            hist8 = _pad_axis(state_lru_conv[o].astype(F32), 1, SUBLANES)
            y_rnn, tail = rglru(proj3, x_cb, y_cb, hist8, state_lru_h[o].astype(F32), *lru_w)
            xs = matmul_res(o_nsa.reshape(-1, qw), y_rnn.reshape(-1, D_RNN), w_out, wl, xs)
            for key, val in (("ck_s", kcm), ("cv_s", vcm), ("sk_s", ksl), ("sv_s", vsl), ("wk_s", kwn), ("wv_s", vwn)):
                outs[key].append(new_rows(val, gw).reshape(nb_s, t_s, G_NSA, hd))
            outs["lh_s"].append(tail[:, t_s - 1])
            lx = proj3[:, :t_s, x_cb * hd:x_cb * hd + D_RNN]
            outs["lc_s"].append(jnp.concatenate([state_lru_conv[o].astype(F32), lx], axis=1)[:, -(CONV_W - 1):])

        dm = H_MEM * hd
        mkv = rms_matmul(mem2, mem_norm_g[layer], mem_wkv_b, layer, dm)
        gains = jnp.concatenate([mem_k_norm_g[layer][None], jnp.ones((1, hd), F32)], axis=0)
        mk, mv = kv_transform(mkv, [(0, "norm"), (1, "copy")], gains, rope_p, H_MEM, n_mem)
        mk3, mv3 = mk.reshape(nb_p, n_mem, dm), mv.reshape(nb_p, n_mem, dm)
        outs["mk_p"].append(mk3.reshape(nb_p, n_mem, H_MEM, hd))
        outs["mv_p"].append(mv3.reshape(nb_p, n_mem, H_MEM, hd))
        xp = mem_attn_block(xp, norm_mem_g[layer], mem_wq_b, mem_q_norm_g[layer], mk3, mv3, mem_wo_b, layer, t_p)
        xs = mem_attn_block(xs, norm_mem_g[layer], mem_wq_b, mem_q_norm_g[layer],
                            cache_mem_k.reshape(-1, n_mem * H_MEM, hd), cache_mem_v.reshape(-1, n_mem * H_MEM, hd),
                            mem_wo_b, layer, rs, seq_off=layer * nb_s)
        xp = mlp_block(xp, norm_mlp_g[layer], mlp_w1_b, mlp_w2_b, layer)
        xs = mlp_block(xs, norm_mlp_g[layer], mlp_w1_b, mlp_w2_b, layer)

    st = lambda k: jnp.stack(outs[k])
    return (xp.reshape(nb_p, t_p, d), xs.reshape(nb_s, rs, d)[:, :t_s], st("ret_p"), st("ret_s"),
            st("dk_p"), st("dv_p"), st("dk_s"), st("dv_s"),
            st("ck_p"), st("cv_p"), st("ck_s"), st("cv_s"),
            st("sk_p"), st("sv_p"), st("sk_s"), st("sv_s"),
            st("wk_p"), st("wv_p"), st("wk_s"), st("wv_s"),
            st("lh_p"), st("lh_s"), st("lc_p"), st("lc_s"),
            st("mk_p"), st("mv_p"))
```

```python
import functools
import math

import jax
import jax.numpy as jnp
from jax import lax
from jax.experimental import pallas as pl
from jax.experimental.pallas import tpu as pltpu

F32 = jnp.float32
BF16 = jnp.bfloat16

HEAD_DIM = 128
ROPE_DIMS = HEAD_DIM // 4
ROPE_THETA = 500000.0
NORM_EPS = 1e-6
H_RET = 8
RET_CHUNK = 128
RET_THETA = 10000.0
H_DIL = 8
DIL_PATTERNS = ((128, 1), (512, 4), (2048, 16))
H_NSA = 8
G_NSA = 2
HPG = H_NSA // G_NSA
NSA_BLOCK = 64
NSA_TOPK = 16
NSA_WINDOW = 512
D_RNN = 1024
RNN_BLOCKS = 8
CONV_W = 4
LRU_C = 8.0
H_MEM = 4
PAGE_SIZE = 128

LANES = 128
SUBLANES = 8
KEY_BLOCK = 128
SAMPLE_ROWS = 8
VMEM_LIMIT = 48 * 1024 * 1024
NEG = -1e30
SCALE = HEAD_DIM ** -0.5


def _cp(*sem):
    return pltpu.CompilerParams(dimension_semantics=sem, vmem_limit_bytes=VMEM_LIMIT)


def _tile(n, pref):
    t = min(n, pref)
    while n % t:
        t -= SUBLANES
    return t


def _rms(x, g):
    return x * lax.rsqrt(jnp.mean(x * x, axis=-1, keepdims=True) + NORM_EPS) * g


def _rope(x, c, sa, sb, half):
    return x * c + pltpu.roll(x, LANES - half, 1) * sa + pltpu.roll(x, half, 1) * sb


def _dot(a, b):
    return jnp.dot(a, b, preferred_element_type=F32)


def _dot_nt(a, b):
    return lax.dot_general(a, b, (((1,), (1,)), ((), ())), preferred_element_type=F32)


def _pad_rows(x, rows):
    if x.shape[0] == rows:
        return x
    return jnp.concatenate([x, jnp.zeros((rows - x.shape[0], x.shape[1]), x.dtype)], axis=0)


def _softmax_step(s, w, vb, m, l, acc):
    s = jnp.where(w > 0, s, NEG)
    m_new = jnp.maximum(m, jnp.max(s, axis=-1, keepdims=True))
    alpha = jnp.exp(m - m_new)
    p = jnp.exp(s - m_new) * w
    l = alpha * l + jnp.sum(p, axis=-1, keepdims=True)
    acc = alpha * acc + _dot(p.astype(BF16), vb)
    return m_new, l, acc


def _softmax_init(rows):
    return (jnp.full((rows, 1), NEG, F32), jnp.zeros((rows, 1), F32), jnp.zeros((rows, HEAD_DIM), F32))


def _softmax_done(l, acc):
    return acc / jnp.maximum(l, 1e-30)


def _dil_weight(d):
    w = jnp.zeros(d.shape, F32)
    for window, dil in DIL_PATTERNS:
        hit = jnp.where(d <= window, 1.0, 0.0)
        if dil > 1:
            rem = (d & (dil - 1)) if dil & (dil - 1) == 0 else lax.rem(d, dil)
            hit = jnp.where(rem == 0, hit, 0.0)
        w = w + hit
    return jnp.where(d >= 0, w, 0.0)


def _iota(shape, dim):
    return lax.broadcasted_iota(jnp.int32, shape, dim)


def _rms_matmul_kernel(x_ref, g_ref, w_ref, o_ref, xn_ref):
    @pl.when(pl.program_id(1) == 0)
    def _():
        xn_ref[...] = _rms(x_ref[...], g_ref[...]).astype(BF16)

    o_ref[...] = _dot(xn_ref[...], w_ref[...])


def rms_matmul(x, g, w, layer, tn):
    m, d = x.shape
    n = w.shape[2]
    tm = _tile(m, 1024)
    return pl.pallas_call(
        _rms_matmul_kernel,
        grid=(m // tm, n // tn),
        in_specs=[pl.BlockSpec((tm, d), lambda i, j: (i, 0)),
                  pl.BlockSpec((1, d), lambda i, j: (0, 0)),
                  pl.BlockSpec((None, d, tn), lambda i, j: (layer, 0, j))],
        out_specs=pl.BlockSpec((tm, tn), lambda i, j: (i, j)),
        out_shape=jax.ShapeDtypeStruct((m, n), F32),
        scratch_shapes=[pltpu.VMEM((tm, d), BF16)],
        compiler_params=_cp("parallel", "arbitrary"),
        name="rms_matmul",
    )(x, g.reshape(1, d), w)


def _matmul_res_kernel(a1_ref, a2_ref, w1_ref, w2_ref, r_ref, o_ref):
    o_ref[...] = r_ref[...] + _dot(a1_ref[...], w1_ref[...]) + _dot(a2_ref[...], w2_ref[...])


def matmul_res(a1, a2, w, layer, res):
    m, k1 = a1.shape
    k2 = a2.shape[1]
    n = w.shape[2]
    assert k1 == k2 and w.shape[1] == k1 + k2
    tm = _tile(m, 1024)
    tn = _tile(n, 1024)
    return pl.pallas_call(
        _matmul_res_kernel,
        grid=(m // tm, n // tn),
        in_specs=[pl.BlockSpec((tm, k1), lambda i, j: (i, 0)),
                  pl.BlockSpec((tm, k2), lambda i, j: (i, 0)),
                  pl.BlockSpec((None, k1, tn), lambda i, j: (layer, 0, j)),
                  pl.BlockSpec((None, k2, tn), lambda i, j: (layer, 1, j)),
                  pl.BlockSpec((tm, tn), lambda i, j: (i, j))],
        out_specs=pl.BlockSpec((tm, tn), lambda i, j: (i, j)),
        out_shape=jax.ShapeDtypeStruct((m, n), F32),
        compiler_params=_cp("parallel", "parallel"),
        name="matmul_res",
    )(a1, a2, w, w, res)


def _mlp_kernel(x_ref, g_ref, w1_ref, w2_ref, o_ref, xn_ref, acc_ref):
    f = pl.program_id(1)

    @pl.when(f == 0)
    def _():
        xn_ref[...] = _rms(x_ref[...], g_ref[...]).astype(BF16)
        acc_ref[...] = jnp.zeros_like(acc_ref)

    hid = jnp.maximum(_dot(xn_ref[...], w1_ref[...]), 0.0)
    acc_ref[...] += _dot((hid * hid).astype(BF16), w2_ref[...])

    @pl.when(f == pl.num_programs(1) - 1)
    def _():
        o_ref[...] = x_ref[...] + acc_ref[...]


def mlp_block(x, g, w1, w2, layer):
    m, d = x.shape
    ff = w1.shape[2]
    tm = _tile(m, 512)
    tf = _tile(ff, 512)
    return pl.pallas_call(
        _mlp_kernel,
        grid=(m // tm, ff // tf),
        in_specs=[pl.BlockSpec((tm, d), lambda i, f: (i, 0)),
                  pl.BlockSpec((1, d), lambda i, f: (0, 0)),
                  pl.BlockSpec((None, d, tf), lambda i, f: (layer, 0, f)),
                  pl.BlockSpec((None, tf, d), lambda i, f: (layer, f, 0))],
        out_specs=pl.BlockSpec((tm, d), lambda i, f: (i, 0)),
        out_shape=jax.ShapeDtypeStruct((m, d), F32),
        scratch_shapes=[pltpu.VMEM((tm, d), BF16), pltpu.VMEM((tm, d), F32)],
        compiler_params=_cp("parallel", "arbitrary"),
        name="mlp_block",
    )(x, g.reshape(1, d), w1, w2)


def _mem_attn_kernel(x_ref, g_ref, wq_ref, qg_ref, k_ref, v_ref, wo_ref, o_ref, oh_ref, *, nb, seg, by_row):
    x = x_ref[...]
    q = _dot(_rms(x, g_ref[...]).astype(BF16), wq_ref[...])
    qg = qg_ref[...]
    n_mem = k_ref.shape[1] // H_MEM if by_row else k_ref.shape[1]
    qh = [(_rms(q[:, h * HEAD_DIM:(h + 1) * HEAD_DIM], qg) * SCALE).astype(BF16) for h in range(H_MEM)]
    pairs = [(b, h) for b in range(nb) for h in range(H_MEM)]

    def mem_head(ref, b, h):
        if by_row:
            return ref[b, pl.ds(h, n_mem, stride=H_MEM), :].astype(BF16)
        return ref[b, :, h * HEAD_DIM:(h + 1) * HEAD_DIM].astype(BF16)

    s = jnp.concatenate([_dot_nt(qh[h][b * seg:(b + 1) * seg], mem_head(k_ref, b, h)) for b, h in pairs], axis=0)
    p = jnp.exp(s - jnp.max(s, axis=-1, keepdims=True))
    pb = (p / jnp.sum(p, axis=-1, keepdims=True)).astype(BF16)
    for j, (b, h) in enumerate(pairs):
        o = _dot(pb[j * seg:(j + 1) * seg], mem_head(v_ref, b, h))
        oh_ref[b * seg:(b + 1) * seg, h * HEAD_DIM:(h + 1) * HEAD_DIM] = o.astype(BF16)
    o_ref[...] = x + _dot(oh_ref[...], wo_ref[...])


def mem_attn_block(x, g, wq, qg, k, v, wo, layer, rows_per_seq, seq_off=0):
    m, d = x.shape
    dm = H_MEM * HEAD_DIM
    by_row = k.shape[2] == HEAD_DIM
    kv_rows, kv_w = k.shape[1], k.shape[2]
    if rows_per_seq >= 128:
        tm = _tile(rows_per_seq, 512)
        nb, seg = 1, tm
        kv_map = lambda i: (seq_off + i // (rows_per_seq // tm), 0, 0)
    else:
        nb = _tile(m // rows_per_seq, 8)
        tm, seg = nb * rows_per_seq, rows_per_seq
        assert seq_off % nb == 0
        kv_map = lambda i: (seq_off // nb + i, 0, 0)
    return pl.pallas_call(
        functools.partial(_mem_attn_kernel, nb=nb, seg=seg, by_row=by_row),
        grid=(m // tm,),
        in_specs=[pl.BlockSpec((tm, d), lambda i: (i, 0)),
                  pl.BlockSpec((1, d), lambda i: (0, 0)),
                  pl.BlockSpec((None, d, dm), lambda i: (layer, 0, 0)),
                  pl.BlockSpec((1, HEAD_DIM), lambda i: (0, 0)),
                  pl.BlockSpec((nb, kv_rows, kv_w), kv_map),
                  pl.BlockSpec((nb, kv_rows, kv_w), kv_map),
                  pl.BlockSpec((None, dm, d), lambda i: (layer, 0, 0))],
        out_specs=pl.BlockSpec((tm, d), lambda i: (i, 0)),
        out_shape=jax.ShapeDtypeStruct((m, d), F32),
        scratch_shapes=[pltpu.VMEM((tm, dm), BF16)],
        compiler_params=_cp("parallel"),
        name="mem_attn_block",
    )(x, g.reshape(1, d), wq, qg.reshape(1, HEAD_DIM), k, v, wo)


def _kv_transform_kernel(*refs, modes, heads):
    ns = len(modes)
    x_refs = refs[:ns]
    g_ref, c_ref, sa_ref, sb_ref = refs[ns:ns + 4]
    o_refs = refs[ns + 4:]
    for s in range(ns):
        for h in range(heads):
            cols = slice(h * HEAD_DIM, (h + 1) * HEAD_DIM)
            x = x_refs[s][:, cols]
            if modes[s] != "copy":
                x = _rms(x, g_ref[s:s + 1, :])
            if modes[s] == "norm_rope":
                x = _rope(x, c_ref[...], sa_ref[...], sb_ref[...], ROPE_DIMS // 2)
            o_refs[s][:, cols] = x


def kv_transform(proj, sections, gains, tables, heads, pos_period):
    m = proj.shape[0]
    width = heads * HEAD_DIM
    tr = _tile(pos_period, 512)
    nper = pos_period // tr
    modes = tuple(mode for _, mode in sections)
    in_specs = [pl.BlockSpec((tr, width), functools.partial(lambda i, cb: (i, cb), cb=cb)) for cb, _ in sections]
    in_specs.append(pl.BlockSpec(gains.shape, lambda i: (0, 0)))
    in_specs += [pl.BlockSpec((tr, HEAD_DIM), lambda i: (i % nper, 0))] * 3
    outs = pl.pallas_call(
        functools.partial(_kv_transform_kernel, modes=modes, heads=heads),
        grid=(m // tr,),
        in_specs=in_specs,
        out_specs=[pl.BlockSpec((tr, width), lambda i: (i, 0))] * len(sections),
        out_shape=[jax.ShapeDtypeStruct((m, width), F32)] * len(sections),
        compiler_params=_cp("parallel"),
        name="kv_transform",
    )(*([proj] * len(sections)), gains, *tables)
    return outs


def _retention_kernel(q_ref, k_ref, v_ref, rg_ref, s0_ref, lg_ref, gn_ref, c_ref, sa_ref, sb_ref,
                      y_ref, s_ref, st_ref, *, rows, c_true, hps):
    ci = pl.program_id(2)
    cc = RET_CHUNK

    @pl.when(ci == 0)
    def _():
        st_ref[...] = s0_ref[0]

    c, sa, sb = c_ref[...], sa_ref[...], sb_ref[...]
    ii = _iota((cc, 1), 0).astype(F32)
    diff = ii - _iota((1, cc), 1).astype(F32)
    for h in range(hps):
        cols = slice(h * HEAD_DIM, (h + 1) * HEAD_DIM)
        lg = lg_ref[h]
        q = _pad_rows(_rope(q_ref[0, :, cols], c, sa, sb, HEAD_DIM // 2) * SCALE, cc)
        k = _pad_rows(_rope(k_ref[0, :, cols], c, sa, sb, HEAD_DIM // 2), cc)
        vb = _pad_rows(v_ref[0, :, cols], cc).astype(BF16)
        qb = q.astype(BF16)
        decay = jnp.where(diff >= 0, jnp.exp(jnp.maximum(diff, 0.0) * lg), 0.0)
        inner = _dot_nt(qb, k.astype(BF16)) * decay
        st = st_ref[h]
        o = _dot(inner.astype(BF16), vb) + _dot(qb, st.astype(BF16)) * jnp.exp((ii + 1.0) * lg)
        kd = jnp.where(ii < c_true, k * jnp.exp((c_true - 1.0 - ii) * lg), 0.0)
        st_ref[h] = jnp.exp(c_true * lg) * st + _dot(kd.T.astype(BF16), vb)
        o = o[:rows]
        mu = jnp.mean(o, axis=-1, keepdims=True)
        var = jnp.mean(jnp.square(o - mu), axis=-1, keepdims=True)
        y = (o - mu) * lax.rsqrt(var + NORM_EPS) * gn_ref[h]
        rg = rg_ref[0, :, cols]
        y_ref[0, :, cols] = (y * (rg * jax.nn.sigmoid(rg))).astype(BF16)

    @pl.when(ci == pl.num_programs(2) - 1)
    def _():
        s_ref[0] = st_ref[...]


def retention(proj3, s0, log_g, gn_g, tables, c_true, seq_off=0):
    n, t, _ = proj3.shape
    rows = min(t, RET_CHUNK)
    nc = t // rows
    h = H_RET
    hps = H_RET
    ng = h // hps
    wid = hps * HEAD_DIM
    blk = lambda sec: pl.BlockSpec((1, rows, wid), functools.partial(lambda b, hg, c, sec: (b, c, sec * ng + hg), sec=sec))
    tab = pl.BlockSpec((rows, HEAD_DIM), lambda b, hg, c: (c, 0))
    y, s_new = pl.pallas_call(
        functools.partial(_retention_kernel, rows=rows, c_true=float(c_true), hps=hps),
        grid=(n, ng, nc),
        in_specs=[blk(0), blk(1), blk(2), blk(3),
                  pl.BlockSpec((1, hps, HEAD_DIM, HEAD_DIM), lambda b, hg, c: (seq_off + b, hg, 0, 0)),
                  pl.BlockSpec((hps, 1, HEAD_DIM), lambda b, hg, c: (hg, 0, 0)),
                  pl.BlockSpec((hps, 1, HEAD_DIM), lambda b, hg, c: (hg, 0, 0)),
                  tab, tab, tab],
        out_specs=[pl.BlockSpec((1, rows, wid), lambda b, hg, c: (b, c, hg)),
                   pl.BlockSpec((1, hps, HEAD_DIM, HEAD_DIM), lambda b, hg, c: (b, hg, 0, 0))],
        out_shape=[jax.ShapeDtypeStruct((n, t, h * HEAD_DIM), BF16),
                   jax.ShapeDtypeStruct((n, h, HEAD_DIM, HEAD_DIM), F32)],
        scratch_shapes=[pltpu.VMEM((hps, HEAD_DIM, HEAD_DIM), F32)],
        compiler_params=_cp("parallel", "parallel", "arbitrary"),
        name="retention",
    )(proj3, proj3, proj3, proj3, s0,
      jnp.broadcast_to(log_g[:, None, None], (h, 1, HEAD_DIM)), gn_g.reshape(h, 1, HEAD_DIM), *tables)
    return y, s_new


def _rows(start, size, stride):
    return pl.ds(start, size) if stride == 1 else pl.ds(start, size, stride=stride)


def _dil_prompt_kernel(q_ref, k_ref, v_ref, qg_ref, c_ref, sa_ref, sb_ref, o_ref,
                       qs_ref, m_ref, l_ref, acc_ref, pm_ref, pls_ref, pacc_ref, qc_ref, kc_ref, vc_ref, *, t):
    chunk = _tile(t, 512)
    for c in range(t // chunk):
        r = slice(c * chunk, (c + 1) * chunk)
        q = _rope(_rms(q_ref[0, r, :], qg_ref[...]), c_ref[r, :], sa_ref[r, :], sb_ref[r, :], ROPE_DIMS // 2)
        qs_ref[r, :] = q * SCALE

    def bias(bq, nk, first_key_dist):
        rel = first_key_dist + _iota((bq, nk), 0) - _iota((bq, nk), 1)
        return jnp.where(rel >= 0, jnp.where(rel <= KEY_BLOCK, 0.0, NEG), NEG)

    for pi, (window, dil) in enumerate(DIL_PATTERNS):
        length = t // dil
        bq = min(length, KEY_BLOCK)
        nk = bq if length == bq else 2 * bq
        for r in range(dil):
            rows, crow = _rows(r, length, dil), slice(r * length, (r + 1) * length)
            qc_ref[crow, :] = qs_ref[rows, :].astype(BF16)
            kc_ref[crow, :] = k_ref[0, rows, :].astype(BF16)
            vc_ref[crow, :] = v_ref[0, rows, :].astype(BF16)
        blocks = [(r * length + i * bq, r * length + max(i - 1, 0) * bq, i == 0 and nk > bq)
                  for r in range(dil) for i in range(length // bq)]
        band, lead = bias(bq, nk, nk - bq), bias(bq, nk, 0)
        s = jnp.concatenate([_dot_nt(qc_ref[q0:q0 + bq, :], kc_ref[k0:k0 + nk, :]) + (lead if first else band)
                             for q0, k0, first in blocks], axis=0)
        m = jnp.max(s, axis=-1, keepdims=True)
        pb = jnp.exp(s - m).astype(BF16)
        m = jnp.broadcast_to(m, (t, HEAD_DIM))
        ones = jnp.ones((nk, HEAD_DIM), BF16)
        l = jnp.concatenate([_dot(pb[j * bq:(j + 1) * bq], ones) for j in range(len(blocks))], axis=0)
        acc = jnp.concatenate([_dot(pb[j * bq:(j + 1) * bq], vc_ref[k0:k0 + nk, :])
                               for j, (q0, k0, first) in enumerate(blocks)], axis=0)
        if pi == 0:
            assert dil == 1
            m_ref[...], l_ref[...], acc_ref[...] = m, l, acc
        else:
            pm_ref[...], pls_ref[...], pacc_ref[...] = m, l, acc
            for r in range(dil):
                rows, crow = _rows(r, length, dil), slice(r * length, (r + 1) * length)
                m0, m1 = m_ref[rows, :], pm_ref[crow, :]
                mx = jnp.maximum(m0, m1)
                e0, e1 = jnp.exp(m0 - mx), jnp.exp(m1 - mx)
                m_ref[rows, :] = mx
                l_ref[rows, :] = e0 * l_ref[rows, :] + e1 * pls_ref[crow, :]
                acc_ref[rows, :] = e0 * acc_ref[rows, :] + e1 * pacc_ref[crow, :]

    for c in range(t // chunk):
        r = slice(c * chunk, (c + 1) * chunk)
        o_ref[0, r, :] = (acc_ref[r, :] / l_ref[r, :]).astype(BF16)


def dilated_prompt(proj3, q_cb, k3, v3, qg, tables):
    n, t, _ = proj3.shape
    for window, dil in DIL_PATTERNS:
        length = t // dil
        assert window // dil == KEY_BLOCK and t % dil == 0
        assert length % KEY_BLOCK == 0 or (length < KEY_BLOCK and length % SUBLANES == 0)
    tab = pl.BlockSpec((t, HEAD_DIM), lambda b, h: (0, 0))
    seq = lambda cb: pl.BlockSpec((1, t, HEAD_DIM), functools.partial(lambda b, h, cb: (b, 0, cb + h), cb=cb))
    return pl.pallas_call(
        functools.partial(_dil_prompt_kernel, t=t),
        grid=(n, H_DIL),
        in_specs=[seq(q_cb), seq(0), seq(0), pl.BlockSpec((1, HEAD_DIM), lambda b, h: (0, 0)), tab, tab, tab],
        out_specs=seq(0),
        out_shape=jax.ShapeDtypeStruct((n, t, H_DIL * HEAD_DIM), BF16),
        scratch_shapes=[pltpu.VMEM((t, HEAD_DIM), F32)] * 7 + [pltpu.VMEM((t, HEAD_DIM), BF16)] * 3,
        compiler_params=_cp("parallel", "parallel"),
        name="dilated_prompt",
    )(proj3, k3, v3, qg.reshape(1, HEAD_DIM), *tables)


def _dil_sample_kernel(q_ref, kc_ref, vc_ref, kn_ref, vn_ref, qg_ref, c_ref, sa_ref, sb_ref, o_ref,
                       qb_ref, m_ref, l_ref, acc_ref, *, lc, ck):
    ci = pl.program_id(1)
    rows = q_ref.shape[1]
    head = lambda a, h: a[h * rows:(h + 1) * rows]

    @pl.when(ci == 0)
    def _():
        qs = []
        for h in range(H_DIL):
            cols = slice(h * HEAD_DIM, (h + 1) * HEAD_DIM)
            q = _rope(_rms(q_ref[0, :, cols], qg_ref[...]), c_ref[...], sa_ref[...], sb_ref[...], ROPE_DIMS // 2)
            qs.append(q * SCALE)
        qb_ref[...] = jnp.concatenate(qs, axis=0)
        m_ref[...], l_ref[...], acc_ref[...] = _softmax_init(H_DIL * rows)

    def log_weight(d):
        w = _dil_weight(d)
        return jnp.where(w > 0, jnp.log(jnp.maximum(w, 1.0)), NEG)

    def step(scores, values):
        s = jnp.concatenate(scores, axis=0)
        m_new = jnp.maximum(m_ref[...], jnp.max(s, axis=-1, keepdims=True))
        alpha = jnp.exp(m_ref[...] - m_new)
        p = jnp.exp(s - m_new)
        pb = p.astype(BF16)
        l = alpha * l_ref[...] + jnp.sum(p, axis=-1, keepdims=True)
        acc = alpha * acc_ref[...] + jnp.concatenate([_dot(head(pb, h), values(h)) for h in range(H_DIL)], axis=0)
        m_ref[...], l_ref[...], acc_ref[...] = m_new, l, acc
        return l, acc

    qb = qb_ref[...].astype(BF16)
    bias_c = log_weight(lc + _iota((rows, ck), 0) - (ci * ck + _iota((rows, ck), 1)))
    keys = lambda h: pl.ds(h, ck, stride=H_DIL)
    step([_dot_nt(head(qb, h), kc_ref[0, keys(h), :].astype(BF16)) + bias_c for h in range(H_DIL)],
         lambda h: vc_ref[0, keys(h), :].astype(BF16))

    @pl.when(ci == pl.num_programs(1) - 1)
    def _():
        bias_n = log_weight(_iota((rows, KEY_BLOCK), 0) - _iota((rows, KEY_BLOCK), 1))
        new = lambda ref, h: _pad_rows(ref[0, :, h * HEAD_DIM:(h + 1) * HEAD_DIM], KEY_BLOCK).astype(BF16)
        l, acc = step([_dot_nt(head(qb, h), new(kn_ref, h)) + bias_n for h in range(H_DIL)], lambda h: new(vn_ref, h))
        o = _softmax_done(l, acc)
        for h in range(H_DIL):
            o_ref[0, :, h * HEAD_DIM:(h + 1) * HEAD_DIM] = head(o, h).astype(BF16)


def dilated_sample(proj3, q_cb, kc3, vc3, seq_off, kn3, vn3, qg, tables):
    n, rows, _ = proj3.shape
    lc = kc3.shape[1] // H_DIL
    wid = H_DIL * HEAD_DIM
    ck = lc // 2 if lc % (2 * KEY_BLOCK) == 0 else lc
    assert (q_cb * HEAD_DIM) % wid == 0
    tab = pl.BlockSpec((rows, HEAD_DIM), lambda b, c: (0, 0))
    cache = pl.BlockSpec((1, ck * H_DIL, HEAD_DIM), lambda b, c: (seq_off + b, c, 0))
    new = pl.BlockSpec((1, rows, wid), lambda b, c: (b, 0, 0))
    return pl.pallas_call(
        functools.partial(_dil_sample_kernel, lc=lc, ck=ck),
        grid=(n, lc // ck),
        in_specs=[pl.BlockSpec((1, rows, wid), lambda b, c: (b, 0, q_cb * HEAD_DIM // wid)), cache, cache, new, new,
                  pl.BlockSpec((1, HEAD_DIM), lambda b, c: (0, 0)), tab, tab, tab],
        out_specs=new,
        out_shape=jax.ShapeDtypeStruct((n, rows, wid), BF16),
        scratch_shapes=[pltpu.VMEM((H_DIL * rows, HEAD_DIM), F32), pltpu.VMEM((H_DIL * rows, 1), F32),
                        pltpu.VMEM((H_DIL * rows, 1), F32), pltpu.VMEM((H_DIL * rows, HEAD_DIM), F32)],
        compiler_params=_cp("parallel", "arbitrary"),
        name="dilated_sample",
    )(proj3, kc3, vc3, kn3, vn3, qg.reshape(1, HEAD_DIM), *tables)


def _nsa_q(q_ref, g, qg_ref, c_ref, sa_ref, sb_ref):
    qs = []
    for hl in range(HPG):
        col = (g * HPG + hl) * HEAD_DIM
        x = _rms(q_ref[0][:, col:col + HEAD_DIM], qg_ref[...])
        qs.append(_rope(x, c_ref[...], sa_ref[...], sb_ref[...], ROPE_DIMS // 2) * SCALE)
    return jnp.concatenate(qs, axis=0).astype(BF16)


def _nsa_compressed(q4, kc, vc, qpos, nb):
    rows = qpos.shape[0]
    nbp = kc.shape[0]
    blk = _iota((rows, nbp), 1)
    cmask = jnp.where((blk + 1) * NSA_BLOCK - 1 <= qpos, 1.0, 0.0)
    cmask4 = jnp.concatenate([cmask] * HPG, axis=0)
    s = jnp.where(cmask4 > 0, _dot_nt(q4, kc.astype(BF16)), NEG)
    p = jnp.exp(s - jnp.max(s, axis=-1, keepdims=True)) * cmask4
    p = p / jnp.maximum(jnp.sum(p, axis=-1, keepdims=True), 1e-30)
    o_c = _dot(p.astype(BF16), vc.astype(BF16))
    imp = p[0:rows]
    for hl in range(1, HPG):
        imp = imp + p[hl * rows:(hl + 1) * rows]
    shift = NSA_BLOCK.bit_length() - 1
    if rows == nbp:
        nbr = -(-nb // SUBLANES) * SUBLANES
        blk_t = _iota((nbr, rows), 0)
        cur_t = jnp.right_shift(qpos[0:1, :] + _iota((1, rows), 1), shift)
        cand_t = blk_t < cur_t
        imp_t = jnp.where(cand_t, imp.T[:nbr], -jnp.inf)
        rank = jnp.zeros((nbr, rows), F32)
        for b in range(nb):
            row = imp_t[b:b + 1, :]
            tie = jnp.where(row == imp_t, jnp.where(blk_t > b, 1.0, 0.0), 0.0)
            rank = rank + jnp.where(row > imp_t, 1.0, tie)
        sel_t = jnp.where(cand_t, jnp.where(rank < min(NSA_TOPK, nb), 1.0, 0.0), 0.0)
        sel = _pad_rows(sel_t, nbp).T
    else:
        cand = blk < jnp.right_shift(qpos, shift)
        imp = jnp.where(cand, imp, -jnp.inf)
        rank = jnp.zeros((rows, nbp), F32)
        for b in range(nb):
            col = imp[:, b:b + 1]
            tie = jnp.where(col == imp, jnp.where(blk > b, 1.0, 0.0), 0.0)
            rank = rank + jnp.where(col > imp, 1.0, tie)
        sel = jnp.where(cand, jnp.where(rank < min(NSA_TOPK, nb), 1.0, 0.0), 0.0)
    sel = jnp.where(blk == jnp.right_shift(qpos, shift), 1.0, sel)
    return o_c, sel


def _nsa_combine(gates, g, o_c, o_s, o_w, rows):
    outs = []
    for hl in range(HPG):
        r = slice(hl * rows, (hl + 1) * rows)
        col = 3 * hl
        outs.append(gates[:, col:col + 1] * o_c[r] + gates[:, col + 1:col + 2] * o_s[r]
                    + gates[:, col + 2:col + 3] * o_w[r])
    return outs


def _nsa_prompt_kernel(q_ref, gt_ref, kc_ref, vc_ref, ks_ref, vs_ref, kw_ref, vw_ref, e_ref,
                       qg_ref, c_ref, sa_ref, sb_ref, o_ref, selx_ref, s_ref, mx_ref, ls_ref, acc_ref, *, tq, nb, t):
    qi = pl.program_id(2)
    q4 = _nsa_q(q_ref, 0, qg_ref, c_ref, sa_ref, sb_ref)
    qpos = qi * tq + _iota((tq, 1), 0)
    o_c, sel = _nsa_compressed(q4, kc_ref[0], vc_ref[0], qpos, nb)
    selx_ref[...] = _dot(sel.astype(BF16), e_ref[...])

    ck = _tile(t, 4 * KEY_BLOCK)
    nchunks = lax.div(qi * tq + (tq + ck - 1), ck)
    lane = _iota((tq, ck), 1)
    mx_ref[...] = jnp.full(mx_ref.shape, NEG, F32)

    def scores(c, _):
        off = pl.multiple_of(c * ck, ck)
        bias = jnp.where(off + lane <= qpos, (selx_ref[:, pl.ds(off, ck)] - 1.0) * -NEG, NEG)
        s = _dot_nt(q4, ks_ref[0, pl.ds(off, ck), :].astype(BF16))
        s = (s.reshape(HPG, tq, ck) + bias[None]).reshape(HPG * tq, ck)
        s_ref[:, pl.ds(off, ck)] = s
        part = s[:, :KEY_BLOCK]
        for j in range(1, ck // KEY_BLOCK):
            part = jnp.maximum(part, s[:, j * KEY_BLOCK:(j + 1) * KEY_BLOCK])
        mx_ref[...] = jnp.maximum(mx_ref[...], part)
        return 0

    lax.fori_loop(0, nchunks, scores, 0)
    m = jnp.max(mx_ref[...], axis=-1, keepdims=True)
    ls_ref[...] = jnp.zeros(ls_ref.shape, F32)
    acc_ref[...] = jnp.zeros(acc_ref.shape, F32)

    def values(c, _):
        off = pl.multiple_of(c * ck, ck)
        s = s_ref[:, pl.ds(off, ck)]
        p = jnp.exp(s - m)
        part = p[:, :KEY_BLOCK]
        for j in range(1, ck // KEY_BLOCK):
            part = part + p[:, j * KEY_BLOCK:(j + 1) * KEY_BLOCK]
        ls_ref[...] += part
        acc_ref[...] += _dot(p.astype(BF16), vs_ref[0, pl.ds(off, ck), :].astype(BF16))
        return 0

    lax.fori_loop(0, nchunks, values, 0)
    o_s = acc_ref[...] / jnp.maximum(jnp.sum(ls_ref[...], axis=-1, keepdims=True), 1e-30)

    span = min(NSA_WINDOW + KEY_BLOCK, t)
    first = jnp.clip(qi - NSA_WINDOW // KEY_BLOCK, 0, (t - span) // KEY_BLOCK)
    off = pl.multiple_of(first * KEY_BLOCK, KEY_BLOCK)
    d = qpos - (off + _iota((tq, span), 1))
    bias = jnp.where(d >= 0, jnp.where(d < NSA_WINDOW, 0.0, NEG), NEG)
    s = _dot_nt(q4, kw_ref[0, pl.ds(off, span), :].astype(BF16))
    s = (s.reshape(HPG, tq, span) + bias[None]).reshape(HPG * tq, span)
    p = jnp.exp(s - jnp.max(s, axis=-1, keepdims=True))
    o_w = _dot(p.astype(BF16), vw_ref[0, pl.ds(off, span), :].astype(BF16)) / jnp.sum(p, axis=-1, keepdims=True)
    gates = jax.nn.sigmoid(gt_ref[0])
    for hl, o in enumerate(_nsa_combine(gates, 0, o_c, o_s, o_w, tq)):
        o_ref[0, :, hl * HEAD_DIM:(hl + 1) * HEAD_DIM] = o.astype(BF16)


def nsa_prompt(proj3, gate_cb, kc3, vc3, ks3, vs3, kw3, vw3, expand, qg, tables, nb):
    n, t, _ = proj3.shape
    tq = KEY_BLOCK
    nbp = kc3.shape[1]
    gw = HPG * HEAD_DIM
    tab = pl.BlockSpec((tq, HEAD_DIM), lambda b, g, i: (i, 0))
    cmp_spec = pl.BlockSpec((1, nbp, HEAD_DIM), lambda b, g, i: (b, 0, g))
    kv = pl.BlockSpec((1, t, HEAD_DIM), lambda b, g, i: (b, 0, g))
    return pl.pallas_call(
        functools.partial(_nsa_prompt_kernel, tq=tq, nb=nb, t=t),
        grid=(n, G_NSA, t // tq),
        in_specs=[pl.BlockSpec((1, tq, gw), lambda b, g, i: (b, i, g)),
                  pl.BlockSpec((1, tq, HEAD_DIM), lambda b, g, i: (b, i, gate_cb + g)),
                  cmp_spec, cmp_spec, kv, kv, kv, kv,
                  pl.BlockSpec((nbp, t), lambda b, g, i: (0, 0)),
                  pl.BlockSpec((1, HEAD_DIM), lambda b, g, i: (0, 0)), tab, tab, tab],
        out_specs=pl.BlockSpec((1, tq, gw), lambda b, g, i: (b, i, g)),
        out_shape=jax.ShapeDtypeStruct((n, t, H_NSA * HEAD_DIM), BF16),
        scratch_shapes=[pltpu.VMEM((tq, t), F32), pltpu.VMEM((HPG * tq, t), F32)]
        + [pltpu.VMEM((HPG * tq, HEAD_DIM), F32)] * 3,
        compiler_params=_cp("parallel", "parallel", "arbitrary"),
        name="nsa_prompt",
    )(proj3, proj3, kc3, vc3, ks3, vs3, kw3, vw3, expand, qg.reshape(1, HEAD_DIM), *tables)


def _compress_rows_kernel(k_ref, v_ref, pk_ref, pv_ref, ko_ref, vo_ref):
    for x_ref, p_ref, o_ref in ((k_ref, pk_ref, ko_ref), (v_ref, pv_ref, vo_ref)):
        x = x_ref[0]
        nblk = x.shape[0] // NSA_BLOCK
        pw = jnp.concatenate([p_ref[...]] * nblk, axis=0)
        o_ref[0] = jnp.sum((x * pw).reshape(nblk, NSA_BLOCK, x.shape[1]), axis=1)


def compress_rows(k3, v3, pwk2, pwv2):
    n, t, w = k3.shape
    tr = _tile(t, 512)
    row = pl.BlockSpec((1, tr, w), lambda b, i: (b, i, 0))
    pw = pl.BlockSpec((NSA_BLOCK, w), lambda b, i: (0, 0))
    out = pl.BlockSpec((1, tr // NSA_BLOCK, w), lambda b, i: (b, i, 0))
    return pl.pallas_call(
        _compress_rows_kernel,
        grid=(n, t // tr),
        in_specs=[row, row, pw, pw],
        out_specs=[out, out],
        out_shape=[jax.ShapeDtypeStruct((n, t // NSA_BLOCK, w), F32)] * 2,
        compiler_params=_cp("parallel", "parallel"),
        name="compress_rows",
    )(k3, v3, pwk2, pwv2)


def _compress_pages_kernel(pt_ref, *refs, pps):
    k_refs, v_refs = refs[:pps], refs[pps:2 * pps]
    pk_ref, pv_ref, ko_ref, vo_ref = refs[2 * pps:]
    bpp = PAGE_SIZE // NSA_BLOCK
    prows = PAGE_SIZE * G_NSA
    for x_refs, p_ref, o_ref in ((k_refs, pk_ref, ko_ref), (v_refs, pv_ref, vo_ref)):
        pw = p_ref[...]
        outs = []
        for x_ref in x_refs:
            z = jnp.sum((x_ref[0] * pw).reshape(bpp, prows // bpp // SUBLANES, SUBLANES, HEAD_DIM), axis=1)
            per_g = [sum(z[:, s] for s in range(g, SUBLANES, G_NSA)) for g in range(G_NSA)]
            outs.append(jnp.concatenate(per_g, axis=1))
        o_ref[0] = jnp.concatenate(outs, axis=0)


def compress_pages(pool_k, pool_v, page_off, page_table, pwk, pwv, pps):
    n, n_pages = page_table.shape
    w = G_NSA * HEAD_DIM
    prows = PAGE_SIZE * G_NSA
    bpp = PAGE_SIZE // NSA_BLOCK
    assert SUBLANES % G_NSA == 0
    page_w = lambda p: jnp.repeat(jnp.tile(p, (bpp, 1)), G_NSA, axis=0)
    pwk2, pwv2 = page_w(pwk), page_w(pwv)
    page = lambda i: pl.BlockSpec((1, prows, HEAD_DIM),
                                  functools.partial(lambda b, c, pt, i: (page_off + pt[b, c * pps + i], 0, 0), i=i))
    pw = pl.BlockSpec((prows, HEAD_DIM), lambda b, c, pt: (0, 0))
    out = pl.BlockSpec((1, pps * bpp, w), lambda b, c, pt: (b, c, 0))
    return pl.pallas_call(
        functools.partial(_compress_pages_kernel, pps=pps),
        grid_spec=pltpu.PrefetchScalarGridSpec(
            num_scalar_prefetch=1, grid=(n, n_pages // pps),
            in_specs=[page(i) for i in range(pps)] * 2 + [pw, pw],
            out_specs=[out, out]),
        out_shape=[jax.ShapeDtypeStruct((n, n_pages * bpp, w), F32)] * 2,
        compiler_params=_cp("parallel", "parallel"),
        name="compress_pages",
    )(page_table, *([pool_k] * pps), *([pool_v] * pps), pwk2, pwv2)


def _nsa_sample_kernel(pt_ref, *refs, pps, past, rows, nb, lw):
    (q_ref, gt_ref, kc_ref, vc_ref), refs = refs[:4], refs[4:]
    kp_refs, vp_refs, refs = refs[:pps], refs[pps:2 * pps], refs[2 * pps:]
    (kn_ref, vn_ref, kwc_ref, vwc_ref, kwn_ref, vwn_ref, e_ref, qg_ref, c_ref, sa_ref, sb_ref,
     o_ref, q4_ref, sel_ref, oc_ref, m_ref, l_ref, acc_ref) = refs
    pc = pl.program_id(1)
    r4 = HPG * rows
    qpos = past + _iota((rows, 1), 0)
    lane = _iota((rows, KEY_BLOCK), 1)

    @pl.when(pc == 0)
    def _():
        for g in range(G_NSA):
            cols = slice(g * HEAD_DIM, (g + 1) * HEAD_DIM)
            q4 = _nsa_q(q_ref, g, qg_ref, c_ref, sa_ref, sb_ref)
            o_c, sel = _nsa_compressed(q4, kc_ref[0][:, cols], vc_ref[0][:, cols], qpos, nb)
            q4_ref[g] = q4
            sel_ref[g] = sel
            oc_ref[g] = o_c
            m0, l0, a0 = _softmax_init(r4)
            m_ref[g], l_ref[g], acc_ref[g] = m0, l0, a0

    nkeys = pps * PAGE_SIZE
    kpos = pc * nkeys + _iota((rows, nkeys), 1)
    for g in range(G_NSA):
        cols = slice(g * HEAD_DIM, (g + 1) * HEAD_DIM)
        grp = pl.ds(g, PAGE_SIZE, stride=G_NSA)
        kb = jnp.concatenate([r[0, grp, :].astype(BF16) for r in kp_refs], axis=0)
        vb = jnp.concatenate([r[0, grp, :].astype(BF16) for r in vp_refs], axis=0)
        selx = _dot(sel_ref[g].astype(BF16), e_ref[...])
        w = jnp.where(kpos <= qpos, selx, 0.0)
        m_ref[g], l_ref[g], acc_ref[g] = _softmax_step(
            _dot_nt(q4_ref[g], kb), jnp.concatenate([w] * HPG, axis=0), vb, m_ref[g], l_ref[g], acc_ref[g])

    @pl.when(pc == pl.num_programs(1) - 1)
    def _():
        gates_all = jax.nn.sigmoid(gt_ref[0])
        tq = _iota((rows, KEY_BLOCK), 0)
        for g in range(G_NSA):
            cols = slice(g * HEAD_DIM, (g + 1) * HEAD_DIM)
            q4 = q4_ref[g]
            kb = _pad_rows(kn_ref[0][:, cols], KEY_BLOCK).astype(BF16)
            vb = _pad_rows(vn_ref[0][:, cols], KEY_BLOCK).astype(BF16)
            w = jnp.where(lane <= tq, 1.0, 0.0)
            _, l, acc = _softmax_step(_dot_nt(q4, kb), jnp.concatenate([w] * HPG, axis=0), vb,
                                      m_ref[g], l_ref[g], acc_ref[g])
            o_s = _softmax_done(l, acc)
            tqw = _iota((rows, lw), 0)
            d = tqw + lw - _iota((rows, lw), 1)
            w = jnp.where(d < NSA_WINDOW, jnp.where(past - d + tqw >= 0, 1.0, 0.0), 0.0)
            grp = pl.ds(g, lw, stride=G_NSA)
            carry = _softmax_step(_dot_nt(q4, kwc_ref[0, grp, :].astype(BF16)), jnp.concatenate([w] * HPG, axis=0),
                                  vwc_ref[0, grp, :].astype(BF16), *_softmax_init(r4))
            kb = _pad_rows(kwn_ref[0][:, cols], KEY_BLOCK).astype(BF16)
            vb = _pad_rows(vwn_ref[0][:, cols], KEY_BLOCK).astype(BF16)
            w = jnp.where(lane <= tq, 1.0, 0.0)
            _, l, acc = _softmax_step(_dot_nt(q4, kb), jnp.concatenate([w] * HPG, axis=0), vb, *carry)
            o_w = _softmax_done(l, acc)
            gates = gates_all[:, g * HEAD_DIM:(g + 1) * HEAD_DIM]
            for hl, o in enumerate(_nsa_combine(gates, g, oc_ref[g], o_s, o_w, rows)):
                col = (g * HPG + hl) * HEAD_DIM
                o_ref[0, :, col:col + HEAD_DIM] = o.astype(BF16)


def nsa_sample(proj3, gate_cb, kc3, vc3, pool_k, pool_v, page_off, page_table, kn3, vn3, kwc3, vwc3, seq_off,
               kwn3, vwn3, expand, qg, tables, pps):
    n, rows, _ = proj3.shape
    n_pages = page_table.shape[1]
    past = n_pages * PAGE_SIZE
    nbp = kc3.shape[1]
    w = G_NSA * HEAD_DIM
    lw = kwc3.shape[1] // G_NSA
    qw = H_NSA * HEAD_DIM
    r4 = HPG * rows
    fixed = lambda shape: pl.BlockSpec(shape, lambda b, c, pt: (0,) * len(shape))
    per_seq = lambda r, width, cb=0, off=0: pl.BlockSpec(
        (1, r, width), functools.partial(lambda b, c, pt, cb, off: (off + b, 0, cb), cb=cb, off=off))
    page = lambda i: pl.BlockSpec((1, PAGE_SIZE * G_NSA, HEAD_DIM),
                                  functools.partial(lambda b, c, pt, i: (page_off + pt[b, c * pps + i], 0, 0), i=i))
    win = per_seq(lw * G_NSA, HEAD_DIM, 0, seq_off)
    in_specs = ([per_seq(rows, qw), per_seq(rows, w, gate_cb // G_NSA), per_seq(nbp, w), per_seq(nbp, w)]
                + [page(i) for i in range(pps)] * 2
                + [per_seq(rows, w), per_seq(rows, w), win, win, per_seq(rows, w), per_seq(rows, w),
                   pl.BlockSpec((nbp, pps * PAGE_SIZE), lambda b, c, pt: (0, c)),
                   fixed((1, HEAD_DIM)), fixed((rows, HEAD_DIM)), fixed((rows, HEAD_DIM)), fixed((rows, HEAD_DIM))])
    return pl.pallas_call(
        functools.partial(_nsa_sample_kernel, pps=pps, past=past, rows=rows, nb=past // NSA_BLOCK, lw=lw),
        grid_spec=pltpu.PrefetchScalarGridSpec(
            num_scalar_prefetch=1, grid=(n, n_pages // pps),
            in_specs=in_specs,
            out_specs=pl.BlockSpec((1, rows, qw), lambda b, c, pt: (b, 0, 0)),
            scratch_shapes=[pltpu.VMEM((G_NSA, r4, HEAD_DIM), BF16), pltpu.VMEM((G_NSA, rows, nbp), F32),
                            pltpu.VMEM((G_NSA, r4, HEAD_DIM), F32), pltpu.VMEM((G_NSA, r4, 1), F32),
                            pltpu.VMEM((G_NSA, r4, 1), F32), pltpu.VMEM((G_NSA, r4, HEAD_DIM), F32)]),
        out_shape=jax.ShapeDtypeStruct((n, rows, qw), BF16),
        compiler_params=_cp("parallel", "arbitrary"),
        name="nsa_sample",
    )(page_table, proj3, proj3, kc3, vc3, *([pool_k] * pps), *([pool_v] * pps), kn3, vn3, kwc3, vwc3, kwn3, vwn3,
      expand, qg.reshape(1, HEAD_DIM), *tables)


def _lru_kernel(x_ref, y_ref, hist_ref, h0_ref, cw_ref, cb_ref, wa_ref, ba_ref, wi_ref, bi_ref, sp_ref,
                o_ref, tail_ref, a_ref, b_ref, *, cps):
    t = x_ref.shape[1]
    bw = D_RNN // RNN_BLOCKS
    row8 = _iota((SUBLANES, bw), 0)
    for k in range(cps):
        cols = slice(k * bw, (k + 1) * bw)
        x = x_ref[0, :, cols]
        hist = hist_ref[0, :, cols]
        xc = cb_ref[:, cols] + x * cw_ref[CONV_W - 1:CONV_W, cols]
        for s in range(1, CONV_W):
            sh = pltpu.roll(x, s, 0)
            head = jnp.where(row8 >= s, sh[:SUBLANES], pltpu.roll(hist, (SUBLANES - (CONV_W - 1) + s) % SUBLANES, 0))
            sh = head if t == SUBLANES else jnp.concatenate([head, sh[SUBLANES:]], axis=0)
            xc = xc + sh * cw_ref[CONV_W - 1 - s:CONV_W - s, cols]
        xb = xc.astype(BF16)
        r = jax.nn.sigmoid(_dot(xb, wa_ref[k]) + ba_ref[:, cols])
        gi = jax.nn.sigmoid(_dot(xb, wi_ref[k]) + bi_ref[:, cols])
        log_a = -LRU_C * r * sp_ref[:, cols]
        th = jnp.tanh(log_a)
        a_ref[:, cols] = jnp.exp(log_a)
        b_ref[:, cols] = jnp.sqrt(-2.0 * th / (1.0 - th)) * (gi * xc)

    wid = cps * bw
    rowsw = _iota((SUBLANES, wid), 0)

    def body(gidx, hc):
        off = pl.multiple_of(gidx * SUBLANES, SUBLANES)
        a = a_ref[pl.ds(off, SUBLANES), :]
        b = b_ref[pl.ds(off, SUBLANES), :]
        for s in (1, 2, 4):
            ok = rowsw >= s
            b = jnp.where(ok, a * pltpu.roll(b, s, 0) + b, b)
            a = jnp.where(ok, a * pltpu.roll(a, s, 0), a)
        hrows = a * hc + b
        a_ref[pl.ds(off, SUBLANES), :] = hrows
        return jnp.broadcast_to(hrows[SUBLANES - 1:SUBLANES, :], (SUBLANES, wid))

    lax.fori_loop(0, t // SUBLANES, body, jnp.broadcast_to(h0_ref[0], (SUBLANES, wid)))
    tail_ref[0] = a_ref[t - SUBLANES:, :]
    chunk = _tile(t, 512)
    for c in range(t // chunk):
        rws = slice(c * chunk, (c + 1) * chunk)
        o_ref[0, rws, :] = (jax.nn.gelu(y_ref[0, rws, :], approximate=True) * a_ref[rws, :]).astype(BF16)


def rglru(proj3, x_cb, y_cb, hist8, h0, conv_w, conv_b, w_a, b_a, w_i, b_i, softplus_neg_lam):
    n, t, _ = proj3.shape
    bw = D_RNN // RNN_BLOCKS
    cps = 4 if t > SUBLANES else RNN_BLOCKS
    wid = cps * bw
    assert (x_cb * HEAD_DIM) % wid == 0 and (y_cb * HEAD_DIM) % wid == 0 and bw == HEAD_DIM
    vec = lambda a: a.reshape(1, D_RNN)
    vspec = pl.BlockSpec((1, wid), lambda b, k: (0, k))
    mspec = pl.BlockSpec((cps, bw, bw), lambda b, k: (k, 0, 0))
    return pl.pallas_call(
        functools.partial(_lru_kernel, cps=cps),
        grid=(n, RNN_BLOCKS // cps),
        in_specs=[pl.BlockSpec((1, t, wid), lambda b, k: (b, 0, x_cb * HEAD_DIM // wid + k)),
                  pl.BlockSpec((1, t, wid), lambda b, k: (b, 0, y_cb * HEAD_DIM // wid + k)),
                  pl.BlockSpec((1, SUBLANES, wid), lambda b, k: (b, 0, k)),
                  pl.BlockSpec((1, 1, wid), lambda b, k: (b, 0, k)),
                  pl.BlockSpec((CONV_W, wid), lambda b, k: (0, k)),
                  vspec, mspec, vspec, mspec, vspec, vspec],
        out_specs=[pl.BlockSpec((1, t, wid), lambda b, k: (b, 0, k)),
                   pl.BlockSpec((1, SUBLANES, wid), lambda b, k: (b, 0, k))],
        out_shape=[jax.ShapeDtypeStruct((n, t, D_RNN), BF16), jax.ShapeDtypeStruct((n, SUBLANES, D_RNN), F32)],
        scratch_shapes=[pltpu.VMEM((t, wid), F32), pltpu.VMEM((t, wid), F32)],
        compiler_params=_cp("parallel", "parallel"),
        name="rglru",
    )(proj3, proj3, hist8, h0.reshape(n, 1, D_RNN), conv_w, vec(conv_b),
      w_a.astype(BF16), vec(b_a), w_i.astype(BF16), vec(b_i), vec(softplus_neg_lam))


def _rope_tables(pos, n_rot, theta):
    half = n_rot // 2
    inv = 1.0 / (theta ** (jnp.arange(half, dtype=F32) * (2.0 / n_rot)))
    ang = pos.astype(F32)[:, None] * inv[None, :]
    cos, sin = jnp.cos(ang), jnp.sin(ang)
    p = pos.shape[0]
    rest = HEAD_DIM - n_rot
    zh = jnp.zeros((p, half), F32)
    c = jnp.concatenate([cos, cos, jnp.ones((p, rest), F32)], axis=1)
    sa = jnp.concatenate([-sin, zh, jnp.zeros((p, rest), F32)], axis=1)
    sb = jnp.concatenate([zh, sin, jnp.zeros((p, rest), F32)], axis=1)
    return c, sa, sb


def _block_expand(nbp, length):
    return (jnp.arange(length, dtype=jnp.int32)[None, :] // NSA_BLOCK
            == jnp.arange(nbp, dtype=jnp.int32)[:, None]).astype(BF16)


def _pad_axis(a, axis, size):
    pad = [(0, 0)] * a.ndim
    pad[axis] = (0, size - a.shape[axis])
    return jnp.pad(a, pad)


def _odd_w_in(w):
    qw = H_NSA * HEAD_DIM
    qkv = qw + 6 * G_NSA * HEAD_DIM
    ng = 3 * H_NSA
    gate = w[:, qkv:qkv + ng]
    per = 3 * HPG
    gcols = [_pad_axis(gate[:, g * per:(g + 1) * per], 1, LANES) for g in range(G_NSA)]
    out = jnp.concatenate([w[:, :qw], w[:, qkv + ng:], w[:, qw:qkv]] + gcols, axis=1)
    return _pad_axis(out, 1, -(-out.shape[1] // 1024) * 1024)


def kernel(x_prompt, x_sample, mem_prompt, state_ret, cache_dil_k, cache_dil_v, cache_nsa_cmp_k, cache_nsa_cmp_v, cache_nsa_slc_k, cache_nsa_slc_v, cache_nsa_win_k, cache_nsa_win_v, state_lru_h, state_lru_conv, cache_mem_k, cache_mem_v, page_table, ev_w_in, ret_gn_g, dil_q_norm_g, dil_k_norm_g, ev_w_out, od_w_in, nsa_q_norm_g, nsa_k_norm_g, nsa_pw_k, nsa_pw_v, lru_conv_w, lru_conv_b, lru_w_a, lru_b_a, lru_w_i, lru_b_i, lru_lambda, od_w_out, norm_mix_g, norm_mem_g, norm_mlp_g, mem_norm_g, mem_w_q, mem_w_k, mem_w_v, mem_q_norm_g, mem_k_norm_g, mem_w_o, mlp_w1, mlp_w2):
    nb_p, t_p, d = x_prompt.shape
    nb_s, t_s, _ = x_sample.shape
    depth = norm_mix_g.shape[0]
    n_pages = page_table.shape[1]
    past = n_pages * PAGE_SIZE
    rs = SAMPLE_ROWS
    assert t_p % RET_CHUNK == 0 and t_s <= rs and past % PAGE_SIZE == 0 and t_s < NSA_BLOCK
    assert cache_nsa_win_k.shape[2] % KEY_BLOCK == 0 and cache_dil_k.shape[2] % KEY_BLOCK == 0
    hd = HEAD_DIM

    xp = x_prompt.reshape(nb_p * t_p, d)
    xs = _pad_axis(x_sample, 1, rs).reshape(nb_s * rs, d)
    mem2 = mem_prompt.reshape(-1, d)
    n_mem = mem_prompt.shape[1]

    pos_p = jnp.arange(t_p, dtype=jnp.int32)
    pos_s = past + jnp.arange(rs, dtype=jnp.int32)
    tile_s = lambda tabs: tuple(jnp.tile(tb, (nb_s, 1)) for tb in tabs)
    rope_p, rope_s = _rope_tables(pos_p, ROPE_DIMS, ROPE_THETA), _rope_tables(pos_s, ROPE_DIMS, ROPE_THETA)
    ret_p, ret_s = _rope_tables(pos_p, hd, RET_THETA), _rope_tables(pos_s, hd, RET_THETA)
    log_g = jnp.log1p(-jnp.exp2(-5.0 - jnp.arange(H_RET, dtype=F32)))

    ev_w_in_b, ev_w_out_b, od_w_out_b = ev_w_in.astype(BF16), ev_w_out.astype(BF16), od_w_out.astype(BF16)
    od_w_in_b = jnp.stack([_odd_w_in(od_w_in[o]) for o in range(od_w_in.shape[0])]).astype(BF16)
    mem_wq_b, mem_wo_b = mem_w_q.astype(BF16), mem_w_o.astype(BF16)
    mem_wkv_b = jnp.concatenate([mem_w_k, mem_w_v], axis=2).astype(BF16)
    mlp_w1_b, mlp_w2_b = mlp_w1.astype(BF16), mlp_w2.astype(BF16)

    outs = {k: [] for k in ("ret_p", "ret_s", "dk_p", "dv_p", "dk_s", "dv_s", "ck_p", "cv_p", "ck_s", "cv_s",
                            "sk_p", "sv_p", "sk_s", "sv_s", "wk_p", "wv_p", "wk_s", "wv_s", "lh_p", "lh_s",
                            "lc_p", "lc_s", "mk_p", "mv_p")}
    new_rows = lambda a, w: a.reshape(nb_s, rs, w)[:, :t_s]

    for layer in range(depth):
        if layer % 2 == 0:
            e = layer // 2
            w_in, w_out, wl = ev_w_in_b, ev_w_out_b, e
            hw = H_RET * hd
            ones = jnp.ones((1, hd), F32)
            gains = jnp.concatenate([dil_k_norm_g[e][None], ones], axis=0)
            sections = [(5, "norm_rope"), (6, "copy")]
            proj = rms_matmul(xp, norm_mix_g[layer], w_in, wl, 1024)
            proj3 = proj.reshape(nb_p, t_p, -1)
            y_ret, s_new = retention(proj3, jnp.zeros((nb_p, H_RET, hd, hd), F32), log_g, ret_gn_g[e], ret_p, RET_CHUNK)
            dk, dv = kv_transform(proj, sections, gains, rope_p, H_DIL, t_p)
            y_dil = dilated_prompt(proj3, 4 * H_RET, dk.reshape(nb_p, t_p, hw), dv.reshape(nb_p, t_p, hw),
                                   dil_q_norm_g[e], rope_p)
            xp = matmul_res(y_ret.reshape(-1, hw), y_dil.reshape(-1, hw), w_out, wl, xp)
            keep = min(DIL_PATTERNS[-1][0], t_p)
            outs["ret_p"].append(s_new)
            outs["dk_p"].append(dk.reshape(nb_p, t_p, H_DIL, hd)[:, t_p - keep:])
            outs["dv_p"].append(dv.reshape(nb_p, t_p, H_DIL, hd)[:, t_p - keep:])
            proj = rms_matmul(xs, norm_mix_g[layer], w_in, wl, 1024)
            proj3 = proj.reshape(nb_s, rs, -1)
            y_ret, s_new = retention(proj3, state_ret.reshape(-1, H_RET, hd, hd), log_g, ret_gn_g[e], ret_s, t_s,
                                     seq_off=e * nb_s)
            dk, dv = kv_transform(proj, sections, gains, tile_s(rope_s), H_DIL, nb_s * rs)
            lc = cache_dil_k.shape[2]
            y_dil = dilated_sample(proj3, 4 * H_RET, cache_dil_k.reshape(-1, lc * H_DIL, hd),
                                   cache_dil_v.reshape(-1, lc * H_DIL, hd), e * nb_s, dk.reshape(nb_s, rs, hw),
                                   dv.reshape(nb_s, rs, hw), dil_q_norm_g[e], rope_s)
            xs = matmul_res(y_ret.reshape(-1, hw), y_dil.reshape(-1, hw), w_out, wl, xs)
            outs["ret_s"].append(s_new)
            outs["dk_s"].append(new_rows(dk, hw).reshape(nb_s, t_s, H_DIL, hd))
            outs["dv_s"].append(new_rows(dv, hw).reshape(nb_s, t_s, H_DIL, hd))
        else:
            o = layer // 2
            w_in, w_out, wl = od_w_in_b, od_w_out_b, o
            qw = H_NSA * hd
            gw = G_NSA * hd
            x_cb = qw // hd
            y_cb = x_cb + D_RNN // hd
            kv_cb = (qw + 2 * D_RNN) // gw
            gate_cb = (qw + 2 * D_RNN + 6 * gw) // hd
            ones = jnp.ones((1, hd), F32)
            kg = nsa_k_norm_g[o]
            gains = jnp.concatenate([kg[0:1], ones, kg[1:2], ones, kg[2:3], ones], axis=0)
            sections = [(kv_cb + i, "norm_rope" if i % 2 == 0 else "copy") for i in range(6)]
            pwk2 = jnp.concatenate([nsa_pw_k[o]] * G_NSA, axis=1)
            pwv2 = jnp.concatenate([nsa_pw_v[o]] * G_NSA, axis=1)
            sp = jax.nn.softplus(-lru_lambda[o].astype(F32))
            lru_w = (lru_conv_w[o], lru_conv_b[o], lru_w_a[o], lru_b_a[o], lru_w_i[o], lru_b_i[o], sp)
            proj = rms_matmul(xp, norm_mix_g[layer], w_in, wl, 1024)
            proj3 = proj.reshape(nb_p, t_p, -1)
            kcm, vcm, ksl, vsl, kwn, vwn = kv_transform(proj, sections, gains, rope_p, G_NSA, t_p)
            r3 = lambda a: a.reshape(nb_p, t_p, gw)
            nb = t_p // NSA_BLOCK
            nbp = -(-nb // LANES) * LANES
            kc, vc = compress_rows(r3(kcm), r3(vcm), pwk2, pwv2)
            o_nsa = nsa_prompt(proj3, gate_cb, _pad_axis(kc, 1, nbp), _pad_axis(vc, 1, nbp), r3(ksl), r3(vsl),
                               r3(kwn), r3(vwn), _block_expand(nbp, t_p), nsa_q_norm_g[o], rope_p, nb)
            y_rnn, tail = rglru(proj3, x_cb, y_cb, jnp.zeros((nb_p, SUBLANES, D_RNN), F32),
                                jnp.zeros((nb_p, D_RNN), F32), *lru_w)
            xp = matmul_res(o_nsa.reshape(-1, qw), y_rnn.reshape(-1, D_RNN), w_out, wl, xp)
            r4 = lambda a: a.reshape(nb_p, t_p, G_NSA, hd)
            keep = min(NSA_WINDOW, t_p)
            for key, val in (("ck_p", r4(kcm)), ("cv_p", r4(vcm)), ("sk_p", r4(ksl)), ("sv_p", r4(vsl)),
                             ("wk_p", r4(kwn)[:, t_p - keep:]), ("wv_p", r4(vwn)[:, t_p - keep:])):
                outs[key].append(val)
            outs["lh_p"].append(tail[:, SUBLANES - 1])
            outs["lc_p"].append(proj3[:, t_p - (CONV_W - 1):, x_cb * hd:x_cb * hd + D_RNN])
            proj = rms_matmul(xs, norm_mix_g[layer], w_in, wl, 1024)
            proj3 = proj.reshape(nb_s, rs, -1)
            kcm, vcm, ksl, vsl, kwn, vwn = kv_transform(proj, sections, gains, tile_s(rope_s), G_NSA, nb_s * rs)
            r3 = lambda a: a.reshape(nb_s, rs, gw)
            nph = cache_nsa_cmp_k.shape[1]
            pool = lambda a: a.reshape(-1, PAGE_SIZE * G_NSA, hd)
            pps = math.gcd(n_pages, 32)
            nb = (past + t_s) // NSA_BLOCK
            nbp = -(-nb // LANES) * LANES
            kc, vc = compress_pages(pool(cache_nsa_cmp_k), pool(cache_nsa_cmp_v), o * nph, page_table,
                                    nsa_pw_k[o], nsa_pw_v[o], pps)
            lw = cache_nsa_win_k.shape[2]
            o_nsa = nsa_sample(proj3, gate_cb, _pad_axis(kc, 1, nbp), _pad_axis(vc, 1, nbp),
                               pool(cache_nsa_slc_k), pool(cache_nsa_slc_v), o * nph, page_table, r3(ksl), r3(vsl),
                               cache_nsa_win_k.reshape(-1, lw * G_NSA, hd), cache_nsa_win_v.reshape(-1, lw * G_NSA, hd),
                               o * nb_s, r3(kwn), r3(vwn), _block_expand(nbp, past), nsa_q_norm_g[o], rope_s, pps)
            hist8 = _pad_axis(state_lru_conv[o].astype(F32), 1, SUBLANES)
            y_rnn, tail = rglru(proj3, x_cb, y_cb, hist8, state_lru_h[o].astype(F32), *lru_w)
            xs = matmul_res(o_nsa.reshape(-1, qw), y_rnn.reshape(-1, D_RNN), w_out, wl, xs)
            for key, val in (("ck_s", kcm), ("cv_s", vcm), ("sk_s", ksl), ("sv_s", vsl), ("wk_s", kwn), ("wv_s", vwn)):
                outs[key].append(new_rows(val, gw).reshape(nb_s, t_s, G_NSA, hd))
            outs["lh_s"].append(tail[:, t_s - 1])
            lx = proj3[:, :t_s, x_cb * hd:x_cb * hd + D_RNN]
            outs["lc_s"].append(jnp.concatenate([state_lru_conv[o].astype(F32), lx], axis=1)[:, -(CONV_W - 1):])

        dm = H_MEM * hd
        mkv = rms_matmul(mem2, mem_norm_g[layer], mem_wkv_b, layer, dm)
        gains = jnp.concatenate([mem_k_norm_g[layer][None], jnp.ones((1, hd), F32)], axis=0)
        mk, mv = kv_transform(mkv, [(0, "norm"), (1, "copy")], gains, rope_p, H_MEM, n_mem)
        mk3, mv3 = mk.reshape(nb_p, n_mem, dm), mv.reshape(nb_p, n_mem, dm)
        outs["mk_p"].append(mk3.reshape(nb_p, n_mem, H_MEM, hd))
        outs["mv_p"].append(mv3.reshape(nb_p, n_mem, H_MEM, hd))
        xp = mem_attn_block(xp, norm_mem_g[layer], mem_wq_b, mem_q_norm_g[layer], mk3, mv3, mem_wo_b, layer, t_p)
        xs = mem_attn_block(xs, norm_mem_g[layer], mem_wq_b, mem_q_norm_g[layer],
                            cache_mem_k.reshape(-1, n_mem * H_MEM, hd), cache_mem_v.reshape(-1, n_mem * H_MEM, hd),
                            mem_wo_b, layer, rs, seq_off=layer * nb_s)
        xp = mlp_block(xp, norm_mlp_g[layer], mlp_w1_b, mlp_w2_b, layer)
        xs = mlp_block(xs, norm_mlp_g[layer], mlp_w1_b, mlp_w2_b, layer)

    st = lambda k: jnp.stack(outs[k])
    return (xp.reshape(nb_p, t_p, d), xs.reshape(nb_s, rs, d)[:, :t_s], st("ret_p"), st("ret_s"),
            st("dk_p"), st("dv_p"), st("dk_s"), st("dv_s"),
            st("ck_p"), st("cv_p"), st("ck_s"), st("cv_s"),
            st("sk_p"), st("sv_p"), st("sk_s"), st("sv_s"),
            st("wk_p"), st("wv_p"), st("wk_s"), st("wv_s"),
            st("lh_p"), st("lh_s"), st("lc_p"), st("lc_s"),
            st("mk_p"), st("mv_p"))
```

```python
import functools
import math

import jax
import jax.numpy as jnp
from jax import lax
from jax.experimental import pallas as pl
from jax.experimental.pallas import tpu as pltpu

F32 = jnp.float32
BF16 = jnp.bfloat16

HEAD_DIM = 128
ROPE_DIMS = HEAD_DIM // 4
ROPE_THETA = 500000.0
NORM_EPS = 1e-6
H_RET = 8
RET_CHUNK = 128
RET_THETA = 10000.0
H_DIL = 8
DIL_PATTERNS = ((128, 1), (512, 4), (2048, 16))
H_NSA = 8
G_NSA = 2
HPG = H_NSA // G_NSA
NSA_BLOCK = 64
NSA_TOPK = 16
NSA_WINDOW = 512
D_RNN = 1024
RNN_BLOCKS = 8
CONV_W = 4
LRU_C = 8.0
H_MEM = 4
PAGE_SIZE = 128

LANES = 128
SUBLANES = 8
KEY_BLOCK = 128
SAMPLE_ROWS = 8
VMEM_LIMIT = 48 * 1024 * 1024
MLP_VMEM_LIMIT = 56 * 1024 * 1024
NEG = -1e30
SCALE = HEAD_DIM ** -0.5


def _cp(*sem):
    return pltpu.CompilerParams(dimension_semantics=sem, vmem_limit_bytes=VMEM_LIMIT)


def _tile(n, pref):
    t = min(n, pref)
    while n % t:
        t -= SUBLANES
    return t


def _rms(x, g):
    return x * lax.rsqrt(jnp.mean(x * x, axis=-1, keepdims=True) + NORM_EPS) * g


def _rope(x, c, sa, sb, half):
    return x * c + pltpu.roll(x, LANES - half, 1) * sa + pltpu.roll(x, half, 1) * sb


def _dot(a, b):
    return jnp.dot(a, b, preferred_element_type=F32)


def _dot_nt(a, b):
    return lax.dot_general(a, b, (((1,), (1,)), ((), ())), preferred_element_type=F32)


def _pad_rows(x, rows):
    if x.shape[0] == rows:
        return x
    return jnp.concatenate([x, jnp.zeros((rows - x.shape[0], x.shape[1]), x.dtype)], axis=0)


def _softmax_step(s, w, vb, m, l, acc):
    s = jnp.where(w > 0, s, NEG)
    m_new = jnp.maximum(m, jnp.max(s, axis=-1, keepdims=True))
    alpha = jnp.exp(m - m_new)
    p = jnp.exp(s - m_new) * w
    l = alpha * l + jnp.sum(p, axis=-1, keepdims=True)
    acc = alpha * acc + _dot(p.astype(BF16), vb)
    return m_new, l, acc


def _softmax_init(rows):
    return (jnp.full((rows, 1), NEG, F32), jnp.zeros((rows, 1), F32), jnp.zeros((rows, HEAD_DIM), F32))


def _softmax_done(l, acc):
    return acc / jnp.maximum(l, 1e-30)


def _dil_weight(d):
    w = jnp.zeros(d.shape, F32)
    for window, dil in DIL_PATTERNS:
        hit = jnp.where(d <= window, 1.0, 0.0)
        if dil > 1:
            rem = (d & (dil - 1)) if dil & (dil - 1) == 0 else lax.rem(d, dil)
            hit = jnp.where(rem == 0, hit, 0.0)
        w = w + hit
    return jnp.where(d >= 0, w, 0.0)


def _iota(shape, dim):
    return lax.broadcasted_iota(jnp.int32, shape, dim)


def _rms_matmul_kernel(x_ref, g_ref, w_ref, o_ref, xn_ref):
    @pl.when(pl.program_id(1) == 0)
    def _():
        xn_ref[...] = _rms(x_ref[...], g_ref[...]).astype(BF16)

    o_ref[...] = _dot(xn_ref[...], w_ref[...])


def rms_matmul(x, g, w, layer, tn):
    m, d = x.shape
    n = w.shape[2]
    tm = _tile(m, 1024)
    return pl.pallas_call(
        _rms_matmul_kernel,
        grid=(m // tm, n // tn),
        in_specs=[pl.BlockSpec((tm, d), lambda i, j: (i, 0)),
                  pl.BlockSpec((1, d), lambda i, j: (0, 0)),
                  pl.BlockSpec((None, d, tn), lambda i, j: (layer, 0, j))],
        out_specs=pl.BlockSpec((tm, tn), lambda i, j: (i, j)),
        out_shape=jax.ShapeDtypeStruct((m, n), F32),
        scratch_shapes=[pltpu.VMEM((tm, d), BF16)],
        compiler_params=_cp("parallel", "arbitrary"),
        name="rms_matmul",
    )(x, g.reshape(1, d), w)


def _matmul_res_kernel(a1_ref, a2_ref, w1_ref, w2_ref, r_ref, o_ref):
    o_ref[...] = r_ref[...] + _dot(a1_ref[...], w1_ref[...]) + _dot(a2_ref[...], w2_ref[...])


def matmul_res(a1, a2, w, layer, res):
    m, k1 = a1.shape
    k2 = a2.shape[1]
    n = w.shape[2]
    assert k1 == k2 and w.shape[1] == k1 + k2
    tm = _tile(m, 1024)
    tn = _tile(n, 1024)
    return pl.pallas_call(
        _matmul_res_kernel,
        grid=(m // tm, n // tn),
        in_specs=[pl.BlockSpec((tm, k1), lambda i, j: (i, 0)),
                  pl.BlockSpec((tm, k2), lambda i, j: (i, 0)),
                  pl.BlockSpec((None, k1, tn), lambda i, j: (layer, 0, j)),
                  pl.BlockSpec((None, k2, tn), lambda i, j: (layer, 1, j)),
                  pl.BlockSpec((tm, tn), lambda i, j: (i, j))],
        out_specs=pl.BlockSpec((tm, tn), lambda i, j: (i, j)),
        out_shape=jax.ShapeDtypeStruct((m, n), F32),
        compiler_params=_cp("parallel", "parallel"),
        name="matmul_res",
    )(a1, a2, w, w, res)


def _mlp_kernel(x_ref, g_ref, w1_ref, w2_ref, o_ref, xn_ref):
    @pl.when(pl.program_id(1) == 0)
    def _():
        x = x_ref[...]
        xn_ref[...] = _rms(x, g_ref[...]).astype(BF16)
        o_ref[...] = x

    hid = jnp.maximum(_dot(xn_ref[...], w1_ref[...].astype(BF16)), 0.0)
    o_ref[...] += _dot((hid * hid).astype(BF16), w2_ref[...].astype(BF16))


def mlp_block(x, g, w1, w2, layer):
    m, d = x.shape
    ff = w1.shape[2]
    tm = _tile(m, 1024)
    tf = _tile(ff, 512)
    once = pl.Buffered(1)
    return pl.pallas_call(
        _mlp_kernel,
        grid=(m // tm, ff // tf),
        in_specs=[pl.BlockSpec((tm, d), lambda i, f: (i, 0)),
                  pl.BlockSpec((1, d), lambda i, f: (0, 0)),
                  pl.BlockSpec((None, d, tf), lambda i, f: (layer, 0, f)),
                  pl.BlockSpec((None, tf, d), lambda i, f: (layer, f, 0))],
        out_specs=pl.BlockSpec((tm, d), lambda i, f: (i, 0), pipeline_mode=once),
        out_shape=jax.ShapeDtypeStruct((m, d), F32),
        scratch_shapes=[pltpu.VMEM((tm, d), BF16)],
        compiler_params=pltpu.CompilerParams(dimension_semantics=("parallel", "arbitrary"),
                                             vmem_limit_bytes=MLP_VMEM_LIMIT),
        name="mlp_block",
    )(x, g.reshape(1, d), w1, w2)


def _mem_attn_kernel(x_ref, g_ref, wq_ref, qg_ref, k_ref, v_ref, wo_ref, o_ref, oh_ref, *, nb, seg, by_row):
    x = x_ref[...]
    q = _dot(_rms(x, g_ref[...]).astype(BF16), wq_ref[...])
    qg = qg_ref[...]
    n_mem = k_ref.shape[1] // H_MEM if by_row else k_ref.shape[1]
    qh = [(_rms(q[:, h * HEAD_DIM:(h + 1) * HEAD_DIM], qg) * SCALE).astype(BF16) for h in range(H_MEM)]
    pairs = [(b, h) for b in range(nb) for h in range(H_MEM)]

    def mem_head(ref, b, h):
        if by_row:
            return ref[b, pl.ds(h, n_mem, stride=H_MEM), :].astype(BF16)
        return ref[b, :, h * HEAD_DIM:(h + 1) * HEAD_DIM].astype(BF16)

    s = jnp.concatenate([_dot_nt(qh[h][b * seg:(b + 1) * seg], mem_head(k_ref, b, h)) for b, h in pairs], axis=0)
    p = jnp.exp(s - jnp.max(s, axis=-1, keepdims=True))
    pb = (p / jnp.sum(p, axis=-1, keepdims=True)).astype(BF16)
    for j, (b, h) in enumerate(pairs):
        o = _dot(pb[j * seg:(j + 1) * seg], mem_head(v_ref, b, h))
        oh_ref[b * seg:(b + 1) * seg, h * HEAD_DIM:(h + 1) * HEAD_DIM] = o.astype(BF16)
    o_ref[...] = x + _dot(oh_ref[...], wo_ref[...])


def mem_attn_block(x, g, wq, qg, k, v, wo, layer, rows_per_seq, seq_off=0):
    m, d = x.shape
    dm = H_MEM * HEAD_DIM
    by_row = k.shape[2] == HEAD_DIM
    kv_rows, kv_w = k.shape[1], k.shape[2]
    if rows_per_seq >= 128:
        tm = _tile(rows_per_seq, 512)
        nb, seg = 1, tm
        kv_map = lambda i: (seq_off + i // (rows_per_seq // tm), 0, 0)
    else:
        nb = _tile(m // rows_per_seq, 8)
        tm, seg = nb * rows_per_seq, rows_per_seq
        assert seq_off % nb == 0
        kv_map = lambda i: (seq_off // nb + i, 0, 0)
    return pl.pallas_call(
        functools.partial(_mem_attn_kernel, nb=nb, seg=seg, by_row=by_row),
        grid=(m // tm,),
        in_specs=[pl.BlockSpec((tm, d), lambda i: (i, 0)),
                  pl.BlockSpec((1, d), lambda i: (0, 0)),
                  pl.BlockSpec((None, d, dm), lambda i: (layer, 0, 0)),
                  pl.BlockSpec((1, HEAD_DIM), lambda i: (0, 0)),
                  pl.BlockSpec((nb, kv_rows, kv_w), kv_map),
                  pl.BlockSpec((nb, kv_rows, kv_w), kv_map),
                  pl.BlockSpec((None, dm, d), lambda i: (layer, 0, 0))],
        out_specs=pl.BlockSpec((tm, d), lambda i: (i, 0)),
        out_shape=jax.ShapeDtypeStruct((m, d), F32),
        scratch_shapes=[pltpu.VMEM((tm, dm), BF16)],
        compiler_params=_cp("parallel"),
        name="mem_attn_block",
    )(x, g.reshape(1, d), wq, qg.reshape(1, HEAD_DIM), k, v, wo)


def _kv_transform_kernel(*refs, modes, heads):
    ns = len(modes)
    x_refs = refs[:ns]
    g_ref, c_ref, sa_ref, sb_ref = refs[ns:ns + 4]
    o_refs = refs[ns + 4:]
    for s in range(ns):
        for h in range(heads):
            cols = slice(h * HEAD_DIM, (h + 1) * HEAD_DIM)
            x = x_refs[s][:, cols]
            if modes[s] != "copy":
                x = _rms(x, g_ref[s:s + 1, :])
            if modes[s] == "norm_rope":
                x = _rope(x, c_ref[...], sa_ref[...], sb_ref[...], ROPE_DIMS // 2)
            o_refs[s][:, cols] = x


def kv_transform(proj, sections, gains, tables, heads, pos_period):
    m = proj.shape[0]
    width = heads * HEAD_DIM
    tr = _tile(pos_period, 512)
    nper = pos_period // tr
    modes = tuple(mode for _, mode in sections)
    in_specs = [pl.BlockSpec((tr, width), functools.partial(lambda i, cb: (i, cb), cb=cb)) for cb, _ in sections]
    in_specs.append(pl.BlockSpec(gains.shape, lambda i: (0, 0)))
    in_specs += [pl.BlockSpec((tr, HEAD_DIM), lambda i: (i % nper, 0))] * 3
    outs = pl.pallas_call(
        functools.partial(_kv_transform_kernel, modes=modes, heads=heads),
        grid=(m // tr,),
        in_specs=in_specs,
        out_specs=[pl.BlockSpec((tr, width), lambda i: (i, 0))] * len(sections),
        out_shape=[jax.ShapeDtypeStruct((m, width), F32)] * len(sections),
        compiler_params=_cp("parallel"),
        name="kv_transform",
    )(*([proj] * len(sections)), gains, *tables)
    return outs


def _retention_kernel(q_ref, k_ref, v_ref, rg_ref, s0_ref, lg_ref, gn_ref, c_ref, sa_ref, sb_ref,
                      y_ref, s_ref, st_ref, *, rows, c_true, hps):
    ci = pl.program_id(2)
    cc = RET_CHUNK

    @pl.when(ci == 0)
    def _():
        st_ref[...] = s0_ref[0]

    c, sa, sb = c_ref[...], sa_ref[...], sb_ref[...]
    ii = _iota((cc, 1), 0).astype(F32)
    diff = ii - _iota((1, cc), 1).astype(F32)
    for h in range(hps):
        cols = slice(h * HEAD_DIM, (h + 1) * HEAD_DIM)
        lg = lg_ref[h]
        q = _pad_rows(_rope(q_ref[0, :, cols], c, sa, sb, HEAD_DIM // 2) * SCALE, cc)
        k = _pad_rows(_rope(k_ref[0, :, cols], c, sa, sb, HEAD_DIM // 2), cc)
        vb = _pad_rows(v_ref[0, :, cols], cc).astype(BF16)
        qb = q.astype(BF16)
        decay = jnp.where(diff >= 0, jnp.exp(jnp.maximum(diff, 0.0) * lg), 0.0)
        inner = _dot_nt(qb, k.astype(BF16)) * decay
        st = st_ref[h]
        o = _dot(inner.astype(BF16), vb) + _dot(qb, st.astype(BF16)) * jnp.exp((ii + 1.0) * lg)
        kd = jnp.where(ii < c_true, k * jnp.exp((c_true - 1.0 - ii) * lg), 0.0)
        st_ref[h] = jnp.exp(c_true * lg) * st + _dot(kd.T.astype(BF16), vb)
        o = o[:rows]
        mu = jnp.mean(o, axis=-1, keepdims=True)
        var = jnp.mean(jnp.square(o - mu), axis=-1, keepdims=True)
        y = (o - mu) * lax.rsqrt(var + NORM_EPS) * gn_ref[h]
        rg = rg_ref[0, :, cols]
        y_ref[0, :, cols] = (y * (rg * jax.nn.sigmoid(rg))).astype(BF16)

    @pl.when(ci == pl.num_programs(2) - 1)
    def _():
        s_ref[0] = st_ref[...]


def retention(proj3, s0, log_g, gn_g, tables, c_true, seq_off=0):
    n, t, _ = proj3.shape
    rows = min(t, RET_CHUNK)
    nc = t // rows
    h = H_RET
    hps = H_RET
    ng = h // hps
    wid = hps * HEAD_DIM
    blk = lambda sec: pl.BlockSpec((1, rows, wid), functools.partial(lambda b, hg, c, sec: (b, c, sec * ng + hg), sec=sec))
    tab = pl.BlockSpec((rows, HEAD_DIM), lambda b, hg, c: (c, 0))
    y, s_new = pl.pallas_call(
        functools.partial(_retention_kernel, rows=rows, c_true=float(c_true), hps=hps),
        grid=(n, ng, nc),
        in_specs=[blk(0), blk(1), blk(2), blk(3),
                  pl.BlockSpec((1, hps, HEAD_DIM, HEAD_DIM), lambda b, hg, c: (seq_off + b, hg, 0, 0)),
                  pl.BlockSpec((hps, 1, HEAD_DIM), lambda b, hg, c: (hg, 0, 0)),
                  pl.BlockSpec((hps, 1, HEAD_DIM), lambda b, hg, c: (hg, 0, 0)),
                  tab, tab, tab],
        out_specs=[pl.BlockSpec((1, rows, wid), lambda b, hg, c: (b, c, hg)),
                   pl.BlockSpec((1, hps, HEAD_DIM, HEAD_DIM), lambda b, hg, c: (b, hg, 0, 0))],
        out_shape=[jax.ShapeDtypeStruct((n, t, h * HEAD_DIM), BF16),
                   jax.ShapeDtypeStruct((n, h, HEAD_DIM, HEAD_DIM), F32)],
        scratch_shapes=[pltpu.VMEM((hps, HEAD_DIM, HEAD_DIM), F32)],
        compiler_params=_cp("parallel", "parallel", "arbitrary"),
        name="retention",
    )(proj3, proj3, proj3, proj3, s0,
      jnp.broadcast_to(log_g[:, None, None], (h, 1, HEAD_DIM)), gn_g.reshape(h, 1, HEAD_DIM), *tables)
    return y, s_new


def _rows(start, size, stride):
    return pl.ds(start, size) if stride == 1 else pl.ds(start, size, stride=stride)


def _dil_prompt_kernel(q_ref, k_ref, v_ref, qg_ref, c_ref, sa_ref, sb_ref, o_ref,
                       qs_ref, m_ref, l_ref, acc_ref, pm_ref, pls_ref, pacc_ref, qc_ref, kc_ref, vc_ref, *, t):
    chunk = _tile(t, 512)
    for c in range(t // chunk):
        r = slice(c * chunk, (c + 1) * chunk)
        q = _rope(_rms(q_ref[0, r, :], qg_ref[...]), c_ref[r, :], sa_ref[r, :], sb_ref[r, :], ROPE_DIMS // 2)
        qs_ref[r, :] = q * SCALE

    def bias(bq, nk, first_key_dist):
        rel = first_key_dist + _iota((bq, nk), 0) - _iota((bq, nk), 1)
        return jnp.where(rel >= 0, jnp.where(rel <= KEY_BLOCK, 0.0, NEG), NEG)

    for pi, (window, dil) in enumerate(DIL_PATTERNS):
        length = t // dil
        bq = min(length, KEY_BLOCK)
        nk = bq if length == bq else 2 * bq
        for r in range(dil):
            rows, crow = _rows(r, length, dil), slice(r * length, (r + 1) * length)
            qc_ref[crow, :] = qs_ref[rows, :].astype(BF16)
            kc_ref[crow, :] = k_ref[0, rows, :].astype(BF16)
            vc_ref[crow, :] = v_ref[0, rows, :].astype(BF16)
        blocks = [(r * length + i * bq, r * length + max(i - 1, 0) * bq, i == 0 and nk > bq)
                  for r in range(dil) for i in range(length // bq)]
        band, lead = bias(bq, nk, nk - bq), bias(bq, nk, 0)
        s = jnp.concatenate([_dot_nt(qc_ref[q0:q0 + bq, :], kc_ref[k0:k0 + nk, :]) + (lead if first else band)
                             for q0, k0, first in blocks], axis=0)
        m = jnp.max(s, axis=-1, keepdims=True)
        pb = jnp.exp(s - m).astype(BF16)
        m = jnp.broadcast_to(m, (t, HEAD_DIM))
        ones = jnp.ones((nk, HEAD_DIM), BF16)
        l = jnp.concatenate([_dot(pb[j * bq:(j + 1) * bq], ones) for j in range(len(blocks))], axis=0)
        acc = jnp.concatenate([_dot(pb[j * bq:(j + 1) * bq], vc_ref[k0:k0 + nk, :])
                               for j, (q0, k0, first) in enumerate(blocks)], axis=0)
        if pi == 0:
            assert dil == 1
            m_ref[...], l_ref[...], acc_ref[...] = m, l, acc
        else:
            pm_ref[...], pls_ref[...], pacc_ref[...] = m, l, acc
            for r in range(dil):
                rows, crow = _rows(r, length, dil), slice(r * length, (r + 1) * length)
                m0, m1 = m_ref[rows, :], pm_ref[crow, :]
                mx = jnp.maximum(m0, m1)
                e0, e1 = jnp.exp(m0 - mx), jnp.exp(m1 - mx)
                m_ref[rows, :] = mx
                l_ref[rows, :] = e0 * l_ref[rows, :] + e1 * pls_ref[crow, :]
                acc_ref[rows, :] = e0 * acc_ref[rows, :] + e1 * pacc_ref[crow, :]

    for c in range(t // chunk):
        r = slice(c * chunk, (c + 1) * chunk)
        o_ref[0, r, :] = (acc_ref[r, :] / l_ref[r, :]).astype(BF16)


def dilated_prompt(proj3, q_cb, k3, v3, qg, tables):
    n, t, _ = proj3.shape
    for window, dil in DIL_PATTERNS:
        length = t // dil
        assert window // dil == KEY_BLOCK and t % dil == 0
        assert length % KEY_BLOCK == 0 or (length < KEY_BLOCK and length % SUBLANES == 0)
    tab = pl.BlockSpec((t, HEAD_DIM), lambda b, h: (0, 0))
    seq = lambda cb: pl.BlockSpec((1, t, HEAD_DIM), functools.partial(lambda b, h, cb: (b, 0, cb + h), cb=cb))
    return pl.pallas_call(
        functools.partial(_dil_prompt_kernel, t=t),
        grid=(n, H_DIL),
        in_specs=[seq(q_cb), seq(0), seq(0), pl.BlockSpec((1, HEAD_DIM), lambda b, h: (0, 0)), tab, tab, tab],
        out_specs=seq(0),
        out_shape=jax.ShapeDtypeStruct((n, t, H_DIL * HEAD_DIM), BF16),
        scratch_shapes=[pltpu.VMEM((t, HEAD_DIM), F32)] * 7 + [pltpu.VMEM((t, HEAD_DIM), BF16)] * 3,
        compiler_params=_cp("parallel", "parallel"),
        name="dilated_prompt",
    )(proj3, k3, v3, qg.reshape(1, HEAD_DIM), *tables)


def _dil_sample_kernel(q_ref, kc_ref, vc_ref, kn_ref, vn_ref, qg_ref, c_ref, sa_ref, sb_ref, o_ref,
                       qb_ref, m_ref, l_ref, acc_ref, *, lc, ck):
    ci = pl.program_id(1)
    rows = q_ref.shape[1]
    head = lambda a, h: a[h * rows:(h + 1) * rows]

    @pl.when(ci == 0)
    def _():
        qs = []
        for h in range(H_DIL):
            cols = slice(h * HEAD_DIM, (h + 1) * HEAD_DIM)
            q = _rope(_rms(q_ref[0, :, cols], qg_ref[...]), c_ref[...], sa_ref[...], sb_ref[...], ROPE_DIMS // 2)
            qs.append(q * SCALE)
        qb_ref[...] = jnp.concatenate(qs, axis=0)
        m_ref[...], l_ref[...], acc_ref[...] = _softmax_init(H_DIL * rows)

    def log_weight(d):
        w = _dil_weight(d)
        return jnp.where(w > 0, jnp.log(jnp.maximum(w, 1.0)), NEG)

    def step(scores, values):
        s = jnp.concatenate(scores, axis=0)
        m_new = jnp.maximum(m_ref[...], jnp.max(s, axis=-1, keepdims=True))
        alpha = jnp.exp(m_ref[...] - m_new)
        p = jnp.exp(s - m_new)
        pb = p.astype(BF16)
        l = alpha * l_ref[...] + jnp.sum(p, axis=-1, keepdims=True)
        acc = alpha * acc_ref[...] + jnp.concatenate([_dot(head(pb, h), values(h)) for h in range(H_DIL)], axis=0)
        m_ref[...], l_ref[...], acc_ref[...] = m_new, l, acc
        return l, acc

    qb = qb_ref[...].astype(BF16)
    bias_c = log_weight(lc + _iota((rows, ck), 0) - (ci * ck + _iota((rows, ck), 1)))
    keys = lambda h: pl.ds(h, ck, stride=H_DIL)
    step([_dot_nt(head(qb, h), kc_ref[0, keys(h), :].astype(BF16)) + bias_c for h in range(H_DIL)],
         lambda h: vc_ref[0, keys(h), :].astype(BF16))

    @pl.when(ci == pl.num_programs(1) - 1)
    def _():
        bias_n = log_weight(_iota((rows, KEY_BLOCK), 0) - _iota((rows, KEY_BLOCK), 1))
        new = lambda ref, h: _pad_rows(ref[0, :, h * HEAD_DIM:(h + 1) * HEAD_DIM], KEY_BLOCK).astype(BF16)
        l, acc = step([_dot_nt(head(qb, h), new(kn_ref, h)) + bias_n for h in range(H_DIL)], lambda h: new(vn_ref, h))
        o = _softmax_done(l, acc)
        for h in range(H_DIL):
            o_ref[0, :, h * HEAD_DIM:(h + 1) * HEAD_DIM] = head(o, h).astype(BF16)


def dilated_sample(proj3, q_cb, kc3, vc3, seq_off, kn3, vn3, qg, tables):
    n, rows, _ = proj3.shape
    lc = kc3.shape[1] // H_DIL
    wid = H_DIL * HEAD_DIM
    ck = lc // 2 if lc % (2 * KEY_BLOCK) == 0 else lc
    assert (q_cb * HEAD_DIM) % wid == 0
    tab = pl.BlockSpec((rows, HEAD_DIM), lambda b, c: (0, 0))
    cache = pl.BlockSpec((1, ck * H_DIL, HEAD_DIM), lambda b, c: (seq_off + b, c, 0))
    new = pl.BlockSpec((1, rows, wid), lambda b, c: (b, 0, 0))
    return pl.pallas_call(
        functools.partial(_dil_sample_kernel, lc=lc, ck=ck),
        grid=(n, lc // ck),
        in_specs=[pl.BlockSpec((1, rows, wid), lambda b, c: (b, 0, q_cb * HEAD_DIM // wid)), cache, cache, new, new,
                  pl.BlockSpec((1, HEAD_DIM), lambda b, c: (0, 0)), tab, tab, tab],
        out_specs=new,
        out_shape=jax.ShapeDtypeStruct((n, rows, wid), BF16),
        scratch_shapes=[pltpu.VMEM((H_DIL * rows, HEAD_DIM), F32), pltpu.VMEM((H_DIL * rows, 1), F32),
                        pltpu.VMEM((H_DIL * rows, 1), F32), pltpu.VMEM((H_DIL * rows, HEAD_DIM), F32)],
        compiler_params=_cp("parallel", "arbitrary"),
        name="dilated_sample",
    )(proj3, kc3, vc3, kn3, vn3, qg.reshape(1, HEAD_DIM), *tables)


def _nsa_q(q_ref, g, qg_ref, c_ref, sa_ref, sb_ref):
    qs = []
    for hl in range(HPG):
        col = (g * HPG + hl) * HEAD_DIM
        x = _rms(q_ref[0][:, col:col + HEAD_DIM], qg_ref[...])
        qs.append(_rope(x, c_ref[...], sa_ref[...], sb_ref[...], ROPE_DIMS // 2) * SCALE)
    return jnp.concatenate(qs, axis=0).astype(BF16)


def _nsa_compressed(q4, kc, vc, qpos, nb):
    rows = qpos.shape[0]
    nbp = kc.shape[0]
    blk = _iota((rows, nbp), 1)
    cmask = jnp.where((blk + 1) * NSA_BLOCK - 1 <= qpos, 1.0, 0.0)
    cmask4 = jnp.concatenate([cmask] * HPG, axis=0)
    s = jnp.where(cmask4 > 0, _dot_nt(q4, kc.astype(BF16)), NEG)
    p = jnp.exp(s - jnp.max(s, axis=-1, keepdims=True)) * cmask4
    p = p / jnp.maximum(jnp.sum(p, axis=-1, keepdims=True), 1e-30)
    o_c = _dot(p.astype(BF16), vc.astype(BF16))
    imp = p[0:rows]
    for hl in range(1, HPG):
        imp = imp + p[hl * rows:(hl + 1) * rows]
    shift = NSA_BLOCK.bit_length() - 1
    if rows == nbp:
        nbr = -(-nb // SUBLANES) * SUBLANES
        blk_t = _iota((nbr, rows), 0)
        cur_t = jnp.right_shift(qpos[0:1, :] + _iota((1, rows), 1), shift)
        cand_t = blk_t < cur_t
        imp_t = jnp.where(cand_t, imp.T[:nbr], -jnp.inf)
        rank = jnp.zeros((nbr, rows), F32)
        for b in range(nb):
            row = imp_t[b:b + 1, :]
            tie = jnp.where(row == imp_t, jnp.where(blk_t > b, 1.0, 0.0), 0.0)
            rank = rank + jnp.where(row > imp_t, 1.0, tie)
        sel_t = jnp.where(cand_t, jnp.where(rank < min(NSA_TOPK, nb), 1.0, 0.0), 0.0)
        sel = _pad_rows(sel_t, nbp).T
    else:
        cand = blk < jnp.right_shift(qpos, shift)
        imp = jnp.where(cand, imp, -jnp.inf)
        rank = jnp.zeros((rows, nbp), F32)
        for b in range(nb):
            col = imp[:, b:b + 1]
            tie = jnp.where(col == imp, jnp.where(blk > b, 1.0, 0.0), 0.0)
            rank = rank + jnp.where(col > imp, 1.0, tie)
        sel = jnp.where(cand, jnp.where(rank < min(NSA_TOPK, nb), 1.0, 0.0), 0.0)
    sel = jnp.where(blk == jnp.right_shift(qpos, shift), 1.0, sel)
    return o_c, sel


def _nsa_combine(gates, g, o_c, o_s, o_w, rows):
    outs = []
    for hl in range(HPG):
        r = slice(hl * rows, (hl + 1) * rows)
        col = 3 * hl
        outs.append(gates[:, col:col + 1] * o_c[r] + gates[:, col + 1:col + 2] * o_s[r]
                    + gates[:, col + 2:col + 3] * o_w[r])
    return outs


def _nsa_prompt_kernel(q_ref, gt_ref, kc_ref, vc_ref, ks_ref, vs_ref, kw_ref, vw_ref, e_ref,
                       qg_ref, c_ref, sa_ref, sb_ref, o_ref, selx_ref, s_ref, mx_ref, ls_ref, acc_ref, *, tq, nb, t):
    qi = pl.program_id(2)
    q4 = _nsa_q(q_ref, 0, qg_ref, c_ref, sa_ref, sb_ref)
    qpos = qi * tq + _iota((tq, 1), 0)
    o_c, sel = _nsa_compressed(q4, kc_ref[0], vc_ref[0], qpos, nb)
    selx_ref[...] = _dot(sel.astype(BF16), e_ref[...])

    ck = _tile(t, 4 * KEY_BLOCK)
    nchunks = lax.div(qi * tq + (tq + ck - 1), ck)
    lane = _iota((tq, ck), 1)
    mx_ref[...] = jnp.full(mx_ref.shape, NEG, F32)

    def scores(c, _):
        off = pl.multiple_of(c * ck, ck)
        bias = jnp.where(off + lane <= qpos, (selx_ref[:, pl.ds(off, ck)] - 1.0) * -NEG, NEG)
        s = _dot_nt(q4, ks_ref[0, pl.ds(off, ck), :].astype(BF16))
        s = (s.reshape(HPG, tq, ck) + bias[None]).reshape(HPG * tq, ck)
        s_ref[:, pl.ds(off, ck)] = s
        part = s[:, :KEY_BLOCK]
        for j in range(1, ck // KEY_BLOCK):
            part = jnp.maximum(part, s[:, j * KEY_BLOCK:(j + 1) * KEY_BLOCK])
        mx_ref[...] = jnp.maximum(mx_ref[...], part)
        return 0

    lax.fori_loop(0, nchunks, scores, 0)
    m = jnp.max(mx_ref[...], axis=-1, keepdims=True)
    ls_ref[...] = jnp.zeros(ls_ref.shape, F32)
    acc_ref[...] = jnp.zeros(acc_ref.shape, F32)

    def values(c, _):
        off = pl.multiple_of(c * ck, ck)
        s = s_ref[:, pl.ds(off, ck)]
        p = jnp.exp(s - m)
        part = p[:, :KEY_BLOCK]
        for j in range(1, ck // KEY_BLOCK):
            part = part + p[:, j * KEY_BLOCK:(j + 1) * KEY_BLOCK]
        ls_ref[...] += part
        acc_ref[...] += _dot(p.astype(BF16), vs_ref[0, pl.ds(off, ck), :].astype(BF16))
        return 0

    lax.fori_loop(0, nchunks, values, 0)
    o_s = acc_ref[...] / jnp.maximum(jnp.sum(ls_ref[...], axis=-1, keepdims=True), 1e-30)

    span = min(NSA_WINDOW + KEY_BLOCK, t)
    first = jnp.clip(qi - NSA_WINDOW // KEY_BLOCK, 0, (t - span) // KEY_BLOCK)
    off = pl.multiple_of(first * KEY_BLOCK, KEY_BLOCK)
    d = qpos - (off + _iota((tq, span), 1))
    bias = jnp.where(d >= 0, jnp.where(d < NSA_WINDOW, 0.0, NEG), NEG)
    s = _dot_nt(q4, kw_ref[0, pl.ds(off, span), :].astype(BF16))
    s = (s.reshape(HPG, tq, span) + bias[None]).reshape(HPG * tq, span)
    p = jnp.exp(s - jnp.max(s, axis=-1, keepdims=True))
    o_w = _dot(p.astype(BF16), vw_ref[0, pl.ds(off, span), :].astype(BF16)) / jnp.sum(p, axis=-1, keepdims=True)
    gates = jax.nn.sigmoid(gt_ref[0])
    for hl, o in enumerate(_nsa_combine(gates, 0, o_c, o_s, o_w, tq)):
        o_ref[0, :, hl * HEAD_DIM:(hl + 1) * HEAD_DIM] = o.astype(BF16)


def nsa_prompt(proj3, gate_cb, kc3, vc3, ks3, vs3, kw3, vw3, expand, qg, tables, nb):
    n, t, _ = proj3.shape
    tq = KEY_BLOCK
    nbp = kc3.shape[1]
    gw = HPG * HEAD_DIM
    tab = pl.BlockSpec((tq, HEAD_DIM), lambda b, g, i: (i, 0))
    cmp_spec = pl.BlockSpec((1, nbp, HEAD_DIM), lambda b, g, i: (b, 0, g))
    kv = pl.BlockSpec((1, t, HEAD_DIM), lambda b, g, i: (b, 0, g))
    return pl.pallas_call(
        functools.partial(_nsa_prompt_kernel, tq=tq, nb=nb, t=t),
        grid=(n, G_NSA, t // tq),
        in_specs=[pl.BlockSpec((1, tq, gw), lambda b, g, i: (b, i, g)),
                  pl.BlockSpec((1, tq, HEAD_DIM), lambda b, g, i: (b, i, gate_cb + g)),
                  cmp_spec, cmp_spec, kv, kv, kv, kv,
                  pl.BlockSpec((nbp, t), lambda b, g, i: (0, 0)),
                  pl.BlockSpec((1, HEAD_DIM), lambda b, g, i: (0, 0)), tab, tab, tab],
        out_specs=pl.BlockSpec((1, tq, gw), lambda b, g, i: (b, i, g)),
        out_shape=jax.ShapeDtypeStruct((n, t, H_NSA * HEAD_DIM), BF16),
        scratch_shapes=[pltpu.VMEM((tq, t), F32), pltpu.VMEM((HPG * tq, t), F32)]
        + [pltpu.VMEM((HPG * tq, HEAD_DIM), F32)] * 3,
        compiler_params=_cp("parallel", "parallel", "arbitrary"),
        name="nsa_prompt",
    )(proj3, proj3, kc3, vc3, ks3, vs3, kw3, vw3, expand, qg.reshape(1, HEAD_DIM), *tables)


def _compress_rows_kernel(k_ref, v_ref, pk_ref, pv_ref, ko_ref, vo_ref):
    for x_ref, p_ref, o_ref in ((k_ref, pk_ref, ko_ref), (v_ref, pv_ref, vo_ref)):
        x = x_ref[0]
        nblk = x.shape[0] // NSA_BLOCK
        pw = jnp.concatenate([p_ref[...]] * nblk, axis=0)
        o_ref[0] = jnp.sum((x * pw).reshape(nblk, NSA_BLOCK, x.shape[1]), axis=1)


def compress_rows(k3, v3, pwk2, pwv2):
    n, t, w = k3.shape
    tr = _tile(t, 512)
    row = pl.BlockSpec((1, tr, w), lambda b, i: (b, i, 0))
    pw = pl.BlockSpec((NSA_BLOCK, w), lambda b, i: (0, 0))
    out = pl.BlockSpec((1, tr // NSA_BLOCK, w), lambda b, i: (b, i, 0))
    return pl.pallas_call(
        _compress_rows_kernel,
        grid=(n, t // tr),
        in_specs=[row, row, pw, pw],
        out_specs=[out, out],
        out_shape=[jax.ShapeDtypeStruct((n, t // NSA_BLOCK, w), F32)] * 2,
        compiler_params=_cp("parallel", "parallel"),
        name="compress_rows",
    )(k3, v3, pwk2, pwv2)


def _compress_pages_kernel(pt_ref, *refs, pps):
    k_refs, v_refs = refs[:pps], refs[pps:2 * pps]
    pk_ref, pv_ref, ko_ref, vo_ref = refs[2 * pps:]
    bpp = PAGE_SIZE // NSA_BLOCK
    prows = PAGE_SIZE * G_NSA
    for x_refs, p_ref, o_ref in ((k_refs, pk_ref, ko_ref), (v_refs, pv_ref, vo_ref)):
        pw = p_ref[...]
        outs = []
        for x_ref in x_refs:
            z = jnp.sum((x_ref[0] * pw).reshape(bpp, prows // bpp // SUBLANES, SUBLANES, HEAD_DIM), axis=1)
            per_g = [sum(z[:, s] for s in range(g, SUBLANES, G_NSA)) for g in range(G_NSA)]
            outs.append(jnp.concatenate(per_g, axis=1))
        o_ref[0] = jnp.concatenate(outs, axis=0)


def compress_pages(pool_k, pool_v, page_off, page_table, pwk, pwv, pps):
    n, n_pages = page_table.shape
    w = G_NSA * HEAD_DIM
    prows = PAGE_SIZE * G_NSA
    bpp = PAGE_SIZE // NSA_BLOCK
    assert SUBLANES % G_NSA == 0
    page_w = lambda p: jnp.repeat(jnp.tile(p, (bpp, 1)), G_NSA, axis=0)
    pwk2, pwv2 = page_w(pwk), page_w(pwv)
    page = lambda i: pl.BlockSpec((1, prows, HEAD_DIM),
                                  functools.partial(lambda b, c, pt, i: (page_off + pt[b, c * pps + i], 0, 0), i=i))
    pw = pl.BlockSpec((prows, HEAD_DIM), lambda b, c, pt: (0, 0))
    out = pl.BlockSpec((1, pps * bpp, w), lambda b, c, pt: (b, c, 0))
    return pl.pallas_call(
        functools.partial(_compress_pages_kernel, pps=pps),
        grid_spec=pltpu.PrefetchScalarGridSpec(
            num_scalar_prefetch=1, grid=(n, n_pages // pps),
            in_specs=[page(i) for i in range(pps)] * 2 + [pw, pw],
            out_specs=[out, out]),
        out_shape=[jax.ShapeDtypeStruct((n, n_pages * bpp, w), F32)] * 2,
        compiler_params=_cp("parallel", "parallel"),
        name="compress_pages",
    )(page_table, *([pool_k] * pps), *([pool_v] * pps), pwk2, pwv2)


def _nsa_sample_kernel(pt_ref, *refs, pps, past, rows, nb, lw):
    (q_ref, gt_ref, kc_ref, vc_ref), refs = refs[:4], refs[4:]
    kp_refs, vp_refs, refs = refs[:pps], refs[pps:2 * pps], refs[2 * pps:]
    (kn_ref, vn_ref, kwc_ref, vwc_ref, kwn_ref, vwn_ref, e_ref, qg_ref, c_ref, sa_ref, sb_ref,
     o_ref, q4_ref, sel_ref, oc_ref, m_ref, l_ref, acc_ref) = refs
    pc = pl.program_id(1)
    r4 = HPG * rows
    qpos = past + _iota((rows, 1), 0)
    lane = _iota((rows, KEY_BLOCK), 1)

    @pl.when(pc == 0)
    def _():
        for g in range(G_NSA):
            cols = slice(g * HEAD_DIM, (g + 1) * HEAD_DIM)
            q4 = _nsa_q(q_ref, g, qg_ref, c_ref, sa_ref, sb_ref)
            o_c, sel = _nsa_compressed(q4, kc_ref[0][:, cols], vc_ref[0][:, cols], qpos, nb)
            q4_ref[g] = q4
            sel_ref[g] = sel
            oc_ref[g] = o_c
            m0, l0, a0 = _softmax_init(r4)
            m_ref[g], l_ref[g], acc_ref[g] = m0, l0, a0

    nkeys = pps * PAGE_SIZE
    kpos = pc * nkeys + _iota((rows, nkeys), 1)
    for g in range(G_NSA):
        cols = slice(g * HEAD_DIM, (g + 1) * HEAD_DIM)
        grp = pl.ds(g, PAGE_SIZE, stride=G_NSA)
        kb = jnp.concatenate([r[0, grp, :].astype(BF16) for r in kp_refs], axis=0)
        vb = jnp.concatenate([r[0, grp, :].astype(BF16) for r in vp_refs], axis=0)
        selx = _dot(sel_ref[g].astype(BF16), e_ref[...])
        w = jnp.where(kpos <= qpos, selx, 0.0)
        m_ref[g], l_ref[g], acc_ref[g] = _softmax_step(
            _dot_nt(q4_ref[g], kb), jnp.concatenate([w] * HPG, axis=0), vb, m_ref[g], l_ref[g], acc_ref[g])

    @pl.when(pc == pl.num_programs(1) - 1)
    def _():
        gates_all = jax.nn.sigmoid(gt_ref[0])
        tq = _iota((rows, KEY_BLOCK), 0)
        for g in range(G_NSA):
            cols = slice(g * HEAD_DIM, (g + 1) * HEAD_DIM)
            q4 = q4_ref[g]
            kb = _pad_rows(kn_ref[0][:, cols], KEY_BLOCK).astype(BF16)
            vb = _pad_rows(vn_ref[0][:, cols], KEY_BLOCK).astype(BF16)
            w = jnp.where(lane <= tq, 1.0, 0.0)
            _, l, acc = _softmax_step(_dot_nt(q4, kb), jnp.concatenate([w] * HPG, axis=0), vb,
                                      m_ref[g], l_ref[g], acc_ref[g])
            o_s = _softmax_done(l, acc)
            tqw = _iota((rows, lw), 0)
            d = tqw + lw - _iota((rows, lw), 1)
            w = jnp.where(d < NSA_WINDOW, jnp.where(past - d + tqw >= 0, 1.0, 0.0), 0.0)
            grp = pl.ds(g, lw, stride=G_NSA)
            carry = _softmax_step(_dot_nt(q4, kwc_ref[0, grp, :].astype(BF16)), jnp.concatenate([w] * HPG, axis=0),
                                  vwc_ref[0, grp, :].astype(BF16), *_softmax_init(r4))
            kb = _pad_rows(kwn_ref[0][:, cols], KEY_BLOCK).astype(BF16)
            vb = _pad_rows(vwn_ref[0][:, cols], KEY_BLOCK).astype(BF16)
            w = jnp.where(lane <= tq, 1.0, 0.0)
            _, l, acc = _softmax_step(_dot_nt(q4, kb), jnp.concatenate([w] * HPG, axis=0), vb, *carry)
            o_w = _softmax_done(l, acc)
            gates = gates_all[:, g * HEAD_DIM:(g + 1) * HEAD_DIM]
            for hl, o in enumerate(_nsa_combine(gates, g, oc_ref[g], o_s, o_w, rows)):
                col = (g * HPG + hl) * HEAD_DIM
                o_ref[0, :, col:col + HEAD_DIM] = o.astype(BF16)


def nsa_sample(proj3, gate_cb, kc3, vc3, pool_k, pool_v, page_off, page_table, kn3, vn3, kwc3, vwc3, seq_off,
               kwn3, vwn3, expand, qg, tables, pps):
    n, rows, _ = proj3.shape
    n_pages = page_table.shape[1]
    past = n_pages * PAGE_SIZE
    nbp = kc3.shape[1]
    w = G_NSA * HEAD_DIM
    lw = kwc3.shape[1] // G_NSA
    qw = H_NSA * HEAD_DIM
    r4 = HPG * rows
    fixed = lambda shape: pl.BlockSpec(shape, lambda b, c, pt: (0,) * len(shape))
    per_seq = lambda r, width, cb=0, off=0: pl.BlockSpec(
        (1, r, width), functools.partial(lambda b, c, pt, cb, off: (off + b, 0, cb), cb=cb, off=off))
    page = lambda i: pl.BlockSpec((1, PAGE_SIZE * G_NSA, HEAD_DIM),
                                  functools.partial(lambda b, c, pt, i: (page_off + pt[b, c * pps + i], 0, 0), i=i))
    win = per_seq(lw * G_NSA, HEAD_DIM, 0, seq_off)
    in_specs = ([per_seq(rows, qw), per_seq(rows, w, gate_cb // G_NSA), per_seq(nbp, w), per_seq(nbp, w)]
                + [page(i) for i in range(pps)] * 2
                + [per_seq(rows, w), per_seq(rows, w), win, win, per_seq(rows, w), per_seq(rows, w),
                   pl.BlockSpec((nbp, pps * PAGE_SIZE), lambda b, c, pt: (0, c)),
                   fixed((1, HEAD_DIM)), fixed((rows, HEAD_DIM)), fixed((rows, HEAD_DIM)), fixed((rows, HEAD_DIM))])
    return pl.pallas_call(
        functools.partial(_nsa_sample_kernel, pps=pps, past=past, rows=rows, nb=past // NSA_BLOCK, lw=lw),
        grid_spec=pltpu.PrefetchScalarGridSpec(
            num_scalar_prefetch=1, grid=(n, n_pages // pps),
            in_specs=in_specs,
            out_specs=pl.BlockSpec((1, rows, qw), lambda b, c, pt: (b, 0, 0)),
            scratch_shapes=[pltpu.VMEM((G_NSA, r4, HEAD_DIM), BF16), pltpu.VMEM((G_NSA, rows, nbp), F32),
                            pltpu.VMEM((G_NSA, r4, HEAD_DIM), F32), pltpu.VMEM((G_NSA, r4, 1), F32),
                            pltpu.VMEM((G_NSA, r4, 1), F32), pltpu.VMEM((G_NSA, r4, HEAD_DIM), F32)]),
        out_shape=jax.ShapeDtypeStruct((n, rows, qw), BF16),
        compiler_params=_cp("parallel", "arbitrary"),
        name="nsa_sample",
    )(page_table, proj3, proj3, kc3, vc3, *([pool_k] * pps), *([pool_v] * pps), kn3, vn3, kwc3, vwc3, kwn3, vwn3,
      expand, qg.reshape(1, HEAD_DIM), *tables)


def _lru_kernel(x_ref, y_ref, hist_ref, h0_ref, cw_ref, cb_ref, wa_ref, ba_ref, wi_ref, bi_ref, sp_ref,
                o_ref, tail_ref, a_ref, b_ref, *, cps):
    t = x_ref.shape[1]
    bw = D_RNN // RNN_BLOCKS
    row8 = _iota((SUBLANES, bw), 0)
    for k in range(cps):
        cols = slice(k * bw, (k + 1) * bw)
        x = x_ref[0, :, cols]
        hist = hist_ref[0, :, cols]
        xc = cb_ref[:, cols] + x * cw_ref[CONV_W - 1:CONV_W, cols]
        for s in range(1, CONV_W):
            sh = pltpu.roll(x, s, 0)
            head = jnp.where(row8 >= s, sh[:SUBLANES], pltpu.roll(hist, (SUBLANES - (CONV_W - 1) + s) % SUBLANES, 0))
            sh = head if t == SUBLANES else jnp.concatenate([head, sh[SUBLANES:]], axis=0)
            xc = xc + sh * cw_ref[CONV_W - 1 - s:CONV_W - s, cols]
        xb = xc.astype(BF16)
        r = jax.nn.sigmoid(_dot(xb, wa_ref[k]) + ba_ref[:, cols])
        gi = jax.nn.sigmoid(_dot(xb, wi_ref[k]) + bi_ref[:, cols])
        log_a = -LRU_C * r * sp_ref[:, cols]
        th = jnp.tanh(log_a)
        a_ref[:, cols] = jnp.exp(log_a)
        b_ref[:, cols] = jnp.sqrt(-2.0 * th / (1.0 - th)) * (gi * xc)

    wid = cps * bw
    rowsw = _iota((SUBLANES, wid), 0)

    def body(gidx, hc):
        off = pl.multiple_of(gidx * SUBLANES, SUBLANES)
        a = a_ref[pl.ds(off, SUBLANES), :]
        b = b_ref[pl.ds(off, SUBLANES), :]
        for s in (1, 2, 4):
            ok = rowsw >= s
            b = jnp.where(ok, a * pltpu.roll(b, s, 0) + b, b)
            a = jnp.where(ok, a * pltpu.roll(a, s, 0), a)
        hrows = a * hc + b
        a_ref[pl.ds(off, SUBLANES), :] = hrows
        return jnp.broadcast_to(hrows[SUBLANES - 1:SUBLANES, :], (SUBLANES, wid))

    lax.fori_loop(0, t // SUBLANES, body, jnp.broadcast_to(h0_ref[0], (SUBLANES, wid)))
    tail_ref[0] = a_ref[t - SUBLANES:, :]
    chunk = _tile(t, 512)
    for c in range(t // chunk):
        rws = slice(c * chunk, (c + 1) * chunk)
        o_ref[0, rws, :] = (jax.nn.gelu(y_ref[0, rws, :], approximate=True) * a_ref[rws, :]).astype(BF16)


def rglru(proj3, x_cb, y_cb, hist8, h0, conv_w, conv_b, w_a, b_a, w_i, b_i, softplus_neg_lam):
    n, t, _ = proj3.shape
    bw = D_RNN // RNN_BLOCKS
    cps = 4 if t > SUBLANES else RNN_BLOCKS
    wid = cps * bw
    assert (x_cb * HEAD_DIM) % wid == 0 and (y_cb * HEAD_DIM) % wid == 0 and bw == HEAD_DIM
    vec = lambda a: a.reshape(1, D_RNN)
    vspec = pl.BlockSpec((1, wid), lambda b, k: (0, k))
    mspec = pl.BlockSpec((cps, bw, bw), lambda b, k: (k, 0, 0))
    return pl.pallas_call(
        functools.partial(_lru_kernel, cps=cps),
        grid=(n, RNN_BLOCKS // cps),
        in_specs=[pl.BlockSpec((1, t, wid), lambda b, k: (b, 0, x_cb * HEAD_DIM // wid + k)),
                  pl.BlockSpec((1, t, wid), lambda b, k: (b, 0, y_cb * HEAD_DIM // wid + k)),
                  pl.BlockSpec((1, SUBLANES, wid), lambda b, k: (b, 0, k)),
                  pl.BlockSpec((1, 1, wid), lambda b, k: (b, 0, k)),
                  pl.BlockSpec((CONV_W, wid), lambda b, k: (0, k)),
                  vspec, mspec, vspec, mspec, vspec, vspec],
        out_specs=[pl.BlockSpec((1, t, wid), lambda b, k: (b, 0, k)),
                   pl.BlockSpec((1, SUBLANES, wid), lambda b, k: (b, 0, k))],
        out_shape=[jax.ShapeDtypeStruct((n, t, D_RNN), BF16), jax.ShapeDtypeStruct((n, SUBLANES, D_RNN), F32)],
        scratch_shapes=[pltpu.VMEM((t, wid), F32), pltpu.VMEM((t, wid), F32)],
        compiler_params=_cp("parallel", "parallel"),
        name="rglru",
    )(proj3, proj3, hist8, h0.reshape(n, 1, D_RNN), conv_w, vec(conv_b),
      w_a.astype(BF16), vec(b_a), w_i.astype(BF16), vec(b_i), vec(softplus_neg_lam))


def _rope_tables(pos, n_rot, theta):
    half = n_rot // 2
    inv = 1.0 / (theta ** (jnp.arange(half, dtype=F32) * (2.0 / n_rot)))
    ang = pos.astype(F32)[:, None] * inv[None, :]
    cos, sin = jnp.cos(ang), jnp.sin(ang)
    p = pos.shape[0]
    rest = HEAD_DIM - n_rot
    zh = jnp.zeros((p, half), F32)
    c = jnp.concatenate([cos, cos, jnp.ones((p, rest), F32)], axis=1)
    sa = jnp.concatenate([-sin, zh, jnp.zeros((p, rest), F32)], axis=1)
    sb = jnp.concatenate([zh, sin, jnp.zeros((p, rest), F32)], axis=1)
    return c, sa, sb


def _block_expand(nbp, length):
    return (jnp.arange(length, dtype=jnp.int32)[None, :] // NSA_BLOCK
            == jnp.arange(nbp, dtype=jnp.int32)[:, None]).astype(BF16)


def _pad_axis(a, axis, size):
    pad = [(0, 0)] * a.ndim
    pad[axis] = (0, size - a.shape[axis])
    return jnp.pad(a, pad)


def _odd_w_in(w):
    qw = H_NSA * HEAD_DIM
    qkv = qw + 6 * G_NSA * HEAD_DIM
    ng = 3 * H_NSA
    gate = w[:, qkv:qkv + ng]
    per = 3 * HPG
    gcols = [_pad_axis(gate[:, g * per:(g + 1) * per], 1, LANES) for g in range(G_NSA)]
    out = jnp.concatenate([w[:, :qw], w[:, qkv + ng:], w[:, qw:qkv]] + gcols, axis=1)
    return _pad_axis(out, 1, -(-out.shape[1] // 1024) * 1024)


def kernel(x_prompt, x_sample, mem_prompt, state_ret, cache_dil_k, cache_dil_v, cache_nsa_cmp_k, cache_nsa_cmp_v, cache_nsa_slc_k, cache_nsa_slc_v, cache_nsa_win_k, cache_nsa_win_v, state_lru_h, state_lru_conv, cache_mem_k, cache_mem_v, page_table, ev_w_in, ret_gn_g, dil_q_norm_g, dil_k_norm_g, ev_w_out, od_w_in, nsa_q_norm_g, nsa_k_norm_g, nsa_pw_k, nsa_pw_v, lru_conv_w, lru_conv_b, lru_w_a, lru_b_a, lru_w_i, lru_b_i, lru_lambda, od_w_out, norm_mix_g, norm_mem_g, norm_mlp_g, mem_norm_g, mem_w_q, mem_w_k, mem_w_v, mem_q_norm_g, mem_k_norm_g, mem_w_o, mlp_w1, mlp_w2):
    nb_p, t_p, d = x_prompt.shape
    nb_s, t_s, _ = x_sample.shape
    depth = norm_mix_g.shape[0]
    n_pages = page_table.shape[1]
    past = n_pages * PAGE_SIZE
    rs = SAMPLE_ROWS
    assert t_p % RET_CHUNK == 0 and t_s <= rs and past % PAGE_SIZE == 0 and t_s < NSA_BLOCK
    assert cache_nsa_win_k.shape[2] % KEY_BLOCK == 0 and cache_dil_k.shape[2] % KEY_BLOCK == 0
    hd = HEAD_DIM

    xp = x_prompt.reshape(nb_p * t_p, d)
    xs = _pad_axis(x_sample, 1, rs).reshape(nb_s * rs, d)
    mem2 = mem_prompt.reshape(-1, d)
    n_mem = mem_prompt.shape[1]

    pos_p = jnp.arange(t_p, dtype=jnp.int32)
    pos_s = past + jnp.arange(rs, dtype=jnp.int32)
    tile_s = lambda tabs: tuple(jnp.tile(tb, (nb_s, 1)) for tb in tabs)
    rope_p, rope_s = _rope_tables(pos_p, ROPE_DIMS, ROPE_THETA), _rope_tables(pos_s, ROPE_DIMS, ROPE_THETA)
    ret_p, ret_s = _rope_tables(pos_p, hd, RET_THETA), _rope_tables(pos_s, hd, RET_THETA)
    log_g = jnp.log1p(-jnp.exp2(-5.0 - jnp.arange(H_RET, dtype=F32)))

    ev_w_in_b, ev_w_out_b, od_w_out_b = ev_w_in.astype(BF16), ev_w_out.astype(BF16), od_w_out.astype(BF16)
    od_w_in_b = jnp.stack([_odd_w_in(od_w_in[o]) for o in range(od_w_in.shape[0])]).astype(BF16)
    mem_wq_b, mem_wo_b = mem_w_q.astype(BF16), mem_w_o.astype(BF16)
    mem_wkv_b = jnp.concatenate([mem_w_k, mem_w_v], axis=2).astype(BF16)

    outs = {k: [] for k in ("ret_p", "ret_s", "dk_p", "dv_p", "dk_s", "dv_s", "ck_p", "cv_p", "ck_s", "cv_s",
                            "sk_p", "sv_p", "sk_s", "sv_s", "wk_p", "wv_p", "wk_s", "wv_s", "lh_p", "lh_s",
                            "lc_p", "lc_s", "mk_p", "mv_p")}
    new_rows = lambda a, w: a.reshape(nb_s, rs, w)[:, :t_s]

    for layer in range(depth):
        if layer % 2 == 0:
            e = layer // 2
            w_in, w_out, wl = ev_w_in_b, ev_w_out_b, e
            hw = H_RET * hd
            ones = jnp.ones((1, hd), F32)
            gains = jnp.concatenate([dil_k_norm_g[e][None], ones], axis=0)
            sections = [(5, "norm_rope"), (6, "copy")]
            proj = rms_matmul(xp, norm_mix_g[layer], w_in, wl, 1024)
            proj3 = proj.reshape(nb_p, t_p, -1)
            y_ret, s_new = retention(proj3, jnp.zeros((nb_p, H_RET, hd, hd), F32), log_g, ret_gn_g[e], ret_p, RET_CHUNK)
            dk, dv = kv_transform(proj, sections, gains, rope_p, H_DIL, t_p)
            y_dil = dilated_prompt(proj3, 4 * H_RET, dk.reshape(nb_p, t_p, hw), dv.reshape(nb_p, t_p, hw),
                                   dil_q_norm_g[e], rope_p)
            xp = matmul_res(y_ret.reshape(-1, hw), y_dil.reshape(-1, hw), w_out, wl, xp)
            keep = min(DIL_PATTERNS[-1][0], t_p)
            outs["ret_p"].append(s_new)
            outs["dk_p"].append(dk.reshape(nb_p, t_p, H_DIL, hd)[:, t_p - keep:])
            outs["dv_p"].append(dv.reshape(nb_p, t_p, H_DIL, hd)[:, t_p - keep:])
            proj = rms_matmul(xs, norm_mix_g[layer], w_in, wl, 1024)
            proj3 = proj.reshape(nb_s, rs, -1)
            y_ret, s_new = retention(proj3, state_ret.reshape(-1, H_RET, hd, hd), log_g, ret_gn_g[e], ret_s, t_s,
                                     seq_off=e * nb_s)
            dk, dv = kv_transform(proj, sections, gains, tile_s(rope_s), H_DIL, nb_s * rs)
            lc = cache_dil_k.shape[2]
            y_dil = dilated_sample(proj3, 4 * H_RET, cache_dil_k.reshape(-1, lc * H_DIL, hd),
                                   cache_dil_v.reshape(-1, lc * H_DIL, hd), e * nb_s, dk.reshape(nb_s, rs, hw),
                                   dv.reshape(nb_s, rs, hw), dil_q_norm_g[e], rope_s)
            xs = matmul_res(y_ret.reshape(-1, hw), y_dil.reshape(-1, hw), w_out, wl, xs)
            outs["ret_s"].append(s_new)
            outs["dk_s"].append(new_rows(dk, hw).reshape(nb_s, t_s, H_DIL, hd))
            outs["dv_s"].append(new_rows(dv, hw).reshape(nb_s, t_s, H_DIL, hd))
        else:
            o = layer // 2
            w_in, w_out, wl = od_w_in_b, od_w_out_b, o
            qw = H_NSA * hd
            gw = G_NSA * hd
            x_cb = qw // hd
            y_cb = x_cb + D_RNN // hd
            kv_cb = (qw + 2 * D_RNN) // gw
            gate_cb = (qw + 2 * D_RNN + 6 * gw) // hd
            ones = jnp.ones((1, hd), F32)
            kg = nsa_k_norm_g[o]
            gains = jnp.concatenate([kg[0:1], ones, kg[1:2], ones, kg[2:3], ones], axis=0)
            sections = [(kv_cb + i, "norm_rope" if i % 2 == 0 else "copy") for i in range(6)]
            pwk2 = jnp.concatenate([nsa_pw_k[o]] * G_NSA, axis=1)
            pwv2 = jnp.concatenate([nsa_pw_v[o]] * G_NSA, axis=1)
            sp = jax.nn.softplus(-lru_lambda[o].astype(F32))
            lru_w = (lru_conv_w[o], lru_conv_b[o], lru_w_a[o], lru_b_a[o], lru_w_i[o], lru_b_i[o], sp)
            proj = rms_matmul(xp, norm_mix_g[layer], w_in, wl, 1024)
            proj3 = proj.reshape(nb_p, t_p, -1)
            kcm, vcm, ksl, vsl, kwn, vwn = kv_transform(proj, sections, gains, rope_p, G_NSA, t_p)
            r3 = lambda a: a.reshape(nb_p, t_p, gw)
            nb = t_p // NSA_BLOCK
            nbp = -(-nb // LANES) * LANES
            kc, vc = compress_rows(r3(kcm), r3(vcm), pwk2, pwv2)
            o_nsa = nsa_prompt(proj3, gate_cb, _pad_axis(kc, 1, nbp), _pad_axis(vc, 1, nbp), r3(ksl), r3(vsl),
                               r3(kwn), r3(vwn), _block_expand(nbp, t_p), nsa_q_norm_g[o], rope_p, nb)
            y_rnn, tail = rglru(proj3, x_cb, y_cb, jnp.zeros((nb_p, SUBLANES, D_RNN), F32),
                                jnp.zeros((nb_p, D_RNN), F32), *lru_w)
            xp = matmul_res(o_nsa.reshape(-1, qw), y_rnn.reshape(-1, D_RNN), w_out, wl, xp)
            r4 = lambda a: a.reshape(nb_p, t_p, G_NSA, hd)
            keep = min(NSA_WINDOW, t_p)
            for key, val in (("ck_p", r4(kcm)), ("cv_p", r4(vcm)), ("sk_p", r4(ksl)), ("sv_p", r4(vsl)),
                             ("wk_p", r4(kwn)[:, t_p - keep:]), ("wv_p", r4(vwn)[:, t_p - keep:])):
                outs[key].append(val)
            outs["lh_p"].append(tail[:, SUBLANES - 1])
            outs["lc_p"].append(proj3[:, t_p - (CONV_W - 1):, x_cb * hd:x_cb * hd + D_RNN])
            proj = rms_matmul(xs, norm_mix_g[layer], w_in, wl, 1024)
            proj3 = proj.reshape(nb_s, rs, -1)
            kcm, vcm, ksl, vsl, kwn, vwn = kv_transform(proj, sections, gains, tile_s(rope_s), G_NSA, nb_s * rs)
            r3 = lambda a: a.reshape(nb_s, rs, gw)
            nph = cache_nsa_cmp_k.shape[1]
            pool = lambda a: a.reshape(-1, PAGE_SIZE * G_NSA, hd)
            pps = math.gcd(n_pages, 32)
            nb = (past + t_s) // NSA_BLOCK
            nbp = -(-nb // LANES) * LANES
            kc, vc = compress_pages(pool(cache_nsa_cmp_k), pool(cache_nsa_cmp_v), o * nph, page_table,
                                    nsa_pw_k[o], nsa_pw_v[o], pps)
            lw = cache_nsa_win_k.shape[2]
            o_nsa = nsa_sample(proj3, gate_cb, _pad_axis(kc, 1, nbp), _pad_axis(vc, 1, nbp),
                               pool(cache_nsa_slc_k), pool(cache_nsa_slc_v), o * nph, page_table, r3(ksl), r3(vsl),
                               cache_nsa_win_k.reshape(-1, lw * G_NSA, hd), cache_nsa_win_v.reshape(-1, lw * G_NSA, hd),
                               o * nb_s, r3(kwn), r3(vwn), _block_expand(nbp, past), nsa_q_norm_g[o], rope_s, pps)
            hist8 = _pad_axis(state_lru_conv[o].astype(F32), 1, SUBLANES)
            y_rnn, tail = rglru(proj3, x_cb, y_cb, hist8, state_lru_h[o].astype(F32), *lru_w)
            xs = matmul_res(o_nsa.reshape(-1, qw), y_rnn.reshape(-1, D_RNN), w_out, wl, xs)
            for key, val in (("ck_s", kcm), ("cv_s", vcm), ("sk_s", ksl), ("sv_s", vsl), ("wk_s", kwn), ("wv_s", vwn)):
                outs[key].append(new_rows(val, gw).reshape(nb_s, t_s, G_NSA, hd))
            outs["lh_s"].append(tail[:, t_s - 1])
            lx = proj3[:, :t_s, x_cb * hd:x_cb * hd + D_RNN]
            outs["lc_s"].append(jnp.concatenate([state_lru_conv[o].astype(F32), lx], axis=1)[:, -(CONV_W - 1):])

        dm = H_MEM * hd
        mkv = rms_matmul(mem2, mem_norm_g[layer], mem_wkv_b, layer, dm)
        gains = jnp.concatenate([mem_k_norm_g[layer][None], jnp.ones((1, hd), F32)], axis=0)
        mk, mv = kv_transform(mkv, [(0, "norm"), (1, "copy")], gains, rope_p, H_MEM, n_mem)
        mk3, mv3 = mk.reshape(nb_p, n_mem, dm), mv.reshape(nb_p, n_mem, dm)
        outs["mk_p"].append(mk3.reshape(nb_p, n_mem, H_MEM, hd))
        outs["mv_p"].append(mv3.reshape(nb_p, n_mem, H_MEM, hd))
        xp = mem_attn_block(xp, norm_mem_g[layer], mem_wq_b, mem_q_norm_g[layer], mk3, mv3, mem_wo_b, layer, t_p)
        xs = mem_attn_block(xs, norm_mem_g[layer], mem_wq_b, mem_q_norm_g[layer],
                            cache_mem_k.reshape(-1, n_mem * H_MEM, hd), cache_mem_v.reshape(-1, n_mem * H_MEM, hd),
                            mem_wo_b, layer, rs, seq_off=layer * nb_s)
        xp = mlp_block(xp, norm_mlp_g[layer], mlp_w1, mlp_w2, layer)
        xs = mlp_block(xs, norm_mlp_g[layer], mlp_w1, mlp_w2, layer)

    st = lambda k: jnp.stack(outs[k])
    return (xp.reshape(nb_p, t_p, d), xs.reshape(nb_s, rs, d)[:, :t_s], st("ret_p"), st("ret_s"),
            st("dk_p"), st("dv_p"), st("dk_s"), st("dv_s"),
            st("ck_p"), st("cv_p"), st("ck_s"), st("cv_s"),
            st("sk_p"), st("sv_p"), st("sk_s"), st("sv_s"),
            st("wk_p"), st("wv_p"), st("wk_s"), st("wv_s"),
            st("lh_p"), st("lh_s"), st("lc_p"), st("lc_s"),
            st("mk_p"), st("mv_p"))
```

```python
import functools
import math

import jax
import jax.numpy as jnp
from jax import lax
from jax.experimental import pallas as pl
from jax.experimental.pallas import tpu as pltpu

F32 = jnp.float32
BF16 = jnp.bfloat16

HEAD_DIM = 128
ROPE_DIMS = HEAD_DIM // 4
ROPE_THETA = 500000.0
NORM_EPS = 1e-6
H_RET = 8
RET_CHUNK = 128
RET_THETA = 10000.0
H_DIL = 8
DIL_PATTERNS = ((128, 1), (512, 4), (2048, 16))
H_NSA = 8
G_NSA = 2
HPG = H_NSA // G_NSA
NSA_BLOCK = 64
NSA_TOPK = 16
NSA_WINDOW = 512
D_RNN = 1024
RNN_BLOCKS = 8
CONV_W = 4
LRU_C = 8.0
H_MEM = 4
PAGE_SIZE = 128

LANES = 128
SUBLANES = 8
KEY_BLOCK = 128
SAMPLE_ROWS = 8
VMEM_LIMIT = 48 * 1024 * 1024
MLP_VMEM_LIMIT = 56 * 1024 * 1024
NEG = -1e30
SCALE = HEAD_DIM ** -0.5


def _cp(*sem):
    return pltpu.CompilerParams(dimension_semantics=sem, vmem_limit_bytes=VMEM_LIMIT)


def _tile(n, pref):
    t = min(n, pref)
    while n % t:
        t -= SUBLANES
    return t


def _rms(x, g):
    return x * lax.rsqrt(jnp.mean(x * x, axis=-1, keepdims=True) + NORM_EPS) * g


def _rope(x, c, sa, sb, half):
    return x * c + pltpu.roll(x, LANES - half, 1) * sa + pltpu.roll(x, half, 1) * sb


def _dot(a, b):
    return jnp.dot(a, b, preferred_element_type=F32)


def _dot_nt(a, b):
    return lax.dot_general(a, b, (((1,), (1,)), ((), ())), preferred_element_type=F32)


def _pad_rows(x, rows):
    if x.shape[0] == rows:
        return x
    return jnp.concatenate([x, jnp.zeros((rows - x.shape[0], x.shape[1]), x.dtype)], axis=0)


def _softmax_step(s, w, vb, m, l, acc):
    s = jnp.where(w > 0, s, NEG)
    m_new = jnp.maximum(m, jnp.max(s, axis=-1, keepdims=True))
    alpha = jnp.exp(m - m_new)
    p = jnp.exp(s - m_new) * w
    l = alpha * l + jnp.sum(p, axis=-1, keepdims=True)
    acc = alpha * acc + _dot(p.astype(BF16), vb)
    return m_new, l, acc


def _softmax_init(rows):
    return (jnp.full((rows, 1), NEG, F32), jnp.zeros((rows, 1), F32), jnp.zeros((rows, HEAD_DIM), F32))


def _softmax_done(l, acc):
    return acc / jnp.maximum(l, 1e-30)


def _dil_weight(d):
    w = jnp.zeros(d.shape, F32)
    for window, dil in DIL_PATTERNS:
        hit = jnp.where(d <= window, 1.0, 0.0)
        if dil > 1:
            rem = (d & (dil - 1)) if dil & (dil - 1) == 0 else lax.rem(d, dil)
            hit = jnp.where(rem == 0, hit, 0.0)
        w = w + hit
    return jnp.where(d >= 0, w, 0.0)


def _iota(shape, dim):
    return lax.broadcasted_iota(jnp.int32, shape, dim)


def _rms_matmul_kernel(x_ref, g_ref, w_ref, o_ref, xn_ref):
    @pl.when(pl.program_id(1) == 0)
    def _():
        xn_ref[...] = _rms(x_ref[...], g_ref[...]).astype(BF16)

    o_ref[...] = _dot(xn_ref[...], w_ref[...])


def rms_matmul(x, g, w, layer, tn):
    m, d = x.shape
    n = w.shape[2]
    tm = _tile(m, 1024)
    return pl.pallas_call(
        _rms_matmul_kernel,
        grid=(m // tm, n // tn),
        in_specs=[pl.BlockSpec((tm, d), lambda i, j: (i, 0)),
                  pl.BlockSpec((1, d), lambda i, j: (0, 0)),
                  pl.BlockSpec((None, d, tn), lambda i, j: (layer, 0, j))],
        out_specs=pl.BlockSpec((tm, tn), lambda i, j: (i, j)),
        out_shape=jax.ShapeDtypeStruct((m, n), F32),
        scratch_shapes=[pltpu.VMEM((tm, d), BF16)],
        compiler_params=_cp("parallel", "arbitrary"),
        name="rms_matmul",
    )(x, g.reshape(1, d), w)


def _matmul_res_kernel(a1_ref, a2_ref, w1_ref, w2_ref, r_ref, o_ref):
    o_ref[...] = r_ref[...] + _dot(a1_ref[...], w1_ref[...]) + _dot(a2_ref[...], w2_ref[...])


def matmul_res(a1, a2, w, layer, res):
    m, k1 = a1.shape
    k2 = a2.shape[1]
    n = w.shape[2]
    assert k1 == k2 and w.shape[1] == k1 + k2
    tm = _tile(m, 1024)
    tn = _tile(n, 1024)
    return pl.pallas_call(
        _matmul_res_kernel,
        grid=(m // tm, n // tn),
        in_specs=[pl.BlockSpec((tm, k1), lambda i, j: (i, 0)),
                  pl.BlockSpec((tm, k2), lambda i, j: (i, 0)),
                  pl.BlockSpec((None, k1, tn), lambda i, j: (layer, 0, j)),
                  pl.BlockSpec((None, k2, tn), lambda i, j: (layer, 1, j)),
                  pl.BlockSpec((tm, tn), lambda i, j: (i, j))],
        out_specs=pl.BlockSpec((tm, tn), lambda i, j: (i, j)),
        out_shape=jax.ShapeDtypeStruct((m, n), F32),
        compiler_params=_cp("parallel", "parallel"),
        name="matmul_res",
    )(a1, a2, w, w, res)


def _mlp_kernel(x_ref, g_ref, w1_ref, w2_ref, o_ref, xn_ref):
    @pl.when(pl.program_id(1) == 0)
    def _():
        x = x_ref[...]
        xn_ref[...] = _rms(x, g_ref[...]).astype(BF16)
        o_ref[...] = x

    hid = jnp.maximum(_dot(xn_ref[...], w1_ref[...].astype(BF16)), 0.0)
    o_ref[...] += _dot((hid * hid).astype(BF16), w2_ref[...].astype(BF16))


def mlp_block(x, g, w1, w2, layer):
    m, d = x.shape
    ff = w1.shape[2]
    tm = _tile(m, 1024)
    tf = _tile(ff, 512)
    once = pl.Buffered(1)
    return pl.pallas_call(
        _mlp_kernel,
        grid=(m // tm, ff // tf),
        in_specs=[pl.BlockSpec((tm, d), lambda i, f: (i, 0)),
                  pl.BlockSpec((1, d), lambda i, f: (0, 0)),
                  pl.BlockSpec((None, d, tf), lambda i, f: (layer, 0, f)),
                  pl.BlockSpec((None, tf, d), lambda i, f: (layer, f, 0))],
        out_specs=pl.BlockSpec((tm, d), lambda i, f: (i, 0), pipeline_mode=once),
        out_shape=jax.ShapeDtypeStruct((m, d), F32),
        scratch_shapes=[pltpu.VMEM((tm, d), BF16)],
        compiler_params=pltpu.CompilerParams(dimension_semantics=("parallel", "arbitrary"),
                                             vmem_limit_bytes=MLP_VMEM_LIMIT),
        name="mlp_block",
    )(x, g.reshape(1, d), w1, w2)


def _mem_attn_kernel(x_ref, g_ref, wq_ref, qg_ref, k_ref, v_ref, wo_ref, o_ref, oh_ref, *, nb, seg, by_row):
    x = x_ref[...]
    q = _dot(_rms(x, g_ref[...]).astype(BF16), wq_ref[...])
    qg = qg_ref[...]
    n_mem = k_ref.shape[1] // H_MEM if by_row else k_ref.shape[1]
    qh = [(_rms(q[:, h * HEAD_DIM:(h + 1) * HEAD_DIM], qg) * SCALE).astype(BF16) for h in range(H_MEM)]
    pairs = [(b, h) for b in range(nb) for h in range(H_MEM)]

    def mem_head(ref, b, h):
        if by_row:
            return ref[b, pl.ds(h, n_mem, stride=H_MEM), :].astype(BF16)
        return ref[b, :, h * HEAD_DIM:(h + 1) * HEAD_DIM].astype(BF16)

    s = jnp.concatenate([_dot_nt(qh[h][b * seg:(b + 1) * seg], mem_head(k_ref, b, h)) for b, h in pairs], axis=0)
    p = jnp.exp(s - jnp.max(s, axis=-1, keepdims=True))
    pb = (p / jnp.sum(p, axis=-1, keepdims=True)).astype(BF16)
    for j, (b, h) in enumerate(pairs):
        o = _dot(pb[j * seg:(j + 1) * seg], mem_head(v_ref, b, h))
        oh_ref[b * seg:(b + 1) * seg, h * HEAD_DIM:(h + 1) * HEAD_DIM] = o.astype(BF16)
    o_ref[...] = x + _dot(oh_ref[...], wo_ref[...])


def mem_attn_block(x, g, wq, qg, k, v, wo, layer, rows_per_seq, seq_off=0):
    m, d = x.shape
    dm = H_MEM * HEAD_DIM
    by_row = k.shape[2] == HEAD_DIM
    kv_rows, kv_w = k.shape[1], k.shape[2]
    if rows_per_seq >= 128:
        tm = _tile(rows_per_seq, 512)
        nb, seg = 1, tm
        kv_map = lambda i: (seq_off + i // (rows_per_seq // tm), 0, 0)
    else:
        nb = _tile(m // rows_per_seq, 8)
        tm, seg = nb * rows_per_seq, rows_per_seq
        assert seq_off % nb == 0
        kv_map = lambda i: (seq_off // nb + i, 0, 0)
    return pl.pallas_call(
        functools.partial(_mem_attn_kernel, nb=nb, seg=seg, by_row=by_row),
        grid=(m // tm,),
        in_specs=[pl.BlockSpec((tm, d), lambda i: (i, 0)),
                  pl.BlockSpec((1, d), lambda i: (0, 0)),
                  pl.BlockSpec((None, d, dm), lambda i: (layer, 0, 0)),
                  pl.BlockSpec((1, HEAD_DIM), lambda i: (0, 0)),
                  pl.BlockSpec((nb, kv_rows, kv_w), kv_map),
                  pl.BlockSpec((nb, kv_rows, kv_w), kv_map),
                  pl.BlockSpec((None, dm, d), lambda i: (layer, 0, 0))],
        out_specs=pl.BlockSpec((tm, d), lambda i: (i, 0)),
        out_shape=jax.ShapeDtypeStruct((m, d), F32),
        scratch_shapes=[pltpu.VMEM((tm, dm), BF16)],
        compiler_params=_cp("parallel"),
        name="mem_attn_block",
    )(x, g.reshape(1, d), wq, qg.reshape(1, HEAD_DIM), k, v, wo)


def _kv_transform_kernel(*refs, modes, heads):
    ns = len(modes)
    x_refs = refs[:ns]
    g_ref, c_ref, sa_ref, sb_ref = refs[ns:ns + 4]
    o_refs = refs[ns + 4:]
    for s in range(ns):
        for h in range(heads):
            cols = slice(h * HEAD_DIM, (h + 1) * HEAD_DIM)
            x = x_refs[s][:, cols]
            if modes[s] != "copy":
                x = _rms(x, g_ref[s:s + 1, :])
            if modes[s] == "norm_rope":
                x = _rope(x, c_ref[...], sa_ref[...], sb_ref[...], ROPE_DIMS // 2)
            o_refs[s][:, cols] = x


def kv_transform(proj, sections, gains, tables, heads, pos_period):
    m = proj.shape[0]
    width = heads * HEAD_DIM
    tr = _tile(pos_period, 512)
    nper = pos_period // tr
    modes = tuple(mode for _, mode in sections)
    in_specs = [pl.BlockSpec((tr, width), functools.partial(lambda i, cb: (i, cb), cb=cb)) for cb, _ in sections]
    in_specs.append(pl.BlockSpec(gains.shape, lambda i: (0, 0)))
    in_specs += [pl.BlockSpec((tr, HEAD_DIM), lambda i: (i % nper, 0))] * 3
    outs = pl.pallas_call(
        functools.partial(_kv_transform_kernel, modes=modes, heads=heads),
        grid=(m // tr,),
        in_specs=in_specs,
        out_specs=[pl.BlockSpec((tr, width), lambda i: (i, 0))] * len(sections),
        out_shape=[jax.ShapeDtypeStruct((m, width), F32)] * len(sections),
        compiler_params=_cp("parallel"),
        name="kv_transform",
    )(*([proj] * len(sections)), gains, *tables)
    return outs


def _retention_kernel(q_ref, k_ref, v_ref, rg_ref, s0_ref, lg_ref, gn_ref, c_ref, sa_ref, sb_ref,
                      y_ref, s_ref, st_ref, dec_ref, qd_ref, kd_ref, *, rows, c_true, hps):
    ci = pl.program_id(2)
    cc = RET_CHUNK

    @pl.when(ci == 0)
    def _():
        st_ref[...] = s0_ref[0]
        ii = _iota((cc, 1), 0).astype(F32)
        diff = ii - _iota((1, cc), 1).astype(F32)
        for h in range(hps):
            lg = lg_ref[h]
            dec_ref[h] = jnp.where(diff >= 0, jnp.exp(jnp.maximum(diff, 0.0) * lg), 0.0)
            qd_ref[h] = jnp.exp((ii + 1.0) * lg)
            kd_ref[h] = jnp.where(ii < c_true, jnp.exp((c_true - 1.0 - ii) * lg), 0.0)

    c, sa, sb = c_ref[...], sa_ref[...], sb_ref[...]
    for h in range(hps):
        cols = slice(h * HEAD_DIM, (h + 1) * HEAD_DIM)
        q = _pad_rows(_rope(q_ref[0, :, cols], c, sa, sb, HEAD_DIM // 2) * SCALE, cc)
        k = _pad_rows(_rope(k_ref[0, :, cols], c, sa, sb, HEAD_DIM // 2), cc)
        vb = _pad_rows(v_ref[0, :, cols], cc).astype(BF16)
        qb = q.astype(BF16)
        inner = _dot_nt(qb, k.astype(BF16)) * dec_ref[h]
        st = st_ref[h]
        o = _dot(inner.astype(BF16), vb) + _dot(qb, st.astype(BF16)) * qd_ref[h]
        kd = k * kd_ref[h]
        st_ref[h] = jnp.exp(c_true * lg_ref[h]) * st + _dot(kd.T.astype(BF16), vb)
        o = o[:rows]
        mu = jnp.mean(o, axis=-1, keepdims=True)
        var = jnp.mean(jnp.square(o - mu), axis=-1, keepdims=True)
        y = (o - mu) * lax.rsqrt(var + NORM_EPS) * gn_ref[h]
        rg = rg_ref[0, :, cols]
        y_ref[0, :, cols] = (y * (rg * jax.nn.sigmoid(rg))).astype(BF16)

    @pl.when(ci == pl.num_programs(2) - 1)
    def _():
        s_ref[0] = st_ref[...]


def retention(proj3, s0, log_g, gn_g, tables, c_true, seq_off=0):
    n, t, _ = proj3.shape
    rows = min(t, RET_CHUNK)
    nc = t // rows
    h = H_RET
    hps = H_RET
    ng = h // hps
    wid = hps * HEAD_DIM
    blk = lambda sec: pl.BlockSpec((1, rows, wid), functools.partial(lambda b, hg, c, sec: (b, c, sec * ng + hg), sec=sec))
    tab = pl.BlockSpec((rows, HEAD_DIM), lambda b, hg, c: (c, 0))
    y, s_new = pl.pallas_call(
        functools.partial(_retention_kernel, rows=rows, c_true=float(c_true), hps=hps),
        grid=(n, ng, nc),
        in_specs=[blk(0), blk(1), blk(2), blk(3),
                  pl.BlockSpec((1, hps, HEAD_DIM, HEAD_DIM), lambda b, hg, c: (seq_off + b, hg, 0, 0)),
                  pl.BlockSpec((hps, 1, HEAD_DIM), lambda b, hg, c: (hg, 0, 0)),
                  pl.BlockSpec((hps, 1, HEAD_DIM), lambda b, hg, c: (hg, 0, 0)),
                  tab, tab, tab],
        out_specs=[pl.BlockSpec((1, rows, wid), lambda b, hg, c: (b, c, hg)),
                   pl.BlockSpec((1, hps, HEAD_DIM, HEAD_DIM), lambda b, hg, c: (b, hg, 0, 0))],
        out_shape=[jax.ShapeDtypeStruct((n, t, h * HEAD_DIM), BF16),
                   jax.ShapeDtypeStruct((n, h, HEAD_DIM, HEAD_DIM), F32)],
        scratch_shapes=[pltpu.VMEM((hps, HEAD_DIM, HEAD_DIM), F32)] * 4,
        compiler_params=_cp("parallel", "parallel", "arbitrary"),
        name="retention",
    )(proj3, proj3, proj3, proj3, s0,
      jnp.broadcast_to(log_g[:, None, None], (h, 1, HEAD_DIM)), gn_g.reshape(h, 1, HEAD_DIM), *tables)
    return y, s_new


def _rows(start, size, stride):
    return pl.ds(start, size) if stride == 1 else pl.ds(start, size, stride=stride)


def _dil_prompt_kernel(q_ref, k_ref, v_ref, qg_ref, c_ref, sa_ref, sb_ref, o_ref,
                       qs_ref, m_ref, l_ref, acc_ref, pm_ref, pls_ref, pacc_ref, qc_ref, kc_ref, vc_ref, *, t):
    chunk = _tile(t, 512)
    for c in range(t // chunk):
        r = slice(c * chunk, (c + 1) * chunk)
        q = _rope(_rms(q_ref[0, r, :], qg_ref[...]), c_ref[r, :], sa_ref[r, :], sb_ref[r, :], ROPE_DIMS // 2)
        qs_ref[r, :] = q * SCALE

    def bias(bq, nk, first_key_dist):
        rel = first_key_dist + _iota((bq, nk), 0) - _iota((bq, nk), 1)
        return jnp.where(rel >= 0, jnp.where(rel <= KEY_BLOCK, 0.0, NEG), NEG)

    for pi, (window, dil) in enumerate(DIL_PATTERNS):
        length = t // dil
        bq = min(length, KEY_BLOCK)
        nk = bq if length == bq else 2 * bq
        for r in range(dil):
            rows, crow = _rows(r, length, dil), slice(r * length, (r + 1) * length)
            qc_ref[crow, :] = qs_ref[rows, :].astype(BF16)
            kc_ref[crow, :] = k_ref[0, rows, :].astype(BF16)
            vc_ref[crow, :] = v_ref[0, rows, :].astype(BF16)
        blocks = [(r * length + i * bq, r * length + max(i - 1, 0) * bq, i == 0 and nk > bq)
                  for r in range(dil) for i in range(length // bq)]
        band, lead = bias(bq, nk, nk - bq), bias(bq, nk, 0)
        s = jnp.concatenate([_dot_nt(qc_ref[q0:q0 + bq, :], kc_ref[k0:k0 + nk, :]) + (lead if first else band)
                             for q0, k0, first in blocks], axis=0)
        m = jnp.max(s, axis=-1, keepdims=True)
        pb = jnp.exp(s - m).astype(BF16)
        m = jnp.broadcast_to(m, (t, HEAD_DIM))
        ones = jnp.ones((nk, HEAD_DIM), BF16)
        l = jnp.concatenate([_dot(pb[j * bq:(j + 1) * bq], ones) for j in range(len(blocks))], axis=0)
        acc = jnp.concatenate([_dot(pb[j * bq:(j + 1) * bq], vc_ref[k0:k0 + nk, :])
                               for j, (q0, k0, first) in enumerate(blocks)], axis=0)
        if pi == 0:
            assert dil == 1
            m_ref[...], l_ref[...], acc_ref[...] = m, l, acc
        else:
            pm_ref[...], pls_ref[...], pacc_ref[...] = m, l, acc
            for r in range(dil):
                rows, crow = _rows(r, length, dil), slice(r * length, (r + 1) * length)
                m0, m1 = m_ref[rows, :], pm_ref[crow, :]
                mx = jnp.maximum(m0, m1)
                e0, e1 = jnp.exp(m0 - mx), jnp.exp(m1 - mx)
                m_ref[rows, :] = mx
                l_ref[rows, :] = e0 * l_ref[rows, :] + e1 * pls_ref[crow, :]
                acc_ref[rows, :] = e0 * acc_ref[rows, :] + e1 * pacc_ref[crow, :]

    for c in range(t // chunk):
        r = slice(c * chunk, (c + 1) * chunk)
        o_ref[0, r, :] = (acc_ref[r, :] / l_ref[r, :]).astype(BF16)


def dilated_prompt(proj3, q_cb, k3, v3, qg, tables):
    n, t, _ = proj3.shape
    for window, dil in DIL_PATTERNS:
        length = t // dil
        assert window // dil == KEY_BLOCK and t % dil == 0
        assert length % KEY_BLOCK == 0 or (length < KEY_BLOCK and length % SUBLANES == 0)
    tab = pl.BlockSpec((t, HEAD_DIM), lambda b, h: (0, 0))
    seq = lambda cb: pl.BlockSpec((1, t, HEAD_DIM), functools.partial(lambda b, h, cb: (b, 0, cb + h), cb=cb))
    return pl.pallas_call(
        functools.partial(_dil_prompt_kernel, t=t),
        grid=(n, H_DIL),
        in_specs=[seq(q_cb), seq(0), seq(0), pl.BlockSpec((1, HEAD_DIM), lambda b, h: (0, 0)), tab, tab, tab],
        out_specs=seq(0),
        out_shape=jax.ShapeDtypeStruct((n, t, H_DIL * HEAD_DIM), BF16),
        scratch_shapes=[pltpu.VMEM((t, HEAD_DIM), F32)] * 7 + [pltpu.VMEM((t, HEAD_DIM), BF16)] * 3,
        compiler_params=_cp("parallel", "parallel"),
        name="dilated_prompt",
    )(proj3, k3, v3, qg.reshape(1, HEAD_DIM), *tables)


def _dil_sample_kernel(q_ref, kc_ref, vc_ref, kn_ref, vn_ref, qg_ref, c_ref, sa_ref, sb_ref, o_ref,
                       qb_ref, m_ref, l_ref, acc_ref, *, lc, ck):
    ci = pl.program_id(1)
    rows = q_ref.shape[1]
    head = lambda a, h: a[h * rows:(h + 1) * rows]

    @pl.when(ci == 0)
    def _():
        qs = []
        for h in range(H_DIL):
            cols = slice(h * HEAD_DIM, (h + 1) * HEAD_DIM)
            q = _rope(_rms(q_ref[0, :, cols], qg_ref[...]), c_ref[...], sa_ref[...], sb_ref[...], ROPE_DIMS // 2)
            qs.append(q * SCALE)
        qb_ref[...] = jnp.concatenate(qs, axis=0)
        m_ref[...], l_ref[...], acc_ref[...] = _softmax_init(H_DIL * rows)

    def log_weight(d):
        w = _dil_weight(d)
        return jnp.where(w > 0, jnp.log(jnp.maximum(w, 1.0)), NEG)

    def step(scores, values):
        s = jnp.concatenate(scores, axis=0)
        m_new = jnp.maximum(m_ref[...], jnp.max(s, axis=-1, keepdims=True))
        alpha = jnp.exp(m_ref[...] - m_new)
        p = jnp.exp(s - m_new)
        pb = p.astype(BF16)
        l = alpha * l_ref[...] + jnp.sum(p, axis=-1, keepdims=True)
        acc = alpha * acc_ref[...] + jnp.concatenate([_dot(head(pb, h), values(h)) for h in range(H_DIL)], axis=0)
        m_ref[...], l_ref[...], acc_ref[...] = m_new, l, acc
        return l, acc

    qb = qb_ref[...].astype(BF16)
    bias_c = log_weight(lc + _iota((rows, ck), 0) - (ci * ck + _iota((rows, ck), 1)))
    keys = lambda h: pl.ds(h, ck, stride=H_DIL)
    step([_dot_nt(head(qb, h), kc_ref[0, keys(h), :].astype(BF16)) + bias_c for h in range(H_DIL)],
         lambda h: vc_ref[0, keys(h), :].astype(BF16))

    @pl.when(ci == pl.num_programs(1) - 1)
    def _():
        bias_n = log_weight(_iota((rows, KEY_BLOCK), 0) - _iota((rows, KEY_BLOCK), 1))
        new = lambda ref, h: _pad_rows(ref[0, :, h * HEAD_DIM:(h + 1) * HEAD_DIM], KEY_BLOCK).astype(BF16)
        l, acc = step([_dot_nt(head(qb, h), new(kn_ref, h)) + bias_n for h in range(H_DIL)], lambda h: new(vn_ref, h))
        o = _softmax_done(l, acc)
        for h in range(H_DIL):
            o_ref[0, :, h * HEAD_DIM:(h + 1) * HEAD_DIM] = head(o, h).astype(BF16)


def dilated_sample(proj3, q_cb, kc3, vc3, seq_off, kn3, vn3, qg, tables):
    n, rows, _ = proj3.shape
    lc = kc3.shape[1] // H_DIL
    wid = H_DIL * HEAD_DIM
    ck = lc // 2 if lc % (2 * KEY_BLOCK) == 0 else lc
    assert (q_cb * HEAD_DIM) % wid == 0
    tab = pl.BlockSpec((rows, HEAD_DIM), lambda b, c: (0, 0))
    cache = pl.BlockSpec((1, ck * H_DIL, HEAD_DIM), lambda b, c: (seq_off + b, c, 0))
    new = pl.BlockSpec((1, rows, wid), lambda b, c: (b, 0, 0))
    return pl.pallas_call(
        functools.partial(_dil_sample_kernel, lc=lc, ck=ck),
        grid=(n, lc // ck),
        in_specs=[pl.BlockSpec((1, rows, wid), lambda b, c: (b, 0, q_cb * HEAD_DIM // wid)), cache, cache, new, new,
                  pl.BlockSpec((1, HEAD_DIM), lambda b, c: (0, 0)), tab, tab, tab],
        out_specs=new,
        out_shape=jax.ShapeDtypeStruct((n, rows, wid), BF16),
        scratch_shapes=[pltpu.VMEM((H_DIL * rows, HEAD_DIM), F32), pltpu.VMEM((H_DIL * rows, 1), F32),
                        pltpu.VMEM((H_DIL * rows, 1), F32), pltpu.VMEM((H_DIL * rows, HEAD_DIM), F32)],
        compiler_params=_cp("parallel", "arbitrary"),
        name="dilated_sample",
    )(proj3, kc3, vc3, kn3, vn3, qg.reshape(1, HEAD_DIM), *tables)


def _nsa_q(q_ref, g, qg_ref, c_ref, sa_ref, sb_ref):
    x = jnp.concatenate([q_ref[0][:, (g * HPG + hl) * HEAD_DIM:(g * HPG + hl + 1) * HEAD_DIM] for hl in range(HPG)],
                        axis=0)
    per_head = lambda t: jnp.concatenate([t[...]] * HPG, axis=0)
    x = _rope(_rms(x, qg_ref[...]), per_head(c_ref), per_head(sa_ref), per_head(sb_ref), ROPE_DIMS // 2)
    return (x * SCALE).astype(BF16)


def _nsa_compressed(q4, kc, vc, qpos, nb):
    rows = qpos.shape[0]
    nbp = kc.shape[0]
    blk = _iota((rows, nbp), 1)
    cmask = jnp.where((blk + 1) * NSA_BLOCK - 1 <= qpos, 1.0, 0.0)
    cmask4 = jnp.concatenate([cmask] * HPG, axis=0)
    s = jnp.where(cmask4 > 0, _dot_nt(q4, kc.astype(BF16)), NEG)
    p = jnp.exp(s - jnp.max(s, axis=-1, keepdims=True)) * cmask4
    p = p / jnp.maximum(jnp.sum(p, axis=-1, keepdims=True), 1e-30)
    o_c = _dot(p.astype(BF16), vc.astype(BF16))
    imp = p[0:rows]
    for hl in range(1, HPG):
        imp = imp + p[hl * rows:(hl + 1) * rows]
    shift = NSA_BLOCK.bit_length() - 1
    if rows == nbp:
        nbr = -(-nb // SUBLANES) * SUBLANES
        blk_t = _iota((nbr, rows), 0)
        cur_t = jnp.right_shift(qpos[0:1, :] + _iota((1, rows), 1), shift)
        cand_t = blk_t < cur_t
        imp_t = jnp.where(cand_t, imp.T[:nbr], -jnp.inf)
        rank = jnp.zeros((nbr, rows), F32)
        for b in range(nb):
            row = imp_t[b:b + 1, :]
            tie = jnp.where(row == imp_t, jnp.where(blk_t > b, 1.0, 0.0), 0.0)
            rank = rank + jnp.where(row > imp_t, 1.0, tie)
        sel_t = jnp.where(cand_t, jnp.where(rank < min(NSA_TOPK, nb), 1.0, 0.0), 0.0)
        sel = _pad_rows(sel_t, nbp).T
    else:
        cand = blk < jnp.right_shift(qpos, shift)
        imp = jnp.where(cand, imp, -jnp.inf)
        rank = jnp.zeros((rows, nbp), F32)
        for b in range(nb):
            col = imp[:, b:b + 1]
            tie = jnp.where(col == imp, jnp.where(blk > b, 1.0, 0.0), 0.0)
            rank = rank + jnp.where(col > imp, 1.0, tie)
        sel = jnp.where(cand, jnp.where(rank < min(NSA_TOPK, nb), 1.0, 0.0), 0.0)
    sel = jnp.where(blk == jnp.right_shift(qpos, shift), 1.0, sel)
    return o_c, sel


def _nsa_combine(gates, g, o_c, o_s, o_w, rows):
    outs = []
    for hl in range(HPG):
        r = slice(hl * rows, (hl + 1) * rows)
        col = 3 * hl
        outs.append(gates[:, col:col + 1] * o_c[r] + gates[:, col + 1:col + 2] * o_s[r]
                    + gates[:, col + 2:col + 3] * o_w[r])
    return outs


def _nsa_prompt_kernel(q_ref, gt_ref, kc_ref, vc_ref, ks_ref, vs_ref, kw_ref, vw_ref, e_ref,
                       qg_ref, c_ref, sa_ref, sb_ref, o_ref, selx_ref, s_ref, mx_ref, ls_ref, acc_ref, *, tq, nb, t):
    qi = pl.program_id(2)
    q4 = _nsa_q(q_ref, 0, qg_ref, c_ref, sa_ref, sb_ref)
    qpos = qi * tq + _iota((tq, 1), 0)
    o_c, sel = _nsa_compressed(q4, kc_ref[0], vc_ref[0], qpos, nb)
    selx_ref[...] = _dot(sel.astype(BF16), e_ref[...])

    ck = _tile(t, 4 * KEY_BLOCK)
    nchunks = lax.div(qi * tq + (tq + ck - 1), ck)
    lane = _iota((tq, ck), 1)
    mx_ref[...] = jnp.full(mx_ref.shape, NEG, F32)

    def scores(c, _):
        off = pl.multiple_of(c * ck, ck)
        bias = jnp.where(off + lane <= qpos, (selx_ref[:, pl.ds(off, ck)] - 1.0) * -NEG, NEG)
        s = _dot_nt(q4, ks_ref[0, pl.ds(off, ck), :].astype(BF16))
        s = (s.reshape(HPG, tq, ck) + bias[None]).reshape(HPG * tq, ck)
        s_ref[:, pl.ds(off, ck)] = s
        part = s[:, :KEY_BLOCK]
        for j in range(1, ck // KEY_BLOCK):
            part = jnp.maximum(part, s[:, j * KEY_BLOCK:(j + 1) * KEY_BLOCK])
        mx_ref[...] = jnp.maximum(mx_ref[...], part)
        return 0

    lax.fori_loop(0, nchunks, scores, 0)
    m = jnp.max(mx_ref[...], axis=-1, keepdims=True)
    ls_ref[...] = jnp.zeros(ls_ref.shape, F32)
    acc_ref[...] = jnp.zeros(acc_ref.shape, F32)

    def values(c, _):
        off = pl.multiple_of(c * ck, ck)
        s = s_ref[:, pl.ds(off, ck)]
        p = jnp.exp(s - m)
        part = p[:, :KEY_BLOCK]
        for j in range(1, ck // KEY_BLOCK):
            part = part + p[:, j * KEY_BLOCK:(j + 1) * KEY_BLOCK]
        ls_ref[...] += part
        acc_ref[...] += _dot(p.astype(BF16), vs_ref[0, pl.ds(off, ck), :].astype(BF16))
        return 0

    lax.fori_loop(0, nchunks, values, 0)
    o_s = acc_ref[...] / jnp.maximum(jnp.sum(ls_ref[...], axis=-1, keepdims=True), 1e-30)

    span = min(NSA_WINDOW + KEY_BLOCK, t)
    first = jnp.clip(qi - NSA_WINDOW // KEY_BLOCK, 0, (t - span) // KEY_BLOCK)
    off = pl.multiple_of(first * KEY_BLOCK, KEY_BLOCK)
    d = qpos - (off + _iota((tq, span), 1))
    bias = jnp.where(d >= 0, jnp.where(d < NSA_WINDOW, 0.0, NEG), NEG)
    s = _dot_nt(q4, kw_ref[0, pl.ds(off, span), :].astype(BF16))
    s = (s.reshape(HPG, tq, span) + bias[None]).reshape(HPG * tq, span)
    p = jnp.exp(s - jnp.max(s, axis=-1, keepdims=True))
    o_w = _dot(p.astype(BF16), vw_ref[0, pl.ds(off, span), :].astype(BF16)) / jnp.sum(p, axis=-1, keepdims=True)
    gates = jax.nn.sigmoid(gt_ref[0])
    for hl, o in enumerate(_nsa_combine(gates, 0, o_c, o_s, o_w, tq)):
        o_ref[0, :, hl * HEAD_DIM:(hl + 1) * HEAD_DIM] = o.astype(BF16)


def nsa_prompt(proj3, gate_cb, kc3, vc3, ks3, vs3, kw3, vw3, expand, qg, tables, nb):
    n, t, _ = proj3.shape
    tq = KEY_BLOCK
    nbp = kc3.shape[1]
    gw = HPG * HEAD_DIM
    tab = pl.BlockSpec((tq, HEAD_DIM), lambda b, g, i: (i, 0))
    cmp_spec = pl.BlockSpec((1, nbp, HEAD_DIM), lambda b, g, i: (b, 0, g))
    kv = pl.BlockSpec((1, t, HEAD_DIM), lambda b, g, i: (b, 0, g))
    return pl.pallas_call(
        functools.partial(_nsa_prompt_kernel, tq=tq, nb=nb, t=t),
        grid=(n, G_NSA, t // tq),
        in_specs=[pl.BlockSpec((1, tq, gw), lambda b, g, i: (b, i, g)),
                  pl.BlockSpec((1, tq, HEAD_DIM), lambda b, g, i: (b, i, gate_cb + g)),
                  cmp_spec, cmp_spec, kv, kv, kv, kv,
                  pl.BlockSpec((nbp, t), lambda b, g, i: (0, 0)),
                  pl.BlockSpec((1, HEAD_DIM), lambda b, g, i: (0, 0)), tab, tab, tab],
        out_specs=pl.BlockSpec((1, tq, gw), lambda b, g, i: (b, i, g)),
        out_shape=jax.ShapeDtypeStruct((n, t, H_NSA * HEAD_DIM), BF16),
        scratch_shapes=[pltpu.VMEM((tq, t), F32), pltpu.VMEM((HPG * tq, t), F32)]
        + [pltpu.VMEM((HPG * tq, HEAD_DIM), F32)] * 3,
        compiler_params=_cp("parallel", "parallel", "arbitrary"),
        name="nsa_prompt",
    )(proj3, proj3, kc3, vc3, ks3, vs3, kw3, vw3, expand, qg.reshape(1, HEAD_DIM), *tables)


def _compress_rows_kernel(k_ref, v_ref, pk_ref, pv_ref, ko_ref, vo_ref):
    for x_ref, p_ref, o_ref in ((k_ref, pk_ref, ko_ref), (v_ref, pv_ref, vo_ref)):
        x = x_ref[0]
        nblk = x.shape[0] // NSA_BLOCK
        pw = jnp.concatenate([p_ref[...]] * nblk, axis=0)
        o_ref[0] = jnp.sum((x * pw).reshape(nblk, NSA_BLOCK, x.shape[1]), axis=1)


def compress_rows(k3, v3, pwk2, pwv2):
    n, t, w = k3.shape
    tr = _tile(t, 512)
    row = pl.BlockSpec((1, tr, w), lambda b, i: (b, i, 0))
    pw = pl.BlockSpec((NSA_BLOCK, w), lambda b, i: (0, 0))
    out = pl.BlockSpec((1, tr // NSA_BLOCK, w), lambda b, i: (b, i, 0))
    return pl.pallas_call(
        _compress_rows_kernel,
        grid=(n, t // tr),
        in_specs=[row, row, pw, pw],
        out_specs=[out, out],
        out_shape=[jax.ShapeDtypeStruct((n, t // NSA_BLOCK, w), F32)] * 2,
        compiler_params=_cp("parallel", "parallel"),
        name="compress_rows",
    )(k3, v3, pwk2, pwv2)


def _compress_pages_kernel(pt_ref, *refs, pps):
    k_refs, v_refs = refs[:pps], refs[pps:2 * pps]
    pk_ref, pv_ref, ko_ref, vo_ref = refs[2 * pps:]
    bpp = PAGE_SIZE // NSA_BLOCK
    prows = PAGE_SIZE * G_NSA
    for x_refs, p_ref, o_ref in ((k_refs, pk_ref, ko_ref), (v_refs, pv_ref, vo_ref)):
        pw = p_ref[...]
        outs = []
        for x_ref in x_refs:
            z = jnp.sum((x_ref[0] * pw).reshape(bpp, prows // bpp // SUBLANES, SUBLANES, HEAD_DIM), axis=1)
            per_g = [sum(z[:, s] for s in range(g, SUBLANES, G_NSA)) for g in range(G_NSA)]
            outs.append(jnp.concatenate(per_g, axis=1))
        o_ref[0] = jnp.concatenate(outs, axis=0)


def compress_pages(pool_k, pool_v, page_off, page_table, pwk, pwv, pps):
    n, n_pages = page_table.shape
    w = G_NSA * HEAD_DIM
    prows = PAGE_SIZE * G_NSA
    bpp = PAGE_SIZE // NSA_BLOCK
    assert SUBLANES % G_NSA == 0
    page_w = lambda p: jnp.repeat(jnp.tile(p, (bpp, 1)), G_NSA, axis=0)
    pwk2, pwv2 = page_w(pwk), page_w(pwv)
    page = lambda i: pl.BlockSpec((1, prows, HEAD_DIM),
                                  functools.partial(lambda b, c, pt, i: (page_off + pt[b, c * pps + i], 0, 0), i=i))
    pw = pl.BlockSpec((prows, HEAD_DIM), lambda b, c, pt: (0, 0))
    out = pl.BlockSpec((1, pps * bpp, w), lambda b, c, pt: (b, c, 0))
    return pl.pallas_call(
        functools.partial(_compress_pages_kernel, pps=pps),
        grid_spec=pltpu.PrefetchScalarGridSpec(
            num_scalar_prefetch=1, grid=(n, n_pages // pps),
            in_specs=[page(i) for i in range(pps)] * 2 + [pw, pw],
            out_specs=[out, out]),
        out_shape=[jax.ShapeDtypeStruct((n, n_pages * bpp, w), F32)] * 2,
        compiler_params=_cp("parallel", "parallel"),
        name="compress_pages",
    )(page_table, *([pool_k] * pps), *([pool_v] * pps), pwk2, pwv2)


def _nsa_sample_kernel(pt_ref, *refs, pps, past, rows, nb, lw):
    (q_ref, gt_ref, kc_ref, vc_ref), refs = refs[:4], refs[4:]
    kp_refs, vp_refs, refs = refs[:pps], refs[pps:2 * pps], refs[2 * pps:]
    (kn_ref, vn_ref, kwc_ref, vwc_ref, kwn_ref, vwn_ref, e_ref, qg_ref, c_ref, sa_ref, sb_ref,
     o_ref, q4_ref, sel_ref, oc_ref, m_ref, l_ref, acc_ref) = refs
    pc = pl.program_id(1)
    r4 = HPG * rows
    qpos = past + _iota((rows, 1), 0)
    lane = _iota((rows, KEY_BLOCK), 1)

    @pl.when(pc == 0)
    def _():
        for g in range(G_NSA):
            cols = slice(g * HEAD_DIM, (g + 1) * HEAD_DIM)
            q4 = _nsa_q(q_ref, g, qg_ref, c_ref, sa_ref, sb_ref)
            o_c, sel = _nsa_compressed(q4, kc_ref[0][:, cols], vc_ref[0][:, cols], qpos, nb)
            q4_ref[g] = q4
            sel_ref[g] = sel
            oc_ref[g] = o_c
            m0, l0, a0 = _softmax_init(r4)
            m_ref[g], l_ref[g], acc_ref[g] = m0, l0, a0

    nkeys = pps * PAGE_SIZE
    kpos = pc * nkeys + _iota((rows, nkeys), 1)
    scores, values = [], []
    for g in range(G_NSA):
        grp = pl.ds(g, PAGE_SIZE, stride=G_NSA)
        kb = jnp.concatenate([r[0, grp, :].astype(BF16) for r in kp_refs], axis=0)
        values.append(jnp.concatenate([r[0, grp, :].astype(BF16) for r in vp_refs], axis=0))
        selx = _dot(sel_ref[g].astype(BF16), e_ref[...])
        bias = jnp.where(kpos <= qpos, (selx - 1.0) * -NEG, NEG)
        s = _dot_nt(q4_ref[g], kb)
        scores.append((s.reshape(HPG, rows, nkeys) + bias[None]).reshape(r4, nkeys))
    s = jnp.concatenate(scores, axis=0)
    m_old = jnp.concatenate([m_ref[g] for g in range(G_NSA)], axis=0)
    m_new = jnp.maximum(m_old, jnp.max(s, axis=-1, keepdims=True))
    alpha = jnp.exp(m_old - m_new)
    p = jnp.exp(s - m_new)
    pb = p.astype(BF16)
    psum = jnp.sum(p, axis=-1, keepdims=True)
    for g in range(G_NSA):
        r = slice(g * r4, (g + 1) * r4)
        m_ref[g] = m_new[r]
        l_ref[g] = alpha[r] * l_ref[g] + psum[r]
        acc_ref[g] = alpha[r] * acc_ref[g] + _dot(pb[r], values[g])

    @pl.when(pc == pl.num_programs(1) - 1)
    def _():
        gates_all = jax.nn.sigmoid(gt_ref[0])
        tq = _iota((rows, KEY_BLOCK), 0)
        for g in range(G_NSA):
            cols = slice(g * HEAD_DIM, (g + 1) * HEAD_DIM)
            q4 = q4_ref[g]
            kb = _pad_rows(kn_ref[0][:, cols], KEY_BLOCK).astype(BF16)
            vb = _pad_rows(vn_ref[0][:, cols], KEY_BLOCK).astype(BF16)
            w = jnp.where(lane <= tq, 1.0, 0.0)
            _, l, acc = _softmax_step(_dot_nt(q4, kb), jnp.concatenate([w] * HPG, axis=0), vb,
                                      m_ref[g], l_ref[g], acc_ref[g])
            o_s = _softmax_done(l, acc)
            tqw = _iota((rows, lw), 0)
            d = tqw + lw - _iota((rows, lw), 1)
            w = jnp.where(d < NSA_WINDOW, jnp.where(past - d + tqw >= 0, 1.0, 0.0), 0.0)
            grp = pl.ds(g, lw, stride=G_NSA)
            carry = _softmax_step(_dot_nt(q4, kwc_ref[0, grp, :].astype(BF16)), jnp.concatenate([w] * HPG, axis=0),
                                  vwc_ref[0, grp, :].astype(BF16), *_softmax_init(r4))
            kb = _pad_rows(kwn_ref[0][:, cols], KEY_BLOCK).astype(BF16)
            vb = _pad_rows(vwn_ref[0][:, cols], KEY_BLOCK).astype(BF16)
            w = jnp.where(lane <= tq, 1.0, 0.0)
            _, l, acc = _softmax_step(_dot_nt(q4, kb), jnp.concatenate([w] * HPG, axis=0), vb, *carry)
            o_w = _softmax_done(l, acc)
            gates = gates_all[:, g * HEAD_DIM:(g + 1) * HEAD_DIM]
            for hl, o in enumerate(_nsa_combine(gates, g, oc_ref[g], o_s, o_w, rows)):
                col = (g * HPG + hl) * HEAD_DIM
                o_ref[0, :, col:col + HEAD_DIM] = o.astype(BF16)


def nsa_sample(proj3, gate_cb, kc3, vc3, pool_k, pool_v, page_off, page_table, kn3, vn3, kwc3, vwc3, seq_off,
               kwn3, vwn3, expand, qg, tables, pps):
    n, rows, _ = proj3.shape
    n_pages = page_table.shape[1]
    past = n_pages * PAGE_SIZE
    nbp = kc3.shape[1]
    w = G_NSA * HEAD_DIM
    lw = kwc3.shape[1] // G_NSA
    qw = H_NSA * HEAD_DIM
    r4 = HPG * rows
    fixed = lambda shape: pl.BlockSpec(shape, lambda b, c, pt: (0,) * len(shape))
    per_seq = lambda r, width, cb=0, off=0: pl.BlockSpec(
        (1, r, width), functools.partial(lambda b, c, pt, cb, off: (off + b, 0, cb), cb=cb, off=off))
    page = lambda i: pl.BlockSpec((1, PAGE_SIZE * G_NSA, HEAD_DIM),
                                  functools.partial(lambda b, c, pt, i: (page_off + pt[b, c * pps + i], 0, 0), i=i))
    win = per_seq(lw * G_NSA, HEAD_DIM, 0, seq_off)
    in_specs = ([per_seq(rows, qw), per_seq(rows, w, gate_cb // G_NSA), per_seq(nbp, w), per_seq(nbp, w)]
                + [page(i) for i in range(pps)] * 2
                + [per_seq(rows, w), per_seq(rows, w), win, win, per_seq(rows, w), per_seq(rows, w),
                   pl.BlockSpec((nbp, pps * PAGE_SIZE), lambda b, c, pt: (0, c)),
                   fixed((1, HEAD_DIM)), fixed((rows, HEAD_DIM)), fixed((rows, HEAD_DIM)), fixed((rows, HEAD_DIM))])
    return pl.pallas_call(
        functools.partial(_nsa_sample_kernel, pps=pps, past=past, rows=rows, nb=past // NSA_BLOCK, lw=lw),
        grid_spec=pltpu.PrefetchScalarGridSpec(
            num_scalar_prefetch=1, grid=(n, n_pages // pps),
            in_specs=in_specs,
            out_specs=pl.BlockSpec((1, rows, qw), lambda b, c, pt: (b, 0, 0)),
            scratch_shapes=[pltpu.VMEM((G_NSA, r4, HEAD_DIM), BF16), pltpu.VMEM((G_NSA, rows, nbp), F32),
                            pltpu.VMEM((G_NSA, r4, HEAD_DIM), F32), pltpu.VMEM((G_NSA, r4, 1), F32),
                            pltpu.VMEM((G_NSA, r4, 1), F32), pltpu.VMEM((G_NSA, r4, HEAD_DIM), F32)]),
        out_shape=jax.ShapeDtypeStruct((n, rows, qw), BF16),
        compiler_params=_cp("parallel", "arbitrary"),
        name="nsa_sample",
    )(page_table, proj3, proj3, kc3, vc3, *([pool_k] * pps), *([pool_v] * pps), kn3, vn3, kwc3, vwc3, kwn3, vwn3,
      expand, qg.reshape(1, HEAD_DIM), *tables)


def _lru_kernel(x_ref, y_ref, hist_ref, h0_ref, cw_ref, cb_ref, wa_ref, ba_ref, wi_ref, bi_ref, sp_ref,
                o_ref, tail_ref, a_ref, b_ref, *, cps):
    t = x_ref.shape[1]
    bw = D_RNN // RNN_BLOCKS
    row8 = _iota((SUBLANES, bw), 0)
    for k in range(cps):
        cols = slice(k * bw, (k + 1) * bw)
        x = x_ref[0, :, cols]
        hist = hist_ref[0, :, cols]
        xc = cb_ref[:, cols] + x * cw_ref[CONV_W - 1:CONV_W, cols]
        for s in range(1, CONV_W):
            sh = pltpu.roll(x, s, 0)
            head = jnp.where(row8 >= s, sh[:SUBLANES], pltpu.roll(hist, (SUBLANES - (CONV_W - 1) + s) % SUBLANES, 0))
            sh = head if t == SUBLANES else jnp.concatenate([head, sh[SUBLANES:]], axis=0)
            xc = xc + sh * cw_ref[CONV_W - 1 - s:CONV_W - s, cols]
        xb = xc.astype(BF16)
        r = jax.nn.sigmoid(_dot(xb, wa_ref[k]) + ba_ref[:, cols])
        gi = jax.nn.sigmoid(_dot(xb, wi_ref[k]) + bi_ref[:, cols])
        log_a = -LRU_C * r * sp_ref[:, cols]
        th = jnp.tanh(log_a)
        a_ref[:, cols] = jnp.exp(log_a)
        b_ref[:, cols] = jnp.sqrt(-2.0 * th / (1.0 - th)) * (gi * xc)

    wid = cps * bw
    rowsw = _iota((SUBLANES, wid), 0)

    def body(gidx, hc):
        off = pl.multiple_of(gidx * SUBLANES, SUBLANES)
        a = a_ref[pl.ds(off, SUBLANES), :]
        b = b_ref[pl.ds(off, SUBLANES), :]
        for s in (1, 2, 4):
            ok = rowsw >= s
            b = jnp.where(ok, a * pltpu.roll(b, s, 0) + b, b)
            a = jnp.where(ok, a * pltpu.roll(a, s, 0), a)
        hrows = a * hc + b
        a_ref[pl.ds(off, SUBLANES), :] = hrows
        return jnp.broadcast_to(hrows[SUBLANES - 1:SUBLANES, :], (SUBLANES, wid))

    lax.fori_loop(0, t // SUBLANES, body, jnp.broadcast_to(h0_ref[0], (SUBLANES, wid)))
    tail_ref[0] = a_ref[t - SUBLANES:, :]
    chunk = _tile(t, 512)
    for c in range(t // chunk):
        rws = slice(c * chunk, (c + 1) * chunk)
        o_ref[0, rws, :] = (jax.nn.gelu(y_ref[0, rws, :], approximate=True) * a_ref[rws, :]).astype(BF16)


def rglru(proj3, x_cb, y_cb, hist8, h0, conv_w, conv_b, w_a, b_a, w_i, b_i, softplus_neg_lam):
    n, t, _ = proj3.shape
    bw = D_RNN // RNN_BLOCKS
    cps = 4 if t > SUBLANES else RNN_BLOCKS
    wid = cps * bw
    assert (x_cb * HEAD_DIM) % wid == 0 and (y_cb * HEAD_DIM) % wid == 0 and bw == HEAD_DIM
    vec = lambda a: a.reshape(1, D_RNN)
    vspec = pl.BlockSpec((1, wid), lambda b, k: (0, k))
    mspec = pl.BlockSpec((cps, bw, bw), lambda b, k: (k, 0, 0))
    return pl.pallas_call(
        functools.partial(_lru_kernel, cps=cps),
        grid=(n, RNN_BLOCKS // cps),
        in_specs=[pl.BlockSpec((1, t, wid), lambda b, k: (b, 0, x_cb * HEAD_DIM // wid + k)),
                  pl.BlockSpec((1, t, wid), lambda b, k: (b, 0, y_cb * HEAD_DIM // wid + k)),
                  pl.BlockSpec((1, SUBLANES, wid), lambda b, k: (b, 0, k)),
                  pl.BlockSpec((1, 1, wid), lambda b, k: (b, 0, k)),
                  pl.BlockSpec((CONV_W, wid), lambda b, k: (0, k)),
                  vspec, mspec, vspec, mspec, vspec, vspec],
        out_specs=[pl.BlockSpec((1, t, wid), lambda b, k: (b, 0, k)),
                   pl.BlockSpec((1, SUBLANES, wid), lambda b, k: (b, 0, k))],
        out_shape=[jax.ShapeDtypeStruct((n, t, D_RNN), BF16), jax.ShapeDtypeStruct((n, SUBLANES, D_RNN), F32)],
        scratch_shapes=[pltpu.VMEM((t, wid), F32), pltpu.VMEM((t, wid), F32)],
        compiler_params=_cp("parallel", "parallel"),
        name="rglru",
    )(proj3, proj3, hist8, h0.reshape(n, 1, D_RNN), conv_w, vec(conv_b),
      w_a.astype(BF16), vec(b_a), w_i.astype(BF16), vec(b_i), vec(softplus_neg_lam))


def _rope_tables(pos, n_rot, theta):
    half = n_rot // 2
    inv = 1.0 / (theta ** (jnp.arange(half, dtype=F32) * (2.0 / n_rot)))
    ang = pos.astype(F32)[:, None] * inv[None, :]
    cos, sin = jnp.cos(ang), jnp.sin(ang)
    p = pos.shape[0]
    rest = HEAD_DIM - n_rot
    zh = jnp.zeros((p, half), F32)
    c = jnp.concatenate([cos, cos, jnp.ones((p, rest), F32)], axis=1)
    sa = jnp.concatenate([-sin, zh, jnp.zeros((p, rest), F32)], axis=1)
    sb = jnp.concatenate([zh, sin, jnp.zeros((p, rest), F32)], axis=1)
    return c, sa, sb


def _block_expand(nbp, length):
    return (jnp.arange(length, dtype=jnp.int32)[None, :] // NSA_BLOCK
            == jnp.arange(nbp, dtype=jnp.int32)[:, None]).astype(BF16)


def _pad_axis(a, axis, size):
    pad = [(0, 0)] * a.ndim
    pad[axis] = (0, size - a.shape[axis])
    return jnp.pad(a, pad)


def _odd_w_in(w):
    qw = H_NSA * HEAD_DIM
    qkv = qw + 6 * G_NSA * HEAD_DIM
    ng = 3 * H_NSA
    gate = w[:, qkv:qkv + ng]
    per = 3 * HPG
    gcols = [_pad_axis(gate[:, g * per:(g + 1) * per], 1, LANES) for g in range(G_NSA)]
    out = jnp.concatenate([w[:, :qw], w[:, qkv + ng:], w[:, qw:qkv]] + gcols, axis=1)
    return _pad_axis(out, 1, -(-out.shape[1] // 1024) * 1024)


def kernel(x_prompt, x_sample, mem_prompt, state_ret, cache_dil_k, cache_dil_v, cache_nsa_cmp_k, cache_nsa_cmp_v, cache_nsa_slc_k, cache_nsa_slc_v, cache_nsa_win_k, cache_nsa_win_v, state_lru_h, state_lru_conv, cache_mem_k, cache_mem_v, page_table, ev_w_in, ret_gn_g, dil_q_norm_g, dil_k_norm_g, ev_w_out, od_w_in, nsa_q_norm_g, nsa_k_norm_g, nsa_pw_k, nsa_pw_v, lru_conv_w, lru_conv_b, lru_w_a, lru_b_a, lru_w_i, lru_b_i, lru_lambda, od_w_out, norm_mix_g, norm_mem_g, norm_mlp_g, mem_norm_g, mem_w_q, mem_w_k, mem_w_v, mem_q_norm_g, mem_k_norm_g, mem_w_o, mlp_w1, mlp_w2):
    nb_p, t_p, d = x_prompt.shape
    nb_s, t_s, _ = x_sample.shape
    depth = norm_mix_g.shape[0]
    n_pages = page_table.shape[1]
    past = n_pages * PAGE_SIZE
    rs = SAMPLE_ROWS
    assert t_p % RET_CHUNK == 0 and t_s <= rs and past % PAGE_SIZE == 0 and t_s < NSA_BLOCK
    assert cache_nsa_win_k.shape[2] % KEY_BLOCK == 0 and cache_dil_k.shape[2] % KEY_BLOCK == 0
    hd = HEAD_DIM

    xp = x_prompt.reshape(nb_p * t_p, d)
    xs = _pad_axis(x_sample, 1, rs).reshape(nb_s * rs, d)
    mem2 = mem_prompt.reshape(-1, d)
    n_mem = mem_prompt.shape[1]

    pos_p = jnp.arange(t_p, dtype=jnp.int32)
    pos_s = past + jnp.arange(rs, dtype=jnp.int32)
    tile_s = lambda tabs: tuple(jnp.tile(tb, (nb_s, 1)) for tb in tabs)
    rope_p, rope_s = _rope_tables(pos_p, ROPE_DIMS, ROPE_THETA), _rope_tables(pos_s, ROPE_DIMS, ROPE_THETA)
    ret_p, ret_s = _rope_tables(pos_p, hd, RET_THETA), _rope_tables(pos_s, hd, RET_THETA)
    log_g = jnp.log1p(-jnp.exp2(-5.0 - jnp.arange(H_RET, dtype=F32)))

    ev_w_in_b, ev_w_out_b, od_w_out_b = ev_w_in.astype(BF16), ev_w_out.astype(BF16), od_w_out.astype(BF16)
    od_w_in_b = jnp.stack([_odd_w_in(od_w_in[o].astype(BF16)) for o in range(od_w_in.shape[0])])
    mem_wq_b, mem_wo_b = mem_w_q.astype(BF16), mem_w_o.astype(BF16)
    mem_wkv_b = jnp.concatenate([mem_w_k, mem_w_v], axis=2).astype(BF16)

    outs = {k: [] for k in ("ret_p", "ret_s", "dk_p", "dv_p", "dk_s", "dv_s", "ck_p", "cv_p", "ck_s", "cv_s",
                            "sk_p", "sv_p", "sk_s", "sv_s", "wk_p", "wv_p", "wk_s", "wv_s", "lh_p", "lh_s",
                            "lc_p", "lc_s", "mk_p", "mv_p")}
    new_rows = lambda a, w: a.reshape(nb_s, rs, w)[:, :t_s]

    for layer in range(depth):
        if layer % 2 == 0:
            e = layer // 2
            w_in, w_out, wl = ev_w_in_b, ev_w_out_b, e
            hw = H_RET * hd
            ones = jnp.ones((1, hd), F32)
            gains = jnp.concatenate([dil_k_norm_g[e][None], ones], axis=0)
            sections = [(5, "norm_rope"), (6, "copy")]
            proj = rms_matmul(xp, norm_mix_g[layer], w_in, wl, 1024)
            proj3 = proj.reshape(nb_p, t_p, -1)
            y_ret, s_new = retention(proj3, jnp.zeros((nb_p, H_RET, hd, hd), F32), log_g, ret_gn_g[e], ret_p, RET_CHUNK)
            dk, dv = kv_transform(proj, sections, gains, rope_p, H_DIL, t_p)
            y_dil = dilated_prompt(proj3, 4 * H_RET, dk.reshape(nb_p, t_p, hw), dv.reshape(nb_p, t_p, hw),
                                   dil_q_norm_g[e], rope_p)
            xp = matmul_res(y_ret.reshape(-1, hw), y_dil.reshape(-1, hw), w_out, wl, xp)
            keep = min(DIL_PATTERNS[-1][0], t_p)
            outs["ret_p"].append(s_new)
            outs["dk_p"].append(dk.reshape(nb_p, t_p, H_DIL, hd)[:, t_p - keep:])
            outs["dv_p"].append(dv.reshape(nb_p, t_p, H_DIL, hd)[:, t_p - keep:])
            proj = rms_matmul(xs, norm_mix_g[layer], w_in, wl, 1024)
            proj3 = proj.reshape(nb_s, rs, -1)
            y_ret, s_new = retention(proj3, state_ret.reshape(-1, H_RET, hd, hd), log_g, ret_gn_g[e], ret_s, t_s,
                                     seq_off=e * nb_s)
            dk, dv = kv_transform(proj, sections, gains, tile_s(rope_s), H_DIL, nb_s * rs)
            lc = cache_dil_k.shape[2]
            y_dil = dilated_sample(proj3, 4 * H_RET, cache_dil_k.reshape(-1, lc * H_DIL, hd),
                                   cache_dil_v.reshape(-1, lc * H_DIL, hd), e * nb_s, dk.reshape(nb_s, rs, hw),
                                   dv.reshape(nb_s, rs, hw), dil_q_norm_g[e], rope_s)
            xs = matmul_res(y_ret.reshape(-1, hw), y_dil.reshape(-1, hw), w_out, wl, xs)
            outs["ret_s"].append(s_new)
            outs["dk_s"].append(new_rows(dk, hw).reshape(nb_s, t_s, H_DIL, hd))
            outs["dv_s"].append(new_rows(dv, hw).reshape(nb_s, t_s, H_DIL, hd))
        else:
            o = layer // 2
            w_in, w_out, wl = od_w_in_b, od_w_out_b, o
            qw = H_NSA * hd
            gw = G_NSA * hd
            x_cb = qw // hd
            y_cb = x_cb + D_RNN // hd
            kv_cb = (qw + 2 * D_RNN) // gw
            gate_cb = (qw + 2 * D_RNN + 6 * gw) // hd
            ones = jnp.ones((1, hd), F32)
            kg = nsa_k_norm_g[o]
            gains = jnp.concatenate([kg[0:1], ones, kg[1:2], ones, kg[2:3], ones], axis=0)
            sections = [(kv_cb + i, "norm_rope" if i % 2 == 0 else "copy") for i in range(6)]
            pwk2 = jnp.concatenate([nsa_pw_k[o]] * G_NSA, axis=1)
            pwv2 = jnp.concatenate([nsa_pw_v[o]] * G_NSA, axis=1)
            sp = jax.nn.softplus(-lru_lambda[o].astype(F32))
            lru_w = (lru_conv_w[o], lru_conv_b[o], lru_w_a[o], lru_b_a[o], lru_w_i[o], lru_b_i[o], sp)
            proj = rms_matmul(xp, norm_mix_g[layer], w_in, wl, 1024)
            proj3 = proj.reshape(nb_p, t_p, -1)
            kcm, vcm, ksl, vsl, kwn, vwn = kv_transform(proj, sections, gains, rope_p, G_NSA, t_p)
            r3 = lambda a: a.reshape(nb_p, t_p, gw)
            nb = t_p // NSA_BLOCK
            nbp = -(-nb // LANES) * LANES
            kc, vc = compress_rows(r3(kcm), r3(vcm), pwk2, pwv2)
            o_nsa = nsa_prompt(proj3, gate_cb, _pad_axis(kc, 1, nbp), _pad_axis(vc, 1, nbp), r3(ksl), r3(vsl),
                               r3(kwn), r3(vwn), _block_expand(nbp, t_p), nsa_q_norm_g[o], rope_p, nb)
            y_rnn, tail = rglru(proj3, x_cb, y_cb, jnp.zeros((nb_p, SUBLANES, D_RNN), F32),
                                jnp.zeros((nb_p, D_RNN), F32), *lru_w)
            xp = matmul_res(o_nsa.reshape(-1, qw), y_rnn.reshape(-1, D_RNN), w_out, wl, xp)
            r4 = lambda a: a.reshape(nb_p, t_p, G_NSA, hd)
            keep = min(NSA_WINDOW, t_p)
            for key, val in (("ck_p", r4(kcm)), ("cv_p", r4(vcm)), ("sk_p", r4(ksl)), ("sv_p", r4(vsl)),
                             ("wk_p", r4(kwn)[:, t_p - keep:]), ("wv_p", r4(vwn)[:, t_p - keep:])):
                outs[key].append(val)
            outs["lh_p"].append(tail[:, SUBLANES - 1])
            outs["lc_p"].append(proj3[:, t_p - (CONV_W - 1):, x_cb * hd:x_cb * hd + D_RNN])
            proj = rms_matmul(xs, norm_mix_g[layer], w_in, wl, 1024)
            proj3 = proj.reshape(nb_s, rs, -1)
            kcm, vcm, ksl, vsl, kwn, vwn = kv_transform(proj, sections, gains, tile_s(rope_s), G_NSA, nb_s * rs)
            r3 = lambda a: a.reshape(nb_s, rs, gw)
            nph = cache_nsa_cmp_k.shape[1]
            pool = lambda a: a.reshape(-1, PAGE_SIZE * G_NSA, hd)
            pps = math.gcd(n_pages, 32)
            nb = (past + t_s) // NSA_BLOCK
            nbp = -(-nb // LANES) * LANES
            kc, vc = compress_pages(pool(cache_nsa_cmp_k), pool(cache_nsa_cmp_v), o * nph, page_table,
                                    nsa_pw_k[o], nsa_pw_v[o], pps)
            lw = cache_nsa_win_k.shape[2]
            o_nsa = nsa_sample(proj3, gate_cb, _pad_axis(kc, 1, nbp), _pad_axis(vc, 1, nbp),
                               pool(cache_nsa_slc_k), pool(cache_nsa_slc_v), o * nph, page_table, r3(ksl), r3(vsl),
                               cache_nsa_win_k.reshape(-1, lw * G_NSA, hd), cache_nsa_win_v.reshape(-1, lw * G_NSA, hd),
                               o * nb_s, r3(kwn), r3(vwn), _block_expand(nbp, past), nsa_q_norm_g[o], rope_s, pps)
            hist8 = _pad_axis(state_lru_conv[o].astype(F32), 1, SUBLANES)
            y_rnn, tail = rglru(proj3, x_cb, y_cb, hist8, state_lru_h[o].astype(F32), *lru_w)
            xs = matmul_res(o_nsa.reshape(-1, qw), y_rnn.reshape(-1, D_RNN), w_out, wl, xs)
            for key, val in (("ck_s", kcm), ("cv_s", vcm), ("sk_s", ksl), ("sv_s", vsl), ("wk_s", kwn), ("wv_s", vwn)):
                outs[key].append(new_rows(val, gw).reshape(nb_s, t_s, G_NSA, hd))
            outs["lh_s"].append(tail[:, t_s - 1])
            lx = proj3[:, :t_s, x_cb * hd:x_cb * hd + D_RNN]
            outs["lc_s"].append(jnp.concatenate([state_lru_conv[o].astype(F32), lx], axis=1)[:, -(CONV_W - 1):])

        dm = H_MEM * hd
        mkv = rms_matmul(mem2, mem_norm_g[layer], mem_wkv_b, layer, dm)
        gains = jnp.concatenate([mem_k_norm_g[layer][None], jnp.ones((1, hd), F32)], axis=0)
        mk, mv = kv_transform(mkv, [(0, "norm"), (1, "copy")], gains, rope_p, H_MEM, n_mem)
        mk3, mv3 = mk.reshape(nb_p, n_mem, dm), mv.reshape(nb_p, n_mem, dm)
        outs["mk_p"].append(mk3.reshape(nb_p, n_mem, H_MEM, hd))
        outs["mv_p"].append(mv3.reshape(nb_p, n_mem, H_MEM, hd))
        xp = mem_attn_block(xp, norm_mem_g[layer], mem_wq_b, mem_q_norm_g[layer], mk3, mv3, mem_wo_b, layer, t_p)
        xs = mem_attn_block(xs, norm_mem_g[layer], mem_wq_b, mem_q_norm_g[layer],
                            cache_mem_k.reshape(-1, n_mem * H_MEM, hd), cache_mem_v.reshape(-1, n_mem * H_MEM, hd),
                            mem_wo_b, layer, rs, seq_off=layer * nb_s)
        xp = mlp_block(xp, norm_mlp_g[layer], mlp_w1, mlp_w2, layer)
        xs = mlp_block(xs, norm_mlp_g[layer], mlp_w1, mlp_w2, layer)

    st = lambda k: jnp.stack(outs[k])
    return (xp.reshape(nb_p, t_p, d), xs.reshape(nb_s, rs, d)[:, :t_s], st("ret_p"), st("ret_s"),
            st("dk_p"), st("dv_p"), st("dk_s"), st("dv_s"),
            st("ck_p"), st("cv_p"), st("ck_s"), st("cv_s"),
            st("sk_p"), st("sv_p"), st("sk_s"), st("sv_s"),
            st("wk_p"), st("wv_p"), st("wk_s"), st("wv_s"),
            st("lh_p"), st("lh_s"), st("lc_p"), st("lc_s"),
            st("mk_p"), st("mv_p"))
```

```python
import functools
import math

import jax
import jax.numpy as jnp
from jax import lax
from jax.experimental import pallas as pl
from jax.experimental.pallas import tpu as pltpu

F32 = jnp.float32
BF16 = jnp.bfloat16

HEAD_DIM = 128
ROPE_DIMS = HEAD_DIM // 4
ROPE_THETA = 500000.0
NORM_EPS = 1e-6
H_RET = 8
RET_CHUNK = 128
RET_THETA = 10000.0
H_DIL = 8
DIL_PATTERNS = ((128, 1), (512, 4), (2048, 16))
H_NSA = 8
G_NSA = 2
HPG = H_NSA // G_NSA
NSA_BLOCK = 64
NSA_TOPK = 16
NSA_WINDOW = 512
D_RNN = 1024
RNN_BLOCKS = 8
CONV_W = 4
LRU_C = 8.0
H_MEM = 4
PAGE_SIZE = 128

LANES = 128
SUBLANES = 8
KEY_BLOCK = 128
SAMPLE_ROWS = 8
VMEM_LIMIT = 48 * 1024 * 1024
MLP_VMEM_LIMIT = 56 * 1024 * 1024
NEG = -1e30
SCALE = HEAD_DIM ** -0.5


def _cp(*sem):
    return pltpu.CompilerParams(dimension_semantics=sem, vmem_limit_bytes=VMEM_LIMIT)


def _tile(n, pref):
    t = min(n, pref)
    while n % t:
        t -= SUBLANES
    return t


def _rms(x, g):
    return x * lax.rsqrt(jnp.mean(x * x, axis=-1, keepdims=True) + NORM_EPS) * g


def _rope(x, c, sa, sb, half):
    return x * c + pltpu.roll(x, LANES - half, 1) * sa + pltpu.roll(x, half, 1) * sb


def _dot(a, b):
    return jnp.dot(a, b, preferred_element_type=F32)


def _dot_nt(a, b):
    return lax.dot_general(a, b, (((1,), (1,)), ((), ())), preferred_element_type=F32)


def _pad_rows(x, rows):
    if x.shape[0] == rows:
        return x
    return jnp.concatenate([x, jnp.zeros((rows - x.shape[0], x.shape[1]), x.dtype)], axis=0)


def _softmax_step(s, w, vb, m, l, acc):
    s = jnp.where(w > 0, s, NEG)
    m_new = jnp.maximum(m, jnp.max(s, axis=-1, keepdims=True))
    alpha = jnp.exp(m - m_new)
    p = jnp.exp(s - m_new) * w
    l = alpha * l + jnp.sum(p, axis=-1, keepdims=True)
    acc = alpha * acc + _dot(p.astype(BF16), vb)
    return m_new, l, acc


def _softmax_init(rows):
    return (jnp.full((rows, 1), NEG, F32), jnp.zeros((rows, 1), F32), jnp.zeros((rows, HEAD_DIM), F32))


def _softmax_done(l, acc):
    return acc / jnp.maximum(l, 1e-30)


def _dil_weight(d, patterns=DIL_PATTERNS):
    w = jnp.zeros(d.shape, F32)
    for window, dil in patterns:
        hit = jnp.where(d <= window, 1.0, 0.0)
        if dil > 1:
            rem = (d & (dil - 1)) if dil & (dil - 1) == 0 else lax.rem(d, dil)
            hit = jnp.where(rem == 0, hit, 0.0)
        w = w + hit
    return jnp.where(d >= 0, w, 0.0)


def _iota(shape, dim):
    return lax.broadcasted_iota(jnp.int32, shape, dim)


def _rms_matmul_kernel(x_ref, g_ref, w_ref, o_ref, xn_ref):
    @pl.when(pl.program_id(1) == 0)
    def _():
        xn_ref[...] = _rms(x_ref[...], g_ref[...]).astype(BF16)

    o_ref[...] = _dot(xn_ref[...], w_ref[...])


def rms_matmul(x, g, w, layer, tn):
    m, d = x.shape
    n = w.shape[2]
    tm = _tile(m, 1024)
    return pl.pallas_call(
        _rms_matmul_kernel,
        grid=(m // tm, n // tn),
        in_specs=[pl.BlockSpec((tm, d), lambda i, j: (i, 0)),
                  pl.BlockSpec((1, d), lambda i, j: (0, 0)),
                  pl.BlockSpec((None, d, tn), lambda i, j: (layer, 0, j))],
        out_specs=pl.BlockSpec((tm, tn), lambda i, j: (i, j)),
        out_shape=jax.ShapeDtypeStruct((m, n), F32),
        scratch_shapes=[pltpu.VMEM((tm, d), BF16)],
        compiler_params=_cp("parallel", "arbitrary"),
        name="rms_matmul",
    )(x, g.reshape(1, d), w)


def _matmul_res_kernel(a1_ref, a2_ref, w1_ref, w2_ref, r_ref, o_ref):
    o_ref[...] = r_ref[...] + _dot(a1_ref[...], w1_ref[...]) + _dot(a2_ref[...], w2_ref[...])


def matmul_res(a1, a2, w, layer, res):
    m, k1 = a1.shape
    k2 = a2.shape[1]
    n = w.shape[2]
    assert k1 == k2 and w.shape[1] == k1 + k2
    tm = _tile(m, 1024)
    tn = _tile(n, 1024)
    return pl.pallas_call(
        _matmul_res_kernel,
        grid=(m // tm, n // tn),
        in_specs=[pl.BlockSpec((tm, k1), lambda i, j: (i, 0)),
                  pl.BlockSpec((tm, k2), lambda i, j: (i, 0)),
                  pl.BlockSpec((None, k1, tn), lambda i, j: (layer, 0, j)),
                  pl.BlockSpec((None, k2, tn), lambda i, j: (layer, 1, j)),
                  pl.BlockSpec((tm, tn), lambda i, j: (i, j))],
        out_specs=pl.BlockSpec((tm, tn), lambda i, j: (i, j)),
        out_shape=jax.ShapeDtypeStruct((m, n), F32),
        compiler_params=_cp("parallel", "parallel"),
        name="matmul_res",
    )(a1, a2, w, w, res)


def _mlp_kernel(x_ref, g_ref, w1_ref, w2_ref, o_ref, xn_ref):
    @pl.when(pl.program_id(1) == 0)
    def _():
        x = x_ref[...]
        xn_ref[...] = _rms(x, g_ref[...]).astype(BF16)
        o_ref[...] = x

    hid = jnp.maximum(_dot(xn_ref[...], w1_ref[...].astype(BF16)), 0.0)
    o_ref[...] += _dot((hid * hid).astype(BF16), w2_ref[...].astype(BF16))


def mlp_block(x, g, w1, w2, layer):
    m, d = x.shape
    ff = w1.shape[2]
    tm = _tile(m, 1024)
    tf = _tile(ff, 512)
    once = pl.Buffered(1)
    return pl.pallas_call(
        _mlp_kernel,
        grid=(m // tm, ff // tf),
        in_specs=[pl.BlockSpec((tm, d), lambda i, f: (i, 0)),
                  pl.BlockSpec((1, d), lambda i, f: (0, 0)),
                  pl.BlockSpec((None, d, tf), lambda i, f: (layer, 0, f)),
                  pl.BlockSpec((None, tf, d), lambda i, f: (layer, f, 0))],
        out_specs=pl.BlockSpec((tm, d), lambda i, f: (i, 0), pipeline_mode=once),
        out_shape=jax.ShapeDtypeStruct((m, d), F32),
        scratch_shapes=[pltpu.VMEM((tm, d), BF16)],
        compiler_params=pltpu.CompilerParams(dimension_semantics=("parallel", "arbitrary"),
                                             vmem_limit_bytes=MLP_VMEM_LIMIT),
        name="mlp_block",
    )(x, g.reshape(1, d), w1, w2)


def _mem_attn_kernel(x_ref, g_ref, wq_ref, qg_ref, k_ref, v_ref, wo_ref, o_ref, oh_ref, *, nb, seg, by_row):
    x = x_ref[...]
    q = _dot(_rms(x, g_ref[...]).astype(BF16), wq_ref[...])
    qg = qg_ref[...]
    n_mem = k_ref.shape[1] // H_MEM if by_row else k_ref.shape[1]
    qh = [(_rms(q[:, h * HEAD_DIM:(h + 1) * HEAD_DIM], qg) * SCALE).astype(BF16) for h in range(H_MEM)]
    pairs = [(b, h) for b in range(nb) for h in range(H_MEM)]

    def mem_head(ref, b, h):
        if by_row:
            return ref[b, pl.ds(h, n_mem, stride=H_MEM), :].astype(BF16)
        return ref[b, :, h * HEAD_DIM:(h + 1) * HEAD_DIM].astype(BF16)

    s = jnp.concatenate([_dot_nt(qh[h][b * seg:(b + 1) * seg], mem_head(k_ref, b, h)) for b, h in pairs], axis=0)
    p = jnp.exp(s - jnp.max(s, axis=-1, keepdims=True))
    pb = (p / jnp.sum(p, axis=-1, keepdims=True)).astype(BF16)
    for j, (b, h) in enumerate(pairs):
        o = _dot(pb[j * seg:(j + 1) * seg], mem_head(v_ref, b, h))
        oh_ref[b * seg:(b + 1) * seg, h * HEAD_DIM:(h + 1) * HEAD_DIM] = o.astype(BF16)
    o_ref[...] = x + _dot(oh_ref[...], wo_ref[...])


def mem_attn_block(x, g, wq, qg, k, v, wo, layer, rows_per_seq, seq_off=0):
    m, d = x.shape
    dm = H_MEM * HEAD_DIM
    by_row = k.shape[2] == HEAD_DIM
    kv_rows, kv_w = k.shape[1], k.shape[2]
    if rows_per_seq >= 128:
        tm = _tile(rows_per_seq, 512)
        nb, seg = 1, tm
        kv_map = lambda i: (seq_off + i // (rows_per_seq // tm), 0, 0)
    else:
        nb = _tile(m // rows_per_seq, 8)
        tm, seg = nb * rows_per_seq, rows_per_seq
        assert seq_off % nb == 0
        kv_map = lambda i: (seq_off // nb + i, 0, 0)
    return pl.pallas_call(
        functools.partial(_mem_attn_kernel, nb=nb, seg=seg, by_row=by_row),
        grid=(m // tm,),
        in_specs=[pl.BlockSpec((tm, d), lambda i: (i, 0)),
                  pl.BlockSpec((1, d), lambda i: (0, 0)),
                  pl.BlockSpec((None, d, dm), lambda i: (layer, 0, 0)),
                  pl.BlockSpec((1, HEAD_DIM), lambda i: (0, 0)),
                  pl.BlockSpec((nb, kv_rows, kv_w), kv_map),
                  pl.BlockSpec((nb, kv_rows, kv_w), kv_map),
                  pl.BlockSpec((None, dm, d), lambda i: (layer, 0, 0))],
        out_specs=pl.BlockSpec((tm, d), lambda i: (i, 0)),
        out_shape=jax.ShapeDtypeStruct((m, d), F32),
        scratch_shapes=[pltpu.VMEM((tm, dm), BF16)],
        compiler_params=_cp("parallel"),
        name="mem_attn_block",
    )(x, g.reshape(1, d), wq, qg.reshape(1, HEAD_DIM), k, v, wo)


def _kv_transform_kernel(*refs, modes, heads):
    ns = len(modes)
    x_refs = refs[:ns]
    g_ref, c_ref, sa_ref, sb_ref = refs[ns:ns + 4]
    o_refs = refs[ns + 4:]
    for s in range(ns):
        for h in range(heads):
            cols = slice(h * HEAD_DIM, (h + 1) * HEAD_DIM)
            x = x_refs[s][:, cols]
            if modes[s] != "copy":
                x = _rms(x, g_ref[s:s + 1, :])
            if modes[s] == "norm_rope":
                x = _rope(x, c_ref[...], sa_ref[...], sb_ref[...], ROPE_DIMS // 2)
            o_refs[s][:, cols] = x


def kv_transform(proj, sections, gains, tables, heads, pos_period):
    m = proj.shape[0]
    width = heads * HEAD_DIM
    tr = _tile(pos_period, 512)
    nper = pos_period // tr
    modes = tuple(mode for _, mode in sections)
    in_specs = [pl.BlockSpec((tr, width), functools.partial(lambda i, cb: (i, cb), cb=cb)) for cb, _ in sections]
    in_specs.append(pl.BlockSpec(gains.shape, lambda i: (0, 0)))
    in_specs += [pl.BlockSpec((tr, HEAD_DIM), lambda i: (i % nper, 0))] * 3
    outs = pl.pallas_call(
        functools.partial(_kv_transform_kernel, modes=modes, heads=heads),
        grid=(m // tr,),
        in_specs=in_specs,
        out_specs=[pl.BlockSpec((tr, width), lambda i: (i, 0))] * len(sections),
        out_shape=[jax.ShapeDtypeStruct((m, width), F32)] * len(sections),
        compiler_params=_cp("parallel"),
        name="kv_transform",
    )(*([proj] * len(sections)), gains, *tables)
    return outs


def _retention_kernel(q_ref, k_ref, v_ref, rg_ref, s0_ref, lg_ref, gn_ref, c_ref, sa_ref, sb_ref,
                      y_ref, s_ref, st_ref, dec_ref, qd_ref, kd_ref, *, rows, c_true, hps):
    ci = pl.program_id(2)
    cc = RET_CHUNK

    @pl.when(ci == 0)
    def _():
        st_ref[...] = s0_ref[0]
        ii = _iota((cc, 1), 0).astype(F32)
        diff = ii - _iota((1, cc), 1).astype(F32)
        for h in range(hps):
            lg = lg_ref[h]
            dec_ref[h] = jnp.where(diff >= 0, jnp.exp(jnp.maximum(diff, 0.0) * lg), 0.0)
            qd_ref[h] = jnp.exp((ii + 1.0) * lg)
            kd_ref[h] = jnp.where(ii < c_true, jnp.exp((c_true - 1.0 - ii) * lg), 0.0)

    c, sa, sb = c_ref[...], sa_ref[...], sb_ref[...]
    for h in range(hps):
        cols = slice(h * HEAD_DIM, (h + 1) * HEAD_DIM)
        q = _pad_rows(_rope(q_ref[0, :, cols], c, sa, sb, HEAD_DIM // 2) * SCALE, cc)
        k = _pad_rows(_rope(k_ref[0, :, cols], c, sa, sb, HEAD_DIM // 2), cc)
        vb = _pad_rows(v_ref[0, :, cols], cc).astype(BF16)
        qb = q.astype(BF16)
        inner = _dot_nt(qb, k.astype(BF16)) * dec_ref[h]
        st = st_ref[h]
        o = _dot(inner.astype(BF16), vb) + _dot(qb, st.astype(BF16)) * qd_ref[h]
        kd = k * kd_ref[h]
        st_ref[h] = jnp.exp(c_true * lg_ref[h]) * st + _dot(kd.T.astype(BF16), vb)
        o = o[:rows]
        mu = jnp.mean(o, axis=-1, keepdims=True)
        var = jnp.mean(jnp.square(o - mu), axis=-1, keepdims=True)
        y = (o - mu) * lax.rsqrt(var + NORM_EPS) * gn_ref[h]
        rg = rg_ref[0, :, cols]
        y_ref[0, :, cols] = (y * (rg * jax.nn.sigmoid(rg))).astype(BF16)

    @pl.when(ci == pl.num_programs(2) - 1)
    def _():
        s_ref[0] = st_ref[...]


def retention(proj3, s0, log_g, gn_g, tables, c_true, seq_off=0):
    n, t, _ = proj3.shape
    rows = min(t, RET_CHUNK)
    nc = t // rows
    h = H_RET
    hps = H_RET
    ng = h // hps
    wid = hps * HEAD_DIM
    blk = lambda sec: pl.BlockSpec((1, rows, wid), functools.partial(lambda b, hg, c, sec: (b, c, sec * ng + hg), sec=sec))
    tab = pl.BlockSpec((rows, HEAD_DIM), lambda b, hg, c: (c, 0))
    y, s_new = pl.pallas_call(
        functools.partial(_retention_kernel, rows=rows, c_true=float(c_true), hps=hps),
        grid=(n, ng, nc),
        in_specs=[blk(0), blk(1), blk(2), blk(3),
                  pl.BlockSpec((1, hps, HEAD_DIM, HEAD_DIM), lambda b, hg, c: (seq_off + b, hg, 0, 0)),
                  pl.BlockSpec((hps, 1, HEAD_DIM), lambda b, hg, c: (hg, 0, 0)),
                  pl.BlockSpec((hps, 1, HEAD_DIM), lambda b, hg, c: (hg, 0, 0)),
                  tab, tab, tab],
        out_specs=[pl.BlockSpec((1, rows, wid), lambda b, hg, c: (b, c, hg)),
                   pl.BlockSpec((1, hps, HEAD_DIM, HEAD_DIM), lambda b, hg, c: (b, hg, 0, 0))],
        out_shape=[jax.ShapeDtypeStruct((n, t, h * HEAD_DIM), BF16),
                   jax.ShapeDtypeStruct((n, h, HEAD_DIM, HEAD_DIM), F32)],
        scratch_shapes=[pltpu.VMEM((hps, HEAD_DIM, HEAD_DIM), F32)] * 4,
        compiler_params=_cp("parallel", "parallel", "arbitrary"),
        name="retention",
    )(proj3, proj3, proj3, proj3, s0,
      jnp.broadcast_to(log_g[:, None, None], (h, 1, HEAD_DIM)), gn_g.reshape(h, 1, HEAD_DIM), *tables)
    return y, s_new


def _rows(start, size, stride):
    return pl.ds(start, size) if stride == 1 else pl.ds(start, size, stride=stride)


def _dil_prompt_kernel(q_ref, k_ref, v_ref, qg_ref, c_ref, sa_ref, sb_ref, o_ref,
                       qs_ref, m_ref, l_ref, acc_ref, pm_ref, pls_ref, pacc_ref, qc_ref, kc_ref, vc_ref, *, t):
    chunk = _tile(t, 512)
    for c in range(t // chunk):
        r = slice(c * chunk, (c + 1) * chunk)
        q = _rope(_rms(q_ref[0, r, :], qg_ref[...]), c_ref[r, :], sa_ref[r, :], sb_ref[r, :], ROPE_DIMS // 2)
        qs_ref[r, :] = q * SCALE

    def bias(bq, nk, first_key_dist):
        rel = first_key_dist + _iota((bq, nk), 0) - _iota((bq, nk), 1)
        return jnp.where(rel >= 0, jnp.where(rel <= KEY_BLOCK, 0.0, NEG), NEG)

    for pi, (window, dil) in enumerate(DIL_PATTERNS):
        length = t // dil
        bq = min(length, KEY_BLOCK)
        nk = bq if length == bq else 2 * bq
        for r in range(dil):
            rows, crow = _rows(r, length, dil), slice(r * length, (r + 1) * length)
            qc_ref[crow, :] = qs_ref[rows, :].astype(BF16)
            kc_ref[crow, :] = k_ref[0, rows, :].astype(BF16)
            vc_ref[crow, :] = v_ref[0, rows, :].astype(BF16)
        blocks = [(r * length + i * bq, r * length + max(i - 1, 0) * bq, i == 0 and nk > bq)
                  for r in range(dil) for i in range(length // bq)]
        band, lead = bias(bq, nk, nk - bq), bias(bq, nk, 0)
        s = jnp.concatenate([_dot_nt(qc_ref[q0:q0 + bq, :], kc_ref[k0:k0 + nk, :]) + (lead if first else band)
                             for q0, k0, first in blocks], axis=0)
        m = jnp.max(s, axis=-1, keepdims=True)
        pb = jnp.exp(s - m).astype(BF16)
        m = jnp.broadcast_to(m, (t, HEAD_DIM))
        ones = jnp.ones((nk, HEAD_DIM), BF16)
        l = jnp.concatenate([_dot(pb[j * bq:(j + 1) * bq], ones) for j in range(len(blocks))], axis=0)
        acc = jnp.concatenate([_dot(pb[j * bq:(j + 1) * bq], vc_ref[k0:k0 + nk, :])
                               for j, (q0, k0, first) in enumerate(blocks)], axis=0)
        if pi == 0:
            assert dil == 1
            m_ref[...], l_ref[...], acc_ref[...] = m, l, acc
        else:
            pm_ref[...], pls_ref[...], pacc_ref[...] = m, l, acc
            for r in range(dil):
                rows, crow = _rows(r, length, dil), slice(r * length, (r + 1) * length)
                m0, m1 = m_ref[rows, :], pm_ref[crow, :]
                mx = jnp.maximum(m0, m1)
                e0, e1 = jnp.exp(m0 - mx), jnp.exp(m1 - mx)
                m_ref[rows, :] = mx
                l_ref[rows, :] = e0 * l_ref[rows, :] + e1 * pls_ref[crow, :]
                acc_ref[rows, :] = e0 * acc_ref[rows, :] + e1 * pacc_ref[crow, :]

    for c in range(t // chunk):
        r = slice(c * chunk, (c + 1) * chunk)
        o_ref[0, r, :] = (acc_ref[r, :] / l_ref[r, :]).astype(BF16)


def dilated_prompt(proj3, q_cb, k3, v3, qg, tables):
    n, t, _ = proj3.shape
    for window, dil in DIL_PATTERNS:
        length = t // dil
        assert window // dil == KEY_BLOCK and t % dil == 0
        assert length % KEY_BLOCK == 0 or (length < KEY_BLOCK and length % SUBLANES == 0)
    tab = pl.BlockSpec((t, HEAD_DIM), lambda b, h: (0, 0))
    seq = lambda cb: pl.BlockSpec((1, t, HEAD_DIM), functools.partial(lambda b, h, cb: (b, 0, cb + h), cb=cb))
    return pl.pallas_call(
        functools.partial(_dil_prompt_kernel, t=t),
        grid=(n, H_DIL),
        in_specs=[seq(q_cb), seq(0), seq(0), pl.BlockSpec((1, HEAD_DIM), lambda b, h: (0, 0)), tab, tab, tab],
        out_specs=seq(0),
        out_shape=jax.ShapeDtypeStruct((n, t, H_DIL * HEAD_DIM), BF16),
        scratch_shapes=[pltpu.VMEM((t, HEAD_DIM), F32)] * 7 + [pltpu.VMEM((t, HEAD_DIM), BF16)] * 3,
        compiler_params=_cp("parallel", "parallel"),
        name="dilated_prompt",
    )(proj3, k3, v3, qg.reshape(1, HEAD_DIM), *tables)


def _dil_sample_kernel(q_ref, kr_ref, vr_ref, kf_ref, vf_ref, kn_ref, vn_ref, qg_ref, c_ref, sa_ref, sb_ref, o_ref,
                       *, lc, rc, ts):
    rows = q_ref.shape[1]
    far_window, dl = DIL_PATTERNS[-1]
    nj = lc // dl
    head = lambda a, h: a[h * rows:(h + 1) * rows]
    per_head = lambda t: jnp.concatenate([t[...]] * H_DIL, axis=0)
    x = jnp.concatenate([q_ref[0, :, h * HEAD_DIM:(h + 1) * HEAD_DIM] for h in range(H_DIL)], axis=0)
    x = _rope(_rms(x, qg_ref[...]), per_head(c_ref), per_head(sa_ref), per_head(sb_ref), ROPE_DIMS // 2)
    qb = (x * SCALE).astype(BF16)

    def log_weight(d, patterns):
        w = _dil_weight(d, patterns)
        return jnp.where(w > 0, jnp.log(jnp.maximum(w, 1.0)), NEG)

    hr = H_DIL * rows
    hbits, rbits, tbits = H_DIL.bit_length() - 1, rows.bit_length() - 1, ts.bit_length() - 1
    q_head = lambda w: jnp.right_shift(_iota((hr, w), 0), rbits)
    q_tok = lambda w: _iota((hr, w), 0) & (rows - 1)
    k_head = lambda w: _iota((hr, w), 1) & (H_DIL - 1)
    k_idx = lambda w: jnp.right_shift(_iota((hr, w), 1), hbits)
    wr = rc * H_DIL
    bias_r = jnp.where(q_head(wr) == k_head(wr), log_weight(rc + q_tok(wr) - k_idx(wr), DIL_PATTERNS[:-1]), NEG)
    wf = nj * ts * H_DIL
    jt = k_idx(wf)
    ok = jnp.where(q_head(wf) == k_head(wf), jnp.where((jt & (ts - 1)) == q_tok(wf), 1.0, 0.0), 0.0)
    bias_f = jnp.where(ok > 0, jnp.where(lc - dl * jnp.right_shift(jt, tbits) <= far_window, 0.0, NEG), NEG)
    bias_n = log_weight(_iota((rows, KEY_BLOCK), 0) - _iota((rows, KEY_BLOCK), 1), DIL_PATTERNS)
    new = lambda ref, h: _pad_rows(ref[0, :, h * HEAD_DIM:(h + 1) * HEAD_DIM], KEY_BLOCK).astype(BF16)

    kf = kf_ref[...].reshape(wf, HEAD_DIM).astype(BF16)
    vf = vf_ref[...].reshape(wf, HEAD_DIM).astype(BF16)
    s = jnp.concatenate([
        _dot_nt(qb, kr_ref[0].astype(BF16)) + bias_r,
        _dot_nt(qb, kf) + bias_f,
        jnp.concatenate([_dot_nt(head(qb, h), new(kn_ref, h)) + bias_n for h in range(H_DIL)], axis=0)], axis=1)
    p = jnp.exp(s - jnp.max(s, axis=-1, keepdims=True))
    pb = (p / jnp.sum(p, axis=-1, keepdims=True)).astype(BF16)
    o = _dot(pb[:, :wr], vr_ref[0].astype(BF16)) + _dot(pb[:, wr:wr + wf], vf)
    for h in range(H_DIL):
        oh = head(o, h) + _dot(head(pb, h)[:, wr + wf:], new(vn_ref, h))
        o_ref[0, :, h * HEAD_DIM:(h + 1) * HEAD_DIM] = oh.astype(BF16)


def dilated_sample(proj3, q_cb, kc, vc, seq_off, kn3, vn3, qg, tables, ts):
    n, rows, _ = proj3.shape
    lc = kc.shape[1]
    wid = H_DIL * HEAD_DIM
    far_window, dl = DIL_PATTERNS[-1]
    rc = min(lc, max(w for w, _ in DIL_PATTERNS[:-1]))
    assert (q_cb * HEAD_DIM) % wid == 0 and lc % dl == 0 and lc % rc == 0 and ts <= min(dl, rows)
    assert (ts * H_DIL) % SUBLANES == 0 and all(v & (v - 1) == 0 for v in (ts, rows, H_DIL))
    rows_view = lambda a: a.reshape(-1, lc * H_DIL, HEAD_DIM)
    group_view = lambda a: a.reshape(-1, lc // dl, dl * H_DIL, HEAD_DIM)
    tab = pl.BlockSpec((rows, HEAD_DIM), lambda b: (0, 0))
    recent = pl.BlockSpec((1, rc * H_DIL, HEAD_DIM), lambda b: (seq_off + b, lc // rc - 1, 0))
    far = pl.BlockSpec((None, lc // dl, ts * H_DIL, HEAD_DIM), lambda b: (seq_off + b, 0, 0, 0))
    new = pl.BlockSpec((1, rows, wid), lambda b: (b, 0, 0))
    return pl.pallas_call(
        functools.partial(_dil_sample_kernel, lc=lc, rc=rc, ts=ts),
        grid=(n,),
        in_specs=[pl.BlockSpec((1, rows, wid), lambda b: (b, 0, q_cb * HEAD_DIM // wid)), recent, recent, far, far,
                  new, new, pl.BlockSpec((1, HEAD_DIM), lambda b: (0, 0)), tab, tab, tab],
        out_specs=new,
        out_shape=jax.ShapeDtypeStruct((n, rows, wid), BF16),
        compiler_params=_cp("parallel"),
        name="dilated_sample",
    )(proj3, rows_view(kc), rows_view(vc), group_view(kc), group_view(vc), kn3, vn3, qg.reshape(1, HEAD_DIM), *tables)


def _nsa_q(q_ref, g, qg_ref, c_ref, sa_ref, sb_ref):
    x = jnp.concatenate([q_ref[0][:, (g * HPG + hl) * HEAD_DIM:(g * HPG + hl + 1) * HEAD_DIM] for hl in range(HPG)],
                        axis=0)
    per_head = lambda t: jnp.concatenate([t[...]] * HPG, axis=0)
    x = _rope(_rms(x, qg_ref[...]), per_head(c_ref), per_head(sa_ref), per_head(sb_ref), ROPE_DIMS // 2)
    return (x * SCALE).astype(BF16)


def _nsa_compressed(q4, kc, vc, qpos, nb):
    rows = qpos.shape[0]
    nbp = kc.shape[0]
    blk = _iota((rows, nbp), 1)
    cmask = jnp.where((blk + 1) * NSA_BLOCK - 1 <= qpos, 1.0, 0.0)
    cmask4 = jnp.concatenate([cmask] * HPG, axis=0)
    s = jnp.where(cmask4 > 0, _dot_nt(q4, kc.astype(BF16)), NEG)
    p = jnp.exp(s - jnp.max(s, axis=-1, keepdims=True)) * cmask4
    p = p / jnp.maximum(jnp.sum(p, axis=-1, keepdims=True), 1e-30)
    o_c = _dot(p.astype(BF16), vc.astype(BF16))
    imp = p[0:rows]
    for hl in range(1, HPG):
        imp = imp + p[hl * rows:(hl + 1) * rows]
    shift = NSA_BLOCK.bit_length() - 1
    if rows == nbp:
        nbr = -(-nb // SUBLANES) * SUBLANES
        blk_t = _iota((nbr, rows), 0)
        cur_t = jnp.right_shift(qpos[0:1, :] + _iota((1, rows), 1), shift)
        cand_t = blk_t < cur_t
        imp_t = jnp.where(cand_t, imp.T[:nbr], -jnp.inf)
        rank = jnp.zeros((nbr, rows), F32)
        for b in range(nb):
            row = imp_t[b:b + 1, :]
            tie = jnp.where(row == imp_t, jnp.where(blk_t > b, 1.0, 0.0), 0.0)
            rank = rank + jnp.where(row > imp_t, 1.0, tie)
        sel_t = jnp.where(cand_t, jnp.where(rank < min(NSA_TOPK, nb), 1.0, 0.0), 0.0)
        sel = _pad_rows(sel_t, nbp).T
    else:
        cand = blk < jnp.right_shift(qpos, shift)
        imp = jnp.where(cand, imp, -jnp.inf)
        rank = jnp.zeros((rows, nbp), F32)
        for b in range(nb):
            col = imp[:, b:b + 1]
            tie = jnp.where(col == imp, jnp.where(blk > b, 1.0, 0.0), 0.0)
            rank = rank + jnp.where(col > imp, 1.0, tie)
        sel = jnp.where(cand, jnp.where(rank < min(NSA_TOPK, nb), 1.0, 0.0), 0.0)
    sel = jnp.where(blk == jnp.right_shift(qpos, shift), 1.0, sel)
    return o_c, sel


def _nsa_combine(gates, g, o_c, o_s, o_w, rows):
    outs = []
    for hl in range(HPG):
        r = slice(hl * rows, (hl + 1) * rows)
        col = 3 * hl
        outs.append(gates[:, col:col + 1] * o_c[r] + gates[:, col + 1:col + 2] * o_s[r]
                    + gates[:, col + 2:col + 3] * o_w[r])
    return outs


def _nsa_prompt_kernel(q_ref, gt_ref, kc_ref, vc_ref, ks_ref, vs_ref, kw_ref, vw_ref, e_ref,
                       qg_ref, c_ref, sa_ref, sb_ref, o_ref, selx_ref, s_ref, mx_ref, ls_ref, acc_ref, *, tq, nb, t):
    qi = pl.program_id(2)
    q4 = _nsa_q(q_ref, 0, qg_ref, c_ref, sa_ref, sb_ref)
    qpos = qi * tq + _iota((tq, 1), 0)
    o_c, sel = _nsa_compressed(q4, kc_ref[0], vc_ref[0], qpos, nb)
    selx_ref[...] = _dot(sel.astype(BF16), e_ref[...])

    ck = _tile(t, 4 * KEY_BLOCK)
    nchunks = lax.div(qi * tq + (tq + ck - 1), ck)
    lane = _iota((tq, ck), 1)
    mx_ref[...] = jnp.full(mx_ref.shape, NEG, F32)

    def scores(c, _):
        off = pl.multiple_of(c * ck, ck)
        bias = jnp.where(off + lane <= qpos, (selx_ref[:, pl.ds(off, ck)] - 1.0) * -NEG, NEG)
        s = _dot_nt(q4, ks_ref[0, pl.ds(off, ck), :].astype(BF16))
        s = (s.reshape(HPG, tq, ck) + bias[None]).reshape(HPG * tq, ck)
        s_ref[:, pl.ds(off, ck)] = s
        part = s[:, :KEY_BLOCK]
        for j in range(1, ck // KEY_BLOCK):
            part = jnp.maximum(part, s[:, j * KEY_BLOCK:(j + 1) * KEY_BLOCK])
        mx_ref[...] = jnp.maximum(mx_ref[...], part)
        return 0

    lax.fori_loop(0, nchunks, scores, 0)
    m = jnp.max(mx_ref[...], axis=-1, keepdims=True)
    ls_ref[...] = jnp.zeros(ls_ref.shape, F32)
    acc_ref[...] = jnp.zeros(acc_ref.shape, F32)

    def values(c, _):
        off = pl.multiple_of(c * ck, ck)
        s = s_ref[:, pl.ds(off, ck)]
        p = jnp.exp(s - m)
        part = p[:, :KEY_BLOCK]
        for j in range(1, ck // KEY_BLOCK):
            part = part + p[:, j * KEY_BLOCK:(j + 1) * KEY_BLOCK]
        ls_ref[...] += part
        acc_ref[...] += _dot(p.astype(BF16), vs_ref[0, pl.ds(off, ck), :].astype(BF16))
        return 0

    lax.fori_loop(0, nchunks, values, 0)
    o_s = acc_ref[...] / jnp.maximum(jnp.sum(ls_ref[...], axis=-1, keepdims=True), 1e-30)

    span = min(NSA_WINDOW + KEY_BLOCK, t)
    first = jnp.clip(qi - NSA_WINDOW // KEY_BLOCK, 0, (t - span) // KEY_BLOCK)
    off = pl.multiple_of(first * KEY_BLOCK, KEY_BLOCK)
    d = qpos - (off + _iota((tq, span), 1))
    bias = jnp.where(d >= 0, jnp.where(d < NSA_WINDOW, 0.0, NEG), NEG)
    s = _dot_nt(q4, kw_ref[0, pl.ds(off, span), :].astype(BF16))
    s = (s.reshape(HPG, tq, span) + bias[None]).reshape(HPG * tq, span)
    p = jnp.exp(s - jnp.max(s, axis=-1, keepdims=True))
    o_w = _dot(p.astype(BF16), vw_ref[0, pl.ds(off, span), :].astype(BF16)) / jnp.sum(p, axis=-1, keepdims=True)
    gates = jax.nn.sigmoid(gt_ref[0])
    for hl, o in enumerate(_nsa_combine(gates, 0, o_c, o_s, o_w, tq)):
        o_ref[0, :, hl * HEAD_DIM:(hl + 1) * HEAD_DIM] = o.astype(BF16)


def nsa_prompt(proj3, gate_cb, kc3, vc3, ks3, vs3, kw3, vw3, expand, qg, tables, nb):
    n, t, _ = proj3.shape
    tq = KEY_BLOCK
    nbp = kc3.shape[1]
    gw = HPG * HEAD_DIM
    tab = pl.BlockSpec((tq, HEAD_DIM), lambda b, g, i: (i, 0))
    cmp_spec = pl.BlockSpec((1, nbp, HEAD_DIM), lambda b, g, i: (b, 0, g))
    kv = pl.BlockSpec((1, t, HEAD_DIM), lambda b, g, i: (b, 0, g))
    return pl.pallas_call(
        functools.partial(_nsa_prompt_kernel, tq=tq, nb=nb, t=t),
        grid=(n, G_NSA, t // tq),
        in_specs=[pl.BlockSpec((1, tq, gw), lambda b, g, i: (b, i, g)),
                  pl.BlockSpec((1, tq, HEAD_DIM), lambda b, g, i: (b, i, gate_cb + g)),
                  cmp_spec, cmp_spec, kv, kv, kv, kv,
                  pl.BlockSpec((nbp, t), lambda b, g, i: (0, 0)),
                  pl.BlockSpec((1, HEAD_DIM), lambda b, g, i: (0, 0)), tab, tab, tab],
        out_specs=pl.BlockSpec((1, tq, gw), lambda b, g, i: (b, i, g)),
        out_shape=jax.ShapeDtypeStruct((n, t, H_NSA * HEAD_DIM), BF16),
        scratch_shapes=[pltpu.VMEM((tq, t), F32), pltpu.VMEM((HPG * tq, t), F32)]
        + [pltpu.VMEM((HPG * tq, HEAD_DIM), F32)] * 3,
        compiler_params=_cp("parallel", "parallel", "arbitrary"),
        name="nsa_prompt",
    )(proj3, proj3, kc3, vc3, ks3, vs3, kw3, vw3, expand, qg.reshape(1, HEAD_DIM), *tables)


def _compress_rows_kernel(k_ref, v_ref, pk_ref, pv_ref, ko_ref, vo_ref):
    for x_ref, p_ref, o_ref in ((k_ref, pk_ref, ko_ref), (v_ref, pv_ref, vo_ref)):
        x = x_ref[0]
        nblk = x.shape[0] // NSA_BLOCK
        pw = jnp.concatenate([p_ref[...]] * nblk, axis=0)
        o_ref[0] = jnp.sum((x * pw).reshape(nblk, NSA_BLOCK, x.shape[1]), axis=1)


def compress_rows(k3, v3, pwk2, pwv2):
    n, t, w = k3.shape
    tr = _tile(t, 512)
    row = pl.BlockSpec((1, tr, w), lambda b, i: (b, i, 0))
    pw = pl.BlockSpec((NSA_BLOCK, w), lambda b, i: (0, 0))
    out = pl.BlockSpec((1, tr // NSA_BLOCK, w), lambda b, i: (b, i, 0))
    return pl.pallas_call(
        _compress_rows_kernel,
        grid=(n, t // tr),
        in_specs=[row, row, pw, pw],
        out_specs=[out, out],
        out_shape=[jax.ShapeDtypeStruct((n, t // NSA_BLOCK, w), F32)] * 2,
        compiler_params=_cp("parallel", "parallel"),
        name="compress_rows",
    )(k3, v3, pwk2, pwv2)


def _compress_pages_kernel(pt_ref, *refs, pps):
    k_refs, v_refs = refs[:pps], refs[pps:2 * pps]
    pk_ref, pv_ref, ko_ref, vo_ref = refs[2 * pps:]
    bpp = PAGE_SIZE // NSA_BLOCK
    prows = PAGE_SIZE * G_NSA
    for x_refs, p_ref, o_ref in ((k_refs, pk_ref, ko_ref), (v_refs, pv_ref, vo_ref)):
        pw = p_ref[...]
        outs = []
        for x_ref in x_refs:
            z = jnp.sum((x_ref[0] * pw).reshape(bpp, prows // bpp // SUBLANES, SUBLANES, HEAD_DIM), axis=1)
            per_g = [sum(z[:, s] for s in range(g, SUBLANES, G_NSA)) for g in range(G_NSA)]
            outs.append(jnp.concatenate(per_g, axis=1))
        o_ref[0] = jnp.concatenate(outs, axis=0)


def compress_pages(pool_k, pool_v, page_off, page_table, pwk, pwv, pps):
    n, n_pages = page_table.shape
    w = G_NSA * HEAD_DIM
    prows = PAGE_SIZE * G_NSA
    bpp = PAGE_SIZE // NSA_BLOCK
    assert SUBLANES % G_NSA == 0
    page_w = lambda p: jnp.repeat(jnp.tile(p, (bpp, 1)), G_NSA, axis=0)
    pwk2, pwv2 = page_w(pwk), page_w(pwv)
    page = lambda i: pl.BlockSpec((1, prows, HEAD_DIM),
                                  functools.partial(lambda b, c, pt, i: (page_off + pt[b, c * pps + i], 0, 0), i=i))
    pw = pl.BlockSpec((prows, HEAD_DIM), lambda b, c, pt: (0, 0))
    out = pl.BlockSpec((1, pps * bpp, w), lambda b, c, pt: (b, c, 0))
    return pl.pallas_call(
        functools.partial(_compress_pages_kernel, pps=pps),
        grid_spec=pltpu.PrefetchScalarGridSpec(
            num_scalar_prefetch=1, grid=(n, n_pages // pps),
            in_specs=[page(i) for i in range(pps)] * 2 + [pw, pw],
            out_specs=[out, out]),
        out_shape=[jax.ShapeDtypeStruct((n, n_pages * bpp, w), F32)] * 2,
        compiler_params=_cp("parallel", "parallel"),
        name="compress_pages",
    )(page_table, *([pool_k] * pps), *([pool_v] * pps), pwk2, pwv2)


def _nsa_sample_kernel(pt_ref, *refs, pps, past, rows, nb, lw):
    (q_ref, gt_ref, kc_ref, vc_ref), refs = refs[:4], refs[4:]
    kp_refs, vp_refs, refs = refs[:pps], refs[pps:2 * pps], refs[2 * pps:]
    (kn_ref, vn_ref, kwc_ref, vwc_ref, kwn_ref, vwn_ref, e_ref, qg_ref, c_ref, sa_ref, sb_ref,
     o_ref, q4_ref, sel_ref, oc_ref, m_ref, l_ref, acc_ref) = refs
    pc = pl.program_id(1)
    r4 = HPG * rows
    qpos = past + _iota((rows, 1), 0)
    lane = _iota((rows, KEY_BLOCK), 1)

    @pl.when(pc == 0)
    def _():
        for g in range(G_NSA):
            cols = slice(g * HEAD_DIM, (g + 1) * HEAD_DIM)
            q4 = _nsa_q(q_ref, g, qg_ref, c_ref, sa_ref, sb_ref)
            o_c, sel = _nsa_compressed(q4, kc_ref[0][:, cols], vc_ref[0][:, cols], qpos, nb)
            q4_ref[g] = q4
            sel_ref[g] = sel
            oc_ref[g] = o_c
            m0, l0, a0 = _softmax_init(r4)
            m_ref[g], l_ref[g], acc_ref[g] = m0, l0, a0

    nkeys = pps * PAGE_SIZE
    kpos = pc * nkeys + _iota((rows, nkeys), 1)
    scores, values = [], []
    for g in range(G_NSA):
        grp = pl.ds(g, PAGE_SIZE, stride=G_NSA)
        kb = jnp.concatenate([r[0, grp, :].astype(BF16) for r in kp_refs], axis=0)
        values.append(jnp.concatenate([r[0, grp, :].astype(BF16) for r in vp_refs], axis=0))
        selx = _dot(sel_ref[g].astype(BF16), e_ref[...])
        bias = jnp.where(kpos <= qpos, (selx - 1.0) * -NEG, NEG)
        s = _dot_nt(q4_ref[g], kb)
        scores.append((s.reshape(HPG, rows, nkeys) + bias[None]).reshape(r4, nkeys))
    s = jnp.concatenate(scores, axis=0)
    m_old = jnp.concatenate([m_ref[g] for g in range(G_NSA)], axis=0)
    m_new = jnp.maximum(m_old, jnp.max(s, axis=-1, keepdims=True))
    alpha = jnp.exp(m_old - m_new)
    p = jnp.exp(s - m_new)
    pb = p.astype(BF16)
    psum = jnp.sum(p, axis=-1, keepdims=True)
    for g in range(G_NSA):
        r = slice(g * r4, (g + 1) * r4)
        m_ref[g] = m_new[r]
        l_ref[g] = alpha[r] * l_ref[g] + psum[r]
        acc_ref[g] = alpha[r] * acc_ref[g] + _dot(pb[r], values[g])

    @pl.when(pc == pl.num_programs(1) - 1)
    def _():
        gates_all = jax.nn.sigmoid(gt_ref[0])
        tq = _iota((rows, KEY_BLOCK), 0)
        for g in range(G_NSA):
            cols = slice(g * HEAD_DIM, (g + 1) * HEAD_DIM)
            q4 = q4_ref[g]
            kb = _pad_rows(kn_ref[0][:, cols], KEY_BLOCK).astype(BF16)
            vb = _pad_rows(vn_ref[0][:, cols], KEY_BLOCK).astype(BF16)
            w = jnp.where(lane <= tq, 1.0, 0.0)
            _, l, acc = _softmax_step(_dot_nt(q4, kb), jnp.concatenate([w] * HPG, axis=0), vb,
                                      m_ref[g], l_ref[g], acc_ref[g])
            o_s = _softmax_done(l, acc)
            tqw = _iota((rows, lw), 0)
            d = tqw + lw - _iota((rows, lw), 1)
            w = jnp.where(d < NSA_WINDOW, jnp.where(past - d + tqw >= 0, 1.0, 0.0), 0.0)
            grp = pl.ds(g, lw, stride=G_NSA)
            carry = _softmax_step(_dot_nt(q4, kwc_ref[0, grp, :].astype(BF16)), jnp.concatenate([w] * HPG, axis=0),
                                  vwc_ref[0, grp, :].astype(BF16), *_softmax_init(r4))
            kb = _pad_rows(kwn_ref[0][:, cols], KEY_BLOCK).astype(BF16)
            vb = _pad_rows(vwn_ref[0][:, cols], KEY_BLOCK).astype(BF16)
            w = jnp.where(lane <= tq, 1.0, 0.0)
            _, l, acc = _softmax_step(_dot_nt(q4, kb), jnp.concatenate([w] * HPG, axis=0), vb, *carry)
            o_w = _softmax_done(l, acc)
            gates = gates_all[:, g * HEAD_DIM:(g + 1) * HEAD_DIM]
            for hl, o in enumerate(_nsa_combine(gates, g, oc_ref[g], o_s, o_w, rows)):
                col = (g * HPG + hl) * HEAD_DIM
                o_ref[0, :, col:col + HEAD_DIM] = o.astype(BF16)


def nsa_sample(proj3, gate_cb, kc3, vc3, pool_k, pool_v, page_off, page_table, kn3, vn3, kwc3, vwc3, seq_off,
               kwn3, vwn3, expand, qg, tables, pps):
    n, rows, _ = proj3.shape
    n_pages = page_table.shape[1]
    past = n_pages * PAGE_SIZE
    nbp = kc3.shape[1]
    w = G_NSA * HEAD_DIM
    lw = kwc3.shape[1] // G_NSA
    qw = H_NSA * HEAD_DIM
    r4 = HPG * rows
    fixed = lambda shape: pl.BlockSpec(shape, lambda b, c, pt: (0,) * len(shape))
    per_seq = lambda r, width, cb=0, off=0: pl.BlockSpec(
        (1, r, width), functools.partial(lambda b, c, pt, cb, off: (off + b, 0, cb), cb=cb, off=off))
    page = lambda i: pl.BlockSpec((1, PAGE_SIZE * G_NSA, HEAD_DIM),
                                  functools.partial(lambda b, c, pt, i: (page_off + pt[b, c * pps + i], 0, 0), i=i))
    win = per_seq(lw * G_NSA, HEAD_DIM, 0, seq_off)
    in_specs = ([per_seq(rows, qw), per_seq(rows, w, gate_cb // G_NSA), per_seq(nbp, w), per_seq(nbp, w)]
                + [page(i) for i in range(pps)] * 2
                + [per_seq(rows, w), per_seq(rows, w), win, win, per_seq(rows, w), per_seq(rows, w),
                   pl.BlockSpec((nbp, pps * PAGE_SIZE), lambda b, c, pt: (0, c)),
                   fixed((1, HEAD_DIM)), fixed((rows, HEAD_DIM)), fixed((rows, HEAD_DIM)), fixed((rows, HEAD_DIM))])
    return pl.pallas_call(
        functools.partial(_nsa_sample_kernel, pps=pps, past=past, rows=rows, nb=past // NSA_BLOCK, lw=lw),
        grid_spec=pltpu.PrefetchScalarGridSpec(
            num_scalar_prefetch=1, grid=(n, n_pages // pps),
            in_specs=in_specs,
            out_specs=pl.BlockSpec((1, rows, qw), lambda b, c, pt: (b, 0, 0)),
            scratch_shapes=[pltpu.VMEM((G_NSA, r4, HEAD_DIM), BF16), pltpu.VMEM((G_NSA, rows, nbp), F32),
                            pltpu.VMEM((G_NSA, r4, HEAD_DIM), F32), pltpu.VMEM((G_NSA, r4, 1), F32),
                            pltpu.VMEM((G_NSA, r4, 1), F32), pltpu.VMEM((G_NSA, r4, HEAD_DIM), F32)]),
        out_shape=jax.ShapeDtypeStruct((n, rows, qw), BF16),
        compiler_params=_cp("parallel", "arbitrary"),
        name="nsa_sample",
    )(page_table, proj3, proj3, kc3, vc3, *([pool_k] * pps), *([pool_v] * pps), kn3, vn3, kwc3, vwc3, kwn3, vwn3,
      expand, qg.reshape(1, HEAD_DIM), *tables)


def _lru_kernel(x_ref, y_ref, hist_ref, h0_ref, cw_ref, cb_ref, wa_ref, ba_ref, wi_ref, bi_ref, sp_ref,
                o_ref, tail_ref, a_ref, b_ref, *, cps):
    t = x_ref.shape[1]
    bw = D_RNN // RNN_BLOCKS
    row8 = _iota((SUBLANES, bw), 0)
    for k in range(cps):
        cols = slice(k * bw, (k + 1) * bw)
        x = x_ref[0, :, cols]
        hist = hist_ref[0, :, cols]
        xc = cb_ref[:, cols] + x * cw_ref[CONV_W - 1:CONV_W, cols]
        for s in range(1, CONV_W):
            sh = pltpu.roll(x, s, 0)
            head = jnp.where(row8 >= s, sh[:SUBLANES], pltpu.roll(hist, (SUBLANES - (CONV_W - 1) + s) % SUBLANES, 0))
            sh = head if t == SUBLANES else jnp.concatenate([head, sh[SUBLANES:]], axis=0)
            xc = xc + sh * cw_ref[CONV_W - 1 - s:CONV_W - s, cols]
        xb = xc.astype(BF16)
        r = jax.nn.sigmoid(_dot(xb, wa_ref[k]) + ba_ref[:, cols])
        gi = jax.nn.sigmoid(_dot(xb, wi_ref[k]) + bi_ref[:, cols])
        log_a = -LRU_C * r * sp_ref[:, cols]
        th = jnp.tanh(log_a)
        a_ref[:, cols] = jnp.exp(log_a)
        b_ref[:, cols] = jnp.sqrt(-2.0 * th / (1.0 - th)) * (gi * xc)

    wid = cps * bw
    rowsw = _iota((SUBLANES, wid), 0)

    def body(gidx, hc):
        off = pl.multiple_of(gidx * SUBLANES, SUBLANES)
        a = a_ref[pl.ds(off, SUBLANES), :]
        b = b_ref[pl.ds(off, SUBLANES), :]
        for s in (1, 2, 4):
            ok = rowsw >= s
            b = jnp.where(ok, a * pltpu.roll(b, s, 0) + b, b)
            a = jnp.where(ok, a * pltpu.roll(a, s, 0), a)
        hrows = a * hc + b
        a_ref[pl.ds(off, SUBLANES), :] = hrows
        return jnp.broadcast_to(hrows[SUBLANES - 1:SUBLANES, :], (SUBLANES, wid))

    lax.fori_loop(0, t // SUBLANES, body, jnp.broadcast_to(h0_ref[0], (SUBLANES, wid)))
    tail_ref[0] = a_ref[t - SUBLANES:, :]
    chunk = _tile(t, 512)
    for c in range(t // chunk):
        rws = slice(c * chunk, (c + 1) * chunk)
        o_ref[0, rws, :] = (jax.nn.gelu(y_ref[0, rws, :], approximate=True) * a_ref[rws, :]).astype(BF16)


def rglru(proj3, x_cb, y_cb, hist8, h0, conv_w, conv_b, w_a, b_a, w_i, b_i, softplus_neg_lam):
    n, t, _ = proj3.shape
    bw = D_RNN // RNN_BLOCKS
    cps = 4 if t > SUBLANES else RNN_BLOCKS
    wid = cps * bw
    assert (x_cb * HEAD_DIM) % wid == 0 and (y_cb * HEAD_DIM) % wid == 0 and bw == HEAD_DIM
    vec = lambda a: a.reshape(1, D_RNN)
    vspec = pl.BlockSpec((1, wid), lambda b, k: (0, k))
    mspec = pl.BlockSpec((cps, bw, bw), lambda b, k: (k, 0, 0))
    return pl.pallas_call(
        functools.partial(_lru_kernel, cps=cps),
        grid=(n, RNN_BLOCKS // cps),
        in_specs=[pl.BlockSpec((1, t, wid), lambda b, k: (b, 0, x_cb * HEAD_DIM // wid + k)),
                  pl.BlockSpec((1, t, wid), lambda b, k: (b, 0, y_cb * HEAD_DIM // wid + k)),
                  pl.BlockSpec((1, SUBLANES, wid), lambda b, k: (b, 0, k)),
                  pl.BlockSpec((1, 1, wid), lambda b, k: (b, 0, k)),
                  pl.BlockSpec((CONV_W, wid), lambda b, k: (0, k)),
                  vspec, mspec, vspec, mspec, vspec, vspec],
        out_specs=[pl.BlockSpec((1, t, wid), lambda b, k: (b, 0, k)),
                   pl.BlockSpec((1, SUBLANES, wid), lambda b, k: (b, 0, k))],
        out_shape=[jax.ShapeDtypeStruct((n, t, D_RNN), BF16), jax.ShapeDtypeStruct((n, SUBLANES, D_RNN), F32)],
        scratch_shapes=[pltpu.VMEM((t, wid), F32), pltpu.VMEM((t, wid), F32)],
        compiler_params=_cp("parallel", "parallel"),
        name="rglru",
    )(proj3, proj3, hist8, h0.reshape(n, 1, D_RNN), conv_w, vec(conv_b),
      w_a.astype(BF16), vec(b_a), w_i.astype(BF16), vec(b_i), vec(softplus_neg_lam))


def _rope_tables(pos, n_rot, theta):
    half = n_rot // 2
    inv = 1.0 / (theta ** (jnp.arange(half, dtype=F32) * (2.0 / n_rot)))
    ang = pos.astype(F32)[:, None] * inv[None, :]
    cos, sin = jnp.cos(ang), jnp.sin(ang)
    p = pos.shape[0]
    rest = HEAD_DIM - n_rot
    zh = jnp.zeros((p, half), F32)
    c = jnp.concatenate([cos, cos, jnp.ones((p, rest), F32)], axis=1)
    sa = jnp.concatenate([-sin, zh, jnp.zeros((p, rest), F32)], axis=1)
    sb = jnp.concatenate([zh, sin, jnp.zeros((p, rest), F32)], axis=1)
    return c, sa, sb


def _block_expand(nbp, length):
    return (jnp.arange(length, dtype=jnp.int32)[None, :] // NSA_BLOCK
            == jnp.arange(nbp, dtype=jnp.int32)[:, None]).astype(BF16)


def _pad_axis(a, axis, size):
    pad = [(0, 0)] * a.ndim
    pad[axis] = (0, size - a.shape[axis])
    return jnp.pad(a, pad)


def _odd_w_in(w):
    qw = H_NSA * HEAD_DIM
    qkv = qw + 6 * G_NSA * HEAD_DIM
    ng = 3 * H_NSA
    gate = w[:, qkv:qkv + ng]
    per = 3 * HPG
    gcols = [_pad_axis(gate[:, g * per:(g + 1) * per], 1, LANES) for g in range(G_NSA)]
    out = jnp.concatenate([w[:, :qw], w[:, qkv + ng:], w[:, qw:qkv]] + gcols, axis=1)
    return _pad_axis(out, 1, -(-out.shape[1] // 1024) * 1024)


def kernel(x_prompt, x_sample, mem_prompt, state_ret, cache_dil_k, cache_dil_v, cache_nsa_cmp_k, cache_nsa_cmp_v, cache_nsa_slc_k, cache_nsa_slc_v, cache_nsa_win_k, cache_nsa_win_v, state_lru_h, state_lru_conv, cache_mem_k, cache_mem_v, page_table, ev_w_in, ret_gn_g, dil_q_norm_g, dil_k_norm_g, ev_w_out, od_w_in, nsa_q_norm_g, nsa_k_norm_g, nsa_pw_k, nsa_pw_v, lru_conv_w, lru_conv_b, lru_w_a, lru_b_a, lru_w_i, lru_b_i, lru_lambda, od_w_out, norm_mix_g, norm_mem_g, norm_mlp_g, mem_norm_g, mem_w_q, mem_w_k, mem_w_v, mem_q_norm_g, mem_k_norm_g, mem_w_o, mlp_w1, mlp_w2):
    nb_p, t_p, d = x_prompt.shape
    nb_s, t_s, _ = x_sample.shape
    depth = norm_mix_g.shape[0]
    n_pages = page_table.shape[1]
    past = n_pages * PAGE_SIZE
    rs = SAMPLE_ROWS
    assert t_p % RET_CHUNK == 0 and t_s <= rs and past % PAGE_SIZE == 0 and t_s < NSA_BLOCK
    assert cache_nsa_win_k.shape[2] % KEY_BLOCK == 0 and cache_dil_k.shape[2] % KEY_BLOCK == 0
    hd = HEAD_DIM

    xp = x_prompt.reshape(nb_p * t_p, d)
    xs = _pad_axis(x_sample, 1, rs).reshape(nb_s * rs, d)
    mem2 = mem_prompt.reshape(-1, d)
    n_mem = mem_prompt.shape[1]

    pos_p = jnp.arange(t_p, dtype=jnp.int32)
    pos_s = past + jnp.arange(rs, dtype=jnp.int32)
    tile_s = lambda tabs: tuple(jnp.tile(tb, (nb_s, 1)) for tb in tabs)
    rope_p, rope_s = _rope_tables(pos_p, ROPE_DIMS, ROPE_THETA), _rope_tables(pos_s, ROPE_DIMS, ROPE_THETA)
    ret_p, ret_s = _rope_tables(pos_p, hd, RET_THETA), _rope_tables(pos_s, hd, RET_THETA)
    log_g = jnp.log1p(-jnp.exp2(-5.0 - jnp.arange(H_RET, dtype=F32)))

    ev_w_in_b, ev_w_out_b, od_w_out_b = ev_w_in.astype(BF16), ev_w_out.astype(BF16), od_w_out.astype(BF16)
    od_w_in_b = jnp.stack([_odd_w_in(od_w_in[o].astype(BF16)) for o in range(od_w_in.shape[0])])
    mem_wq_b, mem_wo_b = mem_w_q.astype(BF16), mem_w_o.astype(BF16)
    mem_wkv_b = jnp.concatenate([mem_w_k, mem_w_v], axis=2).astype(BF16)

    outs = {k: [] for k in ("ret_p", "ret_s", "dk_p", "dv_p", "dk_s", "dv_s", "ck_p", "cv_p", "ck_s", "cv_s",
                            "sk_p", "sv_p", "sk_s", "sv_s", "wk_p", "wv_p", "wk_s", "wv_s", "lh_p", "lh_s",
                            "lc_p", "lc_s", "mk_p", "mv_p")}
    new_rows = lambda a, w: a.reshape(nb_s, rs, w)[:, :t_s]

    for layer in range(depth):
        if layer % 2 == 0:
            e = layer // 2
            w_in, w_out, wl = ev_w_in_b, ev_w_out_b, e
            hw = H_RET * hd
            ones = jnp.ones((1, hd), F32)
            gains = jnp.concatenate([dil_k_norm_g[e][None], ones], axis=0)
            sections = [(5, "norm_rope"), (6, "copy")]
            proj = rms_matmul(xp, norm_mix_g[layer], w_in, wl, 1024)
            proj3 = proj.reshape(nb_p, t_p, -1)
            y_ret, s_new = retention(proj3, jnp.zeros((nb_p, H_RET, hd, hd), F32), log_g, ret_gn_g[e], ret_p, RET_CHUNK)
            dk, dv = kv_transform(proj, sections, gains, rope_p, H_DIL, t_p)
            y_dil = dilated_prompt(proj3, 4 * H_RET, dk.reshape(nb_p, t_p, hw), dv.reshape(nb_p, t_p, hw),
                                   dil_q_norm_g[e], rope_p)
            xp = matmul_res(y_ret.reshape(-1, hw), y_dil.reshape(-1, hw), w_out, wl, xp)
            keep = min(DIL_PATTERNS[-1][0], t_p)
            outs["ret_p"].append(s_new)
            outs["dk_p"].append(dk.reshape(nb_p, t_p, H_DIL, hd)[:, t_p - keep:])
            outs["dv_p"].append(dv.reshape(nb_p, t_p, H_DIL, hd)[:, t_p - keep:])
            proj = rms_matmul(xs, norm_mix_g[layer], w_in, wl, 1024)
            proj3 = proj.reshape(nb_s, rs, -1)
            y_ret, s_new = retention(proj3, state_ret.reshape(-1, H_RET, hd, hd), log_g, ret_gn_g[e], ret_s, t_s,
                                     seq_off=e * nb_s)
            dk, dv = kv_transform(proj, sections, gains, tile_s(rope_s), H_DIL, nb_s * rs)
            lc = cache_dil_k.shape[2]
            y_dil = dilated_sample(proj3, 4 * H_RET, cache_dil_k.reshape(-1, lc, H_DIL, hd),
                                   cache_dil_v.reshape(-1, lc, H_DIL, hd), e * nb_s, dk.reshape(nb_s, rs, hw),
                                   dv.reshape(nb_s, rs, hw), dil_q_norm_g[e], rope_s, t_s)
            xs = matmul_res(y_ret.reshape(-1, hw), y_dil.reshape(-1, hw), w_out, wl, xs)
            outs["ret_s"].append(s_new)
            outs["dk_s"].append(new_rows(dk, hw).reshape(nb_s, t_s, H_DIL, hd))
            outs["dv_s"].append(new_rows(dv, hw).reshape(nb_s, t_s, H_DIL, hd))
        else:
            o = layer // 2
            w_in, w_out, wl = od_w_in_b, od_w_out_b, o
            qw = H_NSA * hd
            gw = G_NSA * hd
            x_cb = qw // hd
            y_cb = x_cb + D_RNN // hd
            kv_cb = (qw + 2 * D_RNN) // gw
            gate_cb = (qw + 2 * D_RNN + 6 * gw) // hd
            ones = jnp.ones((1, hd), F32)
            kg = nsa_k_norm_g[o]
            gains = jnp.concatenate([kg[0:1], ones, kg[1:2], ones, kg[2:3], ones], axis=0)
            sections = [(kv_cb + i, "norm_rope" if i % 2 == 0 else "copy") for i in range(6)]
            pwk2 = jnp.concatenate([nsa_pw_k[o]] * G_NSA, axis=1)
            pwv2 = jnp.concatenate([nsa_pw_v[o]] * G_NSA, axis=1)
            sp = jax.nn.softplus(-lru_lambda[o].astype(F32))
            lru_w = (lru_conv_w[o], lru_conv_b[o], lru_w_a[o], lru_b_a[o], lru_w_i[o], lru_b_i[o], sp)
            proj = rms_matmul(xp, norm_mix_g[layer], w_in, wl, 1024)
            proj3 = proj.reshape(nb_p, t_p, -1)
            kcm, vcm, ksl, vsl, kwn, vwn = kv_transform(proj, sections, gains, rope_p, G_NSA, t_p)
            r3 = lambda a: a.reshape(nb_p, t_p, gw)
            nb = t_p // NSA_BLOCK
            nbp = -(-nb // LANES) * LANES
            kc, vc = compress_rows(r3(kcm), r3(vcm), pwk2, pwv2)
            o_nsa = nsa_prompt(proj3, gate_cb, _pad_axis(kc, 1, nbp), _pad_axis(vc, 1, nbp), r3(ksl), r3(vsl),
                               r3(kwn), r3(vwn), _block_expand(nbp, t_p), nsa_q_norm_g[o], rope_p, nb)
            y_rnn, tail = rglru(proj3, x_cb, y_cb, jnp.zeros((nb_p, SUBLANES, D_RNN), F32),
                                jnp.zeros((nb_p, D_RNN), F32), *lru_w)
            xp = matmul_res(o_nsa.reshape(-1, qw), y_rnn.reshape(-1, D_RNN), w_out, wl, xp)
            r4 = lambda a: a.reshape(nb_p, t_p, G_NSA, hd)
            keep = min(NSA_WINDOW, t_p)
            for key, val in (("ck_p", r4(kcm)), ("cv_p", r4(vcm)), ("sk_p", r4(ksl)), ("sv_p", r4(vsl)),
                             ("wk_p", r4(kwn)[:, t_p - keep:]), ("wv_p", r4(vwn)[:, t_p - keep:])):
                outs[key].append(val)
            outs["lh_p"].append(tail[:, SUBLANES - 1])
            outs["lc_p"].append(proj3[:, t_p - (CONV_W - 1):, x_cb * hd:x_cb * hd + D_RNN])
            proj = rms_matmul(xs, norm_mix_g[layer], w_in, wl, 1024)
            proj3 = proj.reshape(nb_s, rs, -1)
            kcm, vcm, ksl, vsl, kwn, vwn = kv_transform(proj, sections, gains, tile_s(rope_s), G_NSA, nb_s * rs)
            r3 = lambda a: a.reshape(nb_s, rs, gw)
            nph = cache_nsa_cmp_k.shape[1]
            pool = lambda a: a.reshape(-1, PAGE_SIZE * G_NSA, hd)
            pps = math.gcd(n_pages, 32)
            nb = (past + t_s) // NSA_BLOCK
            nbp = -(-nb // LANES) * LANES
            kc, vc = compress_pages(pool(cache_nsa_cmp_k), pool(cache_nsa_cmp_v), o * nph, page_table,
                                    nsa_pw_k[o], nsa_pw_v[o], pps)
            lw = cache_nsa_win_k.shape[2]
            o_nsa = nsa_sample(proj3, gate_cb, _pad_axis(kc, 1, nbp), _pad_axis(vc, 1, nbp),
                               pool(cache_nsa_slc_k), pool(cache_nsa_slc_v), o * nph, page_table, r3(ksl), r3(vsl),
                               cache_nsa_win_k.reshape(-1, lw * G_NSA, hd), cache_nsa_win_v.reshape(-1, lw * G_NSA, hd),
                               o * nb_s, r3(kwn), r3(vwn), _block_expand(nbp, past), nsa_q_norm_g[o], rope_s, pps)
            hist8 = _pad_axis(state_lru_conv[o].astype(F32), 1, SUBLANES)
            y_rnn, tail = rglru(proj3, x_cb, y_cb, hist8, state_lru_h[o].astype(F32), *lru_w)
            xs = matmul_res(o_nsa.reshape(-1, qw), y_rnn.reshape(-1, D_RNN), w_out, wl, xs)
            for key, val in (("ck_s", kcm), ("cv_s", vcm), ("sk_s", ksl), ("sv_s", vsl), ("wk_s", kwn), ("wv_s", vwn)):
                outs[key].append(new_rows(val, gw).reshape(nb_s, t_s, G_NSA, hd))
            outs["lh_s"].append(tail[:, t_s - 1])
            lx = proj3[:, :t_s, x_cb * hd:x_cb * hd + D_RNN]
            outs["lc_s"].append(jnp.concatenate([state_lru_conv[o].astype(F32), lx], axis=1)[:, -(CONV_W - 1):])

        dm = H_MEM * hd
        mkv = rms_matmul(mem2, mem_norm_g[layer], mem_wkv_b, layer, dm)
        gains = jnp.concatenate([mem_k_norm_g[layer][None], jnp.ones((1, hd), F32)], axis=0)
        mk, mv = kv_transform(mkv, [(0, "norm"), (1, "copy")], gains, rope_p, H_MEM, n_mem)
        mk3, mv3 = mk.reshape(nb_p, n_mem, dm), mv.reshape(nb_p, n_mem, dm)
        outs["mk_p"].append(mk3.reshape(nb_p, n_mem, H_MEM, hd))
        outs["mv_p"].append(mv3.reshape(nb_p, n_mem, H_MEM, hd))
        xp = mem_attn_block(xp, norm_mem_g[layer], mem_wq_b, mem_q_norm_g[layer], mk3, mv3, mem_wo_b, layer, t_p)
        xs = mem_attn_block(xs, norm_mem_g[layer], mem_wq_b, mem_q_norm_g[layer],
                            cache_mem_k.reshape(-1, n_mem * H_MEM, hd), cache_mem_v.reshape(-1, n_mem * H_MEM, hd),
                            mem_wo_b, layer, rs, seq_off=layer * nb_s)
        xp = mlp_block(xp, norm_mlp_g[layer], mlp_w1, mlp_w2, layer)
        xs = mlp_block(xs, norm_mlp_g[layer], mlp_w1, mlp_w2, layer)

    st = lambda k: jnp.stack(outs[k])
    return (xp.reshape(nb_p, t_p, d), xs.reshape(nb_s, rs, d)[:, :t_s], st("ret_p"), st("ret_s"),
            st("dk_p"), st("dv_p"), st("dk_s"), st("dv_s"),
            st("ck_p"), st("cv_p"), st("ck_s"), st("cv_s"),
            st("sk_p"), st("sv_p"), st("sk_s"), st("sv_s"),
            st("wk_p"), st("wv_p"), st("wk_s"), st("wv_s"),
            st("lh_p"), st("lh_s"), st("lc_p"), st("lc_s"),
            st("mk_p"), st("mv_p"))
```

```python
import functools
import math

import jax
import jax.numpy as jnp
from jax import lax
from jax.experimental import pallas as pl
from jax.experimental.pallas import tpu as pltpu

F32 = jnp.float32
BF16 = jnp.bfloat16

HEAD_DIM = 128
ROPE_DIMS = HEAD_DIM // 4
ROPE_THETA = 500000.0
NORM_EPS = 1e-6
H_RET = 8
RET_CHUNK = 128
RET_THETA = 10000.0
H_DIL = 8
DIL_PATTERNS = ((128, 1), (512, 4), (2048, 16))
H_NSA = 8
G_NSA = 2
HPG = H_NSA // G_NSA
NSA_BLOCK = 64
NSA_TOPK = 16
NSA_WINDOW = 512
D_RNN = 1024
RNN_BLOCKS = 8
CONV_W = 4
LRU_C = 8.0
H_MEM = 4
PAGE_SIZE = 128

LANES = 128
SUBLANES = 8
KEY_BLOCK = 128
SAMPLE_ROWS = 8
VMEM_LIMIT = 48 * 1024 * 1024
MLP_VMEM_LIMIT = 56 * 1024 * 1024
NEG = -1e30
SCALE = HEAD_DIM ** -0.5


def _cp(*sem):
    return pltpu.CompilerParams(dimension_semantics=sem, vmem_limit_bytes=VMEM_LIMIT)


def _tile(n, pref):
    t = min(n, pref)
    while n % t:
        t -= SUBLANES
    return t


def _rms(x, g):
    return x * lax.rsqrt(jnp.mean(x * x, axis=-1, keepdims=True) + NORM_EPS) * g


def _rope(x, c, sa, sb, half):
    return x * c + pltpu.roll(x, LANES - half, 1) * sa + pltpu.roll(x, half, 1) * sb


def _dot(a, b):
    return jnp.dot(a, b, preferred_element_type=F32)


def _dot_nt(a, b):
    return lax.dot_general(a, b, (((1,), (1,)), ((), ())), preferred_element_type=F32)


def _pad_rows(x, rows):
    if x.shape[0] == rows:
        return x
    return jnp.concatenate([x, jnp.zeros((rows - x.shape[0], x.shape[1]), x.dtype)], axis=0)


def _softmax_step(s, w, vb, m, l, acc):
    s = jnp.where(w > 0, s, NEG)
    m_new = jnp.maximum(m, jnp.max(s, axis=-1, keepdims=True))
    alpha = jnp.exp(m - m_new)
    p = jnp.exp(s - m_new) * w
    l = alpha * l + jnp.sum(p, axis=-1, keepdims=True)
    acc = alpha * acc + _dot(p.astype(BF16), vb)
    return m_new, l, acc


def _softmax_init(rows):
    return (jnp.full((rows, 1), NEG, F32), jnp.zeros((rows, 1), F32), jnp.zeros((rows, HEAD_DIM), F32))


def _softmax_done(l, acc):
    return acc / jnp.maximum(l, 1e-30)


def _dil_weight(d, patterns=DIL_PATTERNS):
    w = jnp.zeros(d.shape, F32)
    for window, dil in patterns:
        hit = jnp.where(d <= window, 1.0, 0.0)
        if dil > 1:
            rem = (d & (dil - 1)) if dil & (dil - 1) == 0 else lax.rem(d, dil)
            hit = jnp.where(rem == 0, hit, 0.0)
        w = w + hit
    return jnp.where(d >= 0, w, 0.0)


def _iota(shape, dim):
    return lax.broadcasted_iota(jnp.int32, shape, dim)


def _rms_matmul_kernel(x_ref, g_ref, w_ref, o_ref, xn_ref):
    @pl.when(pl.program_id(1) == 0)
    def _():
        xn_ref[...] = _rms(x_ref[...], g_ref[...]).astype(BF16)

    o_ref[...] = _dot(xn_ref[...], w_ref[...])


def rms_matmul(x, g, w, layer, tn):
    m, d = x.shape
    n = w.shape[2]
    tm = _tile(m, 1024)
    return pl.pallas_call(
        _rms_matmul_kernel,
        grid=(m // tm, n // tn),
        in_specs=[pl.BlockSpec((tm, d), lambda i, j: (i, 0)),
                  pl.BlockSpec((1, d), lambda i, j: (0, 0)),
                  pl.BlockSpec((None, d, tn), lambda i, j: (layer, 0, j))],
        out_specs=pl.BlockSpec((tm, tn), lambda i, j: (i, j)),
        out_shape=jax.ShapeDtypeStruct((m, n), F32),
        scratch_shapes=[pltpu.VMEM((tm, d), BF16)],
        compiler_params=_cp("parallel", "arbitrary"),
        name="rms_matmul",
    )(x, g.reshape(1, d), w)


def _matmul_res_kernel(a1_ref, a2_ref, w1_ref, w2_ref, r_ref, o_ref):
    o_ref[...] = r_ref[...] + _dot(a1_ref[...], w1_ref[...]) + _dot(a2_ref[...], w2_ref[...])


def matmul_res(a1, a2, w, layer, res):
    m, k1 = a1.shape
    k2 = a2.shape[1]
    n = w.shape[2]
    assert k1 == k2 and w.shape[1] == k1 + k2
    tm = _tile(m, 1024)
    tn = _tile(n, 1024)
    return pl.pallas_call(
        _matmul_res_kernel,
        grid=(m // tm, n // tn),
        in_specs=[pl.BlockSpec((tm, k1), lambda i, j: (i, 0)),
                  pl.BlockSpec((tm, k2), lambda i, j: (i, 0)),
                  pl.BlockSpec((None, k1, tn), lambda i, j: (layer, 0, j)),
                  pl.BlockSpec((None, k2, tn), lambda i, j: (layer, 1, j)),
                  pl.BlockSpec((tm, tn), lambda i, j: (i, j))],
        out_specs=pl.BlockSpec((tm, tn), lambda i, j: (i, j)),
        out_shape=jax.ShapeDtypeStruct((m, n), F32),
        compiler_params=_cp("parallel", "parallel"),
        name="matmul_res",
    )(a1, a2, w, w, res)


def _mlp_kernel(x_ref, g_ref, w1_ref, w2_ref, o_ref, xn_ref):
    @pl.when(pl.program_id(1) == 0)
    def _():
        x = x_ref[...]
        xn_ref[...] = _rms(x, g_ref[...]).astype(BF16)
        o_ref[...] = x

    hid = jnp.maximum(_dot(xn_ref[...], w1_ref[...].astype(BF16)), 0.0)
    o_ref[...] += _dot((hid * hid).astype(BF16), w2_ref[...].astype(BF16))


def mlp_block(x, g, w1, w2, layer):
    m, d = x.shape
    ff = w1.shape[2]
    tm = _tile(m, 1024)
    tf = _tile(ff, 512)
    once = pl.Buffered(1)
    return pl.pallas_call(
        _mlp_kernel,
        grid=(m // tm, ff // tf),
        in_specs=[pl.BlockSpec((tm, d), lambda i, f: (i, 0)),
                  pl.BlockSpec((1, d), lambda i, f: (0, 0)),
                  pl.BlockSpec((None, d, tf), lambda i, f: (layer, 0, f)),
                  pl.BlockSpec((None, tf, d), lambda i, f: (layer, f, 0))],
        out_specs=pl.BlockSpec((tm, d), lambda i, f: (i, 0), pipeline_mode=once),
        out_shape=jax.ShapeDtypeStruct((m, d), F32),
        scratch_shapes=[pltpu.VMEM((tm, d), BF16)],
        compiler_params=pltpu.CompilerParams(dimension_semantics=("parallel", "arbitrary"),
                                             vmem_limit_bytes=MLP_VMEM_LIMIT),
        name="mlp_block",
    )(x, g.reshape(1, d), w1, w2)


def _mem_attn_kernel(x_ref, g_ref, wq_ref, qg_ref, k_ref, v_ref, wo_ref, o_ref, oh_ref, *, nb, seg, by_row):
    x = x_ref[...]
    q = _dot(_rms(x, g_ref[...]).astype(BF16), wq_ref[...])
    qg = qg_ref[...]
    n_mem = k_ref.shape[1] // H_MEM if by_row else k_ref.shape[1]
    qh = [(_rms(q[:, h * HEAD_DIM:(h + 1) * HEAD_DIM], qg) * SCALE).astype(BF16) for h in range(H_MEM)]
    pairs = [(b, h) for b in range(nb) for h in range(H_MEM)]

    def mem_head(ref, b, h):
        if by_row:
            return ref[b, pl.ds(h, n_mem, stride=H_MEM), :].astype(BF16)
        return ref[b, :, h * HEAD_DIM:(h + 1) * HEAD_DIM].astype(BF16)

    s = jnp.concatenate([_dot_nt(qh[h][b * seg:(b + 1) * seg], mem_head(k_ref, b, h)) for b, h in pairs], axis=0)
    p = jnp.exp(s - jnp.max(s, axis=-1, keepdims=True))
    pb = (p / jnp.sum(p, axis=-1, keepdims=True)).astype(BF16)
    for j, (b, h) in enumerate(pairs):
        o = _dot(pb[j * seg:(j + 1) * seg], mem_head(v_ref, b, h))
        oh_ref[b * seg:(b + 1) * seg, h * HEAD_DIM:(h + 1) * HEAD_DIM] = o.astype(BF16)
    o_ref[...] = x + _dot(oh_ref[...], wo_ref[...])


def mem_attn_block(x, g, wq, qg, k, v, wo, layer, rows_per_seq, seq_off=0):
    m, d = x.shape
    dm = H_MEM * HEAD_DIM
    by_row = k.shape[2] == HEAD_DIM
    kv_rows, kv_w = k.shape[1], k.shape[2]
    if rows_per_seq >= 128:
        tm = _tile(rows_per_seq, 512)
        nb, seg = 1, tm
        kv_map = lambda i: (seq_off + i // (rows_per_seq // tm), 0, 0)
    else:
        nb = _tile(m // rows_per_seq, 8)
        tm, seg = nb * rows_per_seq, rows_per_seq
        assert seq_off % nb == 0
        kv_map = lambda i: (seq_off // nb + i, 0, 0)
    return pl.pallas_call(
        functools.partial(_mem_attn_kernel, nb=nb, seg=seg, by_row=by_row),
        grid=(m // tm,),
        in_specs=[pl.BlockSpec((tm, d), lambda i: (i, 0)),
                  pl.BlockSpec((1, d), lambda i: (0, 0)),
                  pl.BlockSpec((None, d, dm), lambda i: (layer, 0, 0)),
                  pl.BlockSpec((1, HEAD_DIM), lambda i: (0, 0)),
                  pl.BlockSpec((nb, kv_rows, kv_w), kv_map),
                  pl.BlockSpec((nb, kv_rows, kv_w), kv_map),
                  pl.BlockSpec((None, dm, d), lambda i: (layer, 0, 0))],
        out_specs=pl.BlockSpec((tm, d), lambda i: (i, 0)),
        out_shape=jax.ShapeDtypeStruct((m, d), F32),
        scratch_shapes=[pltpu.VMEM((tm, dm), BF16)],
        compiler_params=_cp("parallel"),
        name="mem_attn_block",
    )(x, g.reshape(1, d), wq, qg.reshape(1, HEAD_DIM), k, v, wo)


def _kv_transform_kernel(*refs, modes, heads):
    ns = len(modes)
    x_refs = refs[:ns]
    g_ref, c_ref, sa_ref, sb_ref = refs[ns:ns + 4]
    o_refs = refs[ns + 4:]
    for s in range(ns):
        for h in range(heads):
            cols = slice(h * HEAD_DIM, (h + 1) * HEAD_DIM)
            x = x_refs[s][:, cols]
            if modes[s] != "copy":
                x = _rms(x, g_ref[s:s + 1, :])
            if modes[s] == "norm_rope":
                x = _rope(x, c_ref[...], sa_ref[...], sb_ref[...], ROPE_DIMS // 2)
            o_refs[s][:, cols] = x


def kv_transform(proj, sections, gains, tables, heads, pos_period):
    m = proj.shape[0]
    width = heads * HEAD_DIM
    tr = _tile(pos_period, 512)
    nper = pos_period // tr
    modes = tuple(mode for _, mode in sections)
    in_specs = [pl.BlockSpec((tr, width), functools.partial(lambda i, cb: (i, cb), cb=cb)) for cb, _ in sections]
    in_specs.append(pl.BlockSpec(gains.shape, lambda i: (0, 0)))
    in_specs += [pl.BlockSpec((tr, HEAD_DIM), lambda i: (i % nper, 0))] * 3
    outs = pl.pallas_call(
        functools.partial(_kv_transform_kernel, modes=modes, heads=heads),
        grid=(m // tr,),
        in_specs=in_specs,
        out_specs=[pl.BlockSpec((tr, width), lambda i: (i, 0))] * len(sections),
        out_shape=[jax.ShapeDtypeStruct((m, width), F32)] * len(sections),
        compiler_params=_cp("parallel"),
        name="kv_transform",
    )(*([proj] * len(sections)), gains, *tables)
    return outs


def _retention_kernel(q_ref, k_ref, v_ref, rg_ref, s0_ref, lg_ref, gn_ref, c_ref, sa_ref, sb_ref,
                      y_ref, s_ref, st_ref, dec_ref, qd_ref, kd_ref, *, rows, c_true, hps):
    ci = pl.program_id(2)
    cc = RET_CHUNK

    @pl.when(ci == 0)
    def _():
        st_ref[...] = s0_ref[0]
        ii = _iota((cc, 1), 0).astype(F32)
        diff = ii - _iota((1, cc), 1).astype(F32)
        for h in range(hps):
            lg = lg_ref[h]
            dec_ref[h] = jnp.where(diff >= 0, jnp.exp(jnp.maximum(diff, 0.0) * lg), 0.0)
            qd_ref[h] = jnp.exp((ii + 1.0) * lg)
            kd_ref[h] = jnp.where(ii < c_true, jnp.exp((c_true - 1.0 - ii) * lg), 0.0)

    c, sa, sb = c_ref[...], sa_ref[...], sb_ref[...]
    for h in range(hps):
        cols = slice(h * HEAD_DIM, (h + 1) * HEAD_DIM)
        q = _pad_rows(_rope(q_ref[0, :, cols], c, sa, sb, HEAD_DIM // 2) * SCALE, cc)
        k = _pad_rows(_rope(k_ref[0, :, cols], c, sa, sb, HEAD_DIM // 2), cc)
        vb = _pad_rows(v_ref[0, :, cols], cc).astype(BF16)
        qb = q.astype(BF16)
        inner = _dot_nt(qb, k.astype(BF16)) * dec_ref[h]
        st = st_ref[h]
        o = _dot(inner.astype(BF16), vb) + _dot(qb, st.astype(BF16)) * qd_ref[h]
        kd = k * kd_ref[h]
        st_ref[h] = jnp.exp(c_true * lg_ref[h]) * st + _dot(kd.T.astype(BF16), vb)
        o = o[:rows]
        mu = jnp.mean(o, axis=-1, keepdims=True)
        var = jnp.mean(jnp.square(o - mu), axis=-1, keepdims=True)
        y = (o - mu) * lax.rsqrt(var + NORM_EPS) * gn_ref[h]
        rg = rg_ref[0, :, cols]
        y_ref[0, :, cols] = (y * (rg * jax.nn.sigmoid(rg))).astype(BF16)

    @pl.when(ci == pl.num_programs(2) - 1)
    def _():
        s_ref[0] = st_ref[...]


def retention(proj3, s0, log_g, gn_g, tables, c_true, seq_off=0):
    n, t, _ = proj3.shape
    rows = min(t, RET_CHUNK)
    nc = t // rows
    h = H_RET
    hps = H_RET
    ng = h // hps
    wid = hps * HEAD_DIM
    blk = lambda sec: pl.BlockSpec((1, rows, wid), functools.partial(lambda b, hg, c, sec: (b, c, sec * ng + hg), sec=sec))
    tab = pl.BlockSpec((rows, HEAD_DIM), lambda b, hg, c: (c, 0))
    y, s_new = pl.pallas_call(
        functools.partial(_retention_kernel, rows=rows, c_true=float(c_true), hps=hps),
        grid=(n, ng, nc),
        in_specs=[blk(0), blk(1), blk(2), blk(3),
                  pl.BlockSpec((1, hps, HEAD_DIM, HEAD_DIM), lambda b, hg, c: (seq_off + b, hg, 0, 0)),
                  pl.BlockSpec((hps, 1, HEAD_DIM), lambda b, hg, c: (hg, 0, 0)),
                  pl.BlockSpec((hps, 1, HEAD_DIM), lambda b, hg, c: (hg, 0, 0)),
                  tab, tab, tab],
        out_specs=[pl.BlockSpec((1, rows, wid), lambda b, hg, c: (b, c, hg)),
                   pl.BlockSpec((1, hps, HEAD_DIM, HEAD_DIM), lambda b, hg, c: (b, hg, 0, 0))],
        out_shape=[jax.ShapeDtypeStruct((n, t, h * HEAD_DIM), BF16),
                   jax.ShapeDtypeStruct((n, h, HEAD_DIM, HEAD_DIM), F32)],
        scratch_shapes=[pltpu.VMEM((hps, HEAD_DIM, HEAD_DIM), F32)] * 4,
        compiler_params=_cp("parallel", "parallel", "arbitrary"),
        name="retention",
    )(proj3, proj3, proj3, proj3, s0,
      jnp.broadcast_to(log_g[:, None, None], (h, 1, HEAD_DIM)), gn_g.reshape(h, 1, HEAD_DIM), *tables)
    return y, s_new


def _rows(start, size, stride):
    return pl.ds(start, size) if stride == 1 else pl.ds(start, size, stride=stride)


def _dil_prompt_kernel(q_ref, k_ref, v_ref, qg_ref, c_ref, sa_ref, sb_ref, o_ref,
                       qs_ref, m_ref, l_ref, acc_ref, pm_ref, pls_ref, pacc_ref, qc_ref, kc_ref, vc_ref, *, t):
    chunk = _tile(t, 512)
    for c in range(t // chunk):
        r = slice(c * chunk, (c + 1) * chunk)
        q = _rope(_rms(q_ref[0, r, :], qg_ref[...]), c_ref[r, :], sa_ref[r, :], sb_ref[r, :], ROPE_DIMS // 2)
        qs_ref[r, :] = q * SCALE

    def bias(bq, nk, first_key_dist):
        rel = first_key_dist + _iota((bq, nk), 0) - _iota((bq, nk), 1)
        return jnp.where(rel >= 0, jnp.where(rel <= KEY_BLOCK, 0.0, NEG), NEG)

    for pi, (window, dil) in enumerate(DIL_PATTERNS):
        length = t // dil
        bq = min(length, KEY_BLOCK)
        nk = bq if length == bq else 2 * bq
        for r in range(dil):
            rows, crow = _rows(r, length, dil), slice(r * length, (r + 1) * length)
            qc_ref[crow, :] = qs_ref[rows, :].astype(BF16)
            kc_ref[crow, :] = k_ref[0, rows, :].astype(BF16)
            vc_ref[crow, :] = v_ref[0, rows, :].astype(BF16)
        blocks = [(r * length + i * bq, r * length + max(i - 1, 0) * bq, i == 0 and nk > bq)
                  for r in range(dil) for i in range(length // bq)]
        band, lead = bias(bq, nk, nk - bq), bias(bq, nk, 0)
        s = jnp.concatenate([_dot_nt(qc_ref[q0:q0 + bq, :], kc_ref[k0:k0 + nk, :]) + (lead if first else band)
                             for q0, k0, first in blocks], axis=0)
        m = jnp.max(s, axis=-1, keepdims=True)
        pb = jnp.exp(s - m).astype(BF16)
        m = jnp.broadcast_to(m, (t, HEAD_DIM))
        ones = jnp.ones((nk, HEAD_DIM), BF16)
        l = jnp.concatenate([_dot(pb[j * bq:(j + 1) * bq], ones) for j in range(len(blocks))], axis=0)
        acc = jnp.concatenate([_dot(pb[j * bq:(j + 1) * bq], vc_ref[k0:k0 + nk, :])
                               for j, (q0, k0, first) in enumerate(blocks)], axis=0)
        if pi == 0:
            assert dil == 1
            m_ref[...], l_ref[...], acc_ref[...] = m, l, acc
        else:
            pm_ref[...], pls_ref[...], pacc_ref[...] = m, l, acc
            for r in range(dil):
                rows, crow = _rows(r, length, dil), slice(r * length, (r + 1) * length)
                m0, m1 = m_ref[rows, :], pm_ref[crow, :]
                mx = jnp.maximum(m0, m1)
                e0, e1 = jnp.exp(m0 - mx), jnp.exp(m1 - mx)
                m_ref[rows, :] = mx
                l_ref[rows, :] = e0 * l_ref[rows, :] + e1 * pls_ref[crow, :]
                acc_ref[rows, :] = e0 * acc_ref[rows, :] + e1 * pacc_ref[crow, :]

    for c in range(t // chunk):
        r = slice(c * chunk, (c + 1) * chunk)
        o_ref[0, r, :] = (acc_ref[r, :] / l_ref[r, :]).astype(BF16)


def dilated_prompt(proj3, q_cb, k3, v_cb, qg, tables):
    n, t, _ = proj3.shape
    for window, dil in DIL_PATTERNS:
        length = t // dil
        assert window // dil == KEY_BLOCK and t % dil == 0
        assert length % KEY_BLOCK == 0 or (length < KEY_BLOCK and length % SUBLANES == 0)
    tab = pl.BlockSpec((t, HEAD_DIM), lambda b, h: (0, 0))
    seq = lambda cb: pl.BlockSpec((1, t, HEAD_DIM), functools.partial(lambda b, h, cb: (b, 0, cb + h), cb=cb))
    return pl.pallas_call(
        functools.partial(_dil_prompt_kernel, t=t),
        grid=(n, H_DIL),
        in_specs=[seq(q_cb), seq(0), seq(v_cb), pl.BlockSpec((1, HEAD_DIM), lambda b, h: (0, 0)), tab, tab, tab],
        out_specs=seq(0),
        out_shape=jax.ShapeDtypeStruct((n, t, H_DIL * HEAD_DIM), BF16),
        scratch_shapes=[pltpu.VMEM((t, HEAD_DIM), F32)] * 7 + [pltpu.VMEM((t, HEAD_DIM), BF16)] * 3,
        compiler_params=_cp("parallel", "parallel"),
        name="dilated_prompt",
    )(proj3, k3, proj3, qg.reshape(1, HEAD_DIM), *tables)


def _dil_sample_kernel(q_ref, kr_ref, vr_ref, kf_ref, vf_ref, kn_ref, vn_ref, qg_ref, c_ref, sa_ref, sb_ref, o_ref,
                       *, lc, rc, ts):
    rows = q_ref.shape[1]
    far_window, dl = DIL_PATTERNS[-1]
    nj = lc // dl
    head = lambda a, h: a[h * rows:(h + 1) * rows]
    per_head = lambda t: jnp.concatenate([t[...]] * H_DIL, axis=0)
    x = jnp.concatenate([q_ref[0, :, h * HEAD_DIM:(h + 1) * HEAD_DIM] for h in range(H_DIL)], axis=0)
    x = _rope(_rms(x, qg_ref[...]), per_head(c_ref), per_head(sa_ref), per_head(sb_ref), ROPE_DIMS // 2)
    qb = (x * SCALE).astype(BF16)

    def log_weight(d, patterns):
        w = _dil_weight(d, patterns)
        return jnp.where(w > 0, jnp.log(jnp.maximum(w, 1.0)), NEG)

    hr = H_DIL * rows
    hbits, rbits, tbits = H_DIL.bit_length() - 1, rows.bit_length() - 1, ts.bit_length() - 1
    q_head = lambda w: jnp.right_shift(_iota((hr, w), 0), rbits)
    q_tok = lambda w: _iota((hr, w), 0) & (rows - 1)
    k_head = lambda w: _iota((hr, w), 1) & (H_DIL - 1)
    k_idx = lambda w: jnp.right_shift(_iota((hr, w), 1), hbits)
    wr = rc * H_DIL
    bias_r = jnp.where(q_head(wr) == k_head(wr), log_weight(rc + q_tok(wr) - k_idx(wr), DIL_PATTERNS[:-1]), NEG)
    wf = nj * ts * H_DIL
    jt = k_idx(wf)
    ok = jnp.where(q_head(wf) == k_head(wf), jnp.where((jt & (ts - 1)) == q_tok(wf), 1.0, 0.0), 0.0)
    bias_f = jnp.where(ok > 0, jnp.where(lc - dl * jnp.right_shift(jt, tbits) <= far_window, 0.0, NEG), NEG)
    bias_n = log_weight(_iota((rows, KEY_BLOCK), 0) - _iota((rows, KEY_BLOCK), 1), DIL_PATTERNS)
    new = lambda ref, h: _pad_rows(ref[0, :, h * HEAD_DIM:(h + 1) * HEAD_DIM], KEY_BLOCK).astype(BF16)

    kf = kf_ref[...].reshape(wf, HEAD_DIM).astype(BF16)
    vf = vf_ref[...].reshape(wf, HEAD_DIM).astype(BF16)
    s = jnp.concatenate([
        _dot_nt(qb, kr_ref[0].astype(BF16)) + bias_r,
        _dot_nt(qb, kf) + bias_f,
        jnp.concatenate([_dot_nt(head(qb, h), new(kn_ref, h)) + bias_n for h in range(H_DIL)], axis=0)], axis=1)
    p = jnp.exp(s - jnp.max(s, axis=-1, keepdims=True))
    pb = (p / jnp.sum(p, axis=-1, keepdims=True)).astype(BF16)
    o = _dot(pb[:, :wr], vr_ref[0].astype(BF16)) + _dot(pb[:, wr:wr + wf], vf)
    for h in range(H_DIL):
        oh = head(o, h) + _dot(head(pb, h)[:, wr + wf:], new(vn_ref, h))
        o_ref[0, :, h * HEAD_DIM:(h + 1) * HEAD_DIM] = oh.astype(BF16)


def dilated_sample(proj3, q_cb, kc, vc, seq_off, kn3, v_cb, qg, tables, ts):
    n, rows, _ = proj3.shape
    lc = kc.shape[1]
    wid = H_DIL * HEAD_DIM
    far_window, dl = DIL_PATTERNS[-1]
    rc = min(lc, max(w for w, _ in DIL_PATTERNS[:-1]))
    assert (q_cb * HEAD_DIM) % wid == 0 and (v_cb * HEAD_DIM) % wid == 0
    assert lc % dl == 0 and lc % rc == 0 and ts <= min(dl, rows)
    assert (ts * H_DIL) % SUBLANES == 0 and all(v & (v - 1) == 0 for v in (ts, rows, H_DIL))
    rows_view = lambda a: a.reshape(-1, lc * H_DIL, HEAD_DIM)
    group_view = lambda a: a.reshape(-1, lc // dl, dl * H_DIL, HEAD_DIM)
    tab = pl.BlockSpec((rows, HEAD_DIM), lambda b: (0, 0))
    recent = pl.BlockSpec((1, rc * H_DIL, HEAD_DIM), lambda b: (seq_off + b, lc // rc - 1, 0))
    far = pl.BlockSpec((None, lc // dl, ts * H_DIL, HEAD_DIM), lambda b: (seq_off + b, 0, 0, 0))
    new = pl.BlockSpec((1, rows, wid), lambda b: (b, 0, 0))
    return pl.pallas_call(
        functools.partial(_dil_sample_kernel, lc=lc, rc=rc, ts=ts),
        grid=(n,),
        in_specs=[pl.BlockSpec((1, rows, wid), lambda b: (b, 0, q_cb * HEAD_DIM // wid)), recent, recent, far, far,
                  new, pl.BlockSpec((1, rows, wid), lambda b: (b, 0, v_cb * HEAD_DIM // wid)),
                  pl.BlockSpec((1, HEAD_DIM), lambda b: (0, 0)), tab, tab, tab],
        out_specs=new,
        out_shape=jax.ShapeDtypeStruct((n, rows, wid), BF16),
        compiler_params=_cp("parallel"),
        name="dilated_sample",
    )(proj3, rows_view(kc), rows_view(vc), group_view(kc), group_view(vc), kn3, proj3, qg.reshape(1, HEAD_DIM), *tables)


def _nsa_q(q_ref, g, qg_ref, c_ref, sa_ref, sb_ref):
    x = jnp.concatenate([q_ref[0][:, (g * HPG + hl) * HEAD_DIM:(g * HPG + hl + 1) * HEAD_DIM] for hl in range(HPG)],
                        axis=0)
    per_head = lambda t: jnp.concatenate([t[...]] * HPG, axis=0)
    x = _rope(_rms(x, qg_ref[...]), per_head(c_ref), per_head(sa_ref), per_head(sb_ref), ROPE_DIMS // 2)
    return (x * SCALE).astype(BF16)


def _nsa_compressed(q4, kc, vc, qpos, nb):
    rows = qpos.shape[0]
    nbp = kc.shape[0]
    blk = _iota((rows, nbp), 1)
    cmask = jnp.where((blk + 1) * NSA_BLOCK - 1 <= qpos, 1.0, 0.0)
    cmask4 = jnp.concatenate([cmask] * HPG, axis=0)
    s = jnp.where(cmask4 > 0, _dot_nt(q4, kc.astype(BF16)), NEG)
    p = jnp.exp(s - jnp.max(s, axis=-1, keepdims=True)) * cmask4
    p = p / jnp.maximum(jnp.sum(p, axis=-1, keepdims=True), 1e-30)
    o_c = _dot(p.astype(BF16), vc.astype(BF16))
    imp = p[0:rows]
    for hl in range(1, HPG):
        imp = imp + p[hl * rows:(hl + 1) * rows]
    shift = NSA_BLOCK.bit_length() - 1
    if rows == nbp:
        nbr = -(-nb // SUBLANES) * SUBLANES
        blk_t = _iota((nbr, rows), 0)
        cur_t = jnp.right_shift(qpos[0:1, :] + _iota((1, rows), 1), shift)
        cand_t = blk_t < cur_t
        imp_t = jnp.where(cand_t, imp.T[:nbr], -jnp.inf)
        rank = jnp.zeros((nbr, rows), F32)
        for b in range(nb):
            row = imp_t[b:b + 1, :]
            tie = jnp.where(row == imp_t, jnp.where(blk_t > b, 1.0, 0.0), 0.0)
            rank = rank + jnp.where(row > imp_t, 1.0, tie)
        sel_t = jnp.where(cand_t, jnp.where(rank < min(NSA_TOPK, nb), 1.0, 0.0), 0.0)
        sel = _pad_rows(sel_t, nbp).T
    else:
        cand = blk < jnp.right_shift(qpos, shift)
        imp = jnp.where(cand, imp, -jnp.inf)
        rank = jnp.zeros((rows, nbp), F32)
        for b in range(nb):
            col = imp[:, b:b + 1]
            tie = jnp.where(col == imp, jnp.where(blk > b, 1.0, 0.0), 0.0)
            rank = rank + jnp.where(col > imp, 1.0, tie)
        sel = jnp.where(cand, jnp.where(rank < min(NSA_TOPK, nb), 1.0, 0.0), 0.0)
    sel = jnp.where(blk == jnp.right_shift(qpos, shift), 1.0, sel)
    return o_c, sel


def _nsa_combine(gates, g, o_c, o_s, o_w, rows):
    outs = []
    for hl in range(HPG):
        r = slice(hl * rows, (hl + 1) * rows)
        col = 3 * hl
        outs.append(gates[:, col:col + 1] * o_c[r] + gates[:, col + 1:col + 2] * o_s[r]
                    + gates[:, col + 2:col + 3] * o_w[r])
    return outs


def _nsa_prompt_kernel(q_ref, gt_ref, kc_ref, vc_ref, ks_ref, vs_ref, kw_ref, vw_ref, e_ref,
                       qg_ref, c_ref, sa_ref, sb_ref, o_ref, selx_ref, s_ref, mx_ref, ls_ref, acc_ref, *, tq, nb, t):
    qi = pl.program_id(2)
    q4 = _nsa_q(q_ref, 0, qg_ref, c_ref, sa_ref, sb_ref)
    qpos = qi * tq + _iota((tq, 1), 0)
    o_c, sel = _nsa_compressed(q4, kc_ref[0], vc_ref[0], qpos, nb)
    selx_ref[...] = _dot(sel.astype(BF16), e_ref[...])

    ck = _tile(t, 4 * KEY_BLOCK)
    nchunks = lax.div(qi * tq + (tq + ck - 1), ck)
    lane = _iota((tq, ck), 1)
    mx_ref[...] = jnp.full(mx_ref.shape, NEG, F32)

    def scores(c, _):
        off = pl.multiple_of(c * ck, ck)
        bias = jnp.where(off + lane <= qpos, (selx_ref[:, pl.ds(off, ck)] - 1.0) * -NEG, NEG)
        s = _dot_nt(q4, ks_ref[0, pl.ds(off, ck), :].astype(BF16))
        s = (s.reshape(HPG, tq, ck) + bias[None]).reshape(HPG * tq, ck)
        s_ref[:, pl.ds(off, ck)] = s
        part = s[:, :KEY_BLOCK]
        for j in range(1, ck // KEY_BLOCK):
            part = jnp.maximum(part, s[:, j * KEY_BLOCK:(j + 1) * KEY_BLOCK])
        mx_ref[...] = jnp.maximum(mx_ref[...], part)
        return 0

    lax.fori_loop(0, nchunks, scores, 0)
    m = jnp.max(mx_ref[...], axis=-1, keepdims=True)
    ls_ref[...] = jnp.zeros(ls_ref.shape, F32)
    acc_ref[...] = jnp.zeros(acc_ref.shape, F32)

    def values(c, _):
        off = pl.multiple_of(c * ck, ck)
        s = s_ref[:, pl.ds(off, ck)]
        p = jnp.exp(s - m)
        part = p[:, :KEY_BLOCK]
        for j in range(1, ck // KEY_BLOCK):
            part = part + p[:, j * KEY_BLOCK:(j + 1) * KEY_BLOCK]
        ls_ref[...] += part
        acc_ref[...] += _dot(p.astype(BF16), vs_ref[0, pl.ds(off, ck), :].astype(BF16))
        return 0

    lax.fori_loop(0, nchunks, values, 0)
    o_s = acc_ref[...] / jnp.maximum(jnp.sum(ls_ref[...], axis=-1, keepdims=True), 1e-30)

    span = min(NSA_WINDOW + KEY_BLOCK, t)
    first = jnp.clip(qi - NSA_WINDOW // KEY_BLOCK, 0, (t - span) // KEY_BLOCK)
    off = pl.multiple_of(first * KEY_BLOCK, KEY_BLOCK)
    d = qpos - (off + _iota((tq, span), 1))
    bias = jnp.where(d >= 0, jnp.where(d < NSA_WINDOW, 0.0, NEG), NEG)
    s = _dot_nt(q4, kw_ref[0, pl.ds(off, span), :].astype(BF16))
    s = (s.reshape(HPG, tq, span) + bias[None]).reshape(HPG * tq, span)
    p = jnp.exp(s - jnp.max(s, axis=-1, keepdims=True))
    o_w = _dot(p.astype(BF16), vw_ref[0, pl.ds(off, span), :].astype(BF16)) / jnp.sum(p, axis=-1, keepdims=True)
    gates = jax.nn.sigmoid(gt_ref[0])
    for hl, o in enumerate(_nsa_combine(gates, 0, o_c, o_s, o_w, tq)):
        o_ref[0, :, hl * HEAD_DIM:(hl + 1) * HEAD_DIM] = o.astype(BF16)


def nsa_prompt(proj3, gate_cb, kc3, vc3, ks3, vs3, kw3, vw3, expand, qg, tables, nb):
    n, t, _ = proj3.shape
    tq = KEY_BLOCK
    nbp = kc3.shape[1]
    gw = HPG * HEAD_DIM
    tab = pl.BlockSpec((tq, HEAD_DIM), lambda b, g, i: (i, 0))
    cmp_spec = pl.BlockSpec((1, nbp, HEAD_DIM), lambda b, g, i: (b, 0, g))
    kv = pl.BlockSpec((1, t, HEAD_DIM), lambda b, g, i: (b, 0, g))
    return pl.pallas_call(
        functools.partial(_nsa_prompt_kernel, tq=tq, nb=nb, t=t),
        grid=(n, G_NSA, t // tq),
        in_specs=[pl.BlockSpec((1, tq, gw), lambda b, g, i: (b, i, g)),
                  pl.BlockSpec((1, tq, HEAD_DIM), lambda b, g, i: (b, i, gate_cb + g)),
                  cmp_spec, cmp_spec, kv, kv, kv, kv,
                  pl.BlockSpec((nbp, t), lambda b, g, i: (0, 0)),
                  pl.BlockSpec((1, HEAD_DIM), lambda b, g, i: (0, 0)), tab, tab, tab],
        out_specs=pl.BlockSpec((1, tq, gw), lambda b, g, i: (b, i, g)),
        out_shape=jax.ShapeDtypeStruct((n, t, H_NSA * HEAD_DIM), BF16),
        scratch_shapes=[pltpu.VMEM((tq, t), F32), pltpu.VMEM((HPG * tq, t), F32)]
        + [pltpu.VMEM((HPG * tq, HEAD_DIM), F32)] * 3,
        compiler_params=_cp("parallel", "parallel", "arbitrary"),
        name="nsa_prompt",
    )(proj3, proj3, kc3, vc3, ks3, vs3, kw3, vw3, expand, qg.reshape(1, HEAD_DIM), *tables)


def _compress_rows_kernel(k_ref, v_ref, pk_ref, pv_ref, ko_ref, vo_ref):
    for x_ref, p_ref, o_ref in ((k_ref, pk_ref, ko_ref), (v_ref, pv_ref, vo_ref)):
        x = x_ref[0]
        nblk = x.shape[0] // NSA_BLOCK
        pw = jnp.concatenate([p_ref[...]] * nblk, axis=0)
        o_ref[0] = jnp.sum((x * pw).reshape(nblk, NSA_BLOCK, x.shape[1]), axis=1)


def compress_rows(k3, v3, pwk2, pwv2):
    n, t, w = k3.shape
    tr = _tile(t, 512)
    row = pl.BlockSpec((1, tr, w), lambda b, i: (b, i, 0))
    pw = pl.BlockSpec((NSA_BLOCK, w), lambda b, i: (0, 0))
    out = pl.BlockSpec((1, tr // NSA_BLOCK, w), lambda b, i: (b, i, 0))
    return pl.pallas_call(
        _compress_rows_kernel,
        grid=(n, t // tr),
        in_specs=[row, row, pw, pw],
        out_specs=[out, out],
        out_shape=[jax.ShapeDtypeStruct((n, t // NSA_BLOCK, w), F32)] * 2,
        compiler_params=_cp("parallel", "parallel"),
        name="compress_rows",
    )(k3, v3, pwk2, pwv2)


def _compress_pages_kernel(pt_ref, *refs, pps):
    k_refs, v_refs = refs[:pps], refs[pps:2 * pps]
    pk_ref, pv_ref, ko_ref, vo_ref = refs[2 * pps:]
    bpp = PAGE_SIZE // NSA_BLOCK
    prows = PAGE_SIZE * G_NSA
    for x_refs, p_ref, o_ref in ((k_refs, pk_ref, ko_ref), (v_refs, pv_ref, vo_ref)):
        pw = p_ref[...]
        outs = []
        for x_ref in x_refs:
            z = jnp.sum((x_ref[0] * pw).reshape(bpp, prows // bpp // SUBLANES, SUBLANES, HEAD_DIM), axis=1)
            per_g = [sum(z[:, s] for s in range(g, SUBLANES, G_NSA)) for g in range(G_NSA)]
            outs.append(jnp.concatenate(per_g, axis=1))
        o_ref[0] = jnp.concatenate(outs, axis=0)


def compress_pages(pool_k, pool_v, page_off, page_table, pwk, pwv, pps):
    n, n_pages = page_table.shape
    w = G_NSA * HEAD_DIM
    prows = PAGE_SIZE * G_NSA
    bpp = PAGE_SIZE // NSA_BLOCK
    assert SUBLANES % G_NSA == 0
    page_w = lambda p: jnp.repeat(jnp.tile(p, (bpp, 1)), G_NSA, axis=0)
    pwk2, pwv2 = page_w(pwk), page_w(pwv)
    page = lambda i: pl.BlockSpec((1, prows, HEAD_DIM),
                                  functools.partial(lambda b, c, pt, i: (page_off + pt[b, c * pps + i], 0, 0), i=i))
    pw = pl.BlockSpec((prows, HEAD_DIM), lambda b, c, pt: (0, 0))
    out = pl.BlockSpec((1, pps * bpp, w), lambda b, c, pt: (b, c, 0))
    return pl.pallas_call(
        functools.partial(_compress_pages_kernel, pps=pps),
        grid_spec=pltpu.PrefetchScalarGridSpec(
            num_scalar_prefetch=1, grid=(n, n_pages // pps),
            in_specs=[page(i) for i in range(pps)] * 2 + [pw, pw],
            out_specs=[out, out]),
        out_shape=[jax.ShapeDtypeStruct((n, n_pages * bpp, w), F32)] * 2,
        compiler_params=_cp("parallel", "parallel"),
        name="compress_pages",
    )(page_table, *([pool_k] * pps), *([pool_v] * pps), pwk2, pwv2)


def _nsa_sample_kernel(pt_ref, *refs, pps, past, rows, nb, lw):
    (q_ref, gt_ref, kc_ref, vc_ref), refs = refs[:4], refs[4:]
    kp_refs, vp_refs, refs = refs[:pps], refs[pps:2 * pps], refs[2 * pps:]
    (kn_ref, vn_ref, kwc_ref, vwc_ref, kwn_ref, vwn_ref, e_ref, qg_ref, c_ref, sa_ref, sb_ref,
     o_ref, q4_ref, sel_ref, oc_ref, m_ref, l_ref, acc_ref) = refs
    pc = pl.program_id(1)
    r4 = HPG * rows
    qpos = past + _iota((rows, 1), 0)
    lane = _iota((rows, KEY_BLOCK), 1)

    @pl.when(pc == 0)
    def _():
        for g in range(G_NSA):
            cols = slice(g * HEAD_DIM, (g + 1) * HEAD_DIM)
            q4 = _nsa_q(q_ref, g, qg_ref, c_ref, sa_ref, sb_ref)
            o_c, sel = _nsa_compressed(q4, kc_ref[0][:, cols], vc_ref[0][:, cols], qpos, nb)
            q4_ref[g] = q4
            sel_ref[g] = sel
            oc_ref[g] = o_c
            m0, l0, a0 = _softmax_init(r4)
            m_ref[g], l_ref[g], acc_ref[g] = m0, l0, a0

    nkeys = pps * PAGE_SIZE
    kpos = pc * nkeys + _iota((rows, nkeys), 1)
    scores, values = [], []
    for g in range(G_NSA):
        grp = pl.ds(g, PAGE_SIZE, stride=G_NSA)
        kb = jnp.concatenate([r[0, grp, :].astype(BF16) for r in kp_refs], axis=0)
        values.append(jnp.concatenate([r[0, grp, :].astype(BF16) for r in vp_refs], axis=0))
        selx = _dot(sel_ref[g].astype(BF16), e_ref[...])
        bias = jnp.where(kpos <= qpos, (selx - 1.0) * -NEG, NEG)
        s = _dot_nt(q4_ref[g], kb)
        scores.append((s.reshape(HPG, rows, nkeys) + bias[None]).reshape(r4, nkeys))
    s = jnp.concatenate(scores, axis=0)
    m_old = jnp.concatenate([m_ref[g] for g in range(G_NSA)], axis=0)
    m_new = jnp.maximum(m_old, jnp.max(s, axis=-1, keepdims=True))
    alpha = jnp.exp(m_old - m_new)
    p = jnp.exp(s - m_new)
    pb = p.astype(BF16)
    psum = jnp.sum(p, axis=-1, keepdims=True)
    for g in range(G_NSA):
        r = slice(g * r4, (g + 1) * r4)
        m_ref[g] = m_new[r]
        l_ref[g] = alpha[r] * l_ref[g] + psum[r]
        acc_ref[g] = alpha[r] * acc_ref[g] + _dot(pb[r], values[g])

    @pl.when(pc == pl.num_programs(1) - 1)
    def _():
        gates_all = jax.nn.sigmoid(gt_ref[0])
        tq = _iota((rows, KEY_BLOCK), 0)
        for g in range(G_NSA):
            cols = slice(g * HEAD_DIM, (g + 1) * HEAD_DIM)
            q4 = q4_ref[g]
            kb = _pad_rows(kn_ref[0][:, cols], KEY_BLOCK).astype(BF16)
            vb = _pad_rows(vn_ref[0][:, cols], KEY_BLOCK).astype(BF16)
            w = jnp.where(lane <= tq, 1.0, 0.0)
            _, l, acc = _softmax_step(_dot_nt(q4, kb), jnp.concatenate([w] * HPG, axis=0), vb,
                                      m_ref[g], l_ref[g], acc_ref[g])
            o_s = _softmax_done(l, acc)
            tqw = _iota((rows, lw), 0)
            d = tqw + lw - _iota((rows, lw), 1)
            w = jnp.where(d < NSA_WINDOW, jnp.where(past - d + tqw >= 0, 1.0, 0.0), 0.0)
            grp = pl.ds(g, lw, stride=G_NSA)
            carry = _softmax_step(_dot_nt(q4, kwc_ref[0, grp, :].astype(BF16)), jnp.concatenate([w] * HPG, axis=0),
                                  vwc_ref[0, grp, :].astype(BF16), *_softmax_init(r4))
            kb = _pad_rows(kwn_ref[0][:, cols], KEY_BLOCK).astype(BF16)
            vb = _pad_rows(vwn_ref[0][:, cols], KEY_BLOCK).astype(BF16)
            w = jnp.where(lane <= tq, 1.0, 0.0)
            _, l, acc = _softmax_step(_dot_nt(q4, kb), jnp.concatenate([w] * HPG, axis=0), vb, *carry)
            o_w = _softmax_done(l, acc)
            gates = gates_all[:, g * HEAD_DIM:(g + 1) * HEAD_DIM]
            for hl, o in enumerate(_nsa_combine(gates, g, oc_ref[g], o_s, o_w, rows)):
                col = (g * HPG + hl) * HEAD_DIM
                o_ref[0, :, col:col + HEAD_DIM] = o.astype(BF16)


def nsa_sample(proj3, gate_cb, kc3, vc3, pool_k, pool_v, page_off, page_table, kn3, vn3, kwc3, vwc3, seq_off,
               kwn3, vwn3, expand, qg, tables, pps):
    n, rows, _ = proj3.shape
    n_pages = page_table.shape[1]
    past = n_pages * PAGE_SIZE
    nbp = kc3.shape[1]
    w = G_NSA * HEAD_DIM
    lw = kwc3.shape[1] // G_NSA
    qw = H_NSA * HEAD_DIM
    r4 = HPG * rows
    fixed = lambda shape: pl.BlockSpec(shape, lambda b, c, pt: (0,) * len(shape))
    per_seq = lambda r, width, cb=0, off=0: pl.BlockSpec(
        (1, r, width), functools.partial(lambda b, c, pt, cb, off: (off + b, 0, cb), cb=cb, off=off))
    page = lambda i: pl.BlockSpec((1, PAGE_SIZE * G_NSA, HEAD_DIM),
                                  functools.partial(lambda b, c, pt, i: (page_off + pt[b, c * pps + i], 0, 0), i=i))
    win = per_seq(lw * G_NSA, HEAD_DIM, 0, seq_off)
    in_specs = ([per_seq(rows, qw), per_seq(rows, w, gate_cb // G_NSA), per_seq(nbp, w), per_seq(nbp, w)]
                + [page(i) for i in range(pps)] * 2
                + [per_seq(rows, w), per_seq(rows, w), win, win, per_seq(rows, w), per_seq(rows, w),
                   pl.BlockSpec((nbp, pps * PAGE_SIZE), lambda b, c, pt: (0, c)),
                   fixed((1, HEAD_DIM)), fixed((rows, HEAD_DIM)), fixed((rows, HEAD_DIM)), fixed((rows, HEAD_DIM))])
    return pl.pallas_call(
        functools.partial(_nsa_sample_kernel, pps=pps, past=past, rows=rows, nb=past // NSA_BLOCK, lw=lw),
        grid_spec=pltpu.PrefetchScalarGridSpec(
            num_scalar_prefetch=1, grid=(n, n_pages // pps),
            in_specs=in_specs,
            out_specs=pl.BlockSpec((1, rows, qw), lambda b, c, pt: (b, 0, 0)),
            scratch_shapes=[pltpu.VMEM((G_NSA, r4, HEAD_DIM), BF16), pltpu.VMEM((G_NSA, rows, nbp), F32),
                            pltpu.VMEM((G_NSA, r4, HEAD_DIM), F32), pltpu.VMEM((G_NSA, r4, 1), F32),
                            pltpu.VMEM((G_NSA, r4, 1), F32), pltpu.VMEM((G_NSA, r4, HEAD_DIM), F32)]),
        out_shape=jax.ShapeDtypeStruct((n, rows, qw), BF16),
        compiler_params=_cp("parallel", "arbitrary"),
        name="nsa_sample",
    )(page_table, proj3, proj3, kc3, vc3, *([pool_k] * pps), *([pool_v] * pps), kn3, vn3, kwc3, vwc3, kwn3, vwn3,
      expand, qg.reshape(1, HEAD_DIM), *tables)


def _lru_kernel(x_ref, y_ref, hist_ref, h0_ref, cw_ref, cb_ref, wa_ref, ba_ref, wi_ref, bi_ref, sp_ref,
                o_ref, tail_ref, a_ref, b_ref, *, cps):
    t = x_ref.shape[1]
    bw = D_RNN // RNN_BLOCKS
    row8 = _iota((SUBLANES, bw), 0)
    for k in range(cps):
        cols = slice(k * bw, (k + 1) * bw)
        x = x_ref[0, :, cols]
        hist = hist_ref[0, :, cols]
        xc = cb_ref[:, cols] + x * cw_ref[CONV_W - 1:CONV_W, cols]
        for s in range(1, CONV_W):
            sh = pltpu.roll(x, s, 0)
            head = jnp.where(row8 >= s, sh[:SUBLANES], pltpu.roll(hist, (SUBLANES - (CONV_W - 1) + s) % SUBLANES, 0))
            sh = head if t == SUBLANES else jnp.concatenate([head, sh[SUBLANES:]], axis=0)
            xc = xc + sh * cw_ref[CONV_W - 1 - s:CONV_W - s, cols]
        xb = xc.astype(BF16)
        r = jax.nn.sigmoid(_dot(xb, wa_ref[k]) + ba_ref[:, cols])
        gi = jax.nn.sigmoid(_dot(xb, wi_ref[k]) + bi_ref[:, cols])
        log_a = -LRU_C * r * sp_ref[:, cols]
        th = jnp.tanh(log_a)
        a_ref[:, cols] = jnp.exp(log_a)
        b_ref[:, cols] = jnp.sqrt(-2.0 * th / (1.0 - th)) * (gi * xc)

    wid = cps * bw
    rowsw = _iota((SUBLANES, wid), 0)

    def body(gidx, hc):
        off = pl.multiple_of(gidx * SUBLANES, SUBLANES)
        a = a_ref[pl.ds(off, SUBLANES), :]
        b = b_ref[pl.ds(off, SUBLANES), :]
        for s in (1, 2, 4):
            ok = rowsw >= s
            b = jnp.where(ok, a * pltpu.roll(b, s, 0) + b, b)
            a = jnp.where(ok, a * pltpu.roll(a, s, 0), a)
        hrows = a * hc + b
        a_ref[pl.ds(off, SUBLANES), :] = hrows
        return jnp.broadcast_to(hrows[SUBLANES - 1:SUBLANES, :], (SUBLANES, wid))

    lax.fori_loop(0, t // SUBLANES, body, jnp.broadcast_to(h0_ref[0], (SUBLANES, wid)))
    tail_ref[0] = a_ref[t - SUBLANES:, :]
    chunk = _tile(t, 512)
    for c in range(t // chunk):
        rws = slice(c * chunk, (c + 1) * chunk)
        o_ref[0, rws, :] = (jax.nn.gelu(y_ref[0, rws, :], approximate=True) * a_ref[rws, :]).astype(BF16)


def rglru(proj3, x_cb, y_cb, hist8, h0, conv_w, conv_b, w_a, b_a, w_i, b_i, softplus_neg_lam):
    n, t, _ = proj3.shape
    bw = D_RNN // RNN_BLOCKS
    cps = 4 if t > SUBLANES else RNN_BLOCKS
    wid = cps * bw
    assert (x_cb * HEAD_DIM) % wid == 0 and (y_cb * HEAD_DIM) % wid == 0 and bw == HEAD_DIM
    vec = lambda a: a.reshape(1, D_RNN)
    vspec = pl.BlockSpec((1, wid), lambda b, k: (0, k))
    mspec = pl.BlockSpec((cps, bw, bw), lambda b, k: (k, 0, 0))
    return pl.pallas_call(
        functools.partial(_lru_kernel, cps=cps),
        grid=(n, RNN_BLOCKS // cps),
        in_specs=[pl.BlockSpec((1, t, wid), lambda b, k: (b, 0, x_cb * HEAD_DIM // wid + k)),
                  pl.BlockSpec((1, t, wid), lambda b, k: (b, 0, y_cb * HEAD_DIM // wid + k)),
                  pl.BlockSpec((1, SUBLANES, wid), lambda b, k: (b, 0, k)),
                  pl.BlockSpec((1, 1, wid), lambda b, k: (b, 0, k)),
                  pl.BlockSpec((CONV_W, wid), lambda b, k: (0, k)),
                  vspec, mspec, vspec, mspec, vspec, vspec],
        out_specs=[pl.BlockSpec((1, t, wid), lambda b, k: (b, 0, k)),
                   pl.BlockSpec((1, SUBLANES, wid), lambda b, k: (b, 0, k))],
        out_shape=[jax.ShapeDtypeStruct((n, t, D_RNN), BF16), jax.ShapeDtypeStruct((n, SUBLANES, D_RNN), F32)],
        scratch_shapes=[pltpu.VMEM((t, wid), F32), pltpu.VMEM((t, wid), F32)],
        compiler_params=_cp("parallel", "parallel"),
        name="rglru",
    )(proj3, proj3, hist8, h0.reshape(n, 1, D_RNN), conv_w, vec(conv_b),
      w_a.astype(BF16), vec(b_a), w_i.astype(BF16), vec(b_i), vec(softplus_neg_lam))


def _rope_tables(pos, n_rot, theta):
    half = n_rot // 2
    inv = 1.0 / (theta ** (jnp.arange(half, dtype=F32) * (2.0 / n_rot)))
    ang = pos.astype(F32)[:, None] * inv[None, :]
    cos, sin = jnp.cos(ang), jnp.sin(ang)
    p = pos.shape[0]
    rest = HEAD_DIM - n_rot
    zh = jnp.zeros((p, half), F32)
    c = jnp.concatenate([cos, cos, jnp.ones((p, rest), F32)], axis=1)
    sa = jnp.concatenate([-sin, zh, jnp.zeros((p, rest), F32)], axis=1)
    sb = jnp.concatenate([zh, sin, jnp.zeros((p, rest), F32)], axis=1)
    return c, sa, sb


def _block_expand(nbp, length):
    return (jnp.arange(length, dtype=jnp.int32)[None, :] // NSA_BLOCK
            == jnp.arange(nbp, dtype=jnp.int32)[:, None]).astype(BF16)


def _pad_axis(a, axis, size):
    pad = [(0, 0)] * a.ndim
    pad[axis] = (0, size - a.shape[axis])
    return jnp.pad(a, pad)


def _odd_w_in(w):
    qw = H_NSA * HEAD_DIM
    qkv = qw + 6 * G_NSA * HEAD_DIM
    ng = 3 * H_NSA
    gate = w[:, qkv:qkv + ng]
    per = 3 * HPG
    gcols = [_pad_axis(gate[:, g * per:(g + 1) * per], 1, LANES) for g in range(G_NSA)]
    out = jnp.concatenate([w[:, :qw], w[:, qkv + ng:], w[:, qw:qkv]] + gcols, axis=1)
    return _pad_axis(out, 1, -(-out.shape[1] // 1024) * 1024)


def kernel(x_prompt, x_sample, mem_prompt, state_ret, cache_dil_k, cache_dil_v, cache_nsa_cmp_k, cache_nsa_cmp_v, cache_nsa_slc_k, cache_nsa_slc_v, cache_nsa_win_k, cache_nsa_win_v, state_lru_h, state_lru_conv, cache_mem_k, cache_mem_v, page_table, ev_w_in, ret_gn_g, dil_q_norm_g, dil_k_norm_g, ev_w_out, od_w_in, nsa_q_norm_g, nsa_k_norm_g, nsa_pw_k, nsa_pw_v, lru_conv_w, lru_conv_b, lru_w_a, lru_b_a, lru_w_i, lru_b_i, lru_lambda, od_w_out, norm_mix_g, norm_mem_g, norm_mlp_g, mem_norm_g, mem_w_q, mem_w_k, mem_w_v, mem_q_norm_g, mem_k_norm_g, mem_w_o, mlp_w1, mlp_w2):
    nb_p, t_p, d = x_prompt.shape
    nb_s, t_s, _ = x_sample.shape
    depth = norm_mix_g.shape[0]
    n_pages = page_table.shape[1]
    past = n_pages * PAGE_SIZE
    rs = SAMPLE_ROWS
    assert t_p % RET_CHUNK == 0 and t_s <= rs and past % PAGE_SIZE == 0 and t_s < NSA_BLOCK
    assert cache_nsa_win_k.shape[2] % KEY_BLOCK == 0 and cache_dil_k.shape[2] % KEY_BLOCK == 0
    hd = HEAD_DIM

    xp = x_prompt.reshape(nb_p * t_p, d)
    xs = _pad_axis(x_sample, 1, rs).reshape(nb_s * rs, d)
    mem2 = mem_prompt.reshape(-1, d)
    n_mem = mem_prompt.shape[1]

    pos_p = jnp.arange(t_p, dtype=jnp.int32)
    pos_s = past + jnp.arange(rs, dtype=jnp.int32)
    tile_s = lambda tabs: tuple(jnp.tile(tb, (nb_s, 1)) for tb in tabs)
    rope_p, rope_s = _rope_tables(pos_p, ROPE_DIMS, ROPE_THETA), _rope_tables(pos_s, ROPE_DIMS, ROPE_THETA)
    ret_p, ret_s = _rope_tables(pos_p, hd, RET_THETA), _rope_tables(pos_s, hd, RET_THETA)
    log_g = jnp.log1p(-jnp.exp2(-5.0 - jnp.arange(H_RET, dtype=F32)))

    ev_w_in_b, ev_w_out_b, od_w_out_b = ev_w_in.astype(BF16), ev_w_out.astype(BF16), od_w_out.astype(BF16)
    od_w_in_b = jnp.stack([_odd_w_in(od_w_in[o].astype(BF16)) for o in range(od_w_in.shape[0])])
    mem_wq_b, mem_wo_b = mem_w_q.astype(BF16), mem_w_o.astype(BF16)
    mem_wkv_b = jnp.concatenate([mem_w_k, mem_w_v], axis=2).astype(BF16)

    outs = {k: [] for k in ("ret_p", "ret_s", "dk_p", "dv_p", "dk_s", "dv_s", "ck_p", "cv_p", "ck_s", "cv_s",
                            "sk_p", "sv_p", "sk_s", "sv_s", "wk_p", "wv_p", "wk_s", "wv_s", "lh_p", "lh_s",
                            "lc_p", "lc_s", "mk_p", "mv_p")}
    new_rows = lambda a, w: a.reshape(nb_s, rs, w)[:, :t_s]

    for layer in range(depth):
        if layer % 2 == 0:
            e = layer // 2
            w_in, w_out, wl = ev_w_in_b, ev_w_out_b, e
            hw = H_RET * hd
            gains = dil_k_norm_g[e][None]
            sections = [(5, "norm_rope")]
            v_cb = 6 * H_DIL
            v_cols = slice(v_cb * hd, (v_cb + H_DIL) * hd)
            proj = rms_matmul(xp, norm_mix_g[layer], w_in, wl, 1024)
            proj3 = proj.reshape(nb_p, t_p, -1)
            y_ret, s_new = retention(proj3, jnp.zeros((nb_p, H_RET, hd, hd), F32), log_g, ret_gn_g[e], ret_p, RET_CHUNK)
            dk, = kv_transform(proj, sections, gains, rope_p, H_DIL, t_p)
            y_dil = dilated_prompt(proj3, 4 * H_RET, dk.reshape(nb_p, t_p, hw), v_cb, dil_q_norm_g[e], rope_p)
            xp = matmul_res(y_ret.reshape(-1, hw), y_dil.reshape(-1, hw), w_out, wl, xp)
            keep = min(DIL_PATTERNS[-1][0], t_p)
            outs["ret_p"].append(s_new)
            outs["dk_p"].append(dk.reshape(nb_p, t_p, H_DIL, hd)[:, t_p - keep:])
            outs["dv_p"].append(proj3[:, t_p - keep:, v_cols].reshape(nb_p, keep, H_DIL, hd))
            proj = rms_matmul(xs, norm_mix_g[layer], w_in, wl, 1024)
            proj3 = proj.reshape(nb_s, rs, -1)
            y_ret, s_new = retention(proj3, state_ret.reshape(-1, H_RET, hd, hd), log_g, ret_gn_g[e], ret_s, t_s,
                                     seq_off=e * nb_s)
            dk, = kv_transform(proj, sections, gains, tile_s(rope_s), H_DIL, nb_s * rs)
            lc = cache_dil_k.shape[2]
            y_dil = dilated_sample(proj3, 4 * H_RET, cache_dil_k.reshape(-1, lc, H_DIL, hd),
                                   cache_dil_v.reshape(-1, lc, H_DIL, hd), e * nb_s, dk.reshape(nb_s, rs, hw),
                                   v_cb, dil_q_norm_g[e], rope_s, t_s)
            xs = matmul_res(y_ret.reshape(-1, hw), y_dil.reshape(-1, hw), w_out, wl, xs)
            outs["ret_s"].append(s_new)
            outs["dk_s"].append(new_rows(dk, hw).reshape(nb_s, t_s, H_DIL, hd))
            outs["dv_s"].append(proj3[:, :t_s, v_cols].reshape(nb_s, t_s, H_DIL, hd))
        else:
            o = layer // 2
            w_in, w_out, wl = od_w_in_b, od_w_out_b, o
            qw = H_NSA * hd
            gw = G_NSA * hd
            x_cb = qw // hd
            y_cb = x_cb + D_RNN // hd
            kv_cb = (qw + 2 * D_RNN) // gw
            gate_cb = (qw + 2 * D_RNN + 6 * gw) // hd
            ones = jnp.ones((1, hd), F32)
            kg = nsa_k_norm_g[o]
            gains = jnp.concatenate([kg[0:1], ones, kg[1:2], ones, kg[2:3], ones], axis=0)
            sections = [(kv_cb + i, "norm_rope" if i % 2 == 0 else "copy") for i in range(6)]
            pwk2 = jnp.concatenate([nsa_pw_k[o]] * G_NSA, axis=1)
            pwv2 = jnp.concatenate([nsa_pw_v[o]] * G_NSA, axis=1)
            sp = jax.nn.softplus(-lru_lambda[o].astype(F32))
            lru_w = (lru_conv_w[o], lru_conv_b[o], lru_w_a[o], lru_b_a[o], lru_w_i[o], lru_b_i[o], sp)
            proj = rms_matmul(xp, norm_mix_g[layer], w_in, wl, 1024)
            proj3 = proj.reshape(nb_p, t_p, -1)
            kcm, vcm, ksl, vsl, kwn, vwn = kv_transform(proj, sections, gains, rope_p, G_NSA, t_p)
            r3 = lambda a: a.reshape(nb_p, t_p, gw)
            nb = t_p // NSA_BLOCK
            nbp = -(-nb // LANES) * LANES
            kc, vc = compress_rows(r3(kcm), r3(vcm), pwk2, pwv2)
            o_nsa = nsa_prompt(proj3, gate_cb, _pad_axis(kc, 1, nbp), _pad_axis(vc, 1, nbp), r3(ksl), r3(vsl),
                               r3(kwn), r3(vwn), _block_expand(nbp, t_p), nsa_q_norm_g[o], rope_p, nb)
            y_rnn, tail = rglru(proj3, x_cb, y_cb, jnp.zeros((nb_p, SUBLANES, D_RNN), F32),
                                jnp.zeros((nb_p, D_RNN), F32), *lru_w)
            xp = matmul_res(o_nsa.reshape(-1, qw), y_rnn.reshape(-1, D_RNN), w_out, wl, xp)
            r4 = lambda a: a.reshape(nb_p, t_p, G_NSA, hd)
            keep = min(NSA_WINDOW, t_p)
            for key, val in (("ck_p", r4(kcm)), ("cv_p", r4(vcm)), ("sk_p", r4(ksl)), ("sv_p", r4(vsl)),
                             ("wk_p", r4(kwn)[:, t_p - keep:]), ("wv_p", r4(vwn)[:, t_p - keep:])):
                outs[key].append(val)
            outs["lh_p"].append(tail[:, SUBLANES - 1])
            outs["lc_p"].append(proj3[:, t_p - (CONV_W - 1):, x_cb * hd:x_cb * hd + D_RNN])
            proj = rms_matmul(xs, norm_mix_g[layer], w_in, wl, 1024)
            proj3 = proj.reshape(nb_s, rs, -1)
            kcm, vcm, ksl, vsl, kwn, vwn = kv_transform(proj, sections, gains, tile_s(rope_s), G_NSA, nb_s * rs)
            r3 = lambda a: a.reshape(nb_s, rs, gw)
            nph = cache_nsa_cmp_k.shape[1]
            pool = lambda a: a.reshape(-1, PAGE_SIZE * G_NSA, hd)
            pps = math.gcd(n_pages, 32)
            nb = (past + t_s) // NSA_BLOCK
            nbp = -(-nb // LANES) * LANES
            kc, vc = compress_pages(pool(cache_nsa_cmp_k), pool(cache_nsa_cmp_v), o * nph, page_table,
                                    nsa_pw_k[o], nsa_pw_v[o], pps)
            lw = cache_nsa_win_k.shape[2]
            o_nsa = nsa_sample(proj3, gate_cb, _pad_axis(kc, 1, nbp), _pad_axis(vc, 1, nbp),
                               pool(cache_nsa_slc_k), pool(cache_nsa_slc_v), o * nph, page_table, r3(ksl), r3(vsl),
                               cache_nsa_win_k.reshape(-1, lw * G_NSA, hd), cache_nsa_win_v.reshape(-1, lw * G_NSA, hd),
                               o * nb_s, r3(kwn), r3(vwn), _block_expand(nbp, past), nsa_q_norm_g[o], rope_s, pps)
            hist8 = _pad_axis(state_lru_conv[o].astype(F32), 1, SUBLANES)
            y_rnn, tail = rglru(proj3, x_cb, y_cb, hist8, state_lru_h[o].astype(F32), *lru_w)
            xs = matmul_res(o_nsa.reshape(-1, qw), y_rnn.reshape(-1, D_RNN), w_out, wl, xs)
            for key, val in (("ck_s", kcm), ("cv_s", vcm), ("sk_s", ksl), ("sv_s", vsl), ("wk_s", kwn), ("wv_s", vwn)):
                outs[key].append(new_rows(val, gw).reshape(nb_s, t_s, G_NSA, hd))
            outs["lh_s"].append(tail[:, t_s - 1])
            lx = proj3[:, :t_s, x_cb * hd:x_cb * hd + D_RNN]
            outs["lc_s"].append(jnp.concatenate([state_lru_conv[o].astype(F32), lx], axis=1)[:, -(CONV_W - 1):])

        dm = H_MEM * hd
        mkv = rms_matmul(mem2, mem_norm_g[layer], mem_wkv_b, layer, dm)
        gains = jnp.concatenate([mem_k_norm_g[layer][None], jnp.ones((1, hd), F32)], axis=0)
        mk, mv = kv_transform(mkv, [(0, "norm"), (1, "copy")], gains, rope_p, H_MEM, n_mem)
        mk3, mv3 = mk.reshape(nb_p, n_mem, dm), mv.reshape(nb_p, n_mem, dm)
        outs["mk_p"].append(mk3.reshape(nb_p, n_mem, H_MEM, hd))
        outs["mv_p"].append(mv3.reshape(nb_p, n_mem, H_MEM, hd))
        xp = mem_attn_block(xp, norm_mem_g[layer], mem_wq_b, mem_q_norm_g[layer], mk3, mv3, mem_wo_b, layer, t_p)
        xs = mem_attn_block(xs, norm_mem_g[layer], mem_wq_b, mem_q_norm_g[layer],
                            cache_mem_k.reshape(-1, n_mem * H_MEM, hd), cache_mem_v.reshape(-1, n_mem * H_MEM, hd),
                            mem_wo_b, layer, rs, seq_off=layer * nb_s)
        xp = mlp_block(xp, norm_mlp_g[layer], mlp_w1, mlp_w2, layer)
        xs = mlp_block(xs, norm_mlp_g[layer], mlp_w1, mlp_w2, layer)

    st = lambda k: jnp.stack(outs[k])
    return (xp.reshape(nb_p, t_p, d), xs.reshape(nb_s, rs, d)[:, :t_s], st("ret_p"), st("ret_s"),
            st("dk_p"), st("dv_p"), st("dk_s"), st("dv_s"),
            st("ck_p"), st("cv_p"), st("ck_s"), st("cv_s"),
            st("sk_p"), st("sv_p"), st("sk_s"), st("sv_s"),
            st("wk_p"), st("wv_p"), st("wk_s"), st("wv_s"),
            st("lh_p"), st("lh_s"), st("lc_p"), st("lc_s"),
            st("mk_p"), st("mv_p"))
```
